```python
import math
import jax, jax.numpy as jnp
from jax import lax
import numpy as np

D_MODEL = 1024
BATCH = 16
SEQ = 256
DEPTH = 4
DEC_BATCH = 8
DEC_SEQ = 2048
PAST_LEN = 512

GRID_W = 64
N_EVEN = (DEPTH + 1) // 2
N_ODD = DEPTH // 2
NORM_EPS = 1e-6
GDN_HEADS = 4
GDN_DK = 128
GDN_DV = 128
GDN_CONV_W = 5
GDN_CHUNK = 64
HY_CH = 512
HY_ORDER = 2
HY_CONV_W = 3
HY_EMB = 33
HY_BANDS = (HY_EMB - 1) // 2
HY_FILTER_HIDDEN = 64
HY_DECAY_TARGET = 1e-2
HY_SHORT_DECAY_PCT = 0.3
HY_LONG_DECAY_PCT = 1.5
GQA_HEADS = 4
GQA_KV_HEADS = 2
GQA_HEAD_DIM = 128
MLA_HEADS = 4
MLA_Q_RANK = 256
MLA_KV_RANK = 128
MLA_NOPE_DIM = 128
MLA_ROPE_DIM = 64
MLA_V_DIM = 128
ROPE_THETA = 10000.0
Q_BLOCK = 128

GDN_QK_W = GDN_HEADS * GDN_DK
GDN_W = GDN_HEADS * GDN_DV
EVEN_SPLITS = (GDN_QK_W, GDN_QK_W, GDN_W, 2 * GDN_HEADS, 2 * GDN_HEADS, GDN_W, HY_CH, HY_CH, HY_CH, HY_CH)
EVEN_IN = sum(EVEN_SPLITS)
EVEN_MIX = GDN_W + HY_CH
GQA_W = GQA_HEADS * GQA_HEAD_DIM
GQA_KV_W = GQA_KV_HEADS * GQA_HEAD_DIM
MLA_W = MLA_HEADS * MLA_V_DIM
ODD_SPLITS = (GQA_W, GQA_KV_W, GQA_KV_W, GQA_W, MLA_Q_RANK, MLA_KV_RANK, MLA_ROPE_DIM, MLA_W)
ODD_IN = sum(ODD_SPLITS)
ODD_MIX = GQA_W + MLA_W

kernel_name = 'hybrid_gdn_hyena_gqa_mla_diffusion_step'


def split_cols(z, sizes):
    idx = np.cumsum(np.array(sizes))[:-1].tolist()
    return jnp.split(z, idx, axis=-1)


def rms_norm(x, g):
    xf = x.astype(jnp.float32)
    y = xf * lax.rsqrt(jnp.mean(xf * xf, axis=-1, keepdims=True) + NORM_EPS)
    return (y * g.astype(jnp.float32)).astype(x.dtype)


def l2_normalize(x):
    return x * lax.rsqrt(jnp.sum(x * x, axis=-1, keepdims=True) + NORM_EPS)


def adaln(cond, w, b):
    m = jax.nn.silu(cond) @ w + b
    return jnp.split(m[:, None, :], 3, axis=-1)


def depthwise_conv(x, w):
    width = w.shape[0]
    return lax.conv_general_dilated(x, w[:, None, :].astype(x.dtype), window_strides=(1,),
                                    padding=[(width // 2, width // 2)],
                                    dimension_numbers=('NWC', 'WIO', 'NWC'),
                                    feature_group_count=x.shape[-1])


def grid_positions(n_tokens):
    rows = n_tokens // GRID_W
    row = jnp.broadcast_to(jnp.arange(rows)[:, None], (rows, GRID_W)).reshape(-1)
    col = jnp.broadcast_to(jnp.arange(GRID_W)[None, :], (rows, GRID_W)).reshape(-1)
    return row.astype(jnp.float32), col.astype(jnp.float32)


def rope_1d(x, pos):
    d = x.shape[-1]
    half = d // 2
    inv = ROPE_THETA ** (-jnp.arange(half, dtype=jnp.float32) * 2.0 / d)
    ang = pos[:, None] * inv[None, :]
    cos = jnp.cos(ang)[:, None, :]
    sin = jnp.sin(ang)[:, None, :]
    xf = x.astype(jnp.float32)
    x1, x2 = xf[..., :half], xf[..., half:]
    return jnp.concatenate([x1 * cos - x2 * sin, x1 * sin + x2 * cos], axis=-1).astype(x.dtype)


def axial_rope(x):
    row, col = grid_positions(x.shape[1])
    r = x.shape[-1] // 2
    return jnp.concatenate([rope_1d(x[..., :r], row), rope_1d(x[..., r:], col)], axis=-1)


def block_attention(q, k, v, scale):
    B, Lq, Hkv, G, Dq = q.shape
    nb = Lq // Q_BLOCK
    kf = k.astype(jnp.float32)
    vf = v.astype(jnp.float32)
    qb = q.astype(jnp.float32).reshape(B, nb, Q_BLOCK, Hkv, G, Dq).swapaxes(0, 1)

    def one_block(qblk):
        s = jnp.einsum('bqhgd,bkhd->bhgqk', qblk, kf) * scale
        p = jax.nn.softmax(s, axis=-1)
        return jnp.einsum('bhgqk,bkhe->bqhge', p, vf)

    o = lax.map(one_block, qb)
    return o.swapaxes(0, 1).reshape(B, Lq, -1).astype(v.dtype)


def gdn_chunked(q, k, v, g, beta, s0):
    B, L, H, K = q.shape
    V = v.shape[-1]
    C = GDN_CHUNK
    n = L // C

    def chunks(t):
        return t.reshape((B, n, C, H) + t.shape[3:]).swapaxes(2, 3)

    q = chunks(q * (K ** -0.5))
    k = chunks(k)
    v = chunks(v)
    beta = chunks(beta)
    g = jnp.cumsum(chunks(g), axis=-1)
    kb = k * beta[..., None]
    vb = v * beta[..., None]
    tril = jnp.tril(jnp.ones((C, C), dtype=bool))
    strict = jnp.tril(jnp.ones((C, C), dtype=bool), -1)
    diff = g[..., :, None] - g[..., None, :]
    decay = jnp.where(tril, jnp.exp(jnp.where(tril, diff, 0.0)), 0.0)
    lmat = jnp.where(strict, jnp.einsum('bnhcd,bnhed->bnhce', kb, k) * decay, 0.0)
    a = lmat + jnp.eye(C, dtype=jnp.float32)
    rhs = jnp.concatenate([vb, kb * jnp.exp(g)[..., None]], axis=-1)
    sol = lax.linalg.triangular_solve(a, rhs, left_side=True, lower=True, unit_diagonal=True)
    u, w = sol[..., :V], sol[..., V:]
    intra = jnp.einsum('bnhcd,bnhed->bnhce', q, k) * decay
    qg = q * jnp.exp(g)[..., None]
    kdec = k * jnp.exp(g[..., -1:] - g)[..., None]
    glast = jnp.exp(g[..., -1])
    xs = tuple(t.swapaxes(0, 1) for t in (qg, kdec, u, w, intra, glast))

    def step(S, inp):
        qg_i, kd_i, u_i, w_i, in_i, gl_i = inp
        v_new = u_i - jnp.einsum('bhck,bhkv->bhcv', w_i, S)
        o = jnp.einsum('bhck,bhkv->bhcv', qg_i, S) + jnp.einsum('bhce,bhev->bhcv', in_i, v_new)
        S = S * gl_i[..., None, None] + jnp.einsum('bhck,bhcv->bhkv', kd_i, v_new)
        return S, o

    S, o = lax.scan(step, s0, xs)
    return o.transpose(1, 0, 3, 2, 4).reshape(B, L, H, V), S


def gdn_branch(zq, zk, zv, za, zb, zg, conv_w, a_log, dt_bias, norm_g, s0_f, s0_b):
    B, L, _ = zq.shape
    qkv = jax.nn.silu(depthwise_conv(jnp.concatenate([zq, zk, zv], axis=-1), conv_w).astype(jnp.float32))
    q, k, v = split_cols(qkv, (GDN_QK_W, GDN_QK_W, GDN_W))
    q = l2_normalize(q.reshape(B, L, GDN_HEADS, GDN_DK))
    k = l2_normalize(k.reshape(B, L, GDN_HEADS, GDN_DK))
    v = v.reshape(B, L, GDN_HEADS, GDN_DV)
    g = -jnp.exp(a_log.astype(jnp.float32)) * jax.nn.softplus(
        za.astype(jnp.float32).reshape(B, L, 2, GDN_HEADS) + dt_bias.astype(jnp.float32))
    beta = jax.nn.sigmoid(zb.astype(jnp.float32).reshape(B, L, 2, GDN_HEADS))
    o_f, s_f = gdn_chunked(q, k, v, g[:, :, 0], beta[:, :, 0], s0_f.astype(jnp.float32))
    flip = lambda t: jnp.flip(t, axis=1)
    o_b, s_b = gdn_chunked(flip(q), flip(k), flip(v), flip(g[:, :, 1]), flip(beta[:, :, 1]), s0_b.astype(jnp.float32))
    o = o_f + flip(o_b)
    o = rms_norm(o, norm_g) * jax.nn.silu(zg.astype(jnp.float32).reshape(B, L, GDN_HEADS, GDN_DV))
    return o.reshape(B, L, GDN_W).astype(zq.dtype), s_f, s_b


def hyena_filters(L, w1, b1, w2, b2, w3, sin_freq):
    f32 = jnp.float32
    t = jnp.linspace(0.0, 1.0, L, dtype=f32)[:, None]
    wpos = 2.0 * math.pi * jnp.arange(L, dtype=f32)[:, None] / L
    f = jnp.linspace(1e-4, HY_BANDS - 1, HY_BANDS, dtype=f32)[None, :]
    z = jnp.concatenate([t, jnp.cos(f * wpos), -jnp.sin(f * wpos)], axis=-1)
    sf = sin_freq.astype(f32)
    h = jnp.sin(sf[0] * (z @ w1.astype(f32) + b1.astype(f32)))
    h = jnp.sin(sf[1] * (h @ w2.astype(f32) + b2.astype(f32)))
    h = (h @ w3.astype(f32)).reshape(L, HY_ORDER, 2, HY_CH)
    deltas = jnp.abs(jnp.linspace(math.log(HY_DECAY_TARGET) / HY_LONG_DECAY_PCT,
                                  math.log(HY_DECAY_TARGET) / HY_SHORT_DECAY_PCT, HY_CH, dtype=f32))
    window = jnp.exp(-t * deltas[None, :])
    return h * window[:, None, None, :]


def two_sided_filter(h_fwd, h_bwd):
    zero = jnp.zeros((1, h_fwd.shape[1]), h_fwd.dtype)
    return jnp.concatenate([h_fwd, zero, h_bwd[1:][::-1]], axis=0)


def fft_long_conv(u, filt):
    L = u.shape[1]
    U = jnp.fft.rfft(u, n=2 * L, axis=1)
    Hf = jnp.fft.rfft(filt, n=2 * L, axis=0)
    return jnp.fft.irfft(U * Hf[None], n=2 * L, axis=1)[:, :L]


def hyena_branch(zv, z1, z2, zg, conv_w, conv_b, w1, b1, w2, b2, w3, sin_freq, skip):
    u = depthwise_conv(jnp.concatenate([zv, z1, z2], axis=-1), conv_w) + conv_b
    v, x1, x2 = split_cols(u.astype(jnp.float32), (HY_CH, HY_CH, HY_CH))
    filt = hyena_filters(u.shape[1], w1, b1, w2, b2, w3, sin_freq)
    z = v
    for o, xo in enumerate((x1, x2)):
        fo = two_sided_filter(filt[:, o, 0], filt[:, o, 1])
        z = xo * (fft_long_conv(z, fo) + z * skip[o].astype(jnp.float32))
    return (z * jax.nn.silu(zg.astype(jnp.float32))).astype(zv.dtype)


def even_mixer(h, p, s0_f, s0_b):
    z = h @ p['in_w']
    zq, zk, zv, za, zb, zg, hv, h1, h2, hg = split_cols(z, EVEN_SPLITS)
    oa, s_f, s_b = gdn_branch(zq, zk, zv, za, zb, zg, p['gdn_conv_w'], p['gdn_a_log'], p['gdn_dt_bias'],
                              p['gdn_norm_g'], s0_f, s0_b)
    ob = hyena_branch(hv, h1, h2, hg, p['hy_conv_w'], p['hy_conv_b'], p['hy_w1'], p['hy_b1'], p['hy_w2'],
                      p['hy_b2'], p['hy_w3'], p['hy_freq'], p['hy_skip'])
    y = jnp.concatenate([oa, ob], axis=-1) @ p['out_w']
    return y, s_f, s_b


def mla_expand(ckv, kpe, kv_up):
    B, L, _ = ckv.shape
    kv = (ckv @ kv_up).reshape(B, L, MLA_HEADS, MLA_NOPE_DIM + MLA_V_DIM)
    k_nope, vals = kv[..., :MLA_NOPE_DIM], kv[..., MLA_NOPE_DIM:]
    k_pe = jnp.broadcast_to(kpe[:, :, None, :].astype(k_nope.dtype), (B, L, MLA_HEADS, MLA_ROPE_DIM))
    return jnp.concatenate([k_nope, k_pe], axis=-1), vals


def odd_mixer(h, p, cache):
    B, L, _ = h.shape
    z = h @ p['in_w']
    gq, gk, gv, gg, mq, mkv, mpe, mg = split_cols(z, ODD_SPLITS)
    q = rms_norm(gq.reshape(B, L, GQA_HEADS, GQA_HEAD_DIM), p['q_norm_g'])
    k = rms_norm(gk.reshape(B, L, GQA_KV_HEADS, GQA_HEAD_DIM), p['k_norm_g'])
    v = gv.reshape(B, L, GQA_KV_HEADS, GQA_HEAD_DIM)
    qm = (rms_norm(mq, p['mq_norm_g']) @ p['mq_up']).reshape(B, L, MLA_HEADS, MLA_NOPE_DIM + MLA_ROPE_DIM)
    ckv = rms_norm(mkv, p['mkv_norm_g'])
    qm_nope, qm_pe = qm[..., :MLA_NOPE_DIM], qm[..., MLA_NOPE_DIM:]
    if cache is None:
        q_g, keys_g, vals_g = q, k, v
        km, vm = mla_expand(ckv, mpe, p['mkv_up'])
        new = (k, v, ckv, mpe)
    else:
        ck, cv, cckv, ckpe = cache
        q_g = axial_rope(q)
        keys_g = jnp.concatenate([axial_rope(k), ck.astype(k.dtype)], axis=1)
        vals_g = jnp.concatenate([v, cv.astype(v.dtype)], axis=1)
        qm_pe = axial_rope(qm_pe)
        kl, vl = mla_expand(ckv, axial_rope(mpe[:, :, None, :])[:, :, 0], p['mkv_up'])
        kc, vc = mla_expand(cckv.astype(ckv.dtype), ckpe, p['mkv_up'])
        km = jnp.concatenate([kl, kc], axis=1)
        vm = jnp.concatenate([vl, vc], axis=1)
        new = ()
    group = GQA_HEADS // GQA_KV_HEADS
    o_g = block_attention(q_g.reshape(B, L, GQA_KV_HEADS, group, GQA_HEAD_DIM), keys_g, vals_g,
                          GQA_HEAD_DIM ** -0.5) * jax.nn.silu(gg)
    qm_full = jnp.concatenate([qm_nope, qm_pe.astype(qm_nope.dtype)], axis=-1)[:, :, :, None, :]
    o_m = block_attention(qm_full, km, vm, (MLA_NOPE_DIM + MLA_ROPE_DIM) ** -0.5) * jax.nn.silu(mg)
    y = jnp.concatenate([o_g, o_m], axis=-1) @ p['out_w']
    return y, new


def setup_inputs(seed: int = 0) -> dict:
    key = jax.random.key(seed)
    ks = iter(jax.random.split(key, 48))
    f32 = jnp.float32

    def nrm(shape, s):
        return jax.random.normal(next(ks), shape, f32) * s

    x_prompt = nrm((BATCH, SEQ, D_MODEL), 1.0)
    x_sample = nrm((DEC_BATCH, DEC_SEQ, D_MODEL), 1.0)
    state_gdn_fwd = nrm((DEC_BATCH, N_EVEN, GDN_HEADS, GDN_DK, GDN_DV), 0.5)
    state_gdn_bwd = nrm((DEC_BATCH, N_EVEN, GDN_HEADS, GDN_DK, GDN_DV), 0.5)
    cache_gqa_k = nrm((DEC_BATCH, N_ODD, PAST_LEN, GQA_KV_HEADS, GQA_HEAD_DIM), 1.0)
    cache_gqa_v = nrm((DEC_BATCH, N_ODD, PAST_LEN, GQA_KV_HEADS, GQA_HEAD_DIM), 1.0)
    cache_mla_ckv = nrm((DEC_BATCH, N_ODD, PAST_LEN, MLA_KV_RANK), 1.0)
    cache_mla_kpe = nrm((DEC_BATCH, N_ODD, PAST_LEN, MLA_ROPE_DIM), 1.0)
    c = nrm((DEC_BATCH, D_MODEL), 1.0)
    c_ctx = nrm((D_MODEL,), 1.0)
    mod_w = nrm((DEPTH, D_MODEL, 3 * D_MODEL), D_MODEL ** -0.5)
    mod_b = nrm((DEPTH, 3 * D_MODEL), 0.02)
    pre_norm_g = 1.0 + nrm((DEPTH, D_MODEL), 0.02)
    post_norm_g = 1.0 + nrm((DEPTH, D_MODEL), 0.02)
    even_in_w = nrm((N_EVEN, D_MODEL, EVEN_IN), D_MODEL ** -0.5)
    gdn_conv_w = nrm((N_EVEN, GDN_CONV_W, 2 * GDN_QK_W + GDN_W), GDN_CONV_W ** -0.5)
    gdn_a_log = jnp.log(jax.random.uniform(next(ks), (N_EVEN, 2, GDN_HEADS), f32, 1.0, 16.0))
    dt = jnp.exp(jax.random.uniform(next(ks), (N_EVEN, 2, GDN_HEADS), f32, math.log(1e-3), math.log(1e-1)))
    gdn_dt_bias = dt + jnp.log(-jnp.expm1(-dt))
    gdn_norm_g = 1.0 + nrm((N_EVEN, GDN_DV), 0.02)
    hyena_conv_w = nrm((N_EVEN, HY_CONV_W, 3 * HY_CH), HY_CONV_W ** -0.5)
    hyena_conv_b = nrm((N_EVEN, 3 * HY_CH), 0.02)
    hyena_ffn_w1 = nrm((N_EVEN, HY_EMB, HY_FILTER_HIDDEN), HY_EMB ** -0.5)
    hyena_ffn_b1 = nrm((N_EVEN, HY_FILTER_HIDDEN), 0.02)
    hyena_ffn_w2 = nrm((N_EVEN, HY_FILTER_HIDDEN, HY_FILTER_HIDDEN), HY_FILTER_HIDDEN ** -0.5)
    hyena_ffn_b2 = nrm((N_EVEN, HY_FILTER_HIDDEN), 0.02)
    hyena_ffn_w3 = nrm((N_EVEN, HY_FILTER_HIDDEN, HY_ORDER * 2 * HY_CH), 0.1 * HY_FILTER_HIDDEN ** -0.5)
    hyena_sin_freq = 1.0 + nrm((N_EVEN, 2, HY_FILTER_HIDDEN), 0.02)
    hyena_bias = nrm((N_EVEN, HY_ORDER, HY_CH), 0.5)
    even_out_w = nrm((N_EVEN, EVEN_MIX, D_MODEL), EVEN_MIX ** -0.5)
    odd_in_w = nrm((N_ODD, D_MODEL, ODD_IN), D_MODEL ** -0.5)
    gqa_q_norm_g = 1.0 + nrm((N_ODD, GQA_HEAD_DIM), 0.02)
    gqa_k_norm_g = 1.0 + nrm((N_ODD, GQA_HEAD_DIM), 0.02)
    mla_q_norm_g = 1.0 + nrm((N_ODD, MLA_Q_RANK), 0.02)
    mla_q_up = nrm((N_ODD, MLA_Q_RANK, MLA_HEADS * (MLA_NOPE_DIM + MLA_ROPE_DIM)), MLA_Q_RANK ** -0.5)
    mla_kv_norm_g = 1.0 + nrm((N_ODD, MLA_KV_RANK), 0.02)
    mla_kv_up = nrm((N_ODD, MLA_KV_RANK, MLA_HEADS * (MLA_NOPE_DIM + MLA_V_DIM)), MLA_KV_RANK ** -0.5)
    odd_out_w = nrm((N_ODD, ODD_MIX, D_MODEL), ODD_MIX ** -0.5)
    return {'x_prompt': x_prompt, 'x_sample': x_sample,
            'state_gdn_fwd': state_gdn_fwd, 'state_gdn_bwd': state_gdn_bwd,
            'cache_gqa_k': cache_gqa_k, 'cache_gqa_v': cache_gqa_v,
            'cache_mla_ckv': cache_mla_ckv, 'cache_mla_kpe': cache_mla_kpe,
            'c': c, 'c_ctx': c_ctx, 'mod_w': mod_w, 'mod_b': mod_b,
            'pre_norm_g': pre_norm_g, 'post_norm_g': post_norm_g,
            'even_in_w': even_in_w, 'gdn_conv_w': gdn_conv_w, 'gdn_a_log': gdn_a_log,
            'gdn_dt_bias': gdn_dt_bias, 'gdn_norm_g': gdn_norm_g,
            'hyena_conv_w': hyena_conv_w, 'hyena_conv_b': hyena_conv_b,
            'hyena_ffn_w1': hyena_ffn_w1, 'hyena_ffn_b1': hyena_ffn_b1,
            'hyena_ffn_w2': hyena_ffn_w2, 'hyena_ffn_b2': hyena_ffn_b2,
            'hyena_ffn_w3': hyena_ffn_w3, 'hyena_sin_freq': hyena_sin_freq, 'hyena_bias': hyena_bias,
            'even_out_w': even_out_w, 'odd_in_w': odd_in_w,
            'gqa_q_norm_g': gqa_q_norm_g, 'gqa_k_norm_g': gqa_k_norm_g,
            'mla_q_norm_g': mla_q_norm_g, 'mla_q_up': mla_q_up,
            'mla_kv_norm_g': mla_kv_norm_g, 'mla_kv_up': mla_kv_up, 'odd_out_w': odd_out_w}


def reference(x_prompt, x_sample, state_gdn_fwd, state_gdn_bwd, cache_gqa_k, cache_gqa_v, cache_mla_ckv,
              cache_mla_kpe, c, c_ctx, mod_w, mod_b, pre_norm_g, post_norm_g, even_in_w, gdn_conv_w, gdn_a_log,
              gdn_dt_bias, gdn_norm_g, hyena_conv_w, hyena_conv_b, hyena_ffn_w1, hyena_ffn_b1, hyena_ffn_w2,
              hyena_ffn_b2, hyena_ffn_w3, hyena_sin_freq, hyena_bias, even_out_w, odd_in_w, gqa_q_norm_g,
              gqa_k_norm_g, mla_q_norm_g, mla_q_up, mla_kv_norm_g, mla_kv_up, odd_out_w):
    xp, xs = x_prompt, x_sample
    zero_state = jnp.zeros((x_prompt.shape[0], GDN_HEADS, GDN_DK, GDN_DV), jnp.float32)
    new_f, new_b, new_gk, new_gv, new_ckv, new_kpe = [], [], [], [], [], []
    for i in range(DEPTH):
        j = i // 2
        sh_p, sc_p, gt_p = adaln(c_ctx[None, :], mod_w[i], mod_b[i])
        sh_s, sc_s, gt_s = adaln(c, mod_w[i], mod_b[i])
        hp = rms_norm(xp, pre_norm_g[i]) * (1.0 + sc_p) + sh_p
        hs = rms_norm(xs, pre_norm_g[i]) * (1.0 + sc_s) + sh_s
        if i % 2 == 0:
            p = {'in_w': even_in_w[j], 'gdn_conv_w': gdn_conv_w[j], 'gdn_a_log': gdn_a_log[j],
                 'gdn_dt_bias': gdn_dt_bias[j], 'gdn_norm_g': gdn_norm_g[j],
                 'hy_conv_w': hyena_conv_w[j], 'hy_conv_b': hyena_conv_b[j],
                 'hy_w1': hyena_ffn_w1[j], 'hy_b1': hyena_ffn_b1[j], 'hy_w2': hyena_ffn_w2[j],
                 'hy_b2': hyena_ffn_b2[j], 'hy_w3': hyena_ffn_w3[j], 'hy_freq': hyena_sin_freq[j],
                 'hy_skip': hyena_bias[j], 'out_w': even_out_w[j]}
            yp, sf, sb = even_mixer(hp, p, zero_state, zero_state)
            ys, _, _ = even_mixer(hs, p, state_gdn_fwd[:, j], state_gdn_bwd[:, j])
            new_f.append(sf.astype(x_prompt.dtype))
            new_b.append(sb.astype(x_prompt.dtype))
        else:
            p = {'in_w': odd_in_w[j], 'q_norm_g': gqa_q_norm_g[j], 'k_norm_g': gqa_k_norm_g[j],
                 'mq_norm_g': mla_q_norm_g[j], 'mq_up': mla_q_up[j], 'mkv_norm_g': mla_kv_norm_g[j],
                 'mkv_up': mla_kv_up[j], 'out_w': odd_out_w[j]}
            yp, ctx = odd_mixer(hp, p, None)
            ys, _ = odd_mixer(hs, p, (cache_gqa_k[:, j], cache_gqa_v[:, j], cache_mla_ckv[:, j], cache_mla_kpe[:, j]))
            new_gk.append(ctx[0])
            new_gv.append(ctx[1])
            new_ckv.append(ctx[2])
            new_kpe.append(ctx[3])
        xp = xp + gt_p * rms_norm(yp, post_norm_g[i])
        xs = xs + gt_s * rms_norm(ys, post_norm_g[i])
    new_gdn_fwd = jnp.stack(new_f, axis=1)
    new_gdn_bwd = jnp.stack(new_b, axis=1)
    new_gqa_k = jnp.stack(new_gk, axis=1)
    new_gqa_v = jnp.stack(new_gv, axis=1)
    new_mla_ckv = jnp.stack(new_ckv, axis=1)
    new_mla_kpe = jnp.stack(new_kpe, axis=1)
    return (xp, xs, new_gdn_fwd, new_gdn_bwd, new_gqa_k, new_gqa_v, new_mla_ckv, new_mla_kpe)
```

```python
import functools
import math

import numpy as np
import jax
import jax.numpy as jnp
from jax import lax
from jax.experimental import pallas as pl
from jax.experimental.pallas import tpu as pltpu

F32 = jnp.float32
BF16 = jnp.bfloat16

NORM_EPS = 1e-6
ROPE_THETA = 10000.0
GRID_W = 64

GDN_HEADS = 4
GDN_DK = 128
GDN_CHUNK = 64
GDN_CONV_W = 5
HY_CH = 512
HY_BANDS = 16
HY_DECAY_TARGET = 1e-2
HY_SHORT_DECAY_PCT = 0.3
HY_LONG_DECAY_PCT = 1.5
GQA_HEADS = 4
GQA_KV_HEADS = 2
HEAD_DIM = 128
MLA_HEADS = 4
MLA_ROPE_DIM = 64

LANES = 128
VMEM_LIMIT = 56 * 1024 * 1024

TOKEN_TILE = 256
HY_FREQ_BLOCK = 256
HY_CH_BLOCK = 256


def _cparams(sem):
    return pltpu.CompilerParams(dimension_semantics=sem, vmem_limit_bytes=VMEM_LIMIT)


def _dot(a, b):
    return jnp.dot(a, b, preferred_element_type=F32)


def _dot_nt(a, b):
    return lax.dot_general(a, b, (((1,), (1,)), ((), ())), preferred_element_type=F32)


def _dot_tn(a, b):
    return lax.dot_general(a, b, (((0,), (0,)), ((), ())), preferred_element_type=F32)


def _split(a):
    hi = a.astype(BF16)
    lo = (a - hi.astype(F32)).astype(BF16)
    return hi, lo


def _dot3(a, b):
    ah, al = _split(a)
    bh, bl = _split(b)
    return _dot(ah, bh) + (_dot(ah, bl) + _dot(al, bh))


def _silu(x):
    return x * (1.0 / (1.0 + jnp.exp(-x)))


def _sigmoid(x):
    return 1.0 / (1.0 + jnp.exp(-x))


def _softplus(x):
    return jnp.maximum(x, 0.0) + jnp.log(1.0 + jnp.exp(-jnp.abs(x)))


def _rms(x, g):
    return x * lax.rsqrt(jnp.mean(x * x, axis=-1, keepdims=True) + NORM_EPS) * g


def _shift_rows(x, s):
    L = x.shape[0]
    if s == 0:
        return x
    rolled = pltpu.roll(x, (-s) % L, 0)
    row = lax.broadcasted_iota(jnp.int32, x.shape, 0)
    valid = (row + s >= 0) & (row + s < L)
    return jnp.where(valid, rolled, 0.0)


def _mod_kernel(c_ref, w_ref, b_ref, o_ref):
    c = _silu(c_ref[...])
    o_ref[0] = _dot3(c, w_ref[0]) + b_ref[0]


def _modulation(cond, mod_w, mod_b):
    depth, d, d3 = mod_w.shape
    r = cond.shape[0]
    nb = d3 // d
    return pl.pallas_call(
        _mod_kernel,
        grid=(depth, nb),
        in_specs=[pl.BlockSpec((r, d), lambda i, n: (0, 0)),
                  pl.BlockSpec((1, d, d), lambda i, n: (i, 0, n)),
                  pl.BlockSpec((1, 1, d), lambda i, n: (i, 0, n))],
        out_specs=pl.BlockSpec((1, r, d), lambda i, n: (i, 0, n)),
        out_shape=jax.ShapeDtypeStruct((depth, r, d3), F32),
        compiler_params=_cparams(("arbitrary", "arbitrary")),
        name="adaln_modulation",
    )(cond, mod_w, mod_b.reshape(depth, 1, d3))


def _in_proj_kernel(x_ref, sh_ref, sc_ref, g_ref, w_ref, *rest, n_main, has_small):
    if has_small:
        ws_ref, o_ref, os_ref = rest
    else:
        (o_ref,) = rest
    x = x_ref[...]
    h = _rms(x, g_ref[...]) * (1.0 + sc_ref[0]) + sh_ref[0]
    hb = h.astype(BF16)
    step = 512
    for n0 in range(0, n_main, step):
        o_ref[:, n0:n0 + step] = _dot(hb, w_ref[:, n0:n0 + step])
    if has_small:
        os_ref[...] = _dot(hb, ws_ref[...])


def _in_proj(x2, shift, scale, pre_g, w_main, w_small, tiles_per_mod):
    T, D = x2.shape
    N = w_main.shape[1]
    tm = TOKEN_TILE
    if tiles_per_mod:
        mod_map = lambda i: (i // tiles_per_mod, 0, 0)
    else:
        mod_map = lambda i: (0, 0, 0)
    in_specs = [pl.BlockSpec((tm, D), lambda i: (i, 0)),
                pl.BlockSpec((1, 1, D), mod_map),
                pl.BlockSpec((1, 1, D), mod_map),
                pl.BlockSpec((1, D), lambda i: (0, 0)),
                pl.BlockSpec((D, N), lambda i: (0, 0))]
    out_specs = [pl.BlockSpec((tm, N), lambda i: (i, 0))]
    out_shape = [jax.ShapeDtypeStruct((T, N), F32)]
    args = [x2, shift, scale, pre_g, w_main]
    if w_small is not None:
        in_specs.append(pl.BlockSpec((D, LANES), lambda i: (0, 0)))
        out_specs.append(pl.BlockSpec((tm, LANES), lambda i: (i, 0)))
        out_shape.append(jax.ShapeDtypeStruct((T, LANES), F32))
        args.append(w_small)
    res = pl.pallas_call(
        functools.partial(_in_proj_kernel, n_main=N, has_small=w_small is not None),
        grid=(T // tm,),
        in_specs=in_specs, out_specs=out_specs, out_shape=out_shape,
        compiler_params=_cparams(("arbitrary",)),
        name="in_proj",
    )(*args)
    return res


def _out_proj_kernel(a_ref, b_ref, x_ref, gt_ref, g_ref, w_ref, o_ref, *, half):
    y = _dot(a_ref[...].astype(BF16), w_ref[:half, :]) + _dot(b_ref[...].astype(BF16), w_ref[half:, :])
    o_ref[...] = x_ref[...] + gt_ref[0] * _rms(y, g_ref[...])


def _out_proj(a, b, x2, gate, post_g, w, tiles_per_mod):
    T, D = x2.shape
    half = a.shape[1]
    tm = TOKEN_TILE
    if tiles_per_mod:
        mod_map = lambda i: (i // tiles_per_mod, 0, 0)
    else:
        mod_map = lambda i: (0, 0, 0)
    return pl.pallas_call(
        functools.partial(_out_proj_kernel, half=half),
        grid=(T // tm,),
        in_specs=[pl.BlockSpec((tm, half), lambda i: (i, 0)),
                  pl.BlockSpec((tm, half), lambda i: (i, 0)),
                  pl.BlockSpec((tm, D), lambda i: (i, 0)),
                  pl.BlockSpec((1, 1, D), mod_map),
                  pl.BlockSpec((1, D), lambda i: (0, 0)),
                  pl.BlockSpec((2 * half, D), lambda i: (0, 0))],
        out_specs=pl.BlockSpec((tm, D), lambda i: (i, 0)),
        out_shape=jax.ShapeDtypeStruct((T, D), F32),
        compiler_params=_cparams(("arbitrary",)),
        name="out_proj",
    )(a, b, x2, gate, post_g, w)


def _gdn_kernel(zq_ref, zk_ref, zv_ref, zg_ref, zab_ref, cwq_ref, cwk_ref, cwv_ref, ng_ref,
                alog_ref, dtb_ref, *rest, L, has_state):
    if has_state:
        s0f_ref, s0b_ref = rest[:2]
        rest = rest[2:]
    o_ref, sf_out_ref, sb_out_ref, q_s, k_s, v_s, g_s, b_s, of_s, ob_s, sf_s, sb_s = rest
    C = GDN_CHUNK
    n_chunks = L // C
    head = pl.program_id(1)

    def conv_silu(z_ref, w_ref):
        x = z_ref[...]
        half = GDN_CONV_W // 2
        acc = x * w_ref[half:half + 1, :]
        for i in range(GDN_CONV_W):
            if i != half:
                acc = acc + _shift_rows(x, i - half) * w_ref[i:i + 1, :]
        return _silu(acc)

    def l2n(x):
        return x * lax.rsqrt(jnp.sum(x * x, axis=-1, keepdims=True) + NORM_EPS)

    q_s[...] = l2n(conv_silu(zq_ref, cwq_ref)) * (GDN_DK ** -0.5)
    k_s[...] = l2n(conv_silu(zk_ref, cwk_ref))
    v_s[...] = conv_silu(zv_ref, cwv_ref)
    zab = zab_ref[...]
    g_s[...] = -jnp.exp(alog_ref[...]) * _softplus(zab + dtb_ref[...])
    b_s[...] = _sigmoid(zab)
    if has_state:
        sf_s[...] = s0f_ref[0, 0, 0]
        sb_s[...] = s0b_ref[0, 0, 0]
    else:
        sf_s[...] = jnp.zeros_like(sf_s)
        sb_s[...] = jnp.zeros_like(sb_s)

    row = lax.broadcasted_iota(jnp.int32, (C, C), 0)
    col = lax.broadcasted_iota(jnp.int32, (C, C), 1)
    eye = (row == col).astype(F32)
    levels = [(row >> s) == (col >> s) for s in (3, 4, 5)] + [row >= 0]
    lane =lax.broadcasted_iota(jnp.int32, (C, LANES), 1)
    sub = lax.broadcasted_iota(jnp.int32, (LANES, C), 0)
    rowv = lax.broadcasted_iota(jnp.int32, (C, 1), 0)

    def one_chunk(ci, direction, s_ref, out_s):
        if direction == 0:
            incl = row >= col
            strict = row > col
            last = C - 1
        else:
            incl = row <= col
            strict = row < col
            last = 0
        r0 = pl.multiple_of(ci * C, C)
        q = q_s[pl.ds(r0, C), :]
        k = k_s[pl.ds(r0, C), :]
        v = v_s[pl.ds(r0, C), :]
        gall = g_s[pl.ds(r0, C), :]
        ball = b_s[pl.ds(r0, C), :]
        gcum = _dot3(incl.astype(F32), gall)
        gcum_t = gcum.T
        sel = direction * GDN_HEADS + head
        gc = jnp.sum(jnp.where(lane == sel, gcum, 0.0), axis=1, keepdims=True)
        gr = jnp.sum(jnp.where(sub == sel, gcum_t, 0.0), axis=0, keepdims=True)
        beta = jnp.sum(jnp.where(lane == sel + 2 * GDN_HEADS, ball, 0.0), axis=1, keepdims=True)
        decay = jnp.where(incl, jnp.exp(jnp.where(incl, gc - gr, 0.0)), 0.0)
        kb = k * beta
        kbf = k.astype(BF16)
        kk = _dot_nt(kb.astype(BF16), kbf)
        lmat = jnp.where(strict, kk * decay, 0.0)
        p = jnp.where(levels[0], -lmat, 0.0)
        tmat = eye + p
        for _ in range(2):
            p = _dot3(p, p)
            tmat = tmat + _dot3(tmat, p)
        for inner, outer in zip(levels[:-1], levels[1:]):
            e = jnp.where(outer & jnp.logical_not(inner), lmat, 0.0)
            tmat = tmat - _dot3(_dot3(tmat, e), tmat)
        eg = jnp.exp(gc)
        rhs = jnp.concatenate([v * beta, kb * eg], axis=1)
        sol = _dot3(tmat, rhs)
        u = sol[:, :LANES]
        w = sol[:, LANES:]
        intra = _dot_nt(q.astype(BF16), kbf) * decay
        glast = jnp.sum(jnp.where(rowv == last, gc, 0.0), axis=0, keepdims=True)
        kdec = k * jnp.exp(glast - gc)
        s = s_ref[...]
        sb16 = s.astype(BF16)
        v_new = u - _dot(w.astype(BF16), sb16)
        vb16 = v_new.astype(BF16)
        o = _dot((q * eg).astype(BF16), sb16) + _dot(intra.astype(BF16), vb16)
        s_ref[...] = s * jnp.exp(glast) + _dot_tn(kdec.astype(BF16), vb16)
        out_s[pl.ds(r0, C), :] = o

    def body(c, carry):
        one_chunk(c, 0, sf_s, of_s)
        one_chunk(n_chunks - 1 - c, 1, sb_s, ob_s)
        return carry

    lax.fori_loop(0, n_chunks, body, 0)
    o = of_s[...] + ob_s[...]
    o_ref[...] = _rms(o, ng_ref[...]) * _silu(zg_ref[...])
    sf_out_ref[0, 0] = sf_s[...]
    sb_out_ref[0, 0] = sb_s[...]


def _gdn(z, zab, conv_w, norm_g, alog_row, dtb_row, B, L, state_f=None, state_b=None, layer_j=0):
    T = z.shape[0]
    H = GDN_HEADS
    has_state = state_f is not None
    blk = lambda off: pl.BlockSpec((L, LANES), lambda b, h: (b, off + h))
    cw = lambda off: pl.BlockSpec((GDN_CONV_W, LANES), lambda b, h: (0, off + h))
    in_specs = [blk(0), blk(H), blk(2 * H), blk(3 * H),
                pl.BlockSpec((L, LANES), lambda b, h: (b, 0)),
                cw(0), cw(H), cw(2 * H),
                pl.BlockSpec((1, LANES), lambda b, h: (0, 0)),
                pl.BlockSpec((1, LANES), lambda b, h: (0, 0)),
                pl.BlockSpec((1, LANES), lambda b, h: (0, 0))]
    args = [z, z, z, z, zab, conv_w, conv_w, conv_w, norm_g, alog_row, dtb_row]
    if has_state:
        st = pl.BlockSpec((1, 1, 1, GDN_DK, LANES), lambda b, h: (b, layer_j, h, 0, 0))
        in_specs += [st, st]
        args += [state_f, state_b]
    sout = pl.BlockSpec((1, 1, GDN_DK, LANES), lambda b, h: (b, h, 0, 0))
    scratch = [pltpu.VMEM((L, LANES), F32) for _ in range(7)] + [pltpu.VMEM((GDN_DK, LANES), F32)] * 2
    return pl.pallas_call(
        functools.partial(_gdn_kernel, L=L, has_state=has_state),
        grid=(B, H),
        in_specs=in_specs,
        out_specs=[pl.BlockSpec((L, LANES), lambda b, h: (b, h)), sout, sout],
        out_shape=[jax.ShapeDtypeStruct((T, H * LANES), F32),
                   jax.ShapeDtypeStruct((B, H, GDN_DK, LANES), F32),
                   jax.ShapeDtypeStruct((B, H, GDN_DK, LANES), F32)],
        scratch_shapes=scratch,
        compiler_params=_cparams(("arbitrary", "arbitrary")),
        name="gdn",
    )(*args)


def _hy_filter_kernel(f_ref, w1_ref, b1_ref, sf0_ref, w2_ref, b2_ref, sf1_ref, w3_ref, dl_ref, o_ref, *, L):
    rowi = lax.broadcasted_iota(jnp.int32, (L, LANES), 0).astype(F32)
    lane = lax.broadcasted_iota(jnp.int32, (L, LANES), 1)
    t = rowi * (1.0 / (L - 1))
    wpos = rowi * (2.0 * math.pi / L)
    ang = f_ref[...] * wpos
    z = jnp.where(lane == 0, t,
                  jnp.where(lane <= HY_BANDS, jnp.cos(ang),
                            jnp.where(lane <= 2 * HY_BANDS, -jnp.sin(ang), 0.0)))
    h = jnp.sin(sf0_ref[...] * (_dot3(z, w1_ref[...]) + b1_ref[...]))
    h = jnp.sin(sf1_ref[...] * (_dot3(h, w2_ref[...]) + b2_ref[...]))
    tc = lax.broadcasted_iota(jnp.int32, (L, HY_CH), 0).astype(F32) * (1.0 / (L - 1))
    window = jnp.exp(-tc * dl_ref[...])
    for j in range(4):
        o_ref[:, j * HY_CH:(j + 1) * HY_CH] = _dot3(h, w3_ref[:, j * HY_CH:(j + 1) * HY_CH]) * window


def _pad_to(a, shape):
    return jnp.pad(a, [(0, s - d) for s, d in zip(shape, a.shape)])


def _hy_filters(L, w1, b1, w2, b2, w3, sin_freq):
    fvals = np.linspace(1e-4, HY_BANDS - 1, HY_BANDS, dtype=np.float32)
    frow = np.zeros((1, LANES), np.float32)
    frow[0, 1:1 + HY_BANDS] = fvals
    frow[0, 1 + HY_BANDS:1 + 2 * HY_BANDS] = fvals
    deltas = np.abs(np.linspace(math.log(HY_DECAY_TARGET) / HY_LONG_DECAY_PCT,
                                math.log(HY_DECAY_TARGET) / HY_SHORT_DECAY_PCT, HY_CH, dtype=np.float32))
    args = [jnp.asarray(frow),
            _pad_to(w1, (LANES, LANES)), _pad_to(b1[None, :], (1, LANES)), _pad_to(sin_freq[0][None, :], (1, LANES)),
            _pad_to(w2, (LANES, LANES)), _pad_to(b2[None, :], (1, LANES)), _pad_to(sin_freq[1][None, :], (1, LANES)),
            _pad_to(w3, (LANES, 4 * HY_CH)), jnp.asarray(deltas[None, :])]
    return pl.pallas_call(
        functools.partial(_hy_filter_kernel, L=L),
        out_shape=jax.ShapeDtypeStruct((L, 4 * HY_CH), F32),
        compiler_params=pltpu.CompilerParams(vmem_limit_bytes=VMEM_LIMIT),
        name="hyena_filters",
    )(*args)


def _dft_tables(L):
    N = 2 * L
    k = np.arange(L, dtype=np.int64)[:, None]
    s = np.arange(L, dtype=np.int64)[None, :]
    ang = ((2 * k + 1) * s % (2 * N)).astype(np.float64) * (2.0 * math.pi / (2 * N))
    return np.cos(ang).astype(np.float32), np.sin(ang).astype(np.float32)


def _dft_blocks(L, kb):
    cm, sm = _dft_tables(L)
    nk = L // kb
    fwd = np.concatenate([cm.reshape(nk, kb, L), sm.reshape(nk, kb, L)], axis=1)
    inv = np.concatenate([cm.T.reshape(L, nk, kb), sm.T.reshape(L, nk, kb)], axis=2)
    inv = np.ascontiguousarray(inv.transpose(1, 0, 2))
    return jnp.asarray(fwd).astype(BF16), jnp.asarray(inv).astype(BF16)


def _hy_spectrum_kernel(f_ref, flt_ref, hc_ref, hs_ref, *, L, kb):
    row = lax.broadcasted_iota(jnp.int32, (L, HY_CH), 0)
    scale = 1.0 / L
    for o in range(2):
        hf = flt_ref[:, (2 * o) * HY_CH:(2 * o + 1) * HY_CH]
        hb = jnp.where(row == 0, 0.0, flt_ref[:, (2 * o + 1) * HY_CH:(2 * o + 2) * HY_CH])
        a_hi, a_lo = _split(hf + hb)
        d_hi, d_lo = _split(hf - hb)
        fc = f_ref[0, :kb, :]
        fs = f_ref[0, kb:, :]
        hc_ref[o] = (_dot(fc, a_hi) + _dot(fc, a_lo)) * scale
        hs_ref[o] = (_dot(fs, d_hi) + _dot(fs, d_lo)) * scale


def _hy_spectrum(filt, fwd, L, kb):
    nk = L // kb
    return pl.pallas_call(
        functools.partial(_hy_spectrum_kernel, L=L, kb=kb),
        grid=(nk,),
        in_specs=[pl.BlockSpec((1, 2 * kb, L), lambda i: (i, 0, 0)),
                  pl.BlockSpec((L, 4 * HY_CH), lambda i: (0, 0))],
        out_specs=[pl.BlockSpec((2, kb, HY_CH), lambda i: (0, i, 0)),
                   pl.BlockSpec((2, kb, HY_CH), lambda i: (0, i, 0))],
        out_shape=[jax.ShapeDtypeStruct((2, L, HY_CH), F32)] * 2,
        compiler_params=_cparams(("arbitrary",)),
        name="hyena_spectrum",
    )(fwd, filt)


def _hy_conv_kernel(zin_ref, xo_ref, gate_ref, cwz_ref, cbz_ref, cwx_ref, cbx_ref, skip_ref,
                    f_ref, g_ref, hc_ref, hs_ref, o_ref, zb_s, acc_s, *, kb, first, last):
    kstep = pl.program_id(2)
    nk = pl.num_programs(2)

    def conv3(x_ref, w_ref, b_ref):
        x = x_ref[...]
        return (_shift_rows(x, -1) * w_ref[0:1, :] + x * w_ref[1:2, :]
                + _shift_rows(x, 1) * w_ref[2:3, :]) + b_ref[...]

    def z_value():
        return conv3(zin_ref, cwz_ref, cbz_ref) if first else zin_ref[...]

    @pl.when(kstep == 0)
    def _():
        zb_s[...] = z_value().astype(BF16)
        acc_s[...] = jnp.zeros_like(acc_s)

    x = _dot(f_ref[0], zb_s[...])
    xc = x[:kb]
    xs = x[kb:]
    hc = hc_ref[0]
    hs = hs_ref[0]
    y = jnp.concatenate([xc * hc - xs * hs, xc * hs + xs * hc], axis=0).astype(BF16)
    acc_s[...] += _dot(g_ref[0], y)

    @pl.when(kstep == nk - 1)
    def _():
        xo = conv3(xo_ref, cwx_ref, cbx_ref)
        res = xo * (acc_s[...] + z_value() * skip_ref[...])
        if last:
            res = res * _silu(gate_ref[...])
        o_ref[...] = res


def _hy_conv(zsrc, zcol, z, xcol, gcol, conv_w, conv_b, skip, order, fwd, inv, hc, hs, B, L, first, last):
    T = z.shape[0]
    cb = HY_CH_BLOCK
    ncb = HY_CH // cb
    nk, _, kb2 = inv.shape
    kb = kb2 // 2
    data = lambda off: pl.BlockSpec((L, cb), lambda b, c, k: (b, off + c))
    cwspec = lambda off: pl.BlockSpec((3, cb), lambda b, c, k: (0, off + c))
    cbspec = lambda off: pl.BlockSpec((1, cb), lambda b, c, k: (0, off + c))
    in_specs = [data(zcol), data(xcol), data(gcol),
                cwspec(0), cbspec(0), cwspec((1 + order) * ncb), cbspec((1 + order) * ncb),
                pl.BlockSpec((1, cb), lambda b, c, k: (0, c)),
                pl.BlockSpec((1, 2 * kb, L), lambda b, c, k: (k, 0, 0)),
                pl.BlockSpec((1, L, 2 * kb), lambda b, c, k: (k, 0, 0)),
                pl.BlockSpec((1, kb, cb), lambda b, c, k: (order, k, c)),
                pl.BlockSpec((1, kb, cb), lambda b, c, k: (order, k, c))]
    return pl.pallas_call(
        functools.partial(_hy_conv_kernel, kb=kb, first=first, last=last),
        grid=(B, ncb, nk),
        in_specs=in_specs,
        out_specs=pl.BlockSpec((L, cb), lambda b, c, k: (b, c)),
        out_shape=jax.ShapeDtypeStruct((T, HY_CH), F32),
        scratch_shapes=[pltpu.VMEM((L, cb), BF16), pltpu.VMEM((L, cb), F32)],
        compiler_params=_cparams(("arbitrary", "arbitrary", "arbitrary")),
        name="hyena_conv",
    )(zsrc, z, z, conv_w, conv_b, conv_w, conv_b, skip[order:order + 1], fwd, inv, hc, hs)


OD_GQ, OD_GG, OD_MG, OD_GK, OD_GV, OD_MQ, OD_MKV, OD_MPE = 0, 4, 8, 12, 14, 16, 18, 19
OD_WIDTH = 20 * LANES


def _rope_swap(x, quarter):
    n = x.shape[1]
    lane = lax.broadcasted_iota(jnp.int32, x.shape, 1)
    first = (lane & (2 * quarter - 1)) < quarter
    return jnp.where(first, pltpu.roll(x, n - quarter, 1), pltpu.roll(x, quarter, 1))


def _odd_prep_kernel(z_ref, qg_ref, kg_ref, mqg_ref, mqup_ref, mkvg_ref, wk_ref, *rest, rope):
    if rope:
        cg_ref, sg_ref, cm_ref, sm_ref = rest[:4]
        rest = rest[4:]
    q_out, k_out, qa_out, qp_out, ckv_out, kpe_out = rest
    D = HEAD_DIM

    def rot(x, c, s, quarter):
        return x * c + _rope_swap(x, quarter) * s if rope else x

    for h in range(GQA_HEADS):
        q = _rms(z_ref[:, (OD_GQ + h) * D:(OD_GQ + h + 1) * D], qg_ref[...])
        if rope:
            q = rot(q, cg_ref[...], sg_ref[...], D // 4)
        q_out[:, h * D:(h + 1) * D] = (q * (D ** -0.5)).astype(BF16)
    for h in range(GQA_KV_HEADS):
        k = _rms(z_ref[:, (OD_GK + h) * D:(OD_GK + h + 1) * D], kg_ref[...])
        if rope:
            k = rot(k, cg_ref[...], sg_ref[...], D // 4)
        k_out[:, h * D:(h + 1) * D] = k
    mq = _rms(z_ref[:, OD_MQ * D:(OD_MQ + 2) * D], mqg_ref[...]).astype(BF16)
    qm = _dot(mq, mqup_ref[...])
    mscale = (D + MLA_ROPE_DIM) ** -0.5
    for h in range(MLA_HEADS):
        qn = qm[:, h * D:(h + 1) * D].astype(BF16)
        qa_out[:, h * D:(h + 1) * D] = (_dot_nt(qn, wk_ref[h]) * mscale).astype(BF16)
    qpe = qm[:, MLA_HEADS * D:]
    if rope:
        qpe = rot(qpe, cm_ref[...], sm_ref[...], MLA_ROPE_DIM // 4)
    qp_out[...] = (qpe * mscale).astype(BF16)
    ckv_out[...] = _rms(z_ref[:, OD_MKV * D:(OD_MKV + 1) * D], mkvg_ref[...])
    kpe = z_ref[:, OD_MPE * D:(OD_MPE + 1) * D]
    if rope:
        kpe = rot(kpe, cm_ref[:, :D], sm_ref[:, :D], MLA_ROPE_DIM // 4)
    kpe_out[...] = kpe


def _rope_tables(L, R):
    rows = np.arange(L) // GRID_W
    cols = np.arange(L) % GRID_W
    quarter = R // 4
    inv = ROPE_THETA ** (-np.arange(quarter, dtype=np.float32) * 2.0 / (R // 2))
    a_r = rows[:, None].astype(np.float32) * inv[None, :]
    a_c = cols[:, None].astype(np.float32) * inv[None, :]
    cos = np.concatenate([np.cos(a_r), np.cos(a_r), np.cos(a_c), np.cos(a_c)], axis=1)
    sin = np.concatenate([-np.sin(a_r), np.sin(a_r), -np.sin(a_c), np.sin(a_c)], axis=1)
    return cos.astype(np.float32), sin.astype(np.float32)


def _odd_prep(z, q_g, k_g, mq_g, mq_up, mkv_g, wk, L, rope):
    T = z.shape[0]
    tm = TOKEN_TILE
    D = HEAD_DIM
    full = lambda shape: pl.BlockSpec(shape, lambda i: tuple(0 for _ in shape))
    in_specs = [pl.BlockSpec((tm, OD_WIDTH), lambda i: (i, 0)),
                full((1, D)), full((1, D)), full((1, 2 * D)), full(mq_up.shape), full((1, D)), full(wk.shape)]
    args = [z, q_g, k_g, mq_g, mq_up, mkv_g, wk]
    if rope:
        cg, sg = _rope_tables(L, D)
        cm, sm = _rope_tables(L, MLA_ROPE_DIM)
        cm, sm = np.tile(cm, (1, MLA_HEADS)), np.tile(sm, (1, MLA_HEADS))
        per = L // tm
        pos = lambda w: pl.BlockSpec((tm, w), lambda i: (i % per, 0))
        in_specs += [pos(D), pos(D), pos(2 * D), pos(2 * D)]
        args += [jnp.asarray(cg), jnp.asarray(sg), jnp.asarray(cm), jnp.asarray(sm)]
    tile = lambda w: pl.BlockSpec((tm, w), lambda i: (i, 0))
    return pl.pallas_call(
        functools.partial(_odd_prep_kernel, rope=rope),
        grid=(T // tm,),
        in_specs=in_specs,
        out_specs=[tile(4 * D), tile(2 * D), tile(4 * D), tile(2 * D), tile(D), tile(D)],
        out_shape=[jax.ShapeDtypeStruct((T, 4 * D), BF16), jax.ShapeDtypeStruct((T, 2 * D), F32),
                   jax.ShapeDtypeStruct((T, 4 * D), BF16), jax.ShapeDtypeStruct((T, 2 * D), BF16),
                   jax.ShapeDtypeStruct((T, D), F32), jax.ShapeDtypeStruct((T, D), F32)],
        compiler_params=_cparams(("arbitrary",)),
        name="odd_prep",
    )(*args)


def _softmax_pv(score_blocks, value_blocks):
    m = score_blocks[0].max(axis=-1, keepdims=True)
    for s in score_blocks[1:]:
        m = jnp.maximum(m, s.max(axis=-1, keepdims=True))
    acc = None
    den = None
    for s, v in zip(score_blocks, value_blocks):
        p = jnp.exp(s - m)
        d = p.sum(axis=-1, keepdims=True)
        a = _dot(p.astype(BF16), v)
        acc = a if acc is None else acc + a
        den = d if den is None else den + d
    return acc / den


def _gqa_kernel(q_ref, k_ref, v_ref, gate_ref, *rest, cached):
    if cached:
        ck_ref, cv_ref, o_ref = rest
    else:
        (o_ref,) = rest
    D = HEAD_DIM
    group = GQA_HEADS // GQA_KV_HEADS
    q = q_ref[...]
    qs = jnp.concatenate([q[:, g * D:(g + 1) * D] for g in range(group)], axis=0)
    scores = [_dot_nt(qs, k_ref[...].astype(BF16))]
    values = [v_ref[...].astype(BF16)]
    if cached:
        scores.append(_dot_nt(qs, ck_ref[0, 0].astype(BF16)))
        values.append(cv_ref[0, 0].astype(BF16))
    o = _softmax_pv(scores, values)
    tq = q.shape[0]
    o = jnp.concatenate([o[g * tq:(g + 1) * tq] for g in range(group)], axis=1)
    o_ref[...] = o * _silu(gate_ref[...])


def _gqa(qg, kg, z, B, L, tq, cache_k=None, cache_v=None, layer_j=0):
    T = z.shape[0]
    D = HEAD_DIM
    group = GQA_HEADS // GQA_KV_HEADS
    nq = L // tq
    cached = cache_k is not None
    in_specs = [pl.BlockSpec((tq, group * D), lambda b, h, i: (b * nq + i, h)),
                pl.BlockSpec((L, D), lambda b, h, i: (b, h)),
                pl.BlockSpec((L, D), lambda b, h, i: (b, OD_GV + h)),
                pl.BlockSpec((tq, group * D), lambda b, h, i: (b * nq + i, OD_GG // group + h))]
    args = [qg, kg, z, z]
    if cached:
        P = cache_k.shape[2]
        cspec = pl.BlockSpec((1, 1, P, D), lambda b, h, i: (b, layer_j, 0, h))
        in_specs += [cspec, cspec]
        args += [cache_k, cache_v]
    return pl.pallas_call(
        functools.partial(_gqa_kernel, cached=cached),
        grid=(B, GQA_KV_HEADS, nq),
        in_specs=in_specs,
        out_specs=pl.BlockSpec((tq, group * D), lambda b, h, i: (b * nq + i, h)),
        out_shape=jax.ShapeDtypeStruct((T, GQA_HEADS * D), F32),
        compiler_params=_cparams(("arbitrary", "arbitrary", "arbitrary")),
        name="gqa_attention",
    )(*args)


def _mla_kernel(qa_ref, qp_ref, ckv_ref, kpe_ref, gate_ref, wv_ref, *rest, cached):
    if cached:
        cckv_ref, ckpe_ref, o_ref = rest
    else:
        (o_ref,) = rest
    D = HEAD_DIM
    R = MLA_ROPE_DIM
    H = MLA_HEADS
    qa = qa_ref[...]
    qp = qp_ref[...]
    tq = qa.shape[0]
    qas = jnp.concatenate([qa[:, h * D:(h + 1) * D] for h in range(H)], axis=0)
    qps = jnp.concatenate([qp[:, h * R:(h + 1) * R] for h in range(H)], axis=0)
    ckv = ckv_ref[...].astype(BF16)
    kpe = kpe_ref[:, :R].astype(BF16)
    scores = [_dot_nt(qas, ckv) + _dot_nt(qps, kpe)]
    values = [ckv]
    if cached:
        cckv = cckv_ref[0, 0].astype(BF16)
        ckpe = ckpe_ref[0, 0].astype(BF16)
        scores.append(_dot_nt(qas, cckv) + _dot_nt(qps, ckpe))
        values.append(cckv)
    o = _softmax_pv(scores, values).astype(BF16)
    gate = _silu(gate_ref[...])
    for h in range(H):
        o_ref[:, h * D:(h + 1) * D] = _dot(o[h * tq:(h + 1) * tq], wv_ref[h]) * gate[:, h * D:(h + 1) * D]


def _mla(qa, qp, ckv, kpe, z, wv, B, L, tq, cache_ckv=None, cache_kpe=None, layer_j=0):
    T = z.shape[0]
    D = HEAD_DIM
    H = MLA_HEADS
    nq = L // tq
    cached = cache_ckv is not None
    in_specs = [pl.BlockSpec((tq, H * D), lambda b, i: (b * nq + i, 0)),
                pl.BlockSpec((tq, H * MLA_ROPE_DIM), lambda b, i: (b * nq + i, 0)),
                pl.BlockSpec((L, D), lambda b, i: (b, 0)),
                pl.BlockSpec((L, D), lambda b, i: (b, 0)),
                pl.BlockSpec((tq, H * D), lambda b, i: (b * nq + i, OD_MG // H)),
                pl.BlockSpec(wv.shape, lambda b, i: (0, 0, 0))]
    args = [qa, qp, ckv, kpe, z, wv]
    if cached:
        P = cache_ckv.shape[2]
        in_specs += [pl.BlockSpec((1, 1, P, D), lambda b, i: (b, layer_j, 0, 0)),
                     pl.BlockSpec((1, 1, P, MLA_ROPE_DIM), lambda b, i: (b, layer_j, 0, 0))]
        args += [cache_ckv, cache_kpe]
    return pl.pallas_call(
        functools.partial(_mla_kernel, cached=cached),
        grid=(B, nq),
        in_specs=in_specs,
        out_specs=pl.BlockSpec((tq, H * D), lambda b, i: (b * nq + i, 0)),
        out_shape=jax.ShapeDtypeStruct((T, H * D), F32),
        compiler_params=_cparams(("arbitrary", "arbitrary")),
        name="mla_attention",
    )(*args)


def _even_weights(even_in_w, gdn_conv_w, gdn_a_log, gdn_dt_bias, hyena_conv_w, hyena_conv_b, j):
    w = even_in_w[j]
    qkv_w = 3 * GDN_HEADS * LANES
    n_ab = 4 * GDN_HEADS
    w_main = jnp.concatenate([w[:, :qkv_w], w[:, qkv_w + n_ab:]], axis=1).astype(BF16)
    w_small = _pad_to(w[:, qkv_w:qkv_w + n_ab], (w.shape[0], LANES)).astype(BF16)
    n_dir = 2 * GDN_HEADS
    alog_row = _pad_to(gdn_a_log[j].reshape(1, n_dir), (1, LANES))
    dtb_row = _pad_to(gdn_dt_bias[j].reshape(1, n_dir), (1, LANES))
    return w_main, w_small, alog_row, dtb_row


def _even_layer(x2, B, L, shift, scale, gate, tiles_per_mod, pre_g, post_g, wts, hy, out_w,
                gdn_conv_w, gdn_norm_g, hy_conv_w, hy_conv_b, hy_skip, state_f, state_b, j):
    w_main, w_small, alog_row, dtb_row = wts
    fwd, inv, hc, hs = hy
    z, zab = _in_proj(x2, shift, scale, pre_g, w_main, w_small, tiles_per_mod)
    oa, s_f, s_b = _gdn(z, zab, gdn_conv_w, gdn_norm_g, alog_row, dtb_row, B, L, state_f, state_b, j)
    ncb = HY_CH // HY_CH_BLOCK
    col = lambda idx: idx * ncb
    z1 = _hy_conv(z, col(4), z, col(5), col(7), hy_conv_w, hy_conv_b, hy_skip, 0, fwd, inv, hc, hs, B, L,
                  first=True, last=False)
    ob = _hy_conv(z1, 0, z, col(6), col(7), hy_conv_w, hy_conv_b, hy_skip, 1, fwd, inv, hc, hs, B, L,
                  first=False, last=True)
    x_new = _out_proj(oa, ob, x2, gate, post_g, out_w, tiles_per_mod)
    return x_new, s_f, s_b


def _odd_weights(odd_in_w, mla_q_up, mla_kv_up, j):
    w = odd_in_w[j]
    D = HEAD_DIM
    o = np.cumsum([0, 4 * D, 2 * D, 2 * D, 4 * D, 2 * D, D, MLA_ROPE_DIM, 4 * D])
    gq, gk, gv, gg, mq, mkv, mpe, mg = [w[:, o[i]:o[i + 1]] for i in range(8)]
    w_main = jnp.concatenate([gq, gg, mg, gk, gv, mq, mkv, _pad_to(mpe, (w.shape[0], D))], axis=1).astype(BF16)
    up = mla_q_up[j].reshape(-1, MLA_HEADS, D + MLA_ROPE_DIM)
    mq_up = jnp.concatenate([up[:, :, :D].reshape(-1, MLA_HEADS * D),
                             up[:, :, D:].reshape(-1, MLA_HEADS * MLA_ROPE_DIM)], axis=1).astype(BF16)
    kv = mla_kv_up[j].reshape(-1, MLA_HEADS, 2 * D)
    wk = kv[:, :, :D].transpose(1, 0, 2).astype(BF16)
    wv = kv[:, :, D:].transpose(1, 0, 2).astype(BF16)
    return w_main, mq_up, wk, wv


def _odd_layer(x2, B, L, shift, scale, gate, tiles_per_mod, pre_g, post_g, wts, out_w,
               q_g, k_g, mq_g, mkv_g, caches, j):
    w_main, mq_up, wk, wv = wts
    (z,) = _in_proj(x2, shift, scale, pre_g, w_main, None, tiles_per_mod)
    rope = caches is not None
    qg, kg, qa, qp, ckv, kpe = _odd_prep(z, q_g, k_g, mq_g, mq_up, mkv_g, wk, L, rope)
    tq = 128
    if rope:
        ck, cv, cckv, ckpe = caches
        og = _gqa(qg, kg, z, B, L, tq, ck, cv, j)
        om = _mla(qa, qp, ckv, kpe, z, wv, B, L, tq, cckv, ckpe, j)
    else:
        og = _gqa(qg, kg, z, B, L, tq)
        om = _mla(qa, qp, ckv, kpe, z, wv, B, L, tq)
    x_new = _out_proj(og, om, x2, gate, post_g, out_w, tiles_per_mod)
    return x_new, (kg, z, ckv)


def kernel(x_prompt, x_sample, state_gdn_fwd, state_gdn_bwd, cache_gqa_k, cache_gqa_v, cache_mla_ckv, cache_mla_kpe, c, c_ctx, mod_w, mod_b, pre_norm_g, post_norm_g, even_in_w, gdn_conv_w, gdn_a_log, gdn_dt_bias, gdn_norm_g, hyena_conv_w, hyena_conv_b, hyena_ffn_w1, hyena_ffn_b1, hyena_ffn_w2, hyena_ffn_b2, hyena_ffn_w3, hyena_sin_freq, hyena_bias, even_out_w, odd_in_w, gqa_q_norm_g, gqa_k_norm_g, mla_q_norm_g, mla_q_up, mla_kv_norm_g, mla_kv_up, odd_out_w):
    Bp, Lp, D = x_prompt.shape
    Bs, Ls, _ = x_sample.shape
    depth = mod_w.shape[0]
    xp = x_prompt.reshape(Bp * Lp, D)
    xs = x_sample.reshape(Bs * Ls, D)

    n_cond = 1 + Bs
    rows = -(-n_cond // 8) * 8
    cond = _pad_to(jnp.concatenate([c_ctx[None, :], c], axis=0), (rows, D))
    mod = _modulation(cond, mod_w, mod_b)

    P = cache_gqa_k.shape[2]
    ck = cache_gqa_k.reshape(Bs, -1, P, GQA_KV_HEADS * HEAD_DIM)
    cv = cache_gqa_v.reshape(Bs, -1, P, GQA_KV_HEADS * HEAD_DIM)

    dft = {L: _dft_blocks(L, min(HY_FREQ_BLOCK, L)) for L in (Lp, Ls)}
    tpm_s = Ls // TOKEN_TILE

    new_f, new_b, new_gk, new_gv, new_ckv, new_kpe = [], [], [], [], [], []
    for i in range(depth):
        j = i // 2
        m = mod[i]
        sh_p, sc_p, gt_p = [m[0:1, k * D:(k + 1) * D].reshape(1, 1, D) for k in range(3)]
        sh_s, sc_s, gt_s = [m[1:n_cond, k * D:(k + 1) * D].reshape(Bs, 1, D) for k in range(3)]
        pre_g = pre_norm_g[i][None, :]
        post_g = post_norm_g[i][None, :]
        if i % 2 == 0:
            wts = _even_weights(even_in_w, gdn_conv_w, gdn_a_log, gdn_dt_bias, hyena_conv_w, hyena_conv_b, j)
            out_w = even_out_w[j].astype(BF16)
            hy = {}
            for L in (Lp, Ls):
                fwd, inv = dft[L]
                filt = _hy_filters(L, hyena_ffn_w1[j], hyena_ffn_b1[j], hyena_ffn_w2[j], hyena_ffn_b2[j],
                                   hyena_ffn_w3[j], hyena_sin_freq[j])
                hc, hs = _hy_spectrum(filt, fwd, L, min(HY_FREQ_BLOCK, L))
                hy[L] = (fwd, inv, hc, hs)
            common = (gdn_conv_w[j], gdn_norm_g[j][None, :], hyena_conv_w[j], hyena_conv_b[j][None, :], hyena_bias[j])
            xp, sf, sb = _even_layer(xp, Bp, Lp, sh_p, sc_p, gt_p, 0, pre_g, post_g, wts, hy[Lp], out_w,
                                     *common, None, None, j)
            xs, _, _ = _even_layer(xs, Bs, Ls, sh_s, sc_s, gt_s, tpm_s, pre_g, post_g, wts, hy[Ls], out_w,
                                   *common, state_gdn_fwd, state_gdn_bwd, j)
            new_f.append(sf)
            new_b.append(sb)
        else:
            wts = _odd_weights(odd_in_w, mla_q_up, mla_kv_up, j)
            out_w = odd_out_w[j].astype(BF16)
            norms = (gqa_q_norm_g[j][None, :], gqa_k_norm_g[j][None, :], mla_q_norm_g[j][None, :],
                     mla_kv_norm_g[j][None, :])
            xp, (kg, zp, ckv) = _odd_layer(xp, Bp, Lp, sh_p, sc_p, gt_p, 0, pre_g, post_g, wts, out_w,
                                           *norms, None, j)
            xs, _ = _odd_layer(xs, Bs, Ls, sh_s, sc_s, gt_s, tpm_s, pre_g, post_g, wts, out_w,
                               *norms, (ck, cv, cache_mla_ckv, cache_mla_kpe), j)
            new_gk.append(kg.reshape(Bp, Lp, GQA_KV_HEADS, HEAD_DIM))
            new_gv.append(zp[:, OD_GV * LANES:(OD_GV + 2) * LANES].reshape(Bp, Lp, GQA_KV_HEADS, HEAD_DIM))
            new_ckv.append(ckv.reshape(Bp, Lp, HEAD_DIM))
            new_kpe.append(zp[:, OD_MPE * LANES:OD_MPE * LANES + MLA_ROPE_DIM].reshape(Bp, Lp, MLA_ROPE_DIM))
    return (xp.reshape(Bp, Lp, D), xs.reshape(Bs, Ls, D),
            jnp.stack(new_f, axis=1), jnp.stack(new_b, axis=1),
            jnp.stack(new_gk, axis=1), jnp.stack(new_gv, axis=1),
            jnp.stack(new_ckv, axis=1), jnp.stack(new_kpe, axis=1))
```

```python
import functools
import math

import numpy as np
import jax
import jax.numpy as jnp
from jax import lax
from jax.experimental import pallas as pl
from jax.experimental.pallas import tpu as pltpu

F32 = jnp.float32
BF16 = jnp.bfloat16

NORM_EPS = 1e-6
ROPE_THETA = 10000.0
GRID_W = 64

GDN_HEADS = 4
GDN_DK = 128
GDN_CHUNK = 64
GDN_CONV_W = 5
GDN_PREP_UNROLL = 4
HY_CH = 512
HY_BANDS = 16
HY_DECAY_TARGET = 1e-2
HY_SHORT_DECAY_PCT = 0.3
HY_LONG_DECAY_PCT = 1.5
GQA_HEADS = 4
GQA_KV_HEADS = 2
HEAD_DIM = 128
MLA_HEADS = 4
MLA_ROPE_DIM = 64

LANES = 128
VMEM_LIMIT = 56 * 1024 * 1024

TOKEN_TILE = 256
HY_FREQ_BLOCK = 256
HY_CH_BLOCK = 256


def _cparams(sem):
    return pltpu.CompilerParams(dimension_semantics=sem, vmem_limit_bytes=VMEM_LIMIT)


def _dot(a, b):
    return jnp.dot(a, b, preferred_element_type=F32)


def _dot_nt(a, b):
    return lax.dot_general(a, b, (((1,), (1,)), ((), ())), preferred_element_type=F32)


def _dot_tn(a, b):
    return lax.dot_general(a, b, (((0,), (0,)), ((), ())), preferred_element_type=F32)


def _split(a):
    hi = a.astype(BF16)
    lo = (a - hi.astype(F32)).astype(BF16)
    return hi, lo


def _dot3(a, b):
    ah, al = _split(a)
    bh, bl = _split(b)
    return _dot(ah, bh) + (_dot(ah, bl) + _dot(al, bh))


def _bdot(a, b):
    return jnp.einsum('gij,gjk->gik', a, b, preferred_element_type=F32)


def _bdot3(a, b):
    ah, al = _split(a)
    bh, bl = _split(b)
    return _bdot(ah, bh) + (_bdot(ah, bl) + _bdot(al, bh))


def _silu(x):
    return x * (1.0 / (1.0 + jnp.exp(-x)))


def _sigmoid(x):
    return 1.0 / (1.0 + jnp.exp(-x))


def _softplus(x):
    return jnp.maximum(x, 0.0) + jnp.log(1.0 + jnp.exp(-jnp.abs(x)))


def _rms(x, g):
    return x * lax.rsqrt(jnp.mean(x * x, axis=-1, keepdims=True) + NORM_EPS) * g


def _shift_rows(x, s):
    L = x.shape[0]
    if s == 0:
        return x
    rolled = pltpu.roll(x, (-s) % L, 0)
    row = lax.broadcasted_iota(jnp.int32, x.shape, 0)
    valid = (row + s >= 0) & (row + s < L)
    return jnp.where(valid, rolled, 0.0)


def _mod_kernel(c_ref, w_ref, b_ref, o_ref):
    c = _silu(c_ref[...])
    o_ref[0] = _dot3(c, w_ref[0]) + b_ref[0]


def _modulation(cond, mod_w, mod_b):
    depth, d, d3 = mod_w.shape
    r = cond.shape[0]
    nb = d3 // d
    return pl.pallas_call(
        _mod_kernel,
        grid=(depth, nb),
        in_specs=[pl.BlockSpec((r, d), lambda i, n: (0, 0)),
                  pl.BlockSpec((1, d, d), lambda i, n: (i, 0, n)),
                  pl.BlockSpec((1, 1, d), lambda i, n: (i, 0, n))],
        out_specs=pl.BlockSpec((1, r, d), lambda i, n: (i, 0, n)),
        out_shape=jax.ShapeDtypeStruct((depth, r, d3), F32),
        compiler_params=_cparams(("arbitrary", "arbitrary")),
        name="adaln_modulation",
    )(cond, mod_w, mod_b.reshape(depth, 1, d3))


def _in_proj_kernel(x_ref, sh_ref, sc_ref, g_ref, w_ref, *rest, n_main, has_small):
    if has_small:
        ws_ref, o_ref, os_ref = rest
    else:
        (o_ref,) = rest
    x = x_ref[...]
    h = _rms(x, g_ref[...]) * (1.0 + sc_ref[0]) + sh_ref[0]
    hb = h.astype(BF16)
    step = 512
    for n0 in range(0, n_main, step):
        o_ref[:, n0:n0 + step] = _dot(hb, w_ref[:, n0:n0 + step])
    if has_small:
        os_ref[...] = _dot(hb, ws_ref[...])


def _in_proj(x2, shift, scale, pre_g, w_main, w_small, tiles_per_mod):
    T, D = x2.shape
    N = w_main.shape[1]
    tm = TOKEN_TILE
    if tiles_per_mod:
        mod_map = lambda i: (i // tiles_per_mod, 0, 0)
    else:
        mod_map = lambda i: (0, 0, 0)
    in_specs = [pl.BlockSpec((tm, D), lambda i: (i, 0)),
                pl.BlockSpec((1, 1, D), mod_map),
                pl.BlockSpec((1, 1, D), mod_map),
                pl.BlockSpec((1, D), lambda i: (0, 0)),
                pl.BlockSpec((D, N), lambda i: (0, 0))]
    out_specs = [pl.BlockSpec((tm, N), lambda i: (i, 0))]
    out_shape = [jax.ShapeDtypeStruct((T, N), F32)]
    args = [x2, shift, scale, pre_g, w_main]
    if w_small is not None:
        in_specs.append(pl.BlockSpec((D, LANES), lambda i: (0, 0)))
        out_specs.append(pl.BlockSpec((tm, LANES), lambda i: (i, 0)))
        out_shape.append(jax.ShapeDtypeStruct((T, LANES), F32))
        args.append(w_small)
    res = pl.pallas_call(
        functools.partial(_in_proj_kernel, n_main=N, has_small=w_small is not None),
        grid=(T // tm,),
        in_specs=in_specs, out_specs=out_specs, out_shape=out_shape,
        compiler_params=_cparams(("arbitrary",)),
        name="in_proj",
    )(*args)
    return res


def _out_proj_kernel(a_ref, b_ref, x_ref, gt_ref, g_ref, w_ref, o_ref, *, half):
    y = _dot(a_ref[...].astype(BF16), w_ref[:half, :]) + _dot(b_ref[...].astype(BF16), w_ref[half:, :])
    o_ref[...] = x_ref[...] + gt_ref[0] * _rms(y, g_ref[...])


def _out_proj(a, b, x2, gate, post_g, w, tiles_per_mod):
    T, D = x2.shape
    half = a.shape[1]
    tm = TOKEN_TILE
    if tiles_per_mod:
        mod_map = lambda i: (i // tiles_per_mod, 0, 0)
    else:
        mod_map = lambda i: (0, 0, 0)
    return pl.pallas_call(
        functools.partial(_out_proj_kernel, half=half),
        grid=(T // tm,),
        in_specs=[pl.BlockSpec((tm, half), lambda i: (i, 0)),
                  pl.BlockSpec((tm, half), lambda i: (i, 0)),
                  pl.BlockSpec((tm, D), lambda i: (i, 0)),
                  pl.BlockSpec((1, 1, D), mod_map),
                  pl.BlockSpec((1, D), lambda i: (0, 0)),
                  pl.BlockSpec((2 * half, D), lambda i: (0, 0))],
        out_specs=pl.BlockSpec((tm, D), lambda i: (i, 0)),
        out_shape=jax.ShapeDtypeStruct((T, D), F32),
        compiler_params=_cparams(("arbitrary",)),
        name="out_proj",
    )(a, b, x2, gate, post_g, w)


def _gdn_kernel(zq_ref, zk_ref, zv_ref, zg_ref, zab_ref, cwq_ref, cwk_ref, cwv_ref, ng_ref,
                alog_ref, dtb_ref, *rest, L, has_state):
    if has_state:
        s0f_ref, s0b_ref = rest[:2]
        rest = rest[2:]
    (o_ref, sf_out_ref, sb_out_ref, q_s, k_s, v_s, g_s, b_s, o_s, st_s,
     u_s, w_s, qg_s, kdt_s, in_s, gl_s) = rest
    C = GDN_CHUNK
    n_chunks = L // C
    U = min(GDN_PREP_UNROLL, n_chunks)
    G = 2 * U
    head = pl.program_id(1)

    def conv_silu(z_ref, w_ref):
        x = z_ref[...]
        half = GDN_CONV_W // 2
        acc = x * w_ref[half:half + 1, :]
        for i in range(GDN_CONV_W):
            if i != half:
                acc = acc + _shift_rows(x, i - half) * w_ref[i:i + 1, :]
        return _silu(acc)

    def l2n(x):
        return x * lax.rsqrt(jnp.sum(x * x, axis=-1, keepdims=True) + NORM_EPS)

    q_s[...] = l2n(conv_silu(zq_ref, cwq_ref)) * (GDN_DK ** -0.5)
    k_s[...] = l2n(conv_silu(zk_ref, cwk_ref))
    v_s[...] = conv_silu(zv_ref, cwv_ref)
    zab = zab_ref[...]
    g_s[...] = -jnp.exp(alog_ref[...]) * _softplus(zab + dtb_ref[...])
    b_s[...] = _sigmoid(zab)
    if has_state:
        st_s[0] = s0f_ref[0, 0, 0]
        st_s[1] = s0b_ref[0, 0, 0]
    else:
        st_s[...] = jnp.zeros_like(st_s)

    def iota(shape, axis):
        return lax.broadcasted_iota(jnp.int32, shape, axis)

    def direction(shape):
        return iota(shape, 0) & 1

    sq = (G, C, C)
    row = iota(sq, 1)
    col = iota(sq, 2)
    signed = (row - col) * (1 - 2 * direction(sq))
    incl = signed >= 0
    strict = signed > 0
    eye = (row == col).astype(F32)
    same = [(row >> s) == (col >> s) for s in (3, 4, 5)]
    off_blocks = [same[1] & jnp.logical_not(same[0]), same[2] & jnp.logical_not(same[1]),
                  jnp.logical_not(same[2])]
    wide = (G, C, LANES)
    sel_lane = direction(wide) * GDN_HEADS + head
    mask_g = iota(wide, 2) == sel_lane
    mask_b = iota(wide, 2) == sel_lane + 2 * GDN_HEADS
    tall = (G, LANES, C)
    mask_t = iota(tall, 1) == direction(tall) * GDN_HEADS + head
    colv = (G, C, 1)
    mask_last = iota(colv, 1) == (1 - direction(colv)) * (C - 1)
    r2 = lax.broadcasted_iota(jnp.int32, (C, C), 0)
    c2 = lax.broadcasted_iota(jnp.int32, (C, C), 1)
    tri = jnp.concatenate([(r2 >= c2).astype(BF16), (r2 <= c2).astype(BF16)], axis=0)

    def both_dirs(x):
        return jnp.broadcast_to(x[:, None], (U, 2) + x.shape[1:]).reshape((G,) + x.shape[1:])

    def prep_group(c, carry):
        rows = pl.ds(pl.multiple_of(c * (U * C), U * C), U * C)
        q = q_s[rows, :].reshape(U, C, LANES)
        k = k_s[rows, :].reshape(U, C, LANES)
        v = v_s[rows, :].reshape(U, C, LANES)
        ball = b_s[rows, :].reshape(U, C, LANES)
        g_hi, g_lo = _split(g_s[rows, :])
        g_hi = g_hi.reshape(U, C, LANES)
        g_lo = g_lo.reshape(U, C, LANES)
        gcum = jnp.stack([_dot(tri, g_hi[u]) + _dot(tri, g_lo[u]) for u in range(U)])
        gcum = gcum.reshape(G, C, LANES)
        gcum_t = jnp.stack([gcum[g].T for g in range(G)])
        gc = jnp.sum(jnp.where(mask_g, gcum, 0.0), axis=2, keepdims=True)
        gr = jnp.sum(jnp.where(mask_t, gcum_t, 0.0), axis=1, keepdims=True)
        beta = jnp.sum(jnp.where(mask_b, both_dirs(ball), 0.0), axis=2, keepdims=True)
        kbf = k.astype(BF16)
        kq = jnp.einsum('uik,ujk->uij', jnp.concatenate([k, q], axis=1).astype(BF16), kbf,
                        preferred_element_type=F32)
        kk = both_dirs(kq[:, :C])
        qk = both_dirs(kq[:, C:])
        decay = jnp.where(incl, jnp.exp(jnp.where(incl, gc - gr, 0.0)), 0.0)
        lmat = jnp.where(strict, beta * kk * decay, 0.0)
        p = jnp.where(same[0], -lmat, 0.0)
        tmat = eye + p
        for _ in range(2):
            p = _bdot3(p, p)
            tmat = tmat + _bdot3(tmat, p)
        for off in off_blocks:
            tmat = tmat - _bdot3(_bdot3(tmat, jnp.where(off, lmat, 0.0)), tmat)
        eg = jnp.exp(gc)
        k2 = both_dirs(k)
        kb = k2 * beta
        sol = _bdot3(tmat, jnp.concatenate([both_dirs(v) * beta, kb * eg], axis=2))
        glast = jnp.sum(jnp.where(mask_last, gc, 0.0), axis=1, keepdims=True)
        kdec = k2 * jnp.exp(glast - gc)
        kdec_t = jnp.stack([kdec[g].T for g in range(G)]).astype(BF16)
        out_rows = pl.ds(pl.multiple_of(c * (G * C), G * C), G * C)
        u_s[out_rows, :] = sol[:, :, :LANES].reshape(G * C, LANES)
        w_s[out_rows, :] = sol[:, :, LANES:].astype(BF16).reshape(G * C, LANES)
        qg_s[out_rows, :] = (both_dirs(q) * eg).astype(BF16).reshape(G * C, LANES)
        in_s[out_rows, :] = (qk * decay).astype(BF16).reshape(G * C, C)
        kdt_s[pl.ds(pl.multiple_of(c * (G * LANES), G * LANES), G * LANES), :] = kdec_t.reshape(G * LANES, C)
        gl_s[pl.ds(pl.multiple_of(c * (G * 8), G * 8), G * 8), :] = (
            jnp.broadcast_to(jnp.exp(glast), (G, 8, LANES)).reshape(G * 8, LANES))
        return carry

    lax.fori_loop(0, n_chunks // U, prep_group, 0)

    def scan_step(c, carry):
        slot_f = c * 2
        slot_b = (n_chunks - 1 - c) * 2 + 1

        def ld(ref, size):
            return jnp.stack([ref[pl.ds(pl.multiple_of(slot_f * size, size), size), :],
                              ref[pl.ds(pl.multiple_of(slot_b * size, size), size), :]])

        s = st_s[...]
        sb16 = s.astype(BF16)
        v_new = ld(u_s, C) - _bdot(ld(w_s, C), sb16)
        vb16 = v_new.astype(BF16)
        o = _bdot(ld(qg_s, C), sb16) + _bdot(ld(in_s, C), vb16)
        st_s[...] = s * ld(gl_s, 8)[:, 0:1, :] + _bdot(ld(kdt_s, LANES), vb16)
        o_s[pl.ds(pl.multiple_of(c * (2 * C), 2 * C), 2 * C), :] = o.reshape(2 * C, LANES)
        return carry

    lax.fori_loop(0, n_chunks, scan_step, 0)

    def finish(c, carry):
        of = o_s[pl.ds(pl.multiple_of(c * (2 * C), C), C), :]
        ob = o_s[pl.ds(pl.multiple_of((n_chunks - 1 - c) * (2 * C) + C, C), C), :]
        rows = pl.ds(pl.multiple_of(c * C, C), C)
        o_ref[rows, :] = _rms(of + ob, ng_ref[...]) * _silu(zg_ref[rows, :])
        return carry

    lax.fori_loop(0, n_chunks, finish, 0)
    sf_out_ref[0, 0] = st_s[0]
    sb_out_ref[0, 0] = st_s[1]


def _gdn(z, zab, conv_w, norm_g, alog_row, dtb_row, B, L, state_f=None, state_b=None, layer_j=0):
    T = z.shape[0]
    H = GDN_HEADS
    has_state = state_f is not None
    blk = lambda off: pl.BlockSpec((L, LANES), lambda b, h: (b, off + h))
    cw = lambda off: pl.BlockSpec((GDN_CONV_W, LANES), lambda b, h: (0, off + h))
    in_specs = [blk(0), blk(H), blk(2 * H), blk(3 * H),
                pl.BlockSpec((L, LANES), lambda b, h: (b, 0)),
                cw(0), cw(H), cw(2 * H),
                pl.BlockSpec((1, LANES), lambda b, h: (0, 0)),
                pl.BlockSpec((1, LANES), lambda b, h: (0, 0)),
                pl.BlockSpec((1, LANES), lambda b, h: (0, 0))]
    args = [z, z, z, z, zab, conv_w, conv_w, conv_w, norm_g, alog_row, dtb_row]
    if has_state:
        st = pl.BlockSpec((1, 1, 1, GDN_DK, LANES), lambda b, h: (b, layer_j, h, 0, 0))
        in_specs += [st, st]
        args += [state_f, state_b]
    sout = pl.BlockSpec((1, 1, GDN_DK, LANES), lambda b, h: (b, h, 0, 0))
    n_slots = 2 * (L // GDN_CHUNK)
    scratch = ([pltpu.VMEM((L, LANES), F32) for _ in range(5)]
               + [pltpu.VMEM((2 * L, LANES), F32), pltpu.VMEM((2, GDN_DK, LANES), F32)]
               + [pltpu.VMEM((2 * L, LANES), F32)]
               + [pltpu.VMEM((2 * L, LANES), BF16)] * 2
               + [pltpu.VMEM((n_slots * LANES, GDN_CHUNK), BF16)]
               + [pltpu.VMEM((2 * L, GDN_CHUNK), BF16)]
               + [pltpu.VMEM((n_slots * 8, LANES), F32)])
    return pl.pallas_call(
        functools.partial(_gdn_kernel, L=L, has_state=has_state),
        grid=(B, H),
        in_specs=in_specs,
        out_specs=[pl.BlockSpec((L, LANES), lambda b, h: (b, h)), sout, sout],
        out_shape=[jax.ShapeDtypeStruct((T, H * LANES), F32),
                   jax.ShapeDtypeStruct((B, H, GDN_DK, LANES), F32),
                   jax.ShapeDtypeStruct((B, H, GDN_DK, LANES), F32)],
        scratch_shapes=scratch,
        compiler_params=_cparams(("arbitrary", "arbitrary")),
        name="gdn",
    )(*args)


def _hy_filter_kernel(f_ref, w1_ref, b1_ref, sf0_ref, w2_ref, b2_ref, sf1_ref, w3_ref, dl_ref, o_ref, *, L):
    rowi = lax.broadcasted_iota(jnp.int32, (L, LANES), 0).astype(F32)
    lane = lax.broadcasted_iota(jnp.int32, (L, LANES), 1)
    t = rowi * (1.0 / (L - 1))
    wpos = rowi * (2.0 * math.pi / L)
    ang = f_ref[...] * wpos
    z = jnp.where(lane == 0, t,
                  jnp.where(lane <= HY_BANDS, jnp.cos(ang),
                            jnp.where(lane <= 2 * HY_BANDS, -jnp.sin(ang), 0.0)))
    h = jnp.sin(sf0_ref[...] * (_dot3(z, w1_ref[...]) + b1_ref[...]))
    h = jnp.sin(sf1_ref[...] * (_dot3(h, w2_ref[...]) + b2_ref[...]))
    tc = lax.broadcasted_iota(jnp.int32, (L, HY_CH), 0).astype(F32) * (1.0 / (L - 1))
    window = jnp.exp(-tc * dl_ref[...])
    for j in range(4):
        o_ref[:, j * HY_CH:(j + 1) * HY_CH] = _dot3(h, w3_ref[:, j * HY_CH:(j + 1) * HY_CH]) * window


def _pad_to(a, shape):
    return jnp.pad(a, [(0, s - d) for s, d in zip(shape, a.shape)])


def _hy_filters(L, w1, b1, w2, b2, w3, sin_freq):
    fvals = np.linspace(1e-4, HY_BANDS - 1, HY_BANDS, dtype=np.float32)
    frow = np.zeros((1, LANES), np.float32)
    frow[0, 1:1 + HY_BANDS] = fvals
    frow[0, 1 + HY_BANDS:1 + 2 * HY_BANDS] = fvals
    deltas = np.abs(np.linspace(math.log(HY_DECAY_TARGET) / HY_LONG_DECAY_PCT,
                                math.log(HY_DECAY_TARGET) / HY_SHORT_DECAY_PCT, HY_CH, dtype=np.float32))
    args = [jnp.asarray(frow),
            _pad_to(w1, (LANES, LANES)), _pad_to(b1[None, :], (1, LANES)), _pad_to(sin_freq[0][None, :], (1, LANES)),
            _pad_to(w2, (LANES, LANES)), _pad_to(b2[None, :], (1, LANES)), _pad_to(sin_freq[1][None, :], (1, LANES)),
            _pad_to(w3, (LANES, 4 * HY_CH)), jnp.asarray(deltas[None, :])]
    return pl.pallas_call(
        functools.partial(_hy_filter_kernel, L=L),
        out_shape=jax.ShapeDtypeStruct((L, 4 * HY_CH), F32),
        compiler_params=pltpu.CompilerParams(vmem_limit_bytes=VMEM_LIMIT),
        name="hyena_filters",
    )(*args)


def _dft_tables(L):
    N = 2 * L
    k = np.arange(L, dtype=np.int64)[:, None]
    s = np.arange(L, dtype=np.int64)[None, :]
    ang = ((2 * k + 1) * s % (2 * N)).astype(np.float64) * (2.0 * math.pi / (2 * N))
    return np.cos(ang).astype(np.float32), np.sin(ang).astype(np.float32)


def _dft_blocks(L, kb):
    cm, sm = _dft_tables(L)
    nk = L // kb
    fwd = np.concatenate([cm.reshape(nk, kb, L), sm.reshape(nk, kb, L)], axis=1)
    inv = np.concatenate([cm.T.reshape(L, nk, kb), sm.T.reshape(L, nk, kb)], axis=2)
    inv = np.ascontiguousarray(inv.transpose(1, 0, 2))
    return jnp.asarray(fwd).astype(BF16), jnp.asarray(inv).astype(BF16)


def _hy_spectrum_kernel(f_ref, flt_ref, hc_ref, hs_ref, *, L, kb):
    row = lax.broadcasted_iota(jnp.int32, (L, HY_CH), 0)
    scale = 1.0 / L
    for o in range(2):
        hf = flt_ref[:, (2 * o) * HY_CH:(2 * o + 1) * HY_CH]
        hb = jnp.where(row == 0, 0.0, flt_ref[:, (2 * o + 1) * HY_CH:(2 * o + 2) * HY_CH])
        a_hi, a_lo = _split(hf + hb)
        d_hi, d_lo = _split(hf - hb)
        fc = f_ref[0, :kb, :]
        fs = f_ref[0, kb:, :]
        hc_ref[o] = (_dot(fc, a_hi) + _dot(fc, a_lo)) * scale
        hs_ref[o] = (_dot(fs, d_hi) + _dot(fs, d_lo)) * scale


def _hy_spectrum(filt, fwd, L, kb):
    nk = L // kb
    return pl.pallas_call(
        functools.partial(_hy_spectrum_kernel, L=L, kb=kb),
        grid=(nk,),
        in_specs=[pl.BlockSpec((1, 2 * kb, L), lambda i: (i, 0, 0)),
                  pl.BlockSpec((L, 4 * HY_CH), lambda i: (0, 0))],
        out_specs=[pl.BlockSpec((2, kb, HY_CH), lambda i: (0, i, 0)),
                   pl.BlockSpec((2, kb, HY_CH), lambda i: (0, i, 0))],
        out_shape=[jax.ShapeDtypeStruct((2, L, HY_CH), F32)] * 2,
        compiler_params=_cparams(("arbitrary",)),
        name="hyena_spectrum",
    )(fwd, filt)


def _hy_conv_kernel(zin_ref, xo_ref, gate_ref, cwz_ref, cbz_ref, cwx_ref, cbx_ref, skip_ref,
                    f_ref, g_ref, hc_ref, hs_ref, o_ref, zb_s, acc_s, *, kb, first, last):
    kstep = pl.program_id(2)
    nk = pl.num_programs(2)

    def conv3(x_ref, w_ref, b_ref):
        x = x_ref[...]
        return (_shift_rows(x, -1) * w_ref[0:1, :] + x * w_ref[1:2, :]
                + _shift_rows(x, 1) * w_ref[2:3, :]) + b_ref[...]

    def z_value():
        return conv3(zin_ref, cwz_ref, cbz_ref) if first else zin_ref[...]

    @pl.when(kstep == 0)
    def _():
        zb_s[...] = z_value().astype(BF16)
        acc_s[...] = jnp.zeros_like(acc_s)

    x = _dot(f_ref[0], zb_s[...])
    xc = x[:kb]
    xs = x[kb:]
    hc = hc_ref[0]
    hs = hs_ref[0]
    y = jnp.concatenate([xc * hc - xs * hs, xc * hs + xs * hc], axis=0).astype(BF16)
    acc_s[...] += _dot(g_ref[0], y)

    @pl.when(kstep == nk - 1)
    def _():
        xo = conv3(xo_ref, cwx_ref, cbx_ref)
        res = xo * (acc_s[...] + z_value() * skip_ref[...])
        if last:
            res = res * _silu(gate_ref[...])
        o_ref[...] = res


def _hy_conv(zsrc, zcol, z, xcol, gcol, conv_w, conv_b, skip, order, fwd, inv, hc, hs, B, L, first, last):
    T = z.shape[0]
    cb = HY_CH_BLOCK
    ncb = HY_CH // cb
    nk, _, kb2 = inv.shape
    kb = kb2 // 2
    data = lambda off: pl.BlockSpec((L, cb), lambda b, c, k: (b, off + c))
    cwspec = lambda off: pl.BlockSpec((3, cb), lambda b, c, k: (0, off + c))
    cbspec = lambda off: pl.BlockSpec((1, cb), lambda b, c, k: (0, off + c))
    in_specs = [data(zcol), data(xcol), data(gcol),
                cwspec(0), cbspec(0), cwspec((1 + order) * ncb), cbspec((1 + order) * ncb),
                pl.BlockSpec((1, cb), lambda b, c, k: (0, c)),
                pl.BlockSpec((1, 2 * kb, L), lambda b, c, k: (k, 0, 0)),
                pl.BlockSpec((1, L, 2 * kb), lambda b, c, k: (k, 0, 0)),
                pl.BlockSpec((1, kb, cb), lambda b, c, k: (order, k, c)),
                pl.BlockSpec((1, kb, cb), lambda b, c, k: (order, k, c))]
    return pl.pallas_call(
        functools.partial(_hy_conv_kernel, kb=kb, first=first, last=last),
        grid=(B, ncb, nk),
        in_specs=in_specs,
        out_specs=pl.BlockSpec((L, cb), lambda b, c, k: (b, c)),
        out_shape=jax.ShapeDtypeStruct((T, HY_CH), F32),
        scratch_shapes=[pltpu.VMEM((L, cb), BF16), pltpu.VMEM((L, cb), F32)],
        compiler_params=_cparams(("arbitrary", "arbitrary", "arbitrary")),
        name="hyena_conv",
    )(zsrc, z, z, conv_w, conv_b, conv_w, conv_b, skip[order:order + 1], fwd, inv, hc, hs)


OD_GQ, OD_GG, OD_MG, OD_GK, OD_GV, OD_MQ, OD_MKV, OD_MPE = 0, 4, 8, 12, 14, 16, 18, 19
OD_WIDTH = 20 * LANES


def _rope_swap(x, quarter):
    n = x.shape[1]
    lane = lax.broadcasted_iota(jnp.int32, x.shape, 1)
    first = (lane & (2 * quarter - 1)) < quarter
    return jnp.where(first, pltpu.roll(x, n - quarter, 1), pltpu.roll(x, quarter, 1))


def _odd_prep_kernel(z_ref, qg_ref, kg_ref, mqg_ref, mqup_ref, mkvg_ref, wk_ref, *rest, rope):
    if rope:
        cg_ref, sg_ref, cm_ref, sm_ref = rest[:4]
        rest = rest[4:]
    q_out, k_out, qa_out, qp_out, ckv_out, kpe_out = rest
    D = HEAD_DIM

    def rot(x, c, s, quarter):
        return x * c + _rope_swap(x, quarter) * s if rope else x

    for h in range(GQA_HEADS):
        q = _rms(z_ref[:, (OD_GQ + h) * D:(OD_GQ + h + 1) * D], qg_ref[...])
        if rope:
            q = rot(q, cg_ref[...], sg_ref[...], D // 4)
        q_out[:, h * D:(h + 1) * D] = (q * (D ** -0.5)).astype(BF16)
    for h in range(GQA_KV_HEADS):
        k = _rms(z_ref[:, (OD_GK + h) * D:(OD_GK + h + 1) * D], kg_ref[...])
        if rope:
            k = rot(k, cg_ref[...], sg_ref[...], D // 4)
        k_out[:, h * D:(h + 1) * D] = k
    mq = _rms(z_ref[:, OD_MQ * D:(OD_MQ + 2) * D], mqg_ref[...]).astype(BF16)
    qm = _dot(mq, mqup_ref[...])
    mscale = (D + MLA_ROPE_DIM) ** -0.5
    for h in range(MLA_HEADS):
        qn = qm[:, h * D:(h + 1) * D].astype(BF16)
        qa_out[:, h * D:(h + 1) * D] = (_dot_nt(qn, wk_ref[h]) * mscale).astype(BF16)
    qpe = qm[:, MLA_HEADS * D:]
    if rope:
        qpe = rot(qpe, cm_ref[...], sm_ref[...], MLA_ROPE_DIM // 4)
    qp_out[...] = (qpe * mscale).astype(BF16)
    ckv_out[...] = _rms(z_ref[:, OD_MKV * D:(OD_MKV + 1) * D], mkvg_ref[...])
    kpe = z_ref[:, OD_MPE * D:(OD_MPE + 1) * D]
    if rope:
        kpe = rot(kpe, cm_ref[:, :D], sm_ref[:, :D], MLA_ROPE_DIM // 4)
    kpe_out[...] = kpe


def _rope_tables(L, R):
    rows = np.arange(L) // GRID_W
    cols = np.arange(L) % GRID_W
    quarter = R // 4
    inv = ROPE_THETA ** (-np.arange(quarter, dtype=np.float32) * 2.0 / (R // 2))
    a_r = rows[:, None].astype(np.float32) * inv[None, :]
    a_c = cols[:, None].astype(np.float32) * inv[None, :]
    cos = np.concatenate([np.cos(a_r), np.cos(a_r), np.cos(a_c), np.cos(a_c)], axis=1)
    sin = np.concatenate([-np.sin(a_r), np.sin(a_r), -np.sin(a_c), np.sin(a_c)], axis=1)
    return cos.astype(np.float32), sin.astype(np.float32)


def _odd_prep(z, q_g, k_g, mq_g, mq_up, mkv_g, wk, L, rope):
    T = z.shape[0]
    tm = TOKEN_TILE
    D = HEAD_DIM
    full = lambda shape: pl.BlockSpec(shape, lambda i: tuple(0 for _ in shape))
    in_specs = [pl.BlockSpec((tm, OD_WIDTH), lambda i: (i, 0)),
                full((1, D)), full((1, D)), full((1, 2 * D)), full(mq_up.shape), full((1, D)), full(wk.shape)]
    args = [z, q_g, k_g, mq_g, mq_up, mkv_g, wk]
    if rope:
        cg, sg = _rope_tables(L, D)
        cm, sm = _rope_tables(L, MLA_ROPE_DIM)
        cm, sm = np.tile(cm, (1, MLA_HEADS)), np.tile(sm, (1, MLA_HEADS))
        per = L // tm
        pos = lambda w: pl.BlockSpec((tm, w), lambda i: (i % per, 0))
        in_specs += [pos(D), pos(D), pos(2 * D), pos(2 * D)]
        args += [jnp.asarray(cg), jnp.asarray(sg), jnp.asarray(cm), jnp.asarray(sm)]
    tile = lambda w: pl.BlockSpec((tm, w), lambda i: (i, 0))
    return pl.pallas_call(
        functools.partial(_odd_prep_kernel, rope=rope),
        grid=(T // tm,),
        in_specs=in_specs,
        out_specs=[tile(4 * D), tile(2 * D), tile(4 * D), tile(2 * D), tile(D), tile(D)],
        out_shape=[jax.ShapeDtypeStruct((T, 4 * D), BF16), jax.ShapeDtypeStruct((T, 2 * D), F32),
                   jax.ShapeDtypeStruct((T, 4 * D), BF16), jax.ShapeDtypeStruct((T, 2 * D), BF16),
                   jax.ShapeDtypeStruct((T, D), F32), jax.ShapeDtypeStruct((T, D), F32)],
        compiler_params=_cparams(("arbitrary",)),
        name="odd_prep",
    )(*args)


def _softmax_pv(score_blocks, value_blocks):
    m = score_blocks[0].max(axis=-1, keepdims=True)
    for s in score_blocks[1:]:
        m = jnp.maximum(m, s.max(axis=-1, keepdims=True))
    acc = None
    den = None
    for s, v in zip(score_blocks, value_blocks):
        p = jnp.exp(s - m)
        d = p.sum(axis=-1, keepdims=True)
        a = _dot(p.astype(BF16), v)
        acc = a if acc is None else acc + a
        den = d if den is None else den + d
    return acc / den


def _gqa_kernel(q_ref, k_ref, v_ref, gate_ref, *rest, cached):
    if cached:
        ck_ref, cv_ref, o_ref = rest
    else:
        (o_ref,) = rest
    D = HEAD_DIM
    group = GQA_HEADS // GQA_KV_HEADS
    q = q_ref[...]
    qs = jnp.concatenate([q[:, g * D:(g + 1) * D] for g in range(group)], axis=0)
    scores = [_dot_nt(qs, k_ref[...].astype(BF16))]
    values = [v_ref[...].astype(BF16)]
    if cached:
        scores.append(_dot_nt(qs, ck_ref[0, 0].astype(BF16)))
        values.append(cv_ref[0, 0].astype(BF16))
    o = _softmax_pv(scores, values)
    tq = q.shape[0]
    o = jnp.concatenate([o[g * tq:(g + 1) * tq] for g in range(group)], axis=1)
    o_ref[...] = o * _silu(gate_ref[...])


def _gqa(qg, kg, z, B, L, tq, cache_k=None, cache_v=None, layer_j=0):
    T = z.shape[0]
    D = HEAD_DIM
    group = GQA_HEADS // GQA_KV_HEADS
    nq = L // tq
    cached = cache_k is not None
    in_specs = [pl.BlockSpec((tq, group * D), lambda b, h, i: (b * nq + i, h)),
                pl.BlockSpec((L, D), lambda b, h, i: (b, h)),
                pl.BlockSpec((L, D), lambda b, h, i: (b, OD_GV + h)),
                pl.BlockSpec((tq, group * D), lambda b, h, i: (b * nq + i, OD_GG // group + h))]
    args = [qg, kg, z, z]
    if cached:
        P = cache_k.shape[2]
        cspec = pl.BlockSpec((1, 1, P, D), lambda b, h, i: (b, layer_j, 0, h))
        in_specs += [cspec, cspec]
        args += [cache_k, cache_v]
    return pl.pallas_call(
        functools.partial(_gqa_kernel, cached=cached),
        grid=(B, GQA_KV_HEADS, nq),
        in_specs=in_specs,
        out_specs=pl.BlockSpec((tq, group * D), lambda b, h, i: (b * nq + i, h)),
        out_shape=jax.ShapeDtypeStruct((T, GQA_HEADS * D), F32),
        compiler_params=_cparams(("arbitrary", "arbitrary", "arbitrary")),
        name="gqa_attention",
    )(*args)


def _mla_kernel(qa_ref, qp_ref, ckv_ref, kpe_ref, gate_ref, wv_ref, *rest, cached):
    if cached:
        cckv_ref, ckpe_ref, o_ref = rest
    else:
        (o_ref,) = rest
    D = HEAD_DIM
    R = MLA_ROPE_DIM
    H = MLA_HEADS
    qa = qa_ref[...]
    qp = qp_ref[...]
    tq = qa.shape[0]
    qas = jnp.concatenate([qa[:, h * D:(h + 1) * D] for h in range(H)], axis=0)
    qps = jnp.concatenate([qp[:, h * R:(h + 1) * R] for h in range(H)], axis=0)
    ckv = ckv_ref[...].astype(BF16)
    kpe = kpe_ref[:, :R].astype(BF16)
    scores = [_dot_nt(qas, ckv) + _dot_nt(qps, kpe)]
    values = [ckv]
    if cached:
        cckv = cckv_ref[0, 0].astype(BF16)
        ckpe = ckpe_ref[0, 0].astype(BF16)
        scores.append(_dot_nt(qas, cckv) + _dot_nt(qps, ckpe))
        values.append(cckv)
    o = _softmax_pv(scores, values).astype(BF16)
    gate = _silu(gate_ref[...])
    for h in range(H):
        o_ref[:, h * D:(h + 1) * D] = _dot(o[h * tq:(h + 1) * tq], wv_ref[h]) * gate[:, h * D:(h + 1) * D]


def _mla(qa, qp, ckv, kpe, z, wv, B, L, tq, cache_ckv=None, cache_kpe=None, layer_j=0):
    T = z.shape[0]
    D = HEAD_DIM
    H = MLA_HEADS
    nq = L // tq
    cached = cache_ckv is not None
    in_specs = [pl.BlockSpec((tq, H * D), lambda b, i: (b * nq + i, 0)),
                pl.BlockSpec((tq, H * MLA_ROPE_DIM), lambda b, i: (b * nq + i, 0)),
                pl.BlockSpec((L, D), lambda b, i: (b, 0)),
                pl.BlockSpec((L, D), lambda b, i: (b, 0)),
                pl.BlockSpec((tq, H * D), lambda b, i: (b * nq + i, OD_MG // H)),
                pl.BlockSpec(wv.shape, lambda b, i: (0, 0, 0))]
    args = [qa, qp, ckv, kpe, z, wv]
    if cached:
        P = cache_ckv.shape[2]
        in_specs += [pl.BlockSpec((1, 1, P, D), lambda b, i: (b, layer_j, 0, 0)),
                     pl.BlockSpec((1, 1, P, MLA_ROPE_DIM), lambda b, i: (b, layer_j, 0, 0))]
        args += [cache_ckv, cache_kpe]
    return pl.pallas_call(
        functools.partial(_mla_kernel, cached=cached),
        grid=(B, nq),
        in_specs=in_specs,
        out_specs=pl.BlockSpec((tq, H * D), lambda b, i: (b * nq + i, 0)),
        out_shape=jax.ShapeDtypeStruct((T, H * D), F32),
        compiler_params=_cparams(("arbitrary", "arbitrary")),
        name="mla_attention",
    )(*args)


def _even_weights(even_in_w, gdn_conv_w, gdn_a_log, gdn_dt_bias, hyena_conv_w, hyena_conv_b, j):
    w = even_in_w[j]
    qkv_w = 3 * GDN_HEADS * LANES
    n_ab = 4 * GDN_HEADS
    w_main = jnp.concatenate([w[:, :qkv_w], w[:, qkv_w + n_ab:]], axis=1).astype(BF16)
    w_small = _pad_to(w[:, qkv_w:qkv_w + n_ab], (w.shape[0], LANES)).astype(BF16)
    n_dir = 2 * GDN_HEADS
    alog_row = _pad_to(gdn_a_log[j].reshape(1, n_dir), (1, LANES))
    dtb_row = _pad_to(gdn_dt_bias[j].reshape(1, n_dir), (1, LANES))
    return w_main, w_small, alog_row, dtb_row


def _even_layer(x2, B, L, shift, scale, gate, tiles_per_mod, pre_g, post_g, wts, hy, out_w,
                gdn_conv_w, gdn_norm_g, hy_conv_w, hy_conv_b, hy_skip, state_f, state_b, j):
    w_main, w_small, alog_row, dtb_row = wts
    fwd, inv, hc, hs = hy
    z, zab = _in_proj(x2, shift, scale, pre_g, w_main, w_small, tiles_per_mod)
    oa, s_f, s_b = _gdn(z, zab, gdn_conv_w, gdn_norm_g, alog_row, dtb_row, B, L, state_f, state_b, j)
    ncb = HY_CH // HY_CH_BLOCK
    col = lambda idx: idx * ncb
    z1 = _hy_conv(z, col(4), z, col(5), col(7), hy_conv_w, hy_conv_b, hy_skip, 0, fwd, inv, hc, hs, B, L,
                  first=True, last=False)
    ob = _hy_conv(z1, 0, z, col(6), col(7), hy_conv_w, hy_conv_b, hy_skip, 1, fwd, inv, hc, hs, B, L,
                  first=False, last=True)
    x_new = _out_proj(oa, ob, x2, gate, post_g, out_w, tiles_per_mod)
    return x_new, s_f, s_b


def _odd_weights(odd_in_w, mla_q_up, mla_kv_up, j):
    w = odd_in_w[j]
    D = HEAD_DIM
    o = np.cumsum([0, 4 * D, 2 * D, 2 * D, 4 * D, 2 * D, D, MLA_ROPE_DIM, 4 * D])
    gq, gk, gv, gg, mq, mkv, mpe, mg = [w[:, o[i]:o[i + 1]] for i in range(8)]
    w_main = jnp.concatenate([gq, gg, mg, gk, gv, mq, mkv, _pad_to(mpe, (w.shape[0], D))], axis=1).astype(BF16)
    up = mla_q_up[j].reshape(-1, MLA_HEADS, D + MLA_ROPE_DIM)
    mq_up = jnp.concatenate([up[:, :, :D].reshape(-1, MLA_HEADS * D),
                             up[:, :, D:].reshape(-1, MLA_HEADS * MLA_ROPE_DIM)], axis=1).astype(BF16)
    kv = mla_kv_up[j].reshape(-1, MLA_HEADS, 2 * D)
    wk = kv[:, :, :D].transpose(1, 0, 2).astype(BF16)
    wv = kv[:, :, D:].transpose(1, 0, 2).astype(BF16)
    return w_main, mq_up, wk, wv


def _odd_layer(x2, B, L, shift, scale, gate, tiles_per_mod, pre_g, post_g, wts, out_w,
               q_g, k_g, mq_g, mkv_g, caches, j):
    w_main, mq_up, wk, wv = wts
    (z,) = _in_proj(x2, shift, scale, pre_g, w_main, None, tiles_per_mod)
    rope = caches is not None
    qg, kg, qa, qp, ckv, kpe = _odd_prep(z, q_g, k_g, mq_g, mq_up, mkv_g, wk, L, rope)
    tq = 128
    if rope:
        ck, cv, cckv, ckpe = caches
        og = _gqa(qg, kg, z, B, L, tq, ck, cv, j)
        om = _mla(qa, qp, ckv, kpe, z, wv, B, L, tq, cckv, ckpe, j)
    else:
        og = _gqa(qg, kg, z, B, L, tq)
        om = _mla(qa, qp, ckv, kpe, z, wv, B, L, tq)
    x_new = _out_proj(og, om, x2, gate, post_g, out_w, tiles_per_mod)
    return x_new, (kg, z, ckv)


def kernel(x_prompt, x_sample, state_gdn_fwd, state_gdn_bwd, cache_gqa_k, cache_gqa_v, cache_mla_ckv, cache_mla_kpe, c, c_ctx, mod_w, mod_b, pre_norm_g, post_norm_g, even_in_w, gdn_conv_w, gdn_a_log, gdn_dt_bias, gdn_norm_g, hyena_conv_w, hyena_conv_b, hyena_ffn_w1, hyena_ffn_b1, hyena_ffn_w2, hyena_ffn_b2, hyena_ffn_w3, hyena_sin_freq, hyena_bias, even_out_w, odd_in_w, gqa_q_norm_g, gqa_k_norm_g, mla_q_norm_g, mla_q_up, mla_kv_norm_g, mla_kv_up, odd_out_w):
    Bp, Lp, D = x_prompt.shape
    Bs, Ls, _ = x_sample.shape
    depth = mod_w.shape[0]
    xp = x_prompt.reshape(Bp * Lp, D)
    xs = x_sample.reshape(Bs * Ls, D)

    n_cond = 1 + Bs
    rows = -(-n_cond // 8) * 8
    cond = _pad_to(jnp.concatenate([c_ctx[None, :], c], axis=0), (rows, D))
    mod = _modulation(cond, mod_w, mod_b)

    P = cache_gqa_k.shape[2]
    ck = cache_gqa_k.reshape(Bs, -1, P, GQA_KV_HEADS * HEAD_DIM)
    cv = cache_gqa_v.reshape(Bs, -1, P, GQA_KV_HEADS * HEAD_DIM)

    dft = {L: _dft_blocks(L, min(HY_FREQ_BLOCK, L)) for L in (Lp, Ls)}
    tpm_s = Ls // TOKEN_TILE

    new_f, new_b, new_gk, new_gv, new_ckv, new_kpe = [], [], [], [], [], []
    for i in range(depth):
        j = i // 2
        m = mod[i]
        sh_p, sc_p, gt_p = [m[0:1, k * D:(k + 1) * D].reshape(1, 1, D) for k in range(3)]
        sh_s, sc_s, gt_s = [m[1:n_cond, k * D:(k + 1) * D].reshape(Bs, 1, D) for k in range(3)]
        pre_g = pre_norm_g[i][None, :]
        post_g = post_norm_g[i][None, :]
        if i % 2 == 0:
            wts = _even_weights(even_in_w, gdn_conv_w, gdn_a_log, gdn_dt_bias, hyena_conv_w, hyena_conv_b, j)
            out_w = even_out_w[j].astype(BF16)
            hy = {}
            for L in (Lp, Ls):
                fwd, inv = dft[L]
                filt = _hy_filters(L, hyena_ffn_w1[j], hyena_ffn_b1[j], hyena_ffn_w2[j], hyena_ffn_b2[j],
                                   hyena_ffn_w3[j], hyena_sin_freq[j])
                hc, hs = _hy_spectrum(filt, fwd, L, min(HY_FREQ_BLOCK, L))
                hy[L] = (fwd, inv, hc, hs)
            common = (gdn_conv_w[j], gdn_norm_g[j][None, :], hyena_conv_w[j], hyena_conv_b[j][None, :], hyena_bias[j])
            xp, sf, sb = _even_layer(xp, Bp, Lp, sh_p, sc_p, gt_p, 0, pre_g, post_g, wts, hy[Lp], out_w,
                                     *common, None, None, j)
            xs, _, _ = _even_layer(xs, Bs, Ls, sh_s, sc_s, gt_s, tpm_s, pre_g, post_g, wts, hy[Ls], out_w,
                                   *common, state_gdn_fwd, state_gdn_bwd, j)
            new_f.append(sf)
            new_b.append(sb)
        else:
            wts = _odd_weights(odd_in_w, mla_q_up, mla_kv_up, j)
            out_w = odd_out_w[j].astype(BF16)
            norms = (gqa_q_norm_g[j][None, :], gqa_k_norm_g[j][None, :], mla_q_norm_g[j][None, :],
                     mla_kv_norm_g[j][None, :])
            xp, (kg, zp, ckv) = _odd_layer(xp, Bp, Lp, sh_p, sc_p, gt_p, 0, pre_g, post_g, wts, out_w,
                                           *norms, None, j)
            xs, _ = _odd_layer(xs, Bs, Ls, sh_s, sc_s, gt_s, tpm_s, pre_g, post_g, wts, out_w,
                               *norms, (ck, cv, cache_mla_ckv, cache_mla_kpe), j)
            new_gk.append(kg.reshape(Bp, Lp, GQA_KV_HEADS, HEAD_DIM))
            new_gv.append(zp[:, OD_GV * LANES:(OD_GV + 2) * LANES].reshape(Bp, Lp, GQA_KV_HEADS, HEAD_DIM))
            new_ckv.append(ckv.reshape(Bp, Lp, HEAD_DIM))
            new_kpe.append(zp[:, OD_MPE * LANES:OD_MPE * LANES + MLA_ROPE_DIM].reshape(Bp, Lp, MLA_ROPE_DIM))
    return (xp.reshape(Bp, Lp, D), xs.reshape(Bs, Ls, D),
            jnp.stack(new_f, axis=1), jnp.stack(new_b, axis=1),
            jnp.stack(new_gk, axis=1), jnp.stack(new_gv, axis=1),
            jnp.stack(new_ckv, axis=1), jnp.stack(new_kpe, axis=1))
```

```python
import functools
import math

import numpy as np
import jax
import jax.numpy as jnp
from jax import lax
from jax.experimental import pallas as pl
from jax.experimental.pallas import tpu as pltpu

F32 = jnp.float32
BF16 = jnp.bfloat16

NORM_EPS = 1e-6
ROPE_THETA = 10000.0
GRID_W = 64

GDN_HEADS = 4
GDN_DK = 128
GDN_CHUNK = 64
GDN_CONV_W = 5
GDN_PREP_UNROLL = 8
HY_CH = 512
HY_BANDS = 16
HY_DECAY_TARGET = 1e-2
HY_SHORT_DECAY_PCT = 0.3
HY_LONG_DECAY_PCT = 1.5
GQA_HEADS = 4
GQA_KV_HEADS = 2
HEAD_DIM = 128
MLA_HEADS = 4
MLA_ROPE_DIM = 64

LANES = 128
VMEM_LIMIT = 56 * 1024 * 1024

TOKEN_TILE = 256
HY_FREQ_BLOCK = 512
HY_FREQ_SPLIT = 2
HY_CH_BLOCK = 256
GQA_Q_TILE = 256
MLA_Q_TILE = 128
MLA_HEAD_STACKS = 2


def _cparams(sem):
    return pltpu.CompilerParams(dimension_semantics=sem, vmem_limit_bytes=VMEM_LIMIT)


def _dot(a, b):
    return jnp.dot(a, b, preferred_element_type=F32)


def _dot_nt(a, b):
    return lax.dot_general(a, b, (((1,), (1,)), ((), ())), preferred_element_type=F32)


def _dot_tn(a, b):
    return lax.dot_general(a, b, (((0,), (0,)), ((), ())), preferred_element_type=F32)


def _split(a):
    hi = a.astype(BF16)
    lo = (a - hi.astype(F32)).astype(BF16)
    return hi, lo


def _dot3(a, b):
    ah, al = _split(a)
    bh, bl = _split(b)
    return _dot(ah, bh) + (_dot(ah, bl) + _dot(al, bh))


def _bdot(a, b):
    return jnp.einsum('gij,gjk->gik', a, b, preferred_element_type=F32)


def _bdot16(a, b):
    return _bdot(a.astype(BF16), b.astype(BF16))


def _silu(x):
    return x * (1.0 / (1.0 + jnp.exp(-x)))


def _sigmoid(x):
    return 1.0 / (1.0 + jnp.exp(-x))


def _softplus(x):
    return jnp.maximum(x, 0.0) + jnp.log(1.0 + jnp.exp(-jnp.abs(x)))


def _rms(x, g):
    return x * lax.rsqrt(jnp.mean(x * x, axis=-1, keepdims=True) + NORM_EPS) * g


def _shift_rows(x, s):
    L = x.shape[0]
    if s == 0:
        return x
    rolled = pltpu.roll(x, (-s) % L, 0)
    row = lax.broadcasted_iota(jnp.int32, x.shape, 0)
    valid = (row + s >= 0) & (row + s < L)
    return jnp.where(valid, rolled, 0.0)


def _mod_kernel(c_ref, w_ref, b_ref, o_ref):
    c = _silu(c_ref[...])
    o_ref[0] = _dot3(c, w_ref[0]) + b_ref[0]


def _modulation(cond, mod_w, mod_b):
    depth, d, d3 = mod_w.shape
    r = cond.shape[0]
    nb = d3 // d
    return pl.pallas_call(
        _mod_kernel,
        grid=(depth, nb),
        in_specs=[pl.BlockSpec((r, d), lambda i, n: (0, 0)),
                  pl.BlockSpec((1, d, d), lambda i, n: (i, 0, n)),
                  pl.BlockSpec((1, 1, d), lambda i, n: (i, 0, n))],
        out_specs=pl.BlockSpec((1, r, d), lambda i, n: (i, 0, n)),
        out_shape=jax.ShapeDtypeStruct((depth, r, d3), F32),
        compiler_params=_cparams(("arbitrary", "arbitrary")),
        name="adaln_modulation",
    )(cond, mod_w, mod_b.reshape(depth, 1, d3))


def _in_proj_kernel(x_ref, sh_ref, sc_ref, g_ref, w_ref, *rest, n_main, has_small):
    if has_small:
        ws_ref, o_ref, os_ref = rest
    else:
        (o_ref,) = rest
    x = x_ref[...]
    h = _rms(x, g_ref[...]) * (1.0 + sc_ref[0]) + sh_ref[0]
    hb = h.astype(BF16)
    step = 512
    for n0 in range(0, n_main, step):
        o_ref[:, n0:n0 + step] = _dot(hb, w_ref[:, n0:n0 + step])
    if has_small:
        os_ref[...] = _dot(hb, ws_ref[...])


def _in_proj(x2, shift, scale, pre_g, w_main, w_small, tiles_per_mod):
    T, D = x2.shape
    N = w_main.shape[1]
    tm = TOKEN_TILE
    if tiles_per_mod:
        mod_map = lambda i: (i // tiles_per_mod, 0, 0)
    else:
        mod_map = lambda i: (0, 0, 0)
    in_specs = [pl.BlockSpec((tm, D), lambda i: (i, 0)),
                pl.BlockSpec((1, 1, D), mod_map),
                pl.BlockSpec((1, 1, D), mod_map),
                pl.BlockSpec((1, D), lambda i: (0, 0)),
                pl.BlockSpec((D, N), lambda i: (0, 0))]
    out_specs = [pl.BlockSpec((tm, N), lambda i: (i, 0))]
    out_shape = [jax.ShapeDtypeStruct((T, N), F32)]
    args = [x2, shift, scale, pre_g, w_main]
    if w_small is not None:
        in_specs.append(pl.BlockSpec((D, LANES), lambda i: (0, 0)))
        out_specs.append(pl.BlockSpec((tm, LANES), lambda i: (i, 0)))
        out_shape.append(jax.ShapeDtypeStruct((T, LANES), F32))
        args.append(w_small)
    res = pl.pallas_call(
        functools.partial(_in_proj_kernel, n_main=N, has_small=w_small is not None),
        grid=(T // tm,),
        in_specs=in_specs, out_specs=out_specs, out_shape=out_shape,
        compiler_params=_cparams(("arbitrary",)),
        name="in_proj",
    )(*args)
    return res


def _out_proj_kernel(a_ref, b_ref, x_ref, gt_ref, g_ref, w_ref, o_ref, *, half):
    y = _dot(a_ref[...].astype(BF16), w_ref[:half, :]) + _dot(b_ref[...].astype(BF16), w_ref[half:, :])
    o_ref[...] = x_ref[...] + gt_ref[0] * _rms(y, g_ref[...])


def _out_proj(a, b, x2, gate, post_g, w, tiles_per_mod):
    T, D = x2.shape
    half = a.shape[1]
    tm = TOKEN_TILE
    if tiles_per_mod:
        mod_map = lambda i: (i // tiles_per_mod, 0, 0)
    else:
        mod_map = lambda i: (0, 0, 0)
    return pl.pallas_call(
        functools.partial(_out_proj_kernel, half=half),
        grid=(T // tm,),
        in_specs=[pl.BlockSpec((tm, half), lambda i: (i, 0)),
                  pl.BlockSpec((tm, half), lambda i: (i, 0)),
                  pl.BlockSpec((tm, D), lambda i: (i, 0)),
                  pl.BlockSpec((1, 1, D), mod_map),
                  pl.BlockSpec((1, D), lambda i: (0, 0)),
                  pl.BlockSpec((2 * half, D), lambda i: (0, 0))],
        out_specs=pl.BlockSpec((tm, D), lambda i: (i, 0)),
        out_shape=jax.ShapeDtypeStruct((T, D), F32),
        compiler_params=_cparams(("arbitrary",)),
        name="out_proj",
    )(a, b, x2, gate, post_g, w)


def _gdn_kernel(zq_ref, zk_ref, zv_ref, zg_ref, zab_ref, cwq_ref, cwk_ref, cwv_ref, ng_ref,
                alog_ref, dtb_ref, *rest, L, has_state):
    if has_state:
        s0f_ref, s0b_ref = rest[:2]
        rest = rest[2:]
    (o_ref, sf_out_ref, sb_out_ref, q_s, k_s, v_s, g_s, b_s, o_s, st_s,
     oc_s, qe_s, sc_s, sm_s, gl_s) = rest
    C = GDN_CHUNK
    n_chunks = L // C
    U = min(GDN_PREP_UNROLL, n_chunks)
    G = 2 * U
    head = pl.program_id(1)

    def conv_silu(z_ref, w_ref):
        x = z_ref[...]
        half = GDN_CONV_W // 2
        acc = x * w_ref[half:half + 1, :]
        for i in range(GDN_CONV_W):
            if i != half:
                acc = acc + _shift_rows(x, i - half) * w_ref[i:i + 1, :]
        return _silu(acc)

    def l2n(x):
        return x * lax.rsqrt(jnp.sum(x * x, axis=-1, keepdims=True) + NORM_EPS)

    q_s[...] = l2n(conv_silu(zq_ref, cwq_ref)) * (GDN_DK ** -0.5)
    k_s[...] = l2n(conv_silu(zk_ref, cwk_ref))
    v_s[...] = conv_silu(zv_ref, cwv_ref)
    zab = zab_ref[...]
    g_s[...] = -jnp.exp(alog_ref[...]) * _softplus(zab + dtb_ref[...])
    b_s[...] = _sigmoid(zab)
    if has_state:
        st_s[0] = s0f_ref[0, 0, 0]
        st_s[1] = s0b_ref[0, 0, 0]
    else:
        st_s[...] = jnp.zeros_like(st_s)

    def iota(shape, axis):
        return lax.broadcasted_iota(jnp.int32, shape, axis)

    def direction(shape):
        return iota(shape, 0) & 1

    sq = (G, C, C)
    row = iota(sq, 1)
    col = iota(sq, 2)
    signed = (row - col) * (1 - 2 * direction(sq))
    incl = signed >= 0
    strict = signed > 0
    eye = (row == col).astype(F32)
    same = [(row >> s) == (col >> s) for s in (3, 4, 5)]
    off_blocks = [same[1] & jnp.logical_not(same[0]), same[2] & jnp.logical_not(same[1]),
                  jnp.logical_not(same[2])]
    wide = (G, C, LANES)
    sel_lane = direction(wide) * GDN_HEADS + head
    mask_g = iota(wide, 2) == sel_lane
    mask_b = iota(wide, 2) == sel_lane + 2 * GDN_HEADS
    tall = (G, LANES, C)
    mask_t = iota(tall, 1) == direction(tall) * GDN_HEADS + head
    colv = (G, C, 1)
    mask_last = iota(colv, 1) == (1 - direction(colv)) * (C - 1)
    r2 = lax.broadcasted_iota(jnp.int32, (C, C), 0)
    c2 = lax.broadcasted_iota(jnp.int32, (C, C), 1)
    tri = jnp.concatenate([(r2 >= c2).astype(BF16), (r2 <= c2).astype(BF16)], axis=0)

    def both_dirs(x):
        return jnp.broadcast_to(x[:, None], (U, 2) + x.shape[1:]).reshape((G,) + x.shape[1:])

    def prep_group(c, carry):
        rows = pl.ds(pl.multiple_of(c * (U * C), U * C), U * C)
        q = q_s[rows, :].reshape(U, C, LANES)
        k = k_s[rows, :].reshape(U, C, LANES)
        v = v_s[rows, :].reshape(U, C, LANES)
        ball = b_s[rows, :].reshape(U, C, LANES)
        g_hi, g_lo = _split(g_s[rows, :])
        g_hi = g_hi.reshape(U, C, LANES)
        g_lo = g_lo.reshape(U, C, LANES)
        gcum = jnp.stack([_dot(tri, g_hi[u]) + _dot(tri, g_lo[u]) for u in range(U)])
        gcum = gcum.reshape(G, C, LANES)
        gcum_t = jnp.stack([gcum[g].T for g in range(G)])
        gc = jnp.sum(jnp.where(mask_g, gcum, 0.0), axis=2, keepdims=True)
        gr = jnp.sum(jnp.where(mask_t, gcum_t, 0.0), axis=1, keepdims=True)
        beta = jnp.sum(jnp.where(mask_b, both_dirs(ball), 0.0), axis=2, keepdims=True)
        kbf = k.astype(BF16)
        kq = jnp.einsum('uik,ujk->uij', jnp.concatenate([k, q], axis=1).astype(BF16), kbf,
                        preferred_element_type=F32)
        kk = both_dirs(kq[:, :C])
        qk = both_dirs(kq[:, C:])
        decay = jnp.where(incl, jnp.exp(jnp.where(incl, gc - gr, 0.0)), 0.0)
        lmat = jnp.where(strict, beta * kk * decay, 0.0)
        p = jnp.where(same[0], -lmat, 0.0)
        tmat = eye + p
        for _ in range(2):
            p = _bdot16(p, p)
            tmat = tmat + _bdot16(tmat, p)
        for off in off_blocks:
            t16 = tmat.astype(BF16)
            tmat = tmat - _bdot(_bdot(t16, jnp.where(off, lmat, 0.0).astype(BF16)).astype(BF16), t16)
        eg = jnp.exp(gc)
        k2 = both_dirs(k)
        kb = k2 * beta
        sol = _bdot16(tmat, jnp.concatenate([both_dirs(v) * beta, kb * eg], axis=2))
        glast = jnp.sum(jnp.where(mask_last, gc, 0.0), axis=1, keepdims=True)
        kdec = k2 * jnp.exp(glast - gc)
        kdec_t = jnp.stack([kdec[g].T for g in range(G)]).astype(BF16)
        sol16 = sol.astype(BF16)
        r_in = _bdot((qk * decay).astype(BF16), sol16)
        r_kd = _bdot(kdec_t, sol16)
        out_rows = pl.ds(pl.multiple_of(c * (G * C), G * C), G * C)
        oc_s[out_rows, :] = r_in[:, :, :LANES].reshape(G * C, LANES)
        qe_s[out_rows, :] = (both_dirs(q) * eg - r_in[:, :, LANES:]).astype(BF16).reshape(G * C, LANES)
        st_rows = pl.ds(pl.multiple_of(c * (G * LANES), G * LANES), G * LANES)
        sc_s[st_rows, :] = r_kd[:, :, :LANES].reshape(G * LANES, LANES)
        sm_s[st_rows, :] = (-r_kd[:, :, LANES:]).astype(BF16).reshape(G * LANES, LANES)
        gl_s[pl.ds(pl.multiple_of(c * (G * 8), G * 8), G * 8), :] = (
            jnp.broadcast_to(jnp.exp(glast), (G, 8, LANES)).reshape(G * 8, LANES))
        return carry

    lax.fori_loop(0, n_chunks // U, prep_group, 0)

    def scan_step(c, carry):
        slot_f = c * 2
        slot_b = (n_chunks - 1 - c) * 2 + 1

        def ld(ref, size):
            return jnp.stack([ref[pl.ds(pl.multiple_of(slot_f * size, size), size), :],
                              ref[pl.ds(pl.multiple_of(slot_b * size, size), size), :]])

        s = st_s[...]
        sb16 = s.astype(BF16)
        o = _bdot(ld(qe_s, C), sb16) + ld(oc_s, C)
        st_s[...] = s * ld(gl_s, 8)[:, 0:1, :] + (_bdot(ld(sm_s, LANES), sb16) + ld(sc_s, LANES))
        o_s[pl.ds(pl.multiple_of(c * (2 * C), 2 * C), 2 * C), :] = o.reshape(2 * C, LANES)
        return carry

    lax.fori_loop(0, n_chunks, scan_step, 0)

    FIN = 4

    def finish(i, carry):
        parts = []
        for j in range(FIN):
            c = i * FIN + j
            of = o_s[pl.ds(pl.multiple_of(c * (2 * C), C), C), :]
            ob = o_s[pl.ds(pl.multiple_of((n_chunks - 1 - c) * (2 * C) + C, C), C), :]
            parts.append(of + ob)
        rows = pl.ds(pl.multiple_of(i * (FIN * C), FIN * C), FIN * C)
        o_ref[rows, :] = _rms(jnp.concatenate(parts, axis=0), ng_ref[...]) * _silu(zg_ref[rows, :])
        return carry

    lax.fori_loop(0, n_chunks // FIN, finish, 0)
    sf_out_ref[0, 0] = st_s[0]
    sb_out_ref[0, 0] = st_s[1]


def _gdn(z, zab, conv_w, norm_g, alog_row, dtb_row, B, L, state_f=None, state_b=None, layer_j=0):
    T = z.shape[0]
    H = GDN_HEADS
    has_state = state_f is not None
    blk = lambda off: pl.BlockSpec((L, LANES), lambda b, h: (b, off + h))
    cw = lambda off: pl.BlockSpec((GDN_CONV_W, LANES), lambda b, h: (0, off + h))
    in_specs = [blk(0), blk(H), blk(2 * H), blk(3 * H),
                pl.BlockSpec((L, LANES), lambda b, h: (b, 0)),
                cw(0), cw(H), cw(2 * H),
                pl.BlockSpec((1, LANES), lambda b, h: (0, 0)),
                pl.BlockSpec((1, LANES), lambda b, h: (0, 0)),
                pl.BlockSpec((1, LANES), lambda b, h: (0, 0))]
    args = [z, z, z, z, zab, conv_w, conv_w, conv_w, norm_g, alog_row, dtb_row]
    if has_state:
        st = pl.BlockSpec((1, 1, 1, GDN_DK, LANES), lambda b, h: (b, layer_j, h, 0, 0))
        in_specs += [st, st]
        args += [state_f, state_b]
    sout = pl.BlockSpec((1, 1, GDN_DK, LANES), lambda b, h: (b, h, 0, 0))
    n_slots = 2 * (L // GDN_CHUNK)
    scratch = ([pltpu.VMEM((L, LANES), F32) for _ in range(5)]
               + [pltpu.VMEM((2 * L, LANES), F32), pltpu.VMEM((2, GDN_DK, LANES), F32)]
               + [pltpu.VMEM((2 * L, LANES), F32), pltpu.VMEM((2 * L, LANES), BF16)]
               + [pltpu.VMEM((n_slots * GDN_DK, LANES), F32),
                  pltpu.VMEM((n_slots * GDN_DK, LANES), BF16)]
               + [pltpu.VMEM((n_slots * 8, LANES), F32)])
    return pl.pallas_call(
        functools.partial(_gdn_kernel, L=L, has_state=has_state),
        grid=(B, H),
        in_specs=in_specs,
        out_specs=[pl.BlockSpec((L, LANES), lambda b, h: (b, h)), sout, sout],
        out_shape=[jax.ShapeDtypeStruct((T, H * LANES), F32),
                   jax.ShapeDtypeStruct((B, H, GDN_DK, LANES), F32),
                   jax.ShapeDtypeStruct((B, H, GDN_DK, LANES), F32)],
        scratch_shapes=scratch,
        compiler_params=_cparams(("arbitrary", "arbitrary")),
        name="gdn",
    )(*args)


def _hy_filter_kernel(f_ref, w1_ref, b1_ref, sf0_ref, w2_ref, b2_ref, sf1_ref, w3_ref, dl_ref, o_ref, *, L):
    rowi = lax.broadcasted_iota(jnp.int32, (L, LANES), 0).astype(F32)
    lane = lax.broadcasted_iota(jnp.int32, (L, LANES), 1)
    t = rowi * (1.0 / (L - 1))
    wpos = rowi * (2.0 * math.pi / L)
    ang = f_ref[...] * wpos
    z = jnp.where(lane == 0, t,
                  jnp.where(lane <= HY_BANDS, jnp.cos(ang),
                            jnp.where(lane <= 2 * HY_BANDS, -jnp.sin(ang), 0.0)))
    h = jnp.sin(sf0_ref[...] * (_dot3(z, w1_ref[...]) + b1_ref[...]))
    h = jnp.sin(sf1_ref[...] * (_dot3(h, w2_ref[...]) + b2_ref[...]))
    tc = lax.broadcasted_iota(jnp.int32, (L, HY_CH), 0).astype(F32) * (1.0 / (L - 1))
    window = jnp.exp(-tc * dl_ref[...])
    for j in range(4):
        o_ref[:, j * HY_CH:(j + 1) * HY_CH] = _dot3(h, w3_ref[:, j * HY_CH:(j + 1) * HY_CH]) * window


def _pad_to(a, shape):
    return jnp.pad(a, [(0, s - d) for s, d in zip(shape, a.shape)])


def _hy_filters(L, w1, b1, w2, b2, w3, sin_freq):
    fvals = np.linspace(1e-4, HY_BANDS - 1, HY_BANDS, dtype=np.float32)
    frow = np.zeros((1, LANES), np.float32)
    frow[0, 1:1 + HY_BANDS] = fvals
    frow[0, 1 + HY_BANDS:1 + 2 * HY_BANDS] = fvals
    deltas = np.abs(np.linspace(math.log(HY_DECAY_TARGET) / HY_LONG_DECAY_PCT,
                                math.log(HY_DECAY_TARGET) / HY_SHORT_DECAY_PCT, HY_CH, dtype=np.float32))
    args = [jnp.asarray(frow),
            _pad_to(w1, (LANES, LANES)), _pad_to(b1[None, :], (1, LANES)), _pad_to(sin_freq[0][None, :], (1, LANES)),
            _pad_to(w2, (LANES, LANES)), _pad_to(b2[None, :], (1, LANES)), _pad_to(sin_freq[1][None, :], (1, LANES)),
            _pad_to(w3, (LANES, 4 * HY_CH)), jnp.asarray(deltas[None, :])]
    return pl.pallas_call(
        functools.partial(_hy_filter_kernel, L=L),
        out_shape=jax.ShapeDtypeStruct((L, 4 * HY_CH), F32),
        compiler_params=pltpu.CompilerParams(vmem_limit_bytes=VMEM_LIMIT),
        name="hyena_filters",
    )(*args)


def _dft_tables(L):
    N = 2 * L
    k = np.arange(L, dtype=np.int64)[:, None]
    s = np.arange(L, dtype=np.int64)[None, :]
    ang = ((2 * k + 1) * s % (2 * N)).astype(np.float64) * (2.0 * math.pi / (2 * N))
    return np.cos(ang).astype(np.float32), np.sin(ang).astype(np.float32)


def _dft_blocks(L, kb):
    cm, sm = _dft_tables(L)
    nk = L // kb
    fwd = np.concatenate([cm.reshape(nk, kb, L), sm.reshape(nk, kb, L)], axis=1)
    inv = np.concatenate([cm.T.reshape(L, nk, kb), sm.T.reshape(L, nk, kb)], axis=2)
    inv = np.ascontiguousarray(inv.transpose(1, 0, 2))
    return jnp.asarray(fwd).astype(BF16), jnp.asarray(inv).astype(BF16)


def _hy_spectrum_kernel(f_ref, flt_ref, hc_ref, hs_ref, *, L, kb):
    row = lax.broadcasted_iota(jnp.int32, (L, HY_CH), 0)
    scale = 1.0 / L
    for o in range(2):
        hf = flt_ref[:, (2 * o) * HY_CH:(2 * o + 1) * HY_CH]
        hb = jnp.where(row == 0, 0.0, flt_ref[:, (2 * o + 1) * HY_CH:(2 * o + 2) * HY_CH])
        a_hi, a_lo = _split(hf + hb)
        d_hi, d_lo = _split(hf - hb)
        fc = f_ref[0, :kb, :]
        fs = f_ref[0, kb:, :]
        hc_ref[o] = (_dot(fc, a_hi) + _dot(fc, a_lo)) * scale
        hs_ref[o] = (_dot(fs, d_hi) + _dot(fs, d_lo)) * scale


def _hy_spectrum(filt, fwd, L, kb):
    nk = L // kb
    return pl.pallas_call(
        functools.partial(_hy_spectrum_kernel, L=L, kb=kb),
        grid=(nk,),
        in_specs=[pl.BlockSpec((1, 2 * kb, L), lambda i: (i, 0, 0)),
                  pl.BlockSpec((L, 4 * HY_CH), lambda i: (0, 0))],
        out_specs=[pl.BlockSpec((2, kb, HY_CH), lambda i: (0, i, 0)),
                   pl.BlockSpec((2, kb, HY_CH), lambda i: (0, i, 0))],
        out_shape=[jax.ShapeDtypeStruct((2, L, HY_CH), F32)] * 2,
        compiler_params=_cparams(("arbitrary",)),
        name="hyena_spectrum",
    )(fwd, filt)


def _hy_conv_kernel(zin_ref, xo_ref, gate_ref, cwz_ref, cbz_ref, cwx_ref, cbx_ref, skip_ref,
                    f_ref, g_ref, hc_ref, hs_ref, o_ref, zb_s, acc_s, *, kb, first, last):
    kstep = pl.program_id(2)
    nk = pl.num_programs(2)

    def conv3(x_ref, w_ref, b_ref):
        x = x_ref[...]
        return (_shift_rows(x, -1) * w_ref[0:1, :] + x * w_ref[1:2, :]
                + _shift_rows(x, 1) * w_ref[2:3, :]) + b_ref[...]

    def z_value():
        return conv3(zin_ref, cwz_ref, cbz_ref) if first else zin_ref[...]

    @pl.when(kstep == 0)
    def _():
        zb_s[...] = z_value().astype(BF16)
        acc_s[...] = jnp.zeros_like(acc_s)

    zb = zb_s[...]
    kh = kb // HY_FREQ_SPLIT
    total = None
    for h in range(HY_FREQ_SPLIT):
        cos_rows = slice(h * kh, (h + 1) * kh)
        sin_rows = slice(kb + h * kh, kb + (h + 1) * kh)
        xc = _dot(f_ref[0, cos_rows, :], zb)
        xs = _dot(f_ref[0, sin_rows, :], zb)
        hc = hc_ref[0, cos_rows, :]
        hs = hs_ref[0, cos_rows, :]
        yc = (xc * hc - xs * hs).astype(BF16)
        ys = (xc * hs + xs * hc).astype(BF16)
        part = _dot(g_ref[0, :, cos_rows], yc) + _dot(g_ref[0, :, sin_rows], ys)
        total = part if total is None else total + part
    acc_s[...] += total

    @pl.when(kstep == nk - 1)
    def _():
        xo = conv3(xo_ref, cwx_ref, cbx_ref)
        res = xo * (acc_s[...] + z_value() * skip_ref[...])
        if last:
            res = res * _silu(gate_ref[...])
        o_ref[...] = res


def _hy_conv(zsrc, zcol, z, xcol, gcol, conv_w, conv_b, skip, order, fwd, inv, hc, hs, B, L, first, last):
    T = z.shape[0]
    cb = HY_CH_BLOCK
    ncb = HY_CH // cb
    nk, _, kb2 = inv.shape
    kb = kb2 // 2
    data = lambda off: pl.BlockSpec((L, cb), lambda b, c, k: (b, off + c))
    cwspec = lambda off: pl.BlockSpec((3, cb), lambda b, c, k: (0, off + c))
    cbspec = lambda off: pl.BlockSpec((1, cb), lambda b, c, k: (0, off + c))
    in_specs = [data(zcol), data(xcol), data(gcol),
                cwspec(0), cbspec(0), cwspec((1 + order) * ncb), cbspec((1 + order) * ncb),
                pl.BlockSpec((1, cb), lambda b, c, k: (0, c)),
                pl.BlockSpec((1, 2 * kb, L), lambda b, c, k: (k, 0, 0)),
                pl.BlockSpec((1, L, 2 * kb), lambda b, c, k: (k, 0, 0)),
                pl.BlockSpec((1, kb, cb), lambda b, c, k: (order, k, c)),
                pl.BlockSpec((1, kb, cb), lambda b, c, k: (order, k, c))]
    return pl.pallas_call(
        functools.partial(_hy_conv_kernel, kb=kb, first=first, last=last),
        grid=(B, ncb, nk),
        in_specs=in_specs,
        out_specs=pl.BlockSpec((L, cb), lambda b, c, k: (b, c)),
        out_shape=jax.ShapeDtypeStruct((T, HY_CH), F32),
        scratch_shapes=[pltpu.VMEM((L, cb), BF16), pltpu.VMEM((L, cb), F32)],
        compiler_params=_cparams(("arbitrary", "arbitrary", "arbitrary")),
        name="hyena_conv",
    )(zsrc, z, z, conv_w, conv_b, conv_w, conv_b, skip[order:order + 1], fwd, inv, hc, hs)


OD_GQ, OD_GG, OD_MG, OD_GK, OD_GV, OD_MQ, OD_MKV, OD_MPE = 0, 4, 8, 12, 14, 16, 18, 19
OD_WIDTH = 20 * LANES


def _rope_swap(x, quarter):
    n = x.shape[1]
    lane = lax.broadcasted_iota(jnp.int32, x.shape, 1)
    first = (lane & (2 * quarter - 1)) < quarter
    return jnp.where(first, pltpu.roll(x, n - quarter, 1), pltpu.roll(x, quarter, 1))


def _odd_prep_kernel(z_ref, qg_ref, kg_ref, mqg_ref, mqup_ref, mkvg_ref, wk_ref, *rest, rope):
    if rope:
        cg_ref, sg_ref, cm_ref, sm_ref = rest[:4]
        rest = rest[4:]
    q_out, k_out, qa_out, qp_out, ckv_out, k16_out, v16_out, ckv16_out, kpe16_out = rest
    D = HEAD_DIM

    def rot(x, c, s, quarter):
        return x * c + _rope_swap(x, quarter) * s if rope else x

    for h in range(GQA_HEADS):
        q = _rms(z_ref[:, (OD_GQ + h) * D:(OD_GQ + h + 1) * D], qg_ref[...])
        if rope:
            q = rot(q, cg_ref[...], sg_ref[...], D // 4)
        q_out[:, h * D:(h + 1) * D] = (q * (D ** -0.5)).astype(BF16)
    for h in range(GQA_KV_HEADS):
        k = _rms(z_ref[:, (OD_GK + h) * D:(OD_GK + h + 1) * D], kg_ref[...])
        if rope:
            k = rot(k, cg_ref[...], sg_ref[...], D // 4)
        k_out[:, h * D:(h + 1) * D] = k
        k16_out[:, h * D:(h + 1) * D] = k.astype(BF16)
    v16_out[...] = z_ref[:, OD_GV * D:(OD_GV + GQA_KV_HEADS) * D].astype(BF16)
    mq =_rms(z_ref[:, OD_MQ * D:(OD_MQ + 2) * D], mqg_ref[...]).astype(BF16)
    qm = _dot(mq, mqup_ref[...])
    mscale = (D + MLA_ROPE_DIM) ** -0.5
    for h in range(MLA_HEADS):
        qn = qm[:, h * D:(h + 1) * D].astype(BF16)
        qa_out[:, h * D:(h + 1) * D] = (_dot_nt(qn, wk_ref[h]) * mscale).astype(BF16)
    qpe = qm[:, MLA_HEADS * D:]
    if rope:
        qpe = rot(qpe, cm_ref[...], sm_ref[...], MLA_ROPE_DIM // 4)
    qp_out[...] = (qpe * mscale).astype(BF16)
    ckv = _rms(z_ref[:, OD_MKV * D:(OD_MKV + 1) * D], mkvg_ref[...])
    ckv_out[...] = ckv
    ckv16_out[...] = ckv.astype(BF16)
    kpe = z_ref[:, OD_MPE * D:(OD_MPE + 1) * D]
    if rope:
        kpe = rot(kpe, cm_ref[:, :D], sm_ref[:, :D], MLA_ROPE_DIM // 4)
    kpe16_out[...] = kpe.astype(BF16)


def _rope_tables(L, R):
    rows = np.arange(L) // GRID_W
    cols = np.arange(L) % GRID_W
    quarter = R // 4
    inv = ROPE_THETA ** (-np.arange(quarter, dtype=np.float32) * 2.0 / (R // 2))
    a_r = rows[:, None].astype(np.float32) * inv[None, :]
    a_c = cols[:, None].astype(np.float32) * inv[None, :]
    cos = np.concatenate([np.cos(a_r), np.cos(a_r), np.cos(a_c), np.cos(a_c)], axis=1)
    sin = np.concatenate([-np.sin(a_r), np.sin(a_r), -np.sin(a_c), np.sin(a_c)], axis=1)
    return cos.astype(np.float32), sin.astype(np.float32)


def _odd_prep(z, q_g, k_g, mq_g, mq_up, mkv_g, wk, L, rope):
    T = z.shape[0]
    tm = TOKEN_TILE
    D = HEAD_DIM
    full = lambda shape: pl.BlockSpec(shape, lambda i: tuple(0 for _ in shape))
    in_specs = [pl.BlockSpec((tm, OD_WIDTH), lambda i: (i, 0)),
                full((1, D)), full((1, D)), full((1, 2 * D)), full(mq_up.shape), full((1, D)), full(wk.shape)]
    args = [z, q_g, k_g, mq_g, mq_up, mkv_g, wk]
    if rope:
        cg, sg = _rope_tables(L, D)
        cm, sm = _rope_tables(L, MLA_ROPE_DIM)
        cm, sm = np.tile(cm, (1, MLA_HEADS)), np.tile(sm, (1, MLA_HEADS))
        per = L // tm
        pos = lambda w: pl.BlockSpec((tm, w), lambda i: (i % per, 0))
        in_specs += [pos(D), pos(D), pos(2 * D), pos(2 * D)]
        args += [jnp.asarray(cg), jnp.asarray(sg), jnp.asarray(cm), jnp.asarray(sm)]
    tile = lambda w: pl.BlockSpec((tm, w), lambda i: (i, 0))
    return pl.pallas_call(
        functools.partial(_odd_prep_kernel, rope=rope),
        grid=(T // tm,),
        in_specs=in_specs,
        out_specs=[tile(4 * D), tile(2 * D), tile(4 * D), tile(2 * D), tile(D),
                   tile(2 * D), tile(2 * D), tile(D), tile(D)],
        out_shape=[jax.ShapeDtypeStruct((T, 4 * D), BF16), jax.ShapeDtypeStruct((T, 2 * D), F32),
                   jax.ShapeDtypeStruct((T, 4 * D), BF16), jax.ShapeDtypeStruct((T, 2 * D), BF16),
                   jax.ShapeDtypeStruct((T, D), F32),
                   jax.ShapeDtypeStruct((T, 2 * D), BF16), jax.ShapeDtypeStruct((T, 2 * D), BF16),
                   jax.ShapeDtypeStruct((T, D), BF16), jax.ShapeDtypeStruct((T, D), BF16)],
        compiler_params=_cparams(("arbitrary",)),
        name="odd_prep",
    )(*args)


def _softmax_pv(score_blocks, value_blocks):
    m = score_blocks[0].max(axis=-1, keepdims=True)
    for s in score_blocks[1:]:
        m = jnp.maximum(m, s.max(axis=-1, keepdims=True))
    acc = None
    den = None
    for s, v in zip(score_blocks, value_blocks):
        p = jnp.exp(s - m)
        d = p.sum(axis=-1, keepdims=True)
        a = _dot(p.astype(BF16), v)
        acc = a if acc is None else acc + a
        den = d if den is None else den + d
    return acc / den


def _gqa_kernel(q_ref, k_ref, v_ref, gate_ref, *rest, cached):
    if cached:
        ck_ref, cv_ref, o_ref = rest
    else:
        (o_ref,) = rest
    D = HEAD_DIM
    group = GQA_HEADS // GQA_KV_HEADS
    q = q_ref[...]
    keys = [k_ref[...]] + ([ck_ref[0, 0]] if cached else [])
    values = [v_ref[...]] + ([cv_ref[0, 0]] if cached else [])
    outs = []
    for g in range(group):
        qh = q[:, g * D:(g + 1) * D]
        outs.append(_softmax_pv([_dot_nt(qh, kk) for kk in keys], values))
    o_ref[...] = jnp.concatenate(outs, axis=1) * _silu(gate_ref[...])


def _gqa(qg, kg, vg, z, B, L, tq, cache_k=None, cache_v=None, layer_j=0):
    T = z.shape[0]
    D = HEAD_DIM
    group = GQA_HEADS // GQA_KV_HEADS
    nq = L // tq
    cached = cache_k is not None
    in_specs = [pl.BlockSpec((tq, group * D), lambda b, h, i: (b * nq + i, h)),
                pl.BlockSpec((L, D), lambda b, h, i: (b, h)),
                pl.BlockSpec((L, D), lambda b, h, i: (b, h)),
                pl.BlockSpec((tq, group * D), lambda b, h, i: (b * nq + i, OD_GG // group + h))]
    args = [qg, kg, vg, z]
    if cached:
        P = cache_k.shape[2]
        cspec = pl.BlockSpec((1, 1, P, D), lambda b, h, i: (b, layer_j, 0, h))
        in_specs += [cspec, cspec]
        args += [cache_k, cache_v]
    return pl.pallas_call(
        functools.partial(_gqa_kernel, cached=cached),
        grid=(B, GQA_KV_HEADS, nq),
        in_specs=in_specs,
        out_specs=pl.BlockSpec((tq, group * D), lambda b, h, i: (b * nq + i, h)),
        out_shape=jax.ShapeDtypeStruct((T, GQA_HEADS * D), F32),
        compiler_params=_cparams(("arbitrary", "arbitrary", "arbitrary")),
        name="gqa_attention",
    )(*args)


def _mla_kernel(qa_ref, qp_ref, ckv_ref, kpe_ref, gate_ref, wv_ref, *rest, cached):
    if cached:
        cckv_ref, ckpe_ref, o_ref = rest
    else:
        (o_ref,) = rest
    D = HEAD_DIM
    R = MLA_ROPE_DIM
    H = MLA_HEADS
    qa = qa_ref[...]
    qp = qp_ref[...]
    tq = qa.shape[0]
    latents = [ckv_ref[...]] + ([cckv_ref[0, 0]] if cached else [])
    ropes = [kpe_ref[:, :R]] + ([ckpe_ref[0, 0]] if cached else [])
    per = H // MLA_HEAD_STACKS
    outs = []
    for s in range(MLA_HEAD_STACKS):
        heads = range(s * per, (s + 1) * per)
        qas = jnp.concatenate([qa[:, h * D:(h + 1) * D] for h in heads], axis=0)
        qps = jnp.concatenate([qp[:, h * R:(h + 1) * R] for h in heads], axis=0)
        scores = [_dot_nt(qas, lat) + _dot_nt(qps, rp) for lat, rp in zip(latents, ropes)]
        o = _softmax_pv(scores, latents).astype(BF16)
        for i, h in enumerate(heads):
            outs.append(_dot(o[i * tq:(i + 1) * tq], wv_ref[h]))
    o_ref[...] = jnp.concatenate(outs, axis=1) * _silu(gate_ref[...])


def _mla(qa, qp, ckv, kpe, z, wv, B, L, tq, cache_ckv=None, cache_kpe=None, layer_j=0):
    T = z.shape[0]
    D = HEAD_DIM
    H = MLA_HEADS
    nq = L // tq
    cached = cache_ckv is not None
    in_specs = [pl.BlockSpec((tq, H * D), lambda b, i: (b * nq + i, 0)),
                pl.BlockSpec((tq, H * MLA_ROPE_DIM), lambda b, i: (b * nq + i, 0)),
                pl.BlockSpec((L, D), lambda b, i: (b, 0)),
                pl.BlockSpec((L, D), lambda b, i: (b, 0)),
                pl.BlockSpec((tq, H * D), lambda b, i: (b * nq + i, OD_MG // H)),
                pl.BlockSpec(wv.shape, lambda b, i: (0, 0, 0))]
    args = [qa, qp, ckv, kpe, z, wv]
    if cached:
        P = cache_ckv.shape[2]
        in_specs += [pl.BlockSpec((1, 1, P, D), lambda b, i: (b, layer_j, 0, 0)),
                     pl.BlockSpec((1, 1, P, MLA_ROPE_DIM), lambda b, i: (b, layer_j, 0, 0))]
        args += [cache_ckv, cache_kpe]
    return pl.pallas_call(
        functools.partial(_mla_kernel, cached=cached),
        grid=(B, nq),
        in_specs=in_specs,
        out_specs=pl.BlockSpec((tq, H * D), lambda b, i: (b * nq + i, 0)),
        out_shape=jax.ShapeDtypeStruct((T, H * D), F32),
        compiler_params=_cparams(("arbitrary", "arbitrary")),
        name="mla_attention",
    )(*args)


def _even_weights(even_in_w, gdn_conv_w, gdn_a_log, gdn_dt_bias, hyena_conv_w, hyena_conv_b, j):
    w = even_in_w[j]
    qkv_w = 3 * GDN_HEADS * LANES
    n_ab = 4 * GDN_HEADS
    w_main = jnp.concatenate([w[:, :qkv_w], w[:, qkv_w + n_ab:]], axis=1).astype(BF16)
    w_small = _pad_to(w[:, qkv_w:qkv_w + n_ab], (w.shape[0], LANES)).astype(BF16)
    n_dir = 2 * GDN_HEADS
    alog_row = _pad_to(gdn_a_log[j].reshape(1, n_dir), (1, LANES))
    dtb_row = _pad_to(gdn_dt_bias[j].reshape(1, n_dir), (1, LANES))
    return w_main, w_small, alog_row, dtb_row


def _even_layer(x2, B, L, shift, scale, gate, tiles_per_mod, pre_g, post_g, wts, hy, out_w,
                gdn_conv_w, gdn_norm_g, hy_conv_w, hy_conv_b, hy_skip, state_f, state_b, j):
    w_main, w_small, alog_row, dtb_row = wts
    fwd, inv, hc, hs = hy
    z, zab = _in_proj(x2, shift, scale, pre_g, w_main, w_small, tiles_per_mod)
    oa, s_f, s_b = _gdn(z, zab, gdn_conv_w, gdn_norm_g, alog_row, dtb_row, B, L, state_f, state_b, j)
    ncb = HY_CH // HY_CH_BLOCK
    col = lambda idx: idx * ncb
    z1 = _hy_conv(z, col(4), z, col(5), col(7), hy_conv_w, hy_conv_b, hy_skip, 0, fwd, inv, hc, hs, B, L,
                  first=True, last=False)
    ob = _hy_conv(z1, 0, z, col(6), col(7), hy_conv_w, hy_conv_b, hy_skip, 1, fwd, inv, hc, hs, B, L,
                  first=False, last=True)
    x_new = _out_proj(oa, ob, x2, gate, post_g, out_w, tiles_per_mod)
    return x_new, s_f, s_b


def _odd_weights(odd_in_w, mla_q_up, mla_kv_up, j):
    w = odd_in_w[j]
    D = HEAD_DIM
    o = np.cumsum([0, 4 * D, 2 * D, 2 * D, 4 * D, 2 * D, D, MLA_ROPE_DIM, 4 * D])
    gq, gk, gv, gg, mq, mkv, mpe, mg = [w[:, o[i]:o[i + 1]] for i in range(8)]
    w_main = jnp.concatenate([gq, gg, mg, gk, gv, mq, mkv, _pad_to(mpe, (w.shape[0], D))], axis=1).astype(BF16)
    up = mla_q_up[j].reshape(-1, MLA_HEADS, D + MLA_ROPE_DIM)
    mq_up = jnp.concatenate([up[:, :, :D].reshape(-1, MLA_HEADS * D),
                             up[:, :, D:].reshape(-1, MLA_HEADS * MLA_ROPE_DIM)], axis=1).astype(BF16)
    kv = mla_kv_up[j].reshape(-1, MLA_HEADS, 2 * D)
    wk = kv[:, :, :D].transpose(1, 0, 2).astype(BF16)
    wv = kv[:, :, D:].transpose(1, 0, 2).astype(BF16)
    return w_main, mq_up, wk, wv


def _odd_layer(x2, B, L, shift, scale, gate, tiles_per_mod, pre_g, post_g, wts, out_w,
               q_g, k_g, mq_g, mkv_g, caches, j):
    w_main, mq_up, wk, wv = wts
    (z,) = _in_proj(x2, shift, scale, pre_g, w_main, None, tiles_per_mod)
    rope = caches is not None
    qg, kg, qa, qp, ckv, k16, v16, ckv16, kpe16 = _odd_prep(z, q_g, k_g, mq_g, mq_up, mkv_g, wk, L, rope)
    if rope:
        ck, cv, cckv, ckpe = caches
        og = _gqa(qg, k16, v16, z, B, L, GQA_Q_TILE, ck, cv, j)
        om = _mla(qa, qp, ckv16, kpe16, z, wv, B, L, MLA_Q_TILE, cckv, ckpe, j)
    else:
        og = _gqa(qg, k16, v16, z, B, L, GQA_Q_TILE)
        om = _mla(qa, qp, ckv16, kpe16, z, wv, B, L, MLA_Q_TILE)
    x_new = _out_proj(og, om, x2, gate, post_g, out_w, tiles_per_mod)
    return x_new, (kg, z, ckv)


def kernel(x_prompt, x_sample, state_gdn_fwd, state_gdn_bwd, cache_gqa_k, cache_gqa_v, cache_mla_ckv, cache_mla_kpe, c, c_ctx, mod_w, mod_b, pre_norm_g, post_norm_g, even_in_w, gdn_conv_w, gdn_a_log, gdn_dt_bias, gdn_norm_g, hyena_conv_w, hyena_conv_b, hyena_ffn_w1, hyena_ffn_b1, hyena_ffn_w2, hyena_ffn_b2, hyena_ffn_w3, hyena_sin_freq, hyena_bias, even_out_w, odd_in_w, gqa_q_norm_g, gqa_k_norm_g, mla_q_norm_g, mla_q_up, mla_kv_norm_g, mla_kv_up, odd_out_w):
    Bp, Lp, D = x_prompt.shape
    Bs, Ls, _ = x_sample.shape
    depth = mod_w.shape[0]
    xp = x_prompt.reshape(Bp * Lp, D)
    xs = x_sample.reshape(Bs * Ls, D)

    n_cond = 1 + Bs
    rows = -(-n_cond // 8) * 8
    cond = _pad_to(jnp.concatenate([c_ctx[None, :], c], axis=0), (rows, D))
    mod = _modulation(cond, mod_w, mod_b)

    P = cache_gqa_k.shape[2]
    ck = cache_gqa_k.reshape(Bs, -1, P, GQA_KV_HEADS * HEAD_DIM).astype(BF16)
    cv = cache_gqa_v.reshape(Bs, -1, P, GQA_KV_HEADS * HEAD_DIM).astype(BF16)
    cckv = cache_mla_ckv.astype(BF16)
    ckpe = cache_mla_kpe.astype(BF16)

    dft = {L: _dft_blocks(L, min(HY_FREQ_BLOCK, L)) for L in (Lp, Ls)}
    tpm_s = Ls // TOKEN_TILE

    new_f, new_b, new_gk, new_gv, new_ckv, new_kpe = [], [], [], [], [], []
    for i in range(depth):
        j = i // 2
        m = mod[i]
        sh_p, sc_p, gt_p = [m[0:1, k * D:(k + 1) * D].reshape(1, 1, D) for k in range(3)]
        sh_s, sc_s, gt_s = [m[1:n_cond, k * D:(k + 1) * D].reshape(Bs, 1, D) for k in range(3)]
        pre_g = pre_norm_g[i][None, :]
        post_g = post_norm_g[i][None, :]
        if i % 2 == 0:
            wts = _even_weights(even_in_w, gdn_conv_w, gdn_a_log, gdn_dt_bias, hyena_conv_w, hyena_conv_b, j)
            out_w = even_out_w[j].astype(BF16)
            hy = {}
            for L in (Lp, Ls):
                fwd, inv = dft[L]
                filt = _hy_filters(L, hyena_ffn_w1[j], hyena_ffn_b1[j], hyena_ffn_w2[j], hyena_ffn_b2[j],
                                   hyena_ffn_w3[j], hyena_sin_freq[j])
                hc, hs = _hy_spectrum(filt, fwd, L, min(HY_FREQ_BLOCK, L))
                hy[L] = (fwd, inv, hc, hs)
            common = (gdn_conv_w[j], gdn_norm_g[j][None, :], hyena_conv_w[j], hyena_conv_b[j][None, :], hyena_bias[j])
            xp, sf, sb = _even_layer(xp, Bp, Lp, sh_p, sc_p, gt_p, 0, pre_g, post_g, wts, hy[Lp], out_w,
                                     *common, None, None, j)
            xs, _, _ = _even_layer(xs, Bs, Ls, sh_s, sc_s, gt_s, tpm_s, pre_g, post_g, wts, hy[Ls], out_w,
                                   *common, state_gdn_fwd, state_gdn_bwd, j)
            new_f.append(sf)
            new_b.append(sb)
        else:
            wts = _odd_weights(odd_in_w, mla_q_up, mla_kv_up, j)
            out_w = odd_out_w[j].astype(BF16)
            norms = (gqa_q_norm_g[j][None, :], gqa_k_norm_g[j][None, :], mla_q_norm_g[j][None, :],
                     mla_kv_norm_g[j][None, :])
            xp, (kg, zp, ckv) = _odd_layer(xp, Bp, Lp, sh_p, sc_p, gt_p, 0, pre_g, post_g, wts, out_w,
                                           *norms, None, j)
            xs, _ = _odd_layer(xs, Bs, Ls, sh_s, sc_s, gt_s, tpm_s, pre_g, post_g, wts, out_w,
                               *norms, (ck, cv, cckv, ckpe), j)
            new_gk.append(kg.reshape(Bp, Lp, GQA_KV_HEADS, HEAD_DIM))
            new_gv.append(zp[:, OD_GV * LANES:(OD_GV + 2) * LANES].reshape(Bp, Lp, GQA_KV_HEADS, HEAD_DIM))
            new_ckv.append(ckv.reshape(Bp, Lp, HEAD_DIM))
            new_kpe.append(zp[:, OD_MPE * LANES:OD_MPE * LANES + MLA_ROPE_DIM].reshape(Bp, Lp, MLA_ROPE_DIM))
    return (xp.reshape(Bp, Lp, D), xs.reshape(Bs, Ls, D),
            jnp.stack(new_f, axis=1), jnp.stack(new_b, axis=1),
            jnp.stack(new_gk, axis=1), jnp.stack(new_gv, axis=1),
            jnp.stack(new_ckv, axis=1), jnp.stack(new_kpe, axis=1))
```

```python
import functools
import math

import numpy as np
import jax
import jax.numpy as jnp
from jax import lax
from jax.experimental import pallas as pl
from jax.experimental.pallas import tpu as pltpu

F32 = jnp.float32
BF16 = jnp.bfloat16

NORM_EPS = 1e-6
ROPE_THETA = 10000.0
GRID_W = 64

GDN_HEADS = 4
GDN_DK = 128
GDN_CHUNK = 64
GDN_CONV_W = 5
GDN_PREP_UNROLL = 16
HY_CH = 512
HY_BANDS = 16
HY_DECAY_TARGET = 1e-2
HY_SHORT_DECAY_PCT = 0.3
HY_LONG_DECAY_PCT = 1.5
GQA_HEADS = 4
GQA_KV_HEADS = 2
HEAD_DIM = 128
MLA_HEADS = 4
MLA_ROPE_DIM = 64

LOG2E = math.log2(math.e)

LANES = 128
VMEM_LIMIT = 56 * 1024 * 1024

TOKEN_TILE = 256
HY_FREQ_BLOCK = 512
HY_FREQ_SPLIT = 2
HY_CH_BLOCK = 512
GQA_Q_TILE = 256
MLA_Q_TILE = 128
MLA_HEAD_STACKS = 2


def _cparams(sem):
    return pltpu.CompilerParams(dimension_semantics=sem, vmem_limit_bytes=VMEM_LIMIT)


def _dot(a, b):
    return jnp.dot(a, b, preferred_element_type=F32)


def _dot_nt(a, b):
    return lax.dot_general(a, b, (((1,), (1,)), ((), ())), preferred_element_type=F32)


def _dot_tn(a, b):
    return lax.dot_general(a, b, (((0,), (0,)), ((), ())), preferred_element_type=F32)


def _split(a):
    hi = a.astype(BF16)
    lo = (a - hi.astype(F32)).astype(BF16)
    return hi, lo


def _dot3(a, b):
    ah, al = _split(a)
    bh, bl = _split(b)
    return _dot(ah, bh) + (_dot(ah, bl) + _dot(al, bh))


def _bdot(a, b):
    return jnp.einsum('gij,gjk->gik', a, b, preferred_element_type=F32)


def _bdot16(a, b):
    return _bdot(a.astype(BF16), b.astype(BF16))


def _silu(x):
    return x * (1.0 / (1.0 + jnp.exp(-x)))


def _sigmoid(x):
    return 1.0 / (1.0 + jnp.exp(-x))


def _softplus(x):
    return jnp.maximum(x, 0.0) + jnp.log(1.0 + jnp.exp(-jnp.abs(x)))


def _rms(x, g):
    return x * lax.rsqrt(jnp.mean(x * x, axis=-1, keepdims=True) + NORM_EPS) * g


def _shift_rows(x, s):
    L = x.shape[0]
    if s == 0:
        return x
    rolled = pltpu.roll(x, (-s) % L, 0)
    row = lax.broadcasted_iota(jnp.int32, x.shape, 0)
    valid = (row + s >= 0) & (row + s < L)
    return jnp.where(valid, rolled, 0.0)


def _mod_kernel(c_ref, w_ref, b_ref, o_ref):
    c = _silu(c_ref[...])
    o_ref[0] = _dot3(c, w_ref[0]) + b_ref[0]


def _modulation(cond, mod_w, mod_b):
    depth, d, d3 = mod_w.shape
    r = cond.shape[0]
    nb = d3 // d
    return pl.pallas_call(
        _mod_kernel,
        grid=(depth, nb),
        in_specs=[pl.BlockSpec((r, d), lambda i, n: (0, 0)),
                  pl.BlockSpec((1, d, d), lambda i, n: (i, 0, n)),
                  pl.BlockSpec((1, 1, d), lambda i, n: (i, 0, n))],
        out_specs=pl.BlockSpec((1, r, d), lambda i, n: (i, 0, n)),
        out_shape=jax.ShapeDtypeStruct((depth, r, d3), F32),
        compiler_params=_cparams(("arbitrary", "arbitrary")),
        name="adaln_modulation",
    )(cond, mod_w, mod_b.reshape(depth, 1, d3))


def _in_proj_kernel(x_ref, sh_ref, sc_ref, g_ref, w_ref, *rest, n_main, has_small):
    if has_small:
        ws_ref, o_ref, os_ref = rest
    else:
        (o_ref,) = rest
    x = x_ref[...]
    h = _rms(x, g_ref[...]) * (1.0 + sc_ref[0]) + sh_ref[0]
    hb = h.astype(BF16)
    step = 512
    for n0 in range(0, n_main, step):
        o_ref[:, n0:n0 + step] = _dot(hb, w_ref[:, n0:n0 + step])
    if has_small:
        os_ref[...] = _dot(hb, ws_ref[...])


def _in_proj(x2, shift, scale, pre_g, w_main, w_small, tiles_per_mod):
    T, D = x2.shape
    N = w_main.shape[1]
    tm = TOKEN_TILE
    if tiles_per_mod:
        mod_map = lambda i: (i // tiles_per_mod, 0, 0)
    else:
        mod_map = lambda i: (0, 0, 0)
    in_specs = [pl.BlockSpec((tm, D), lambda i: (i, 0)),
                pl.BlockSpec((1, 1, D), mod_map),
                pl.BlockSpec((1, 1, D), mod_map),
                pl.BlockSpec((1, D), lambda i: (0, 0)),
                pl.BlockSpec((D, N), lambda i: (0, 0))]
    out_specs = [pl.BlockSpec((tm, N), lambda i: (i, 0))]
    out_shape = [jax.ShapeDtypeStruct((T, N), F32)]
    args = [x2, shift, scale, pre_g, w_main]
    if w_small is not None:
        in_specs.append(pl.BlockSpec((D, LANES), lambda i: (0, 0)))
        out_specs.append(pl.BlockSpec((tm, LANES), lambda i: (i, 0)))
        out_shape.append(jax.ShapeDtypeStruct((T, LANES), F32))
        args.append(w_small)
    res = pl.pallas_call(
        functools.partial(_in_proj_kernel, n_main=N, has_small=w_small is not None),
        grid=(T // tm,),
        in_specs=in_specs, out_specs=out_specs, out_shape=out_shape,
        compiler_params=_cparams(("arbitrary",)),
        name="in_proj",
    )(*args)
    return res


def _out_proj_kernel(a_ref, b_ref, x_ref, gt_ref, g_ref, w_ref, o_ref, *, half):
    y = _dot(a_ref[...].astype(BF16), w_ref[:half, :]) + _dot(b_ref[...].astype(BF16), w_ref[half:, :])
    o_ref[...] = x_ref[...] + gt_ref[0] * _rms(y, g_ref[...])


def _out_proj(a, b, x2, gate, post_g, w, tiles_per_mod):
    T, D = x2.shape
    half = a.shape[1]
    tm = TOKEN_TILE
    if tiles_per_mod:
        mod_map = lambda i: (i // tiles_per_mod, 0, 0)
    else:
        mod_map = lambda i: (0, 0, 0)
    return pl.pallas_call(
        functools.partial(_out_proj_kernel, half=half),
        grid=(T // tm,),
        in_specs=[pl.BlockSpec((tm, half), lambda i: (i, 0)),
                  pl.BlockSpec((tm, half), lambda i: (i, 0)),
                  pl.BlockSpec((tm, D), lambda i: (i, 0)),
                  pl.BlockSpec((1, 1, D), mod_map),
                  pl.BlockSpec((1, D), lambda i: (0, 0)),
                  pl.BlockSpec((2 * half, D), lambda i: (0, 0))],
        out_specs=pl.BlockSpec((tm, D), lambda i: (i, 0)),
        out_shape=jax.ShapeDtypeStruct((T, D), F32),
        compiler_params=_cparams(("arbitrary",)),
        name="out_proj",
    )(a, b, x2, gate, post_g, w)


def _gdn_kernel(zq_ref, zk_ref, zv_ref, zg_ref, zab_ref, cwq_ref, cwk_ref, cwv_ref, ng_ref,
                alog_ref, dtb_ref, *rest, L, has_state):
    if has_state:
        s0f_ref, s0b_ref = rest[:2]
        rest = rest[2:]
    (o_ref, sf_out_ref, sb_out_ref, q_s, k_s, v_s, g_s, b_s, o_s, st_s,
     oc_s, qe_s, sc_s, sm_s, gl_s) = rest
    C = GDN_CHUNK
    n_chunks = L // C
    U = min(GDN_PREP_UNROLL, n_chunks)
    G = 2 * U
    head = pl.program_id(1)

    def conv_silu(z_ref, w_ref):
        x = z_ref[...]
        half = GDN_CONV_W // 2
        acc = x * w_ref[half:half + 1, :]
        for i in range(GDN_CONV_W):
            if i != half:
                acc = acc + _shift_rows(x, i - half) * w_ref[i:i + 1, :]
        return _silu(acc)

    def l2n(x):
        return x * lax.rsqrt(jnp.sum(x * x, axis=-1, keepdims=True) + NORM_EPS)

    q_s[...] = l2n(conv_silu(zq_ref, cwq_ref)) * (GDN_DK ** -0.5)
    k_s[...] = l2n(conv_silu(zk_ref, cwk_ref))
    v_s[...] = conv_silu(zv_ref, cwv_ref)
    zab = zab_ref[...]
    g_s[...] = -jnp.exp(alog_ref[...]) * _softplus(zab + dtb_ref[...])
    b_s[...] = _sigmoid(zab)
    if has_state:
        st_s[0] = s0f_ref[0, 0, 0]
        st_s[1] = s0b_ref[0, 0, 0]
    else:
        st_s[...] = jnp.zeros_like(st_s)

    def iota(shape, axis):
        return lax.broadcasted_iota(jnp.int32, shape, axis)

    def direction(shape):
        return iota(shape, 0) & 1

    sq = (G, C, C)
    row = iota(sq, 1)
    col = iota(sq, 2)
    signed = (row - col) * (1 - 2 * direction(sq))
    incl = signed >= 0
    strict = signed > 0
    eye = (row == col).astype(F32)
    same = [(row >> s) == (col >> s) for s in (3, 4, 5)]
    off_blocks = [same[1] & jnp.logical_not(same[0]), same[2] & jnp.logical_not(same[1]),
                  jnp.logical_not(same[2])]
    wide = (G, C, LANES)
    sel_lane = direction(wide) * GDN_HEADS + head
    mask_g = iota(wide, 2) == sel_lane
    mask_b = iota(wide, 2) == sel_lane + 2 * GDN_HEADS
    tall = (G, LANES, C)
    mask_t = iota(tall, 1) == direction(tall) * GDN_HEADS + head
    colv = (G, C, 1)
    mask_last = iota(colv, 1) == (1 - direction(colv)) * (C - 1)
    r2 = lax.broadcasted_iota(jnp.int32, (C, C), 0)
    c2 = lax.broadcasted_iota(jnp.int32, (C, C), 1)
    tri = jnp.concatenate([(r2 >= c2).astype(BF16), (r2 <= c2).astype(BF16)], axis=0)

    def both_dirs(x):
        return jnp.broadcast_to(x[:, None], (U, 2) + x.shape[1:]).reshape((G,) + x.shape[1:])

    def prep_group(c, carry):
        rows = pl.ds(pl.multiple_of(c * (U * C), U * C), U * C)
        q = q_s[rows, :].reshape(U, C, LANES)
        k = k_s[rows, :].reshape(U, C, LANES)
        v = v_s[rows, :].reshape(U, C, LANES)
        ball = b_s[rows, :].reshape(U, C, LANES)
        g_hi, g_lo = _split(g_s[rows, :])
        g_hi = g_hi.reshape(U, C, LANES)
        g_lo = g_lo.reshape(U, C, LANES)
        gcum = jnp.stack([_dot(tri, g_hi[u]) + _dot(tri, g_lo[u]) for u in range(U)])
        gcum = gcum.reshape(G, C, LANES)
        gcum_t = jnp.stack([gcum[g].T for g in range(G)])
        gc = jnp.sum(jnp.where(mask_g, gcum, 0.0), axis=2, keepdims=True)
        gr = jnp.sum(jnp.where(mask_t, gcum_t, 0.0), axis=1, keepdims=True)
        beta = jnp.sum(jnp.where(mask_b, both_dirs(ball), 0.0), axis=2, keepdims=True)
        kbf = k.astype(BF16)
        kq = jnp.einsum('uik,ujk->uij', jnp.concatenate([k, q], axis=1).astype(BF16), kbf,
                        preferred_element_type=F32)
        kk = both_dirs(kq[:, :C])
        qk = both_dirs(kq[:, C:])
        decay = jnp.where(incl, jnp.exp(jnp.where(incl, gc - gr, 0.0)), 0.0)
        lmat = jnp.where(strict, beta * kk * decay, 0.0)
        p = jnp.where(same[0], -lmat, 0.0)
        tmat = eye + p
        for _ in range(2):
            p = _bdot16(p, p)
            tmat = tmat + _bdot16(tmat, p)
        for off in off_blocks:
            t16 = tmat.astype(BF16)
            tmat = tmat - _bdot(_bdot(t16, jnp.where(off, lmat, 0.0).astype(BF16)).astype(BF16), t16)
        eg = jnp.exp(gc)
        k2 = both_dirs(k)
        kb = k2 * beta
        sol = _bdot16(tmat, jnp.concatenate([both_dirs(v) * beta, kb * eg], axis=2))
        glast = jnp.sum(jnp.where(mask_last, gc, 0.0), axis=1, keepdims=True)
        kdec = k2 * jnp.exp(glast - gc)
        kdec_t = jnp.stack([kdec[g].T for g in range(G)]).astype(BF16)
        sol16 = sol.astype(BF16)
        r_in = _bdot((qk * decay).astype(BF16), sol16)
        r_kd = _bdot(kdec_t, sol16)
        out_rows = pl.ds(pl.multiple_of(c * (G * C), G * C), G * C)
        oc_s[out_rows, :] = r_in[:, :, :LANES].reshape(G * C, LANES)
        qe_s[out_rows, :] = (both_dirs(q) * eg - r_in[:, :, LANES:]).astype(BF16).reshape(G * C, LANES)
        st_rows = pl.ds(pl.multiple_of(c * (G * LANES), G * LANES), G * LANES)
        sc_s[st_rows, :] = r_kd[:, :, :LANES].reshape(G * LANES, LANES)
        sm_s[st_rows, :] = (-r_kd[:, :, LANES:]).astype(BF16).reshape(G * LANES, LANES)
        gl_s[pl.ds(pl.multiple_of(c * (G * 8), G * 8), G * 8), :] = (
            jnp.broadcast_to(jnp.exp(glast), (G, 8, LANES)).reshape(G * 8, LANES))
        return carry

    lax.fori_loop(0, n_chunks // U, prep_group, 0)

    def scan_step(c, carry):
        slot_f = c * 2
        slot_b = (n_chunks - 1 - c) * 2 + 1

        def ld(ref, size):
            return jnp.stack([ref[pl.ds(pl.multiple_of(slot_f * size, size), size), :],
                              ref[pl.ds(pl.multiple_of(slot_b * size, size), size), :]])

        s = st_s[...]
        sb16 = s.astype(BF16)
        o = _bdot(ld(qe_s, C), sb16) + ld(oc_s, C)
        st_s[...] = s * ld(gl_s, 8)[:, 0:1, :] + (_bdot(ld(sm_s, LANES), sb16) + ld(sc_s, LANES))
        o_s[pl.ds(pl.multiple_of(c * (2 * C), 2 * C), 2 * C), :] = o.reshape(2 * C, LANES)
        return carry

    lax.fori_loop(0, n_chunks, scan_step, 0)

    FIN = 4

    def finish(i, carry):
        parts = []
        for j in range(FIN):
            c = i * FIN + j
            of = o_s[pl.ds(pl.multiple_of(c * (2 * C), C), C), :]
            ob = o_s[pl.ds(pl.multiple_of((n_chunks - 1 - c) * (2 * C) + C, C), C), :]
            parts.append(of + ob)
        rows = pl.ds(pl.multiple_of(i * (FIN * C), FIN * C), FIN * C)
        o_ref[rows, :] = _rms(jnp.concatenate(parts, axis=0), ng_ref[...]) * _silu(zg_ref[rows, :])
        return carry

    lax.fori_loop(0, n_chunks // FIN, finish, 0)
    sf_out_ref[0, 0] = st_s[0]
    sb_out_ref[0, 0] = st_s[1]


def _gdn(z, zab, conv_w, norm_g, alog_row, dtb_row, B, L, state_f=None, state_b=None, layer_j=0):
    T = z.shape[0]
    H = GDN_HEADS
    has_state = state_f is not None
    blk = lambda off: pl.BlockSpec((L, LANES), lambda b, h: (b, off + h))
    cw = lambda off: pl.BlockSpec((GDN_CONV_W, LANES), lambda b, h: (0, off + h))
    in_specs = [blk(0), blk(H), blk(2 * H), blk(3 * H),
                pl.BlockSpec((L, LANES), lambda b, h: (b, 0)),
                cw(0), cw(H), cw(2 * H),
                pl.BlockSpec((1, LANES), lambda b, h: (0, 0)),
                pl.BlockSpec((1, LANES), lambda b, h: (0, 0)),
                pl.BlockSpec((1, LANES), lambda b, h: (0, 0))]
    args = [z, z, z, z, zab, conv_w, conv_w, conv_w, norm_g, alog_row, dtb_row]
    if has_state:
        st = pl.BlockSpec((1, 1, 1, GDN_DK, LANES), lambda b, h: (b, layer_j, h, 0, 0))
        in_specs += [st, st]
        args += [state_f, state_b]
    sout = pl.BlockSpec((1, 1, GDN_DK, LANES), lambda b, h: (b, h, 0, 0))
    n_slots = 2 * (L // GDN_CHUNK)
    scratch = ([pltpu.VMEM((L, LANES), F32) for _ in range(5)]
               + [pltpu.VMEM((2 * L, LANES), F32), pltpu.VMEM((2, GDN_DK, LANES), F32)]
               + [pltpu.VMEM((2 * L, LANES), F32), pltpu.VMEM((2 * L, LANES), BF16)]
               + [pltpu.VMEM((n_slots * GDN_DK, LANES), F32),
                  pltpu.VMEM((n_slots * GDN_DK, LANES), BF16)]
               + [pltpu.VMEM((n_slots * 8, LANES), F32)])
    return pl.pallas_call(
        functools.partial(_gdn_kernel, L=L, has_state=has_state),
        grid=(B, H),
        in_specs=in_specs,
        out_specs=[pl.BlockSpec((L, LANES), lambda b, h: (b, h)), sout, sout],
        out_shape=[jax.ShapeDtypeStruct((T, H * LANES), F32),
                   jax.ShapeDtypeStruct((B, H, GDN_DK, LANES), F32),
                   jax.ShapeDtypeStruct((B, H, GDN_DK, LANES), F32)],
        scratch_shapes=scratch,
        compiler_params=_cparams(("arbitrary", "arbitrary")),
        name="gdn",
    )(*args)


def _hy_filter_kernel(f_ref, w1_ref, b1_ref, sf0_ref, w2_ref, b2_ref, sf1_ref, w3_ref, dl_ref, o_ref, *, L):
    rowi = lax.broadcasted_iota(jnp.int32, (L, LANES), 0).astype(F32)
    lane = lax.broadcasted_iota(jnp.int32, (L, LANES), 1)
    t = rowi * (1.0 / (L - 1))
    wpos = rowi * (2.0 * math.pi / L)
    ang = f_ref[...] * wpos
    z = jnp.where(lane == 0, t,
                  jnp.where(lane <= HY_BANDS, jnp.cos(ang),
                            jnp.where(lane <= 2 * HY_BANDS, -jnp.sin(ang), 0.0)))
    h = jnp.sin(sf0_ref[...] * (_dot3(z, w1_ref[...]) + b1_ref[...]))
    h = jnp.sin(sf1_ref[...] * (_dot3(h, w2_ref[...]) + b2_ref[...]))
    tc = lax.broadcasted_iota(jnp.int32, (L, HY_CH), 0).astype(F32) * (1.0 / (L - 1))
    window = jnp.exp(-tc * dl_ref[...])
    for j in range(4):
        o_ref[:, j * HY_CH:(j + 1) * HY_CH] = _dot3(h, w3_ref[:, j * HY_CH:(j + 1) * HY_CH]) * window


def _pad_to(a, shape):
    return jnp.pad(a, [(0, s - d) for s, d in zip(shape, a.shape)])


def _hy_filters(L, w1, b1, w2, b2, w3, sin_freq):
    fvals = np.linspace(1e-4, HY_BANDS - 1, HY_BANDS, dtype=np.float32)
    frow = np.zeros((1, LANES), np.float32)
    frow[0, 1:1 + HY_BANDS] = fvals
    frow[0, 1 + HY_BANDS:1 + 2 * HY_BANDS] = fvals
    deltas = np.abs(np.linspace(math.log(HY_DECAY_TARGET) / HY_LONG_DECAY_PCT,
                                math.log(HY_DECAY_TARGET) / HY_SHORT_DECAY_PCT, HY_CH, dtype=np.float32))
    args = [jnp.asarray(frow),
            _pad_to(w1, (LANES, LANES)), _pad_to(b1[None, :], (1, LANES)), _pad_to(sin_freq[0][None, :], (1, LANES)),
            _pad_to(w2, (LANES, LANES)), _pad_to(b2[None, :], (1, LANES)), _pad_to(sin_freq[1][None, :], (1, LANES)),
            _pad_to(w3, (LANES, 4 * HY_CH)), jnp.asarray(deltas[None, :])]
    return pl.pallas_call(
        functools.partial(_hy_filter_kernel, L=L),
        out_shape=jax.ShapeDtypeStruct((L, 4 * HY_CH), F32),
        compiler_params=pltpu.CompilerParams(vmem_limit_bytes=VMEM_LIMIT),
        name="hyena_filters",
    )(*args)


def _dft_tables(L):
    N = 2 * L
    k = np.arange(L, dtype=np.int64)[:, None]
    s = np.arange(L, dtype=np.int64)[None, :]
    ang = ((2 * k + 1) * s % (2 * N)).astype(np.float64) * (2.0 * math.pi / (2 * N))
    return np.cos(ang).astype(np.float32), np.sin(ang).astype(np.float32)


def _dft_blocks(L, kb):
    cm, sm = _dft_tables(L)
    nk = L // kb
    fwd = np.concatenate([cm.reshape(nk, kb, L), sm.reshape(nk, kb, L)], axis=1)
    inv = np.concatenate([cm.T.reshape(L, nk, kb), sm.T.reshape(L, nk, kb)], axis=2)
    inv = np.ascontiguousarray(inv.transpose(1, 0, 2))
    return jnp.asarray(fwd).astype(BF16), jnp.asarray(inv).astype(BF16)


def _hy_spectrum_kernel(f_ref, flt_ref, hc_ref, hs_ref, *, L, kb):
    row = lax.broadcasted_iota(jnp.int32, (L, HY_CH), 0)
    scale = 1.0 / L
    for o in range(2):
        hf = flt_ref[:, (2 * o) * HY_CH:(2 * o + 1) * HY_CH]
        hb = jnp.where(row == 0, 0.0, flt_ref[:, (2 * o + 1) * HY_CH:(2 * o + 2) * HY_CH])
        a_hi, a_lo = _split(hf + hb)
        d_hi, d_lo = _split(hf - hb)
        fc = f_ref[0, :kb, :]
        fs = f_ref[0, kb:, :]
        hc_ref[o] = (_dot(fc, a_hi) + _dot(fc, a_lo)) * scale
        hs_ref[o] = (_dot(fs, d_hi) + _dot(fs, d_lo)) * scale


def _hy_spectrum(filt, fwd, L, kb):
    nk = L // kb
    return pl.pallas_call(
        functools.partial(_hy_spectrum_kernel, L=L, kb=kb),
        grid=(nk,),
        in_specs=[pl.BlockSpec((1, 2 * kb, L), lambda i: (i, 0, 0)),
                  pl.BlockSpec((L, 4 * HY_CH), lambda i: (0, 0))],
        out_specs=[pl.BlockSpec((2, kb, HY_CH), lambda i: (0, i, 0)),
                   pl.BlockSpec((2, kb, HY_CH), lambda i: (0, i, 0))],
        out_shape=[jax.ShapeDtypeStruct((2, L, HY_CH), F32)] * 2,
        compiler_params=_cparams(("arbitrary",)),
        name="hyena_spectrum",
    )(fwd, filt)


def _hy_conv_kernel(zin_ref, xo_ref, gate_ref, cwz_ref, cbz_ref, cwx_ref, cbx_ref, skip_ref,
                    f_ref, g_ref, hc_ref, hs_ref, o_ref, zb_s, acc_s, *, kb, first, last):
    kstep = pl.program_id(2)
    nk = pl.num_programs(2)

    def conv3(x_ref, w_ref, b_ref):
        x = x_ref[...]
        return (_shift_rows(x, -1) * w_ref[0:1, :] + x * w_ref[1:2, :]
                + _shift_rows(x, 1) * w_ref[2:3, :]) + b_ref[...]

    def z_value():
        return conv3(zin_ref, cwz_ref, cbz_ref) if first else zin_ref[...]

    @pl.when(kstep == 0)
    def _():
        zb_s[...] = z_value().astype(BF16)
        acc_s[...] = jnp.zeros_like(acc_s)

    zb = zb_s[...]
    kh = kb // HY_FREQ_SPLIT
    ycs, yss = [], []
    for h in range(HY_FREQ_SPLIT):
        cos_rows = slice(h * kh, (h + 1) * kh)
        sin_rows = slice(kb + h * kh, kb + (h + 1) * kh)
        xc = _dot(f_ref[0, cos_rows, :], zb)
        xs = _dot(f_ref[0, sin_rows, :], zb)
        hc = hc_ref[0, cos_rows, :]
        hs = hs_ref[0, cos_rows, :]
        ycs.append((xc * hc - xs * hs).astype(BF16))
        yss.append((xc * hs + xs * hc).astype(BF16))
    acc_s[...] += _dot(g_ref[0], jnp.concatenate(ycs + yss, axis=0))

    @pl.when(kstep == nk - 1)
    def _():
        xo = conv3(xo_ref, cwx_ref, cbx_ref)
        res = xo * (acc_s[...] + z_value() * skip_ref[...])
        if last:
            res = res * _silu(gate_ref[...])
        o_ref[...] = res


def _hy_conv(zsrc, zcol, z, xcol, gcol, conv_w, conv_b, skip, order, fwd, inv, hc, hs, B, L, first, last):
    T = z.shape[0]
    cb = HY_CH_BLOCK
    ncb = HY_CH // cb
    nk, _, kb2 = inv.shape
    kb = kb2 // 2
    data = lambda off: pl.BlockSpec((L, cb), lambda b, c, k: (b, off + c), pipeline_mode=pl.Buffered(1))
    cwspec = lambda off: pl.BlockSpec((3, cb), lambda b, c, k: (0, off + c))
    cbspec = lambda off: pl.BlockSpec((1, cb), lambda b, c, k: (0, off + c))
    in_specs = [data(zcol), data(xcol), data(gcol),
                cwspec(0), cbspec(0), cwspec((1 + order) * ncb), cbspec((1 + order) * ncb),
                pl.BlockSpec((1, cb), lambda b, c, k: (0, c)),
                pl.BlockSpec((1, 2 * kb, L), lambda b, c, k: (k, 0, 0)),
                pl.BlockSpec((1, L, 2 * kb), lambda b, c, k: (k, 0, 0)),
                pl.BlockSpec((1, kb, cb), lambda b, c, k: (order, k, c)),
                pl.BlockSpec((1, kb, cb), lambda b, c, k: (order, k, c))]
    return pl.pallas_call(
        functools.partial(_hy_conv_kernel, kb=kb, first=first, last=last),
        grid=(B, ncb, nk),
        in_specs=in_specs,
        out_specs=pl.BlockSpec((L, cb), lambda b, c, k: (b, c)),
        out_shape=jax.ShapeDtypeStruct((T, HY_CH), F32),
        scratch_shapes=[pltpu.VMEM((L, cb), BF16), pltpu.VMEM((L, cb), F32)],
        compiler_params=_cparams(("arbitrary", "arbitrary", "arbitrary")),
        name="hyena_conv",
    )(zsrc, z, z, conv_w, conv_b, conv_w, conv_b, skip[order:order + 1], fwd, inv, hc, hs)


OD_GQ, OD_GG, OD_MG, OD_GK, OD_GV, OD_MQ, OD_MKV, OD_MPE = 0, 4, 8, 12, 14, 16, 18, 19
OD_WIDTH = 20 * LANES


def _rope_swap(x, quarter):
    n = x.shape[1]
    lane = lax.broadcasted_iota(jnp.int32, x.shape, 1)
    first = (lane & (2 * quarter - 1)) < quarter
    return jnp.where(first, pltpu.roll(x, n - quarter, 1), pltpu.roll(x, quarter, 1))


def _odd_in_kernel(x_ref, sh_ref, sc_ref, g_ref, w_ref, qg_ref, kg_ref, mqg_ref, mqup_ref, mkvg_ref, wk_ref,
                   *rest, rope):
    if rope:
        cg_ref, sg_ref, cm_ref, sm_ref = rest[:4]
        rest = rest[4:]
    q_out, qc_out, k16_out, v16_out, kc16_out, gate_out, k_out, v_out, ckv_out, kpe_out = rest
    D = HEAD_DIM
    hb = (_rms(x_ref[...], g_ref[...]) * (1.0 + sc_ref[0]) + sh_ref[0]).astype(BF16)

    def proj(col, width):
        return _dot(hb, w_ref[:, col * D:(col + width) * D])

    def rot(x, c, s, quarter):
        return x * c + _rope_swap(x, quarter) * s if rope else x

    gscale = (D ** -0.5) * LOG2E
    zq = proj(OD_GQ, GQA_HEADS)
    for h in range(GQA_HEADS):
        q = rot(_rms(zq[:, h * D:(h + 1) * D], qg_ref[...]), cg_ref[...] if rope else None,
                sg_ref[...] if rope else None, D // 4)
        q_out[:, h * D:(h + 1) * D] = (q * gscale).astype(BF16)
    zk = proj(OD_GK, GQA_KV_HEADS)
    zv = proj(OD_GV, GQA_KV_HEADS)
    v_out[...] = zv
    v16_out[...] = zv.astype(BF16)
    for h in range(GQA_KV_HEADS):
        kn = _rms(zk[:, h * D:(h + 1) * D], kg_ref[...])
        k_out[:, h * D:(h + 1) * D] = kn
        k16_out[:, h * D:(h + 1) * D] = rot(kn, cg_ref[...] if rope else None,
                                            sg_ref[...] if rope else None, D // 4).astype(BF16)
    gate_out[...] = proj(OD_GG, GQA_HEADS + MLA_HEADS)

    mq = _rms(proj(OD_MQ, 2), mqg_ref[...]).astype(BF16)
    qm = _dot(mq, mqup_ref[...])
    mscale = ((D + MLA_ROPE_DIM) ** -0.5) * LOG2E
    qpe = qm[:, MLA_HEADS * D:]
    if rope:
        qpe = rot(qpe, cm_ref[...], sm_ref[...], MLA_ROPE_DIM // 4)
    for h in range(MLA_HEADS):
        qn = qm[:, h * D:(h + 1) * D].astype(BF16)
        qc_out[:, 2 * h * D:(2 * h + 1) * D] = (_dot_nt(qn, wk_ref[h]) * mscale).astype(BF16)
        qc_out[:, (2 * h + 1) * D:(2 * h + 2) * D] = (qpe[:, h * D:(h + 1) * D] * mscale).astype(BF16)
    ckv = _rms(proj(OD_MKV, 1), mkvg_ref[...])
    ckv_out[...] = ckv
    kpe = proj(OD_MPE, 1)
    kpe_out[...] = kpe
    if rope:
        kpe = rot(kpe, cm_ref[:, :D], sm_ref[:, :D], MLA_ROPE_DIM // 4)
    kc16_out[:, :D] = ckv.astype(BF16)
    kc16_out[:, D:] = kpe.astype(BF16)


def _rope_tables(L, R):
    rows = np.arange(L) // GRID_W
    cols = np.arange(L) % GRID_W
    quarter = R // 4
    inv = ROPE_THETA ** (-np.arange(quarter, dtype=np.float32) * 2.0 / (R // 2))
    a_r = rows[:, None].astype(np.float32) * inv[None, :]
    a_c = cols[:, None].astype(np.float32) * inv[None, :]
    cos = np.concatenate([np.cos(a_r), np.cos(a_r), np.cos(a_c), np.cos(a_c)], axis=1)
    sin = np.concatenate([-np.sin(a_r), np.sin(a_r), -np.sin(a_c), np.sin(a_c)], axis=1)
    return cos.astype(np.float32), sin.astype(np.float32)


def _odd_in(x2, shift, scale, pre_g, w_main, q_g, k_g, mq_g, mq_up, mkv_g, wk, L, tiles_per_mod, rope):
    T, dm = x2.shape
    tm = TOKEN_TILE
    D = HEAD_DIM
    if tiles_per_mod:
        mod_map = lambda i: (i // tiles_per_mod, 0, 0)
    else:
        mod_map = lambda i: (0, 0, 0)
    full = lambda shape: pl.BlockSpec(shape, lambda i: tuple(0 for _ in shape))
    in_specs = [pl.BlockSpec((tm, dm), lambda i: (i, 0)),
                pl.BlockSpec((1, 1, dm), mod_map), pl.BlockSpec((1, 1, dm), mod_map),
                full((1, dm)), full(w_main.shape),
                full((1, D)), full((1, D)), full((1, 2 * D)), full(mq_up.shape), full((1, D)), full(wk.shape)]
    args = [x2, shift, scale, pre_g, w_main, q_g, k_g, mq_g, mq_up, mkv_g, wk]
    if rope:
        cg, sg = _rope_tables(L, D)
        cm, sm = _rope_tables(L, MLA_ROPE_DIM)
        widen = lambda t: np.tile(np.concatenate([t, np.zeros_like(t)], axis=1), (1, MLA_HEADS))
        per = L // tm
        pos = lambda w: pl.BlockSpec((tm, w), lambda i: (i % per, 0))
        in_specs += [pos(D), pos(D), pos(MLA_HEADS * D), pos(MLA_HEADS * D)]
        args += [jnp.asarray(cg), jnp.asarray(sg), jnp.asarray(widen(cm)), jnp.asarray(widen(sm))]
    tile = lambda w: pl.BlockSpec((tm, w), lambda i: (i, 0))
    widths = [(4 * D, BF16), (8 * D, BF16), (2 * D, BF16), (2 * D, BF16), (2 * D, BF16), (8 * D, F32),
              (2 * D, F32), (2 * D, F32), (D, F32), (D, F32)]
    return pl.pallas_call(
        functools.partial(_odd_in_kernel, rope=rope),
        grid=(T // tm,),
        in_specs=in_specs,
        out_specs=[tile(w) for w, _ in widths],
        out_shape=[jax.ShapeDtypeStruct((T, w), dt) for w, dt in widths],
        compiler_params=_cparams(("arbitrary",)),
        name="odd_in_proj",
    )(*args)


def _softmax_pv(score_blocks, value_blocks):
    m = score_blocks[0].max(axis=-1, keepdims=True)
    for s in score_blocks[1:]:
        m = jnp.maximum(m, s.max(axis=-1, keepdims=True))
    acc = None
    den = None
    for s, v in zip(score_blocks, value_blocks):
        p = jnp.exp2(s - m)
        d = p.sum(axis=-1, keepdims=True)
        a = _dot(p.astype(BF16), v)
        acc = a if acc is None else acc + a
        den = d if den is None else den + d
    return acc / den


def _gqa_kernel(q_ref, k_ref, v_ref, gate_ref, *rest, cached):
    if cached:
        ck_ref, cv_ref, o_ref = rest
    else:
        (o_ref,) = rest
    D = HEAD_DIM
    group = GQA_HEADS // GQA_KV_HEADS
    q = q_ref[...]
    keys = [k_ref[...]] + ([ck_ref[0, 0]] if cached else [])
    values = [v_ref[...]] + ([cv_ref[0, 0]] if cached else [])
    outs = []
    for g in range(group):
        qh = q[:, g * D:(g + 1) * D]
        outs.append(_softmax_pv([_dot_nt(qh, kk) for kk in keys], values))
    o_ref[...] = jnp.concatenate(outs, axis=1) * _silu(gate_ref[...])


def _gqa(qg, kg, vg, gates, B, L, tq, cache_k=None, cache_v=None, layer_j=0):
    T = qg.shape[0]
    D = HEAD_DIM
    group = GQA_HEADS // GQA_KV_HEADS
    nq = L // tq
    cached = cache_k is not None
    in_specs = [pl.BlockSpec((tq, group * D), lambda b, h, i: (b * nq + i, h)),
                pl.BlockSpec((L, D), lambda b, h, i: (b, h)),
                pl.BlockSpec((L, D), lambda b, h, i: (b, h)),
                pl.BlockSpec((tq, group * D), lambda b, h, i: (b * nq + i, h))]
    args = [qg, kg, vg, gates]
    if cached:
        P = cache_k.shape[2]
        cspec = pl.BlockSpec((1, 1, P, D), lambda b, h, i: (b, layer_j, 0, h))
        in_specs += [cspec, cspec]
        args += [cache_k, cache_v]
    return pl.pallas_call(
        functools.partial(_gqa_kernel, cached=cached),
        grid=(B, GQA_KV_HEADS, nq),
        in_specs=in_specs,
        out_specs=pl.BlockSpec((tq, group * D), lambda b, h, i: (b * nq + i, h)),
        out_shape=jax.ShapeDtypeStruct((T, GQA_HEADS * D), F32),
        compiler_params=_cparams(("arbitrary", "arbitrary", "arbitrary")),
        name="gqa_attention",
    )(*args)


def _mla_kernel(qc_ref, kc_ref, gate_ref, wv_ref, *rest, cached):
    if cached:
        ckc_ref, o_ref = rest
    else:
        (o_ref,) = rest
    D = HEAD_DIM
    H = MLA_HEADS
    qc = qc_ref[...]
    tq = qc.shape[0]
    keys = [kc_ref[...]] + ([ckc_ref[0, 0]] if cached else [])
    per = H // MLA_HEAD_STACKS
    outs = []
    for s in range(MLA_HEAD_STACKS):
        heads = range(s * per, (s + 1) * per)
        qs = jnp.concatenate([qc[:, 2 * h * D:(2 * h + 2) * D] for h in heads], axis=0)
        o = _softmax_pv([_dot_nt(qs, kk) for kk in keys], [kk[:, :D] for kk in keys]).astype(BF16)
        for i, h in enumerate(heads):
            outs.append(_dot(o[i * tq:(i + 1) * tq], wv_ref[h]))
    o_ref[...] = jnp.concatenate(outs, axis=1) * _silu(gate_ref[...])


def _mla(qc, kc, gates, wv, B, L, tq, cache_kc=None, layer_j=0):
    T = qc.shape[0]
    D = HEAD_DIM
    H = MLA_HEADS
    nq = L // tq
    cached = cache_kc is not None
    in_specs = [pl.BlockSpec((tq, 2 * H * D), lambda b, i: (b * nq + i, 0)),
                pl.BlockSpec((L, 2 * D), lambda b, i: (b, 0)),
                pl.BlockSpec((tq, H * D), lambda b, i: (b * nq + i, 1)),
                pl.BlockSpec(wv.shape, lambda b, i: (0, 0, 0))]
    args = [qc, kc, gates, wv]
    if cached:
        P = cache_kc.shape[2]
        in_specs += [pl.BlockSpec((1, 1, P, 2 * D), lambda b, i: (b, layer_j, 0, 0))]
        args += [cache_kc]
    return pl.pallas_call(
        functools.partial(_mla_kernel, cached=cached),
        grid=(B, nq),
        in_specs=in_specs,
        out_specs=pl.BlockSpec((tq, H * D), lambda b, i: (b * nq + i, 0)),
        out_shape=jax.ShapeDtypeStruct((T, H * D), F32),
        compiler_params=_cparams(("arbitrary", "arbitrary")),
        name="mla_attention",
    )(*args)


def _even_weights(even_in_w, gdn_conv_w, gdn_a_log, gdn_dt_bias, hyena_conv_w, hyena_conv_b, j):
    w = even_in_w[j]
    qkv_w = 3 * GDN_HEADS * LANES
    n_ab = 4 * GDN_HEADS
    w_main = jnp.concatenate([w[:, :qkv_w], w[:, qkv_w + n_ab:]], axis=1).astype(BF16)
    w_small = _pad_to(w[:, qkv_w:qkv_w + n_ab], (w.shape[0], LANES)).astype(BF16)
    n_dir = 2 * GDN_HEADS
    alog_row = _pad_to(gdn_a_log[j].reshape(1, n_dir), (1, LANES))
    dtb_row = _pad_to(gdn_dt_bias[j].reshape(1, n_dir), (1, LANES))
    return w_main, w_small, alog_row, dtb_row


def _even_layer(x2, B, L, shift, scale, gate, tiles_per_mod, pre_g, post_g, wts, hy, out_w,
                gdn_conv_w, gdn_norm_g, hy_conv_w, hy_conv_b, hy_skip, state_f, state_b, j):
    w_main, w_small, alog_row, dtb_row = wts
    fwd, inv, hc, hs = hy
    z, zab = _in_proj(x2, shift, scale, pre_g, w_main, w_small, tiles_per_mod)
    oa, s_f, s_b = _gdn(z, zab, gdn_conv_w, gdn_norm_g, alog_row, dtb_row, B, L, state_f, state_b, j)
    ncb = HY_CH // HY_CH_BLOCK
    col = lambda idx: idx * ncb
    z1 = _hy_conv(z, col(4), z, col(5), col(7), hy_conv_w, hy_conv_b, hy_skip, 0, fwd, inv, hc, hs, B, L,
                  first=True, last=False)
    ob = _hy_conv(z1, 0, z, col(6), col(7), hy_conv_w, hy_conv_b, hy_skip, 1, fwd, inv, hc, hs, B, L,
                  first=False, last=True)
    x_new = _out_proj(oa, ob, x2, gate, post_g, out_w, tiles_per_mod)
    return x_new, s_f, s_b


def _odd_weights(odd_in_w, mla_q_up, mla_kv_up, j):
    w = odd_in_w[j]
    D = HEAD_DIM
    o = np.cumsum([0, 4 * D, 2 * D, 2 * D, 4 * D, 2 * D, D, MLA_ROPE_DIM, 4 * D])
    gq, gk, gv, gg, mq, mkv, mpe, mg = [w[:, o[i]:o[i + 1]] for i in range(8)]
    w_main = jnp.concatenate([gq, gg, mg, gk, gv, mq, mkv, _pad_to(mpe, (w.shape[0], D))], axis=1).astype(BF16)
    up = mla_q_up[j].reshape(-1, MLA_HEADS, D + MLA_ROPE_DIM)
    rope_cols = _pad_to(up[:, :, D:], (up.shape[0], MLA_HEADS, D))
    mq_up = jnp.concatenate([up[:, :, :D].reshape(-1, MLA_HEADS * D),
                             rope_cols.reshape(-1, MLA_HEADS * D)], axis=1).astype(BF16)
    kv = mla_kv_up[j].reshape(-1, MLA_HEADS, 2 * D)
    wk = kv[:, :, :D].transpose(1, 0, 2).astype(BF16)
    wv = kv[:, :, D:].transpose(1, 0, 2).astype(BF16)
    return w_main, mq_up, wk, wv


def _odd_layer(x2, B, L, shift, scale, gate, tiles_per_mod, pre_g, post_g, wts, out_w,
               q_g, k_g, mq_g, mkv_g, caches, j):
    w_main, mq_up, wk, wv = wts
    rope = caches is not None
    qg, qc, k16, v16, kc16, gates, kg, v32, ckv, kpe = _odd_in(
        x2, shift, scale, pre_g, w_main, q_g, k_g, mq_g, mq_up, mkv_g, wk, L, tiles_per_mod, rope)
    if rope:
        ck, cv, ckc = caches
        og = _gqa(qg, k16, v16, gates, B, L, GQA_Q_TILE, ck, cv, j)
        om = _mla(qc, kc16, gates, wv, B, L, MLA_Q_TILE, ckc, j)
    else:
        og = _gqa(qg, k16, v16, gates, B, L, GQA_Q_TILE)
        om = _mla(qc, kc16, gates, wv, B, L, MLA_Q_TILE)
    x_new = _out_proj(og, om, x2, gate, post_g, out_w, tiles_per_mod)
    return x_new, (kg, v32, ckv, kpe)


def kernel(x_prompt, x_sample, state_gdn_fwd, state_gdn_bwd, cache_gqa_k, cache_gqa_v, cache_mla_ckv, cache_mla_kpe, c, c_ctx, mod_w, mod_b, pre_norm_g, post_norm_g, even_in_w, gdn_conv_w, gdn_a_log, gdn_dt_bias, gdn_norm_g, hyena_conv_w, hyena_conv_b, hyena_ffn_w1, hyena_ffn_b1, hyena_ffn_w2, hyena_ffn_b2, hyena_ffn_w3, hyena_sin_freq, hyena_bias, even_out_w, odd_in_w, gqa_q_norm_g, gqa_k_norm_g, mla_q_norm_g, mla_q_up, mla_kv_norm_g, mla_kv_up, odd_out_w):
    Bp, Lp, D = x_prompt.shape
    Bs, Ls, _ = x_sample.shape
    depth = mod_w.shape[0]
    xp = x_prompt.reshape(Bp * Lp, D)
    xs = x_sample.reshape(Bs * Ls, D)

    n_cond = 1 + Bs
    rows = -(-n_cond // 8) * 8
    cond = _pad_to(jnp.concatenate([c_ctx[None, :], c], axis=0), (rows, D))
    mod = _modulation(cond, mod_w, mod_b)

    P = cache_gqa_k.shape[2]
    n_odd = cache_gqa_k.shape[1]
    ck = cache_gqa_k.reshape(Bs, n_odd, P, GQA_KV_HEADS * HEAD_DIM).astype(BF16)
    cv = cache_gqa_v.reshape(Bs, n_odd, P, GQA_KV_HEADS * HEAD_DIM).astype(BF16)
    ckc = jnp.concatenate([cache_mla_ckv, _pad_to(cache_mla_kpe, cache_mla_ckv.shape)], axis=-1).astype(BF16)

    dft = {L: _dft_blocks(L, min(HY_FREQ_BLOCK, L)) for L in (Lp, Ls)}
    tpm_s = Ls // TOKEN_TILE

    new_f, new_b, new_gk, new_gv, new_ckv, new_kpe = [], [], [], [], [], []
    for i in range(depth):
        j = i // 2
        m = mod[i]
        sh_p, sc_p, gt_p = [m[0:1, k * D:(k + 1) * D].reshape(1, 1, D) for k in range(3)]
        sh_s, sc_s, gt_s = [m[1:n_cond, k * D:(k + 1) * D].reshape(Bs, 1, D) for k in range(3)]
        pre_g = pre_norm_g[i][None, :]
        post_g = post_norm_g[i][None, :]
        if i % 2 == 0:
            wts = _even_weights(even_in_w, gdn_conv_w, gdn_a_log, gdn_dt_bias, hyena_conv_w, hyena_conv_b, j)
            out_w = even_out_w[j].astype(BF16)
            hy = {}
            for L in (Lp, Ls):
                fwd, inv = dft[L]
                filt = _hy_filters(L, hyena_ffn_w1[j], hyena_ffn_b1[j], hyena_ffn_w2[j], hyena_ffn_b2[j],
                                   hyena_ffn_w3[j], hyena_sin_freq[j])
                hc, hs = _hy_spectrum(filt, fwd, L, min(HY_FREQ_BLOCK, L))
                hy[L] = (fwd, inv, hc, hs)
            common = (gdn_conv_w[j], gdn_norm_g[j][None, :], hyena_conv_w[j], hyena_conv_b[j][None, :], hyena_bias[j])
            xp, sf, sb = _even_layer(xp, Bp, Lp, sh_p, sc_p, gt_p, 0, pre_g, post_g, wts, hy[Lp], out_w,
                                     *common, None, None, j)
            xs, _, _ = _even_layer(xs, Bs, Ls, sh_s, sc_s, gt_s, tpm_s, pre_g, post_g, wts, hy[Ls], out_w,
                                   *common, state_gdn_fwd, state_gdn_bwd, j)
            new_f.append(sf)
            new_b.append(sb)
        else:
            wts = _odd_weights(odd_in_w, mla_q_up, mla_kv_up, j)
            out_w = odd_out_w[j].astype(BF16)
            norms = (gqa_q_norm_g[j][None, :], gqa_k_norm_g[j][None, :], mla_q_norm_g[j][None, :],
                     mla_kv_norm_g[j][None, :])
            xp, (kg, v32, ckv, kpe) = _odd_layer(xp, Bp, Lp, sh_p, sc_p, gt_p, 0, pre_g, post_g, wts, out_w,
                                                 *norms, None, j)
            xs, _ = _odd_layer(xs, Bs, Ls, sh_s, sc_s, gt_s, tpm_s, pre_g, post_g, wts, out_w,
                               *norms, (ck, cv, ckc), j)
            new_gk.append(kg.reshape(Bp, Lp, GQA_KV_HEADS, HEAD_DIM))
            new_gv.append(v32.reshape(Bp, Lp, GQA_KV_HEADS, HEAD_DIM))
            new_ckv.append(ckv.reshape(Bp, Lp, HEAD_DIM))
            new_kpe.append(kpe[:, :MLA_ROPE_DIM].reshape(Bp, Lp, MLA_ROPE_DIM))
    return (xp.reshape(Bp, Lp, D), xs.reshape(Bs, Ls, D),
            jnp.stack(new_f, axis=1), jnp.stack(new_b, axis=1),
            jnp.stack(new_gk, axis=1), jnp.stack(new_gv, axis=1),
            jnp.stack(new_ckv, axis=1), jnp.stack(new_kpe, axis=1))
```

```python
import functools
import math

import numpy as np
import jax
import jax.numpy as jnp
from jax import lax
from jax.experimental import pallas as pl
from jax.experimental.pallas import tpu as pltpu

F32 = jnp.float32
BF16 = jnp.bfloat16
MIX_DTYPE = BF16

NORM_EPS = 1e-6
ROPE_THETA = 10000.0
GRID_W = 64

GDN_HEADS = 4
GDN_DK = 128
GDN_CHUNK = 64
GDN_CONV_W = 5
GDN_PREP_UNROLL = 16
GDN_ALL_HEADS_MAX_LEN = 256
HY_CH = 512
HY_BANDS = 16
HY_DECAY_TARGET = 1e-2
HY_SHORT_DECAY_PCT = 0.3
HY_LONG_DECAY_PCT = 1.5
GQA_HEADS = 4
GQA_KV_HEADS = 2
HEAD_DIM = 128
MLA_HEADS = 4
MLA_ROPE_DIM = 64

LOG2E = math.log2(math.e)

LANES = 128
VMEM_LIMIT = 56 * 1024 * 1024

TOKEN_TILE = 256
HY_FREQ_BLOCK = 512
HY_FREQ_SPLIT = 2
HY_CH_BLOCK = 512
GQA_Q_TILE = 256
MLA_Q_TILE = 128
MLA_HEAD_STACKS = 2


def _cparams(sem):
    return pltpu.CompilerParams(dimension_semantics=sem, vmem_limit_bytes=VMEM_LIMIT)


def _dot(a, b):
    return jnp.dot(a, b, preferred_element_type=F32)


def _dot_nt(a, b):
    return lax.dot_general(a, b, (((1,), (1,)), ((), ())), preferred_element_type=F32)


def _dot_tn(a, b):
    return lax.dot_general(a, b, (((0,), (0,)), ((), ())), preferred_element_type=F32)


def _split(a):
    hi = a.astype(BF16)
    lo = (a - hi.astype(F32)).astype(BF16)
    return hi, lo


def _dot3(a, b):
    ah, al = _split(a)
    bh, bl = _split(b)
    return _dot(ah, bh) + (_dot(ah, bl) + _dot(al, bh))


def _bdot(a, b):
    return jnp.einsum('gij,gjk->gik', a, b, preferred_element_type=F32)


def _bdot16(a, b):
    return _bdot(a.astype(BF16), b.astype(BF16))


def _silu(x):
    return x * (1.0 / (1.0 + jnp.exp(-x)))


def _sigmoid(x):
    return 1.0 / (1.0 + jnp.exp(-x))


def _softplus(x):
    return jnp.maximum(x, 0.0) + jnp.log(1.0 + jnp.exp(-jnp.abs(x)))


def _rms(x, g):
    return x * lax.rsqrt(jnp.mean(x * x, axis=-1, keepdims=True) + NORM_EPS) * g


def _shift_rows(x, s):
    L = x.shape[0]
    if s == 0:
        return x
    rolled = pltpu.roll(x, (-s) % L, 0)
    row = lax.broadcasted_iota(jnp.int32, x.shape, 0)
    valid = (row + s >= 0) & (row + s < L)
    return jnp.where(valid, rolled, 0.0)


def _mod_kernel(c_ref, w_ref, b_ref, o_ref):
    c = _silu(c_ref[...])
    o_ref[0] = _dot3(c, w_ref[0]) + b_ref[0]


def _modulation(cond, mod_w, mod_b):
    depth, d, d3 = mod_w.shape
    r = cond.shape[0]
    nb = d3 // d
    return pl.pallas_call(
        _mod_kernel,
        grid=(depth, nb),
        in_specs=[pl.BlockSpec((r, d), lambda i, n: (0, 0)),
                  pl.BlockSpec((1, d, d), lambda i, n: (i, 0, n)),
                  pl.BlockSpec((1, 1, d), lambda i, n: (i, 0, n))],
        out_specs=pl.BlockSpec((1, r, d), lambda i, n: (i, 0, n)),
        out_shape=jax.ShapeDtypeStruct((depth, r, d3), F32),
        compiler_params=_cparams(("arbitrary", "arbitrary")),
        name="adaln_modulation",
    )(cond, mod_w, mod_b.reshape(depth, 1, d3))


def _in_proj_kernel(x_ref, sh_ref, sc_ref, g_ref, w_ref, *rest, n_main, has_small):
    if has_small:
        ws_ref, o_ref, os_ref = rest
    else:
        (o_ref,) = rest
    x = x_ref[...]
    h = _rms(x, g_ref[...]) * (1.0 + sc_ref[0]) + sh_ref[0]
    hb = h.astype(BF16)
    step = 512
    for n0 in range(0, n_main, step):
        o_ref[:, n0:n0 + step] = _dot(hb, w_ref[:, n0:n0 + step])
    if has_small:
        os_ref[...] = _dot(hb, ws_ref[...])


def _in_proj(x2, shift, scale, pre_g, w_main, w_small, tiles_per_mod):
    T, D = x2.shape
    N = w_main.shape[1]
    tm = TOKEN_TILE
    if tiles_per_mod:
        mod_map = lambda i: (i // tiles_per_mod, 0, 0)
    else:
        mod_map = lambda i: (0, 0, 0)
    in_specs = [pl.BlockSpec((tm, D), lambda i: (i, 0)),
                pl.BlockSpec((1, 1, D), mod_map),
                pl.BlockSpec((1, 1, D), mod_map),
                pl.BlockSpec((1, D), lambda i: (0, 0)),
                pl.BlockSpec((D, N), lambda i: (0, 0))]
    out_specs = [pl.BlockSpec((tm, N), lambda i: (i, 0))]
    out_shape = [jax.ShapeDtypeStruct((T, N), F32)]
    args = [x2, shift, scale, pre_g, w_main]
    if w_small is not None:
        in_specs.append(pl.BlockSpec((D, LANES), lambda i: (0, 0)))
        out_specs.append(pl.BlockSpec((tm, LANES), lambda i: (i, 0)))
        out_shape.append(jax.ShapeDtypeStruct((T, LANES), F32))
        args.append(w_small)
    res = pl.pallas_call(
        functools.partial(_in_proj_kernel, n_main=N, has_small=w_small is not None),
        grid=(T // tm,),
        in_specs=in_specs, out_specs=out_specs, out_shape=out_shape,
        compiler_params=_cparams(("arbitrary",)),
        name="in_proj",
    )(*args)
    return res


def _out_proj_kernel(a_ref, b_ref, x_ref, gt_ref, g_ref, w_ref, o_ref, *, half):
    y = _dot(a_ref[...].astype(BF16), w_ref[:half, :]) + _dot(b_ref[...].astype(BF16), w_ref[half:, :])
    o_ref[...] = x_ref[...] + gt_ref[0] * _rms(y, g_ref[...])


def _out_proj(a, b, x2, gate, post_g, w, tiles_per_mod):
    T, D = x2.shape
    half = a.shape[1]
    tm = TOKEN_TILE
    if tiles_per_mod:
        mod_map = lambda i: (i // tiles_per_mod, 0, 0)
    else:
        mod_map = lambda i: (0, 0, 0)
    return pl.pallas_call(
        functools.partial(_out_proj_kernel, half=half),
        grid=(T // tm,),
        in_specs=[pl.BlockSpec((tm, half), lambda i: (i, 0)),
                  pl.BlockSpec((tm, half), lambda i: (i, 0)),
                  pl.BlockSpec((tm, D), lambda i: (i, 0)),
                  pl.BlockSpec((1, 1, D), mod_map),
                  pl.BlockSpec((1, D), lambda i: (0, 0)),
                  pl.BlockSpec((2 * half, D), lambda i: (0, 0))],
        out_specs=pl.BlockSpec((tm, D), lambda i: (i, 0)),
        out_shape=jax.ShapeDtypeStruct((T, D), F32),
        compiler_params=_cparams(("arbitrary",)),
        name="out_proj",
    )(a, b, x2, gate, post_g, w)


def _gdn_kernel(zq_ref, zk_ref, zv_ref, zg_ref, zab_ref, cwq_ref, cwk_ref, cwv_ref, ng_ref,
                alog_ref, dtb_ref, *rest, L, HB, has_state):
    if has_state:
        s0f_ref, s0b_ref = rest[:2]
        rest = rest[2:]
    (o_ref, sf_out_ref, sb_out_ref, q_s, k_s, v_s, g_s, b_s, o_s, st_s,
     oc_s, qe_s, sc_s, sm_s, gl_s) = rest
    C = GDN_CHUNK
    n_chunks = L // C
    U = min(GDN_PREP_UNROLL, n_chunks)
    HU = HB * U
    G = 2 * HU
    head0 = pl.program_id(1) * HB

    def conv_silu(x, w_ref, lanes):
        half = GDN_CONV_W // 2
        acc = x * w_ref[half:half + 1, lanes]
        for i in range(GDN_CONV_W):
            if i != half:
                acc = acc + _shift_rows(x, i - half) * w_ref[i:i + 1, lanes]
        return _silu(acc)

    def l2n(x):
        return x * lax.rsqrt(jnp.sum(x * x, axis=-1, keepdims=True) + NORM_EPS)

    for hh in range(HB):
        lanes = slice(hh * LANES, (hh + 1) * LANES)
        q_s[hh] = l2n(conv_silu(zq_ref[:, lanes], cwq_ref, lanes)) * (GDN_DK ** -0.5)
        k_s[hh] = l2n(conv_silu(zk_ref[:, lanes], cwk_ref, lanes))
        v_s[hh] = conv_silu(zv_ref[:, lanes], cwv_ref, lanes)
        if has_state:
            st_s[2 * hh] = s0f_ref[0, 0, hh]
            st_s[2 * hh + 1] = s0b_ref[0, 0, hh]
    if not has_state:
        st_s[...] = jnp.zeros_like(st_s)
    zab = zab_ref[...]
    g_s[...] = -jnp.exp(alog_ref[...]) * _softplus(zab + dtb_ref[...])
    b_s[...] = _sigmoid(zab)

    def iota(shape, axis):
        return lax.broadcasted_iota(jnp.int32, shape, axis)

    def direction(shape):
        return iota(shape, 0) & 1

    chain_shift = (2 * U).bit_length() - 1
    assert 2 * U == 1 << chain_shift

    def chain_head(shape):
        return head0 + (iota(shape, 0) >> chain_shift)

    sq = (G, C, C)
    row = iota(sq, 1)
    col = iota(sq, 2)
    signed = (row - col) * (1 - 2 * direction(sq))
    incl = signed >= 0
    strict = signed > 0
    eye = (row == col).astype(F32)
    same = [(row >> s) == (col >> s) for s in (3, 4, 5)]
    off_blocks = [same[1] & jnp.logical_not(same[0]), same[2] & jnp.logical_not(same[1]),
                  jnp.logical_not(same[2])]
    wide = (G, C, LANES)
    sel_lane = direction(wide) * GDN_HEADS + chain_head(wide)
    mask_g = iota(wide, 2) == sel_lane
    mask_b = iota(wide, 2) == sel_lane + 2 * GDN_HEADS
    tall = (G, LANES, C)
    mask_t = iota(tall, 1) == direction(tall) * GDN_HEADS + chain_head(tall)
    colv = (G, C, 1)
    mask_last = iota(colv, 1) == (1 - direction(colv)) * (C - 1)
    r2 = lax.broadcasted_iota(jnp.int32, (C, C), 0)
    c2 = lax.broadcasted_iota(jnp.int32, (C, C), 1)
    tri = jnp.concatenate([(r2 >= c2).astype(BF16), (r2 <= c2).astype(BF16)], axis=0)

    def both_dirs(x):
        return jnp.broadcast_to(x[:, None], (HU, 2) + x.shape[1:]).reshape((G,) + x.shape[1:])

    def all_heads(x):
        return jnp.broadcast_to(x[None], (HB,) + x.shape).reshape((HB * x.shape[0],) + x.shape[1:])

    def prep_group(c, carry):
        rows = pl.ds(pl.multiple_of(c * (U * C), U * C), U * C)
        q = q_s[:, rows, :].reshape(HU, C, LANES)
        k = k_s[:, rows, :].reshape(HU, C, LANES)
        v = v_s[:, rows, :].reshape(HU, C, LANES)
        ball = all_heads(b_s[rows, :].reshape(U, C, LANES))
        g_hi, g_lo = _split(g_s[rows, :])
        g_hi = g_hi.reshape(U, C, LANES)
        g_lo = g_lo.reshape(U, C, LANES)
        gcum = jnp.stack([_dot(tri, g_hi[u]) + _dot(tri, g_lo[u]) for u in range(U)])
        gcum = gcum.reshape(2 * U, C, LANES)
        gcum_t = all_heads(jnp.stack([gcum[g].T for g in range(2 * U)]))
        gcum = all_heads(gcum)
        gc = jnp.sum(jnp.where(mask_g, gcum, 0.0), axis=2, keepdims=True)
        gr = jnp.sum(jnp.where(mask_t, gcum_t, 0.0), axis=1, keepdims=True)
        beta = jnp.sum(jnp.where(mask_b, both_dirs(ball), 0.0), axis=2, keepdims=True)
        kbf = k.astype(BF16)
        kq = jnp.einsum('uik,ujk->uij', jnp.concatenate([k, q], axis=1).astype(BF16), kbf,
                        preferred_element_type=F32)
        kk = both_dirs(kq[:, :C])
        qk = both_dirs(kq[:, C:])
        decay = jnp.where(incl, jnp.exp(jnp.where(incl, gc - gr, 0.0)), 0.0)
        lmat = jnp.where(strict, beta * kk * decay, 0.0)
        p = jnp.where(same[0], -lmat, 0.0)
        tmat = eye + p
        for _ in range(2):
            p = _bdot16(p, p)
            tmat = tmat + _bdot16(tmat, p)
        for off in off_blocks:
            t16 = tmat.astype(BF16)
            tmat = tmat - _bdot(_bdot(t16, jnp.where(off, lmat, 0.0).astype(BF16)).astype(BF16), t16)
        eg = jnp.exp(gc)
        k2 = both_dirs(k)
        kb = k2 * beta
        sol = _bdot16(tmat, jnp.concatenate([both_dirs(v) * beta, kb * eg], axis=2))
        glast = jnp.sum(jnp.where(mask_last, gc, 0.0), axis=1, keepdims=True)
        kdec = k2 * jnp.exp(glast - gc)
        kdec_t = jnp.stack([kdec[g].T for g in range(G)]).astype(BF16)
        sol16 = sol.astype(BF16)
        r_in = _bdot((qk * decay).astype(BF16), sol16)
        r_kd = _bdot(kdec_t, sol16)
        def put(ref, val, size):
            n = 2 * U * size
            ref[:, pl.ds(pl.multiple_of(c * n, n), n), :] = val.reshape(HB, n, val.shape[-1])

        put(oc_s, r_in[:, :, :LANES], C)
        put(qe_s, (both_dirs(q) * eg - r_in[:, :, LANES:]).astype(BF16), C)
        put(sc_s, r_kd[:, :, :LANES], LANES)
        put(sm_s, (-r_kd[:, :, LANES:]).astype(BF16), LANES)
        put(gl_s, jnp.broadcast_to(jnp.exp(glast), (G, 8, LANES)), 8)
        return carry

    lax.fori_loop(0, n_chunks // U, prep_group, 0)

    def scan_step(c, carry):
        slot_f = c * 2
        slot_b = (n_chunks - 1 - c) * 2 + 1

        def ld(ref, size):
            parts = []
            for hh in range(HB):
                parts.append(ref[hh, pl.ds(pl.multiple_of(slot_f * size, size), size), :])
                parts.append(ref[hh, pl.ds(pl.multiple_of(slot_b * size, size), size), :])
            return jnp.stack(parts)

        s = st_s[...]
        sb16 = s.astype(BF16)
        o = _bdot(ld(qe_s, C), sb16) + ld(oc_s, C)
        st_s[...] = s * ld(gl_s, 8)[:, 0:1, :] + (_bdot(ld(sm_s, LANES), sb16) + ld(sc_s, LANES))
        o_s[:, pl.ds(pl.multiple_of(c * (2 * C), 2 * C), 2 * C), :] = o.reshape(HB, 2 * C, LANES)
        return carry

    lax.fori_loop(0, n_chunks, scan_step, 0)

    FIN = 4

    def finish(i, carry):
        rows = pl.ds(pl.multiple_of(i * (FIN * C), FIN * C), FIN * C)
        outs = []
        for hh in range(HB):
            parts = []
            for j in range(FIN):
                c = i * FIN + j
                of = o_s[hh, pl.ds(pl.multiple_of(c * (2 * C), C), C), :]
                ob = o_s[hh, pl.ds(pl.multiple_of((n_chunks - 1 - c) * (2 * C) + C, C), C), :]
                parts.append(of + ob)
            outs.append(_rms(jnp.concatenate(parts, axis=0), ng_ref[...]))
        o = jnp.concatenate(outs, axis=1) * _silu(zg_ref[rows, :])
        o_ref[rows, :] = o.astype(o_ref.dtype)
        return carry

    lax.fori_loop(0, n_chunks // FIN, finish, 0)
    for hh in range(HB):
        sf_out_ref[0, hh] = st_s[2 * hh]
        sb_out_ref[0, hh] = st_s[2 * hh + 1]


def _gdn(z, zab, conv_w, norm_g, alog_row, dtb_row, B, L, state_f=None, state_b=None, layer_j=0):
    T = z.shape[0]
    H = GDN_HEADS
    has_state = state_f is not None
    HB = H if L <= GDN_ALL_HEADS_MAX_LEN else 1
    nh = H // HB
    w = HB * LANES
    blk = lambda i: pl.BlockSpec((L, w), lambda b, h: (b, i * nh + h))
    cw = lambda i: pl.BlockSpec((GDN_CONV_W, w), lambda b, h: (0, i * nh + h))
    in_specs = [blk(0), blk(1), blk(2), blk(3),
                pl.BlockSpec((L, LANES), lambda b, h: (b, 0)),
                cw(0), cw(1), cw(2),
                pl.BlockSpec((1, LANES), lambda b, h: (0, 0)),
                pl.BlockSpec((1, LANES), lambda b, h: (0, 0)),
                pl.BlockSpec((1, LANES), lambda b, h: (0, 0))]
    args = [z, z, z, z, zab, conv_w, conv_w, conv_w, norm_g, alog_row, dtb_row]
    if has_state:
        st = pl.BlockSpec((1, 1, HB, GDN_DK, LANES), lambda b, h: (b, layer_j, h, 0, 0))
        in_specs += [st, st]
        args += [state_f, state_b]
    sout = pl.BlockSpec((1, HB, GDN_DK, LANES), lambda b, h: (b, h, 0, 0))
    n_slots = 2 * (L // GDN_CHUNK)
    scratch = ([pltpu.VMEM((HB, L, LANES), F32) for _ in range(3)]
               + [pltpu.VMEM((L, LANES), F32) for _ in range(2)]
               + [pltpu.VMEM((HB, 2 * L, LANES), F32), pltpu.VMEM((2 * HB, GDN_DK, LANES), F32)]
               + [pltpu.VMEM((HB, 2 * L, LANES), F32), pltpu.VMEM((HB, 2 * L, LANES), BF16)]
               + [pltpu.VMEM((HB, n_slots * GDN_DK, LANES), F32),
                  pltpu.VMEM((HB, n_slots * GDN_DK, LANES), BF16)]
               + [pltpu.VMEM((HB, n_slots * 8, LANES), F32)])
    return pl.pallas_call(
        functools.partial(_gdn_kernel, L=L, HB=HB, has_state=has_state),
        grid=(B, nh),
        in_specs=in_specs,
        out_specs=[pl.BlockSpec((L, w), lambda b, h: (b, h)), sout, sout],
        out_shape=[jax.ShapeDtypeStruct((T, H * LANES), MIX_DTYPE),
                   jax.ShapeDtypeStruct((B, H, GDN_DK, LANES), F32),
                   jax.ShapeDtypeStruct((B, H, GDN_DK, LANES), F32)],
        scratch_shapes=scratch,
        compiler_params=_cparams(("arbitrary", "arbitrary")),
        name="gdn",
    )(*args)


def _hy_filter_kernel(f_ref, w1_ref, b1_ref, sf0_ref, w2_ref, b2_ref, sf1_ref, w3_ref, dl_ref, o_ref, *, L):
    rowi = lax.broadcasted_iota(jnp.int32, (L, LANES), 0).astype(F32)
    lane = lax.broadcasted_iota(jnp.int32, (L, LANES), 1)
    t = rowi * (1.0 / (L - 1))
    wpos = rowi * (2.0 * math.pi / L)
    ang = f_ref[...] * wpos
    z = jnp.where(lane == 0, t,
                  jnp.where(lane <= HY_BANDS, jnp.cos(ang),
                            jnp.where(lane <= 2 * HY_BANDS, -jnp.sin(ang), 0.0)))
    h = jnp.sin(sf0_ref[...] * (_dot3(z, w1_ref[...]) + b1_ref[...]))
    h = jnp.sin(sf1_ref[...] * (_dot3(h, w2_ref[...]) + b2_ref[...]))
    tc = lax.broadcasted_iota(jnp.int32, (L, HY_CH), 0).astype(F32) * (1.0 / (L - 1))
    window = jnp.exp(-tc * dl_ref[...])
    for j in range(4):
        o_ref[:, j * HY_CH:(j + 1) * HY_CH] = _dot3(h, w3_ref[:, j * HY_CH:(j + 1) * HY_CH]) * window


def _pad_to(a, shape):
    return jnp.pad(a, [(0, s - d) for s, d in zip(shape, a.shape)])


def _hy_filters(L, w1, b1, w2, b2, w3, sin_freq):
    fvals = np.linspace(1e-4, HY_BANDS - 1, HY_BANDS, dtype=np.float32)
    frow = np.zeros((1, LANES), np.float32)
    frow[0, 1:1 + HY_BANDS] = fvals
    frow[0, 1 + HY_BANDS:1 + 2 * HY_BANDS] = fvals
    deltas = np.abs(np.linspace(math.log(HY_DECAY_TARGET) / HY_LONG_DECAY_PCT,
                                math.log(HY_DECAY_TARGET) / HY_SHORT_DECAY_PCT, HY_CH, dtype=np.float32))
    args = [jnp.asarray(frow),
            _pad_to(w1, (LANES, LANES)), _pad_to(b1[None, :], (1, LANES)), _pad_to(sin_freq[0][None, :], (1, LANES)),
            _pad_to(w2, (LANES, LANES)), _pad_to(b2[None, :], (1, LANES)), _pad_to(sin_freq[1][None, :], (1, LANES)),
            _pad_to(w3, (LANES, 4 * HY_CH)), jnp.asarray(deltas[None, :])]
    return pl.pallas_call(
        functools.partial(_hy_filter_kernel, L=L),
        out_shape=jax.ShapeDtypeStruct((L, 4 * HY_CH), F32),
        compiler_params=pltpu.CompilerParams(vmem_limit_bytes=VMEM_LIMIT),
        name="hyena_filters",
    )(*args)


def _dft_tables(L):
    N = 2 * L
    k = np.arange(L, dtype=np.int64)[:, None]
    s = np.arange(L, dtype=np.int64)[None, :]
    ang = ((2 * k + 1) * s % (2 * N)).astype(np.float64) * (2.0 * math.pi / (2 * N))
    return np.cos(ang).astype(np.float32), np.sin(ang).astype(np.float32)


def _dft_blocks(L, kb):
    cm, sm = _dft_tables(L)
    nk = L // kb
    fwd = np.concatenate([cm.reshape(nk, kb, L), sm.reshape(nk, kb, L)], axis=1)
    inv = np.concatenate([cm.T.reshape(L, nk, kb), sm.T.reshape(L, nk, kb)], axis=2)
    inv = np.ascontiguousarray(inv.transpose(1, 0, 2))
    return jnp.asarray(fwd).astype(BF16), jnp.asarray(inv).astype(BF16)


def _hy_spectrum_kernel(f_ref, flt_ref, hc_ref, hs_ref, *, L, kb):
    row = lax.broadcasted_iota(jnp.int32, (L, HY_CH), 0)
    scale = 1.0 / L
    for o in range(2):
        hf = flt_ref[:, (2 * o) * HY_CH:(2 * o + 1) * HY_CH]
        hb = jnp.where(row == 0, 0.0, flt_ref[:, (2 * o + 1) * HY_CH:(2 * o + 2) * HY_CH])
        a_hi, a_lo = _split(hf + hb)
        d_hi, d_lo = _split(hf - hb)
        fc = f_ref[0, :kb, :]
        fs = f_ref[0, kb:, :]
        hc_ref[o] = (_dot(fc, a_hi) + _dot(fc, a_lo)) * scale
        hs_ref[o] = (_dot(fs, d_hi) + _dot(fs, d_lo)) * scale


def _hy_spectrum(filt, fwd, L, kb):
    nk = L // kb
    return pl.pallas_call(
        functools.partial(_hy_spectrum_kernel, L=L, kb=kb),
        grid=(nk,),
        in_specs=[pl.BlockSpec((1, 2 * kb, L), lambda i: (i, 0, 0)),
                  pl.BlockSpec((L, 4 * HY_CH), lambda i: (0, 0))],
        out_specs=[pl.BlockSpec((2, kb, HY_CH), lambda i: (0, i, 0)),
                   pl.BlockSpec((2, kb, HY_CH), lambda i: (0, i, 0))],
        out_shape=[jax.ShapeDtypeStruct((2, L, HY_CH), F32)] * 2,
        compiler_params=_cparams(("arbitrary",)),
        name="hyena_spectrum",
    )(fwd, filt)


def _hy_conv_kernel(zin_ref, xo_ref, gate_ref, cwz_ref, cbz_ref, cwx_ref, cbx_ref, skip_ref,
                    f_ref, g_ref, hc_ref, hs_ref, o_ref, zb_s, acc_s, *, kb, first, last):
    kstep = pl.program_id(2)
    nk = pl.num_programs(2)

    def conv3(x_ref, w_ref, b_ref):
        x = x_ref[...]
        return (_shift_rows(x, -1) * w_ref[0:1, :] + x * w_ref[1:2, :]
                + _shift_rows(x, 1) * w_ref[2:3, :]) + b_ref[...]

    def z_value():
        return conv3(zin_ref, cwz_ref, cbz_ref) if first else zin_ref[...]

    @pl.when(kstep == 0)
    def _():
        zb_s[...] = z_value().astype(BF16)
        acc_s[...] = jnp.zeros_like(acc_s)

    zb = zb_s[...]
    kh = kb // HY_FREQ_SPLIT
    ycs, yss = [], []
    for h in range(HY_FREQ_SPLIT):
        cos_rows = slice(h * kh, (h + 1) * kh)
        sin_rows = slice(kb + h * kh, kb + (h + 1) * kh)
        xc = _dot(f_ref[0, cos_rows, :], zb)
        xs = _dot(f_ref[0, sin_rows, :], zb)
        hc = hc_ref[0, cos_rows, :]
        hs = hs_ref[0, cos_rows, :]
        ycs.append((xc * hc - xs * hs).astype(BF16))
        yss.append((xc * hs + xs * hc).astype(BF16))
    acc_s[...] += _dot(g_ref[0], jnp.concatenate(ycs + yss, axis=0))

    @pl.when(kstep == nk - 1)
    def _():
        xo = conv3(xo_ref, cwx_ref, cbx_ref)
        res = xo * (acc_s[...] + z_value() * skip_ref[...])
        if last:
            res = res * _silu(gate_ref[...])
        o_ref[...] = res.astype(o_ref.dtype)


def _hy_conv(zsrc, zcol, z, xcol, gcol, conv_w, conv_b, skip, order, fwd, inv, hc, hs, B, L, first, last):
    T = z.shape[0]
    cb = HY_CH_BLOCK
    ncb = HY_CH // cb
    nk, _, kb2 = inv.shape
    kb = kb2 // 2
    data = lambda off, **kw: pl.BlockSpec((L, cb), lambda b, c, k: (b, off + c), **kw)
    late = dict(pipeline_mode=pl.Buffered(1))
    cwspec = lambda off: pl.BlockSpec((3, cb), lambda b, c, k: (0, off + c))
    cbspec = lambda off: pl.BlockSpec((1, cb), lambda b, c, k: (0, off + c))
    in_specs = [data(zcol), data(xcol, **late), data(gcol, **late),
                cwspec(0), cbspec(0), cwspec((1 + order) * ncb), cbspec((1 + order) * ncb),
                pl.BlockSpec((1, cb), lambda b, c, k: (0, c)),
                pl.BlockSpec((1, 2 * kb, L), lambda b, c, k: (k, 0, 0)),
                pl.BlockSpec((1, L, 2 * kb), lambda b, c, k: (k, 0, 0)),
                pl.BlockSpec((1, kb, cb), lambda b, c, k: (order, k, c)),
                pl.BlockSpec((1, kb, cb), lambda b, c, k: (order, k, c))]
    return pl.pallas_call(
        functools.partial(_hy_conv_kernel, kb=kb, first=first, last=last),
        grid=(B, ncb, nk),
        in_specs=in_specs,
        out_specs=pl.BlockSpec((L, cb), lambda b, c, k: (b, c)),
        out_shape=jax.ShapeDtypeStruct((T, HY_CH), MIX_DTYPE if last else F32),
        scratch_shapes=[pltpu.VMEM((L, cb), BF16), pltpu.VMEM((L, cb), F32)],
        compiler_params=_cparams(("arbitrary", "arbitrary", "arbitrary")),
        name="hyena_conv",
    )(zsrc, z, z, conv_w, conv_b, conv_w, conv_b, skip[order:order + 1], fwd, inv, hc, hs)


OD_GQ, OD_GG, OD_MG, OD_GK, OD_GV, OD_MQ, OD_MKV, OD_MPE = 0, 4, 8, 12, 14, 16, 18, 19
OD_WIDTH = 20 * LANES


def _rope_swap(x, quarter):
    n = x.shape[1]
    lane = lax.broadcasted_iota(jnp.int32, x.shape, 1)
    first = (lane & (2 * quarter - 1)) < quarter
    return jnp.where(first, pltpu.roll(x, n - quarter, 1), pltpu.roll(x, quarter, 1))


def _odd_in_kernel(x_ref, sh_ref, sc_ref, g_ref, w_ref, qg_ref, kg_ref, mqg_ref, mqup_ref, mkvg_ref, wk_ref,
                   *rest, rope):
    if rope:
        cg_ref, sg_ref, cm_ref, sm_ref = rest[:4]
        rest = rest[4:]
    q_out, qc_out, k16_out, v16_out, kc16_out, gate_out, k_out, v_out, ckv_out, kpe_out = rest
    D = HEAD_DIM
    hb = (_rms(x_ref[...], g_ref[...]) * (1.0 + sc_ref[0]) + sh_ref[0]).astype(BF16)

    def proj(col, width):
        return _dot(hb, w_ref[:, col * D:(col + width) * D])

    def rot(x, c, s, quarter):
        return x * c + _rope_swap(x, quarter) * s if rope else x

    gscale = (D ** -0.5) * LOG2E
    zq = proj(OD_GQ, GQA_HEADS)
    for h in range(GQA_HEADS):
        q = rot(_rms(zq[:, h * D:(h + 1) * D], qg_ref[...]), cg_ref[...] if rope else None,
                sg_ref[...] if rope else None, D // 4)
        q_out[:, h * D:(h + 1) * D] = (q * gscale).astype(BF16)
    zk = proj(OD_GK, GQA_KV_HEADS)
    zv = proj(OD_GV, GQA_KV_HEADS)
    v_out[...] = zv
    v16_out[...] = zv.astype(BF16)
    for h in range(GQA_KV_HEADS):
        kn = _rms(zk[:, h * D:(h + 1) * D], kg_ref[...])
        k_out[:, h * D:(h + 1) * D] = kn
        k16_out[:, h * D:(h + 1) * D] = rot(kn, cg_ref[...] if rope else None,
                                            sg_ref[...] if rope else None, D // 4).astype(BF16)
    gate_out[...] = proj(OD_GG, GQA_HEADS + MLA_HEADS)

    mq = _rms(proj(OD_MQ, 2), mqg_ref[...]).astype(BF16)
    qm = _dot(mq, mqup_ref[...])
    mscale = ((D + MLA_ROPE_DIM) ** -0.5) * LOG2E
    qpe = qm[:, MLA_HEADS * D:]
    if rope:
        qpe = rot(qpe, cm_ref[...], sm_ref[...], MLA_ROPE_DIM // 4)
    for h in range(MLA_HEADS):
        qn = qm[:, h * D:(h + 1) * D].astype(BF16)
        qc_out[:, 2 * h * D:(2 * h + 1) * D] = (_dot_nt(qn, wk_ref[h]) * mscale).astype(BF16)
        qc_out[:, (2 * h + 1) * D:(2 * h + 2) * D] = (qpe[:, h * D:(h + 1) * D] * mscale).astype(BF16)
    ckv = _rms(proj(OD_MKV, 1), mkvg_ref[...])
    ckv_out[...] = ckv
    kpe = proj(OD_MPE, 1)
    kpe_out[...] = kpe
    if rope:
        kpe = rot(kpe, cm_ref[:, :D], sm_ref[:, :D], MLA_ROPE_DIM // 4)
    kc16_out[:, :D] = ckv.astype(BF16)
    kc16_out[:, D:] = kpe.astype(BF16)


def _rope_tables(L, R):
    rows = np.arange(L) // GRID_W
    cols = np.arange(L) % GRID_W
    quarter = R // 4
    inv = ROPE_THETA ** (-np.arange(quarter, dtype=np.float32) * 2.0 / (R // 2))
    a_r = rows[:, None].astype(np.float32) * inv[None, :]
    a_c = cols[:, None].astype(np.float32) * inv[None, :]
    cos = np.concatenate([np.cos(a_r), np.cos(a_r), np.cos(a_c), np.cos(a_c)], axis=1)
    sin = np.concatenate([-np.sin(a_r), np.sin(a_r), -np.sin(a_c), np.sin(a_c)], axis=1)
    return cos.astype(np.float32), sin.astype(np.float32)


def _odd_in(x2, shift, scale, pre_g, w_main, q_g, k_g, mq_g, mq_up, mkv_g, wk, L, tiles_per_mod, rope):
    T, dm = x2.shape
    tm = TOKEN_TILE
    D = HEAD_DIM
    if tiles_per_mod:
        mod_map = lambda i: (i // tiles_per_mod, 0, 0)
    else:
        mod_map = lambda i: (0, 0, 0)
    full = lambda shape: pl.BlockSpec(shape, lambda i: tuple(0 for _ in shape))
    in_specs = [pl.BlockSpec((tm, dm), lambda i: (i, 0)),
                pl.BlockSpec((1, 1, dm), mod_map), pl.BlockSpec((1, 1, dm), mod_map),
                full((1, dm)), full(w_main.shape),
                full((1, D)), full((1, D)), full((1, 2 * D)), full(mq_up.shape), full((1, D)), full(wk.shape)]
    args = [x2, shift, scale, pre_g, w_main, q_g, k_g, mq_g, mq_up, mkv_g, wk]
    if rope:
        cg, sg = _rope_tables(L, D)
        cm, sm = _rope_tables(L, MLA_ROPE_DIM)
        widen = lambda t: np.tile(np.concatenate([t, np.zeros_like(t)], axis=1), (1, MLA_HEADS))
        per = L // tm
        pos = lambda w: pl.BlockSpec((tm, w), lambda i: (i % per, 0))
        in_specs += [pos(D), pos(D), pos(MLA_HEADS * D), pos(MLA_HEADS * D)]
        args += [jnp.asarray(cg), jnp.asarray(sg), jnp.asarray(widen(cm)), jnp.asarray(widen(sm))]
    tile = lambda w: pl.BlockSpec((tm, w), lambda i: (i, 0))
    widths = [(4 * D, BF16), (8 * D, BF16), (2 * D, BF16), (2 * D, BF16), (2 * D, BF16), (8 * D, F32),
              (2 * D, F32), (2 * D, F32), (D, F32), (D, F32)]
    return pl.pallas_call(
        functools.partial(_odd_in_kernel, rope=rope),
        grid=(T // tm,),
        in_specs=in_specs,
        out_specs=[tile(w) for w, _ in widths],
        out_shape=[jax.ShapeDtypeStruct((T, w), dt) for w, dt in widths],
        compiler_params=_cparams(("arbitrary",)),
        name="odd_in_proj",
    )(*args)


def _softmax_pv(score_blocks, value_blocks):
    m = score_blocks[0].max(axis=-1, keepdims=True)
    for s in score_blocks[1:]:
        m = jnp.maximum(m, s.max(axis=-1, keepdims=True))
    acc = None
    den = None
    for s, v in zip(score_blocks, value_blocks):
        p = jnp.exp2(s - m)
        d = p.sum(axis=-1, keepdims=True)
        a = _dot(p.astype(BF16), v)
        acc = a if acc is None else acc + a
        den = d if den is None else den + d
    return acc / den


def _gqa_kernel(q_ref, k_ref, v_ref, gate_ref, *rest, cached):
    if cached:
        ck_ref, cv_ref, o_ref = rest
    else:
        (o_ref,) = rest
    D = HEAD_DIM
    group = GQA_HEADS // GQA_KV_HEADS
    q = q_ref[...]
    keys = [k_ref[...]] + ([ck_ref[0, 0]] if cached else [])
    values = [v_ref[...]] + ([cv_ref[0, 0]] if cached else [])
    outs = []
    for g in range(group):
        qh = q[:, g * D:(g + 1) * D]
        outs.append(_softmax_pv([_dot_nt(qh, kk) for kk in keys], values))
    o_ref[...] = (jnp.concatenate(outs, axis=1) * _silu(gate_ref[...])).astype(o_ref.dtype)


def _gqa(qg, kg, vg, gates, B, L, tq, cache_k=None, cache_v=None, layer_j=0):
    T = qg.shape[0]
    D = HEAD_DIM
    group = GQA_HEADS // GQA_KV_HEADS
    nq = L // tq
    cached = cache_k is not None
    in_specs = [pl.BlockSpec((tq, group * D), lambda b, h, i: (b * nq + i, h)),
                pl.BlockSpec((L, D), lambda b, h, i: (b, h)),
                pl.BlockSpec((L, D), lambda b, h, i: (b, h)),
                pl.BlockSpec((tq, group * D), lambda b, h, i: (b * nq + i, h))]
    args = [qg, kg, vg, gates]
    if cached:
        P = cache_k.shape[2]
        cspec = pl.BlockSpec((1, 1, P, D), lambda b, h, i: (b, layer_j, 0, h))
        in_specs += [cspec, cspec]
        args += [cache_k, cache_v]
    return pl.pallas_call(
        functools.partial(_gqa_kernel, cached=cached),
        grid=(B, GQA_KV_HEADS, nq),
        in_specs=in_specs,
        out_specs=pl.BlockSpec((tq, group * D), lambda b, h, i: (b * nq + i, h)),
        out_shape=jax.ShapeDtypeStruct((T, GQA_HEADS * D), MIX_DTYPE),
        compiler_params=_cparams(("arbitrary", "arbitrary", "arbitrary")),
        name="gqa_attention",
    )(*args)


def _mla_kernel(qc_ref, kc_ref, gate_ref, wv_ref, *rest, cached):
    if cached:
        ckc_ref, o_ref = rest
    else:
        (o_ref,) = rest
    D = HEAD_DIM
    H = MLA_HEADS
    qc = qc_ref[...]
    tq = qc.shape[0]
    keys = [kc_ref[...]] + ([ckc_ref[0, 0]] if cached else [])
    per = H // MLA_HEAD_STACKS
    outs = []
    for s in range(MLA_HEAD_STACKS):
        heads = range(s * per, (s + 1) * per)
        qs = jnp.concatenate([qc[:, 2 * h * D:(2 * h + 2) * D] for h in heads], axis=0)
        o = _softmax_pv([_dot_nt(qs, kk) for kk in keys], [kk[:, :D] for kk in keys]).astype(BF16)
        for i, h in enumerate(heads):
            outs.append(_dot(o[i * tq:(i + 1) * tq], wv_ref[h]))
    o_ref[...] = (jnp.concatenate(outs, axis=1) * _silu(gate_ref[...])).astype(o_ref.dtype)


def _mla(qc, kc, gates, wv, B, L, tq, cache_kc=None, layer_j=0):
    T = qc.shape[0]
    D = HEAD_DIM
    H = MLA_HEADS
    nq = L // tq
    cached = cache_kc is not None
    in_specs = [pl.BlockSpec((tq, 2 * H * D), lambda b, i: (b * nq + i, 0)),
                pl.BlockSpec((L, 2 * D), lambda b, i: (b, 0)),
                pl.BlockSpec((tq, H * D), lambda b, i: (b * nq + i, 1)),
                pl.BlockSpec(wv.shape, lambda b, i: (0, 0, 0))]
    args = [qc, kc, gates, wv]
    if cached:
        P = cache_kc.shape[2]
        in_specs += [pl.BlockSpec((1, 1, P, 2 * D), lambda b, i: (b, layer_j, 0, 0))]
        args += [cache_kc]
    return pl.pallas_call(
        functools.partial(_mla_kernel, cached=cached),
        grid=(B, nq),
        in_specs=in_specs,
        out_specs=pl.BlockSpec((tq, H * D), lambda b, i: (b * nq + i, 0)),
        out_shape=jax.ShapeDtypeStruct((T, H * D), MIX_DTYPE),
        compiler_params=_cparams(("arbitrary", "arbitrary")),
        name="mla_attention",
    )(*args)


def _even_weights(even_in_w, gdn_conv_w, gdn_a_log, gdn_dt_bias, hyena_conv_w, hyena_conv_b, j):
    w = even_in_w[j]
    qkv_w = 3 * GDN_HEADS * LANES
    n_ab = 4 * GDN_HEADS
    w_main = jnp.concatenate([w[:, :qkv_w], w[:, qkv_w + n_ab:]], axis=1).astype(BF16)
    w_small = _pad_to(w[:, qkv_w:qkv_w + n_ab], (w.shape[0], LANES)).astype(BF16)
    n_dir = 2 * GDN_HEADS
    alog_row = _pad_to(gdn_a_log[j].reshape(1, n_dir), (1, LANES))
    dtb_row = _pad_to(gdn_dt_bias[j].reshape(1, n_dir), (1, LANES))
    return w_main, w_small, alog_row, dtb_row


def _even_layer(x2, B, L, shift, scale, gate, tiles_per_mod, pre_g, post_g, wts, hy, out_w,
                gdn_conv_w, gdn_norm_g, hy_conv_w, hy_conv_b, hy_skip, state_f, state_b, j):
    w_main, w_small, alog_row, dtb_row = wts
    fwd, inv, hc, hs = hy
    z, zab = _in_proj(x2, shift, scale, pre_g, w_main, w_small, tiles_per_mod)
    oa, s_f, s_b = _gdn(z, zab, gdn_conv_w, gdn_norm_g, alog_row, dtb_row, B, L, state_f, state_b, j)
    ncb = HY_CH // HY_CH_BLOCK
    col = lambda idx: idx * ncb
    z1 = _hy_conv(z, col(4), z, col(5), col(7), hy_conv_w, hy_conv_b, hy_skip, 0, fwd, inv, hc, hs, B, L,
                  first=True, last=False)
    ob = _hy_conv(z1, 0, z, col(6), col(7), hy_conv_w, hy_conv_b, hy_skip, 1, fwd, inv, hc, hs, B, L,
                  first=False, last=True)
    x_new = _out_proj(oa, ob, x2, gate, post_g, out_w, tiles_per_mod)
    return x_new, s_f, s_b


def _odd_weights(odd_in_w, mla_q_up, mla_kv_up, j):
    w = odd_in_w[j]
    D = HEAD_DIM
    o = np.cumsum([0, 4 * D, 2 * D, 2 * D, 4 * D, 2 * D, D, MLA_ROPE_DIM, 4 * D])
    gq, gk, gv, gg, mq, mkv, mpe, mg = [w[:, o[i]:o[i + 1]] for i in range(8)]
    w_main = jnp.concatenate([gq, gg, mg, gk, gv, mq, mkv, _pad_to(mpe, (w.shape[0], D))], axis=1).astype(BF16)
    up = mla_q_up[j].reshape(-1, MLA_HEADS, D + MLA_ROPE_DIM)
    rope_cols = _pad_to(up[:, :, D:], (up.shape[0], MLA_HEADS, D))
    mq_up = jnp.concatenate([up[:, :, :D].reshape(-1, MLA_HEADS * D),
                             rope_cols.reshape(-1, MLA_HEADS * D)], axis=1).astype(BF16)
    kv = mla_kv_up[j].reshape(-1, MLA_HEADS, 2 * D)
    wk = kv[:, :, :D].transpose(1, 0, 2).astype(BF16)
    wv = kv[:, :, D:].transpose(1, 0, 2).astype(BF16)
    return w_main, mq_up, wk, wv


def _odd_layer(x2, B, L, shift, scale, gate, tiles_per_mod, pre_g, post_g, wts, out_w,
               q_g, k_g, mq_g, mkv_g, caches, j):
    w_main, mq_up, wk, wv = wts
    rope = caches is not None
    qg, qc, k16, v16, kc16, gates, kg, v32, ckv, kpe = _odd_in(
        x2, shift, scale, pre_g, w_main, q_g, k_g, mq_g, mq_up, mkv_g, wk, L, tiles_per_mod, rope)
    if rope:
        ck, cv, ckc = caches
        og = _gqa(qg, k16, v16, gates, B, L, GQA_Q_TILE, ck, cv, j)
        om = _mla(qc, kc16, gates, wv, B, L, MLA_Q_TILE, ckc, j)
    else:
        og = _gqa(qg, k16, v16, gates, B, L, GQA_Q_TILE)
        om = _mla(qc, kc16, gates, wv, B, L, MLA_Q_TILE)
    x_new = _out_proj(og, om, x2, gate, post_g, out_w, tiles_per_mod)
    return x_new, (kg, v32, ckv, kpe)


def kernel(x_prompt, x_sample, state_gdn_fwd, state_gdn_bwd, cache_gqa_k, cache_gqa_v, cache_mla_ckv, cache_mla_kpe, c, c_ctx, mod_w, mod_b, pre_norm_g, post_norm_g, even_in_w, gdn_conv_w, gdn_a_log, gdn_dt_bias, gdn_norm_g, hyena_conv_w, hyena_conv_b, hyena_ffn_w1, hyena_ffn_b1, hyena_ffn_w2, hyena_ffn_b2, hyena_ffn_w3, hyena_sin_freq, hyena_bias, even_out_w, odd_in_w, gqa_q_norm_g, gqa_k_norm_g, mla_q_norm_g, mla_q_up, mla_kv_norm_g, mla_kv_up, odd_out_w):
    Bp, Lp, D = x_prompt.shape
    Bs, Ls, _ = x_sample.shape
    depth = mod_w.shape[0]
    xp = x_prompt.reshape(Bp * Lp, D)
    xs = x_sample.reshape(Bs * Ls, D)

    n_cond = 1 + Bs
    rows = -(-n_cond // 8) * 8
    cond = _pad_to(jnp.concatenate([c_ctx[None, :], c], axis=0), (rows, D))
    mod = _modulation(cond, mod_w, mod_b)

    P = cache_gqa_k.shape[2]
    n_odd = cache_gqa_k.shape[1]
    ck = cache_gqa_k.reshape(Bs, n_odd, P, GQA_KV_HEADS * HEAD_DIM).astype(BF16)
    cv = cache_gqa_v.reshape(Bs, n_odd, P, GQA_KV_HEADS * HEAD_DIM).astype(BF16)
    ckc = jnp.concatenate([cache_mla_ckv, _pad_to(cache_mla_kpe, cache_mla_ckv.shape)], axis=-1).astype(BF16)

    dft = {L: _dft_blocks(L, min(HY_FREQ_BLOCK, L)) for L in (Lp, Ls)}
    tpm_s = Ls // TOKEN_TILE

    new_f, new_b, new_gk, new_gv, new_ckv, new_kpe = [], [], [], [], [], []
    for i in range(depth):
        j = i // 2
        m = mod[i]
        sh_p, sc_p, gt_p = [m[0:1, k * D:(k + 1) * D].reshape(1, 1, D) for k in range(3)]
        sh_s, sc_s, gt_s = [m[1:n_cond, k * D:(k + 1) * D].reshape(Bs, 1, D) for k in range(3)]
        pre_g = pre_norm_g[i][None, :]
        post_g = post_norm_g[i][None, :]
        if i % 2 == 0:
            wts = _even_weights(even_in_w, gdn_conv_w, gdn_a_log, gdn_dt_bias, hyena_conv_w, hyena_conv_b, j)
            out_w = even_out_w[j].astype(BF16)
            hy = {}
            for L in (Lp, Ls):
                fwd, inv = dft[L]
                filt = _hy_filters(L, hyena_ffn_w1[j], hyena_ffn_b1[j], hyena_ffn_w2[j], hyena_ffn_b2[j],
                                   hyena_ffn_w3[j], hyena_sin_freq[j])
                hc, hs = _hy_spectrum(filt, fwd, L, min(HY_FREQ_BLOCK, L))
                hy[L] = (fwd, inv, hc, hs)
            common = (gdn_conv_w[j], gdn_norm_g[j][None, :], hyena_conv_w[j], hyena_conv_b[j][None, :], hyena_bias[j])
            xp, sf, sb = _even_layer(xp, Bp, Lp, sh_p, sc_p, gt_p, 0, pre_g, post_g, wts, hy[Lp], out_w,
                                     *common, None, None, j)
            xs, _, _ = _even_layer(xs, Bs, Ls, sh_s, sc_s, gt_s, tpm_s, pre_g, post_g, wts, hy[Ls], out_w,
                                   *common, state_gdn_fwd, state_gdn_bwd, j)
            new_f.append(sf)
            new_b.append(sb)
        else:
            wts = _odd_weights(odd_in_w, mla_q_up, mla_kv_up, j)
            out_w = odd_out_w[j].astype(BF16)
            norms = (gqa_q_norm_g[j][None, :], gqa_k_norm_g[j][None, :], mla_q_norm_g[j][None, :],
                     mla_kv_norm_g[j][None, :])
            xp, (kg, v32, ckv, kpe) = _odd_layer(xp, Bp, Lp, sh_p, sc_p, gt_p, 0, pre_g, post_g, wts, out_w,
                                                 *norms, None, j)
            xs, _ = _odd_layer(xs, Bs, Ls, sh_s, sc_s, gt_s, tpm_s, pre_g, post_g, wts, out_w,
                               *norms, (ck, cv, ckc), j)
            new_gk.append(kg.reshape(Bp, Lp, GQA_KV_HEADS, HEAD_DIM))
            new_gv.append(v32.reshape(Bp, Lp, GQA_KV_HEADS, HEAD_DIM))
            new_ckv.append(ckv.reshape(Bp, Lp, HEAD_DIM))
            new_kpe.append(kpe[:, :MLA_ROPE_DIM].reshape(Bp, Lp, MLA_ROPE_DIM))
    return (xp.reshape(Bp, Lp, D), xs.reshape(Bs, Ls, D),
            jnp.stack(new_f, axis=1), jnp.stack(new_b, axis=1),
            jnp.stack(new_gk, axis=1), jnp.stack(new_gv, axis=1),
            jnp.stack(new_ckv, axis=1), jnp.stack(new_kpe, axis=1))
```

```python
import functools
import math

import numpy as np
import jax
import jax.numpy as jnp
from jax import lax
from jax.experimental import pallas as pl
from jax.experimental.pallas import tpu as pltpu

F32 = jnp.float32
BF16 = jnp.bfloat16
MIX_DTYPE = BF16

NORM_EPS = 1e-6
ROPE_THETA = 10000.0
GRID_W = 64

GDN_HEADS = 4
GDN_DK = 128
GDN_CHUNK = 64
GDN_CONV_W = 5
GDN_PREP_CHAINS = 32
GDN_ALL_HEADS_MAX_LEN = 256
GDN_HEADS_LONG = 1
HY_CH = 512
HY_BANDS = 16
HY_DECAY_TARGET = 1e-2
HY_SHORT_DECAY_PCT = 0.3
HY_LONG_DECAY_PCT = 1.5
GQA_HEADS = 4
GQA_KV_HEADS = 2
HEAD_DIM = 128
MLA_HEADS = 4
MLA_ROPE_DIM = 64

LOG2E = math.log2(math.e)

LANES = 128
VMEM_LIMIT = 56 * 1024 * 1024

TOKEN_TILE = 256
HY_FREQ_BLOCK = 512
HY_FREQ_SPLIT = 2
HY_CH_BLOCK = 512
GQA_Q_TILE = 256
MLA_Q_TILE = 128
MLA_HEAD_STACKS = 2


def _cparams(sem):
    return pltpu.CompilerParams(dimension_semantics=sem, vmem_limit_bytes=VMEM_LIMIT)


def _dot(a, b):
    return jnp.dot(a, b, preferred_element_type=F32)


def _dot_nt(a, b):
    return lax.dot_general(a, b, (((1,), (1,)), ((), ())), preferred_element_type=F32)


def _dot_tn(a, b):
    return lax.dot_general(a, b, (((0,), (0,)), ((), ())), preferred_element_type=F32)


def _split(a):
    hi = a.astype(BF16)
    lo = (a - hi.astype(F32)).astype(BF16)
    return hi, lo


def _dot3(a, b):
    ah, al = _split(a)
    bh, bl = _split(b)
    return _dot(ah, bh) + (_dot(ah, bl) + _dot(al, bh))


def _bdot(a, b):
    return jnp.einsum('gij,gjk->gik', a, b, preferred_element_type=F32)


def _bdot16(a, b):
    return _bdot(a.astype(BF16), b.astype(BF16))


def _silu(x):
    return x * (1.0 / (1.0 + jnp.exp(-x)))


def _sigmoid(x):
    return 1.0 / (1.0 + jnp.exp(-x))


def _softplus(x):
    return jnp.maximum(x, 0.0) + jnp.log(1.0 + jnp.exp(-jnp.abs(x)))


def _rms(x, g):
    return x * lax.rsqrt(jnp.mean(x * x, axis=-1, keepdims=True) + NORM_EPS) * g


def _shift_rows(x, s):
    L = x.shape[0]
    if s == 0:
        return x
    rolled = pltpu.roll(x, (-s) % L, 0)
    row = lax.broadcasted_iota(jnp.int32, x.shape, 0)
    valid = (row + s >= 0) & (row + s < L)
    return jnp.where(valid, rolled, 0.0)


def _mod_kernel(c_ref, w_ref, b_ref, o_ref):
    c = _silu(c_ref[...])
    o_ref[0] = _dot3(c, w_ref[0]) + b_ref[0]


def _modulation(cond, mod_w, mod_b):
    depth, d, d3 = mod_w.shape
    r = cond.shape[0]
    nb = d3 // d
    return pl.pallas_call(
        _mod_kernel,
        grid=(depth, nb),
        in_specs=[pl.BlockSpec((r, d), lambda i, n: (0, 0)),
                  pl.BlockSpec((1, d, d), lambda i, n: (i, 0, n)),
                  pl.BlockSpec((1, 1, d), lambda i, n: (i, 0, n))],
        out_specs=pl.BlockSpec((1, r, d), lambda i, n: (i, 0, n)),
        out_shape=jax.ShapeDtypeStruct((depth, r, d3), F32),
        compiler_params=_cparams(("arbitrary", "arbitrary")),
        name="adaln_modulation",
    )(cond, mod_w, mod_b.reshape(depth, 1, d3))


def _in_proj_kernel(x_ref, sh_ref, sc_ref, g_ref, w_ref, *rest, n_main, has_small):
    if has_small:
        ws_ref, o_ref, os_ref = rest
    else:
        (o_ref,) = rest
    x = x_ref[...]
    h = _rms(x, g_ref[...]) * (1.0 + sc_ref[0]) + sh_ref[0]
    hb = h.astype(BF16)
    step = 512
    for n0 in range(0, n_main, step):
        o_ref[:, n0:n0 + step] = _dot(hb, w_ref[:, n0:n0 + step])
    if has_small:
        os_ref[...] = _dot(hb, ws_ref[...])


def _in_proj(x2, shift, scale, pre_g, w_main, w_small, tiles_per_mod):
    T, D = x2.shape
    N = w_main.shape[1]
    tm = TOKEN_TILE
    if tiles_per_mod:
        mod_map = lambda i: (i // tiles_per_mod, 0, 0)
    else:
        mod_map = lambda i: (0, 0, 0)
    in_specs = [pl.BlockSpec((tm, D), lambda i: (i, 0)),
                pl.BlockSpec((1, 1, D), mod_map),
                pl.BlockSpec((1, 1, D), mod_map),
                pl.BlockSpec((1, D), lambda i: (0, 0)),
                pl.BlockSpec((D, N), lambda i: (0, 0))]
    out_specs = [pl.BlockSpec((tm, N), lambda i: (i, 0))]
    out_shape = [jax.ShapeDtypeStruct((T, N), F32)]
    args = [x2, shift, scale, pre_g, w_main]
    if w_small is not None:
        in_specs.append(pl.BlockSpec((D, LANES), lambda i: (0, 0)))
        out_specs.append(pl.BlockSpec((tm, LANES), lambda i: (i, 0)))
        out_shape.append(jax.ShapeDtypeStruct((T, LANES), F32))
        args.append(w_small)
    res = pl.pallas_call(
        functools.partial(_in_proj_kernel, n_main=N, has_small=w_small is not None),
        grid=(T // tm,),
        in_specs=in_specs, out_specs=out_specs, out_shape=out_shape,
        compiler_params=_cparams(("arbitrary",)),
        name="in_proj",
    )(*args)
    return res


def _out_proj_kernel(a_ref, b_ref, x_ref, gt_ref, g_ref, w_ref, o_ref, *, half):
    y = _dot(a_ref[...].astype(BF16), w_ref[:half, :]) + _dot(b_ref[...].astype(BF16), w_ref[half:, :])
    o_ref[...] = x_ref[...] + gt_ref[0] * _rms(y, g_ref[...])


def _out_proj(a, b, x2, gate, post_g, w, tiles_per_mod):
    T, D = x2.shape
    half = a.shape[1]
    tm = TOKEN_TILE
    if tiles_per_mod:
        mod_map = lambda i: (i // tiles_per_mod, 0, 0)
    else:
        mod_map = lambda i: (0, 0, 0)
    return pl.pallas_call(
        functools.partial(_out_proj_kernel, half=half),
        grid=(T // tm,),
        in_specs=[pl.BlockSpec((tm, half), lambda i: (i, 0)),
                  pl.BlockSpec((tm, half), lambda i: (i, 0)),
                  pl.BlockSpec((tm, D), lambda i: (i, 0)),
                  pl.BlockSpec((1, 1, D), mod_map),
                  pl.BlockSpec((1, D), lambda i: (0, 0)),
                  pl.BlockSpec((2 * half, D), lambda i: (0, 0))],
        out_specs=pl.BlockSpec((tm, D), lambda i: (i, 0)),
        out_shape=jax.ShapeDtypeStruct((T, D), F32),
        compiler_params=_cparams(("arbitrary",)),
        name="out_proj",
    )(a, b, x2, gate, post_g, w)


def _gdn_kernel(zq_ref, zk_ref, zv_ref, zg_ref, zab_ref, cwq_ref, cwk_ref, cwv_ref, ng_ref,
                alog_ref, dtb_ref, *rest, L, HB, has_state):
    if has_state:
        s0f_ref, s0b_ref = rest[:2]
        rest = rest[2:]
    (o_ref, sf_out_ref, sb_out_ref, q_s, k_s, v_s, g_s, b_s, st_s,
     oc_s, qe_s, sc_s, sm_s, gl_s, pad_s) = rest
    C = GDN_CHUNK
    n_chunks = L // C
    U = min(GDN_PREP_CHAINS // (2 * HB), n_chunks)
    HU = HB * U
    G = 2 * HU
    head0 = pl.program_id(1) * HB

    pad = 8
    pad_s[0:pad, :] = jnp.zeros((pad, LANES), F32)
    pad_s[pad + L:2 * pad + L, :] = jnp.zeros((pad, LANES), F32)

    def conv_silu(x, w_ref, lanes):
        half = GDN_CONV_W // 2
        pad_s[pad:pad + L, :] = x
        acc = x * w_ref[half:half + 1, lanes]
        for i in range(GDN_CONV_W):
            if i != half:
                start = pad + i - half
                acc = acc + pad_s[start:start + L, :] * w_ref[i:i + 1, lanes]
        return _silu(acc)

    def l2n(x):
        return x * lax.rsqrt(jnp.sum(x * x, axis=-1, keepdims=True) + NORM_EPS)

    for hh in range(HB):
        lanes = slice(hh * LANES, (hh + 1) * LANES)
        q_s[hh] = l2n(conv_silu(zq_ref[:, lanes], cwq_ref, lanes)) * (GDN_DK ** -0.5)
        k_s[hh] = l2n(conv_silu(zk_ref[:, lanes], cwk_ref, lanes))
        v_s[hh] = conv_silu(zv_ref[:, lanes], cwv_ref, lanes)
        if has_state:
            st_s[2 * hh] = s0f_ref[0, 0, hh]
            st_s[2 * hh + 1] = s0b_ref[0, 0, hh]
    if not has_state:
        st_s[...] = jnp.zeros_like(st_s)
    zab = zab_ref[...]
    g_s[...] = -jnp.exp(alog_ref[...]) * _softplus(zab + dtb_ref[...])
    b_s[...] = _sigmoid(zab)

    def iota(shape, axis):
        return lax.broadcasted_iota(jnp.int32, shape, axis)

    def direction(shape):
        return iota(shape, 0) & 1

    chain_shift = (2 * U).bit_length() - 1
    assert 2 * U == 1 << chain_shift

    def chain_head(shape):
        return head0 + (iota(shape, 0) >> chain_shift)

    sq = (G, C, C)
    row = iota(sq, 1)
    col = iota(sq, 2)
    signed = (row - col) * (1 - 2 * direction(sq))
    incl = signed >= 0
    strict = signed > 0
    eye = (row == col).astype(F32)
    same = [(row >> s) == (col >> s) for s in (3, 4, 5)]
    off_blocks = [same[1] & jnp.logical_not(same[0]), same[2] & jnp.logical_not(same[1]),
                  jnp.logical_not(same[2])]
    wide = (G, C, LANES)
    sel_lane = direction(wide) * GDN_HEADS + chain_head(wide)
    mask_g = iota(wide, 2) == sel_lane
    mask_b = iota(wide, 2) == sel_lane + 2 * GDN_HEADS
    tall = (G, LANES, C)
    mask_t = iota(tall, 1) == direction(tall) * GDN_HEADS + chain_head(tall)
    colv = (G, C, 1)
    mask_last = iota(colv, 1) == (1 - direction(colv)) * (C - 1)
    r2 = lax.broadcasted_iota(jnp.int32, (C, C), 0)
    c2 = lax.broadcasted_iota(jnp.int32, (C, C), 1)
    tri = jnp.concatenate([(r2 >= c2).astype(BF16), (r2 <= c2).astype(BF16)], axis=0)

    def both_dirs(x):
        return jnp.broadcast_to(x[:, None], (HU, 2) + x.shape[1:]).reshape((G,) + x.shape[1:])

    def all_heads(x):
        return jnp.broadcast_to(x[None], (HB,) + x.shape).reshape((HB * x.shape[0],) + x.shape[1:])

    def prep_group(c, carry):
        rows = pl.ds(pl.multiple_of(c * (U * C), U * C), U * C)
        q = q_s[:, rows, :].reshape(HU, C, LANES)
        k = k_s[:, rows, :].reshape(HU, C, LANES)
        v = v_s[:, rows, :].reshape(HU, C, LANES)
        ball = all_heads(b_s[rows, :].reshape(U, C, LANES))
        g_hi, g_lo = _split(g_s[rows, :])
        g_hi = g_hi.reshape(U, C, LANES)
        g_lo = g_lo.reshape(U, C, LANES)
        gcum = jnp.stack([_dot(tri, g_hi[u]) + _dot(tri, g_lo[u]) for u in range(U)])
        gcum = gcum.reshape(2 * U, C, LANES)
        gcum_t = all_heads(jnp.stack([gcum[g].T for g in range(2 * U)]))
        gcum = all_heads(gcum)
        gc = jnp.sum(jnp.where(mask_g, gcum, 0.0), axis=2, keepdims=True)
        gr = jnp.sum(jnp.where(mask_t, gcum_t, 0.0), axis=1, keepdims=True)
        beta = jnp.sum(jnp.where(mask_b, both_dirs(ball), 0.0), axis=2, keepdims=True)
        kbf = k.astype(BF16)
        kq = jnp.einsum('uik,ujk->uij', jnp.concatenate([k, q], axis=1).astype(BF16), kbf,
                        preferred_element_type=F32)
        kk = both_dirs(kq[:, :C])
        qk = both_dirs(kq[:, C:])
        decay = jnp.where(incl, jnp.exp(jnp.where(incl, gc - gr, 0.0)), 0.0)
        lmat = jnp.where(strict, beta * kk * decay, 0.0)
        p = jnp.where(same[0], -lmat, 0.0)
        tmat = eye + p
        for _ in range(2):
            p = _bdot16(p, p)
            tmat = tmat + _bdot16(tmat, p)
        for off in off_blocks:
            t16 = tmat.astype(BF16)
            tmat = tmat - _bdot(_bdot(t16, jnp.where(off, lmat, 0.0).astype(BF16)).astype(BF16), t16)
        eg = jnp.exp(gc)
        k2 = both_dirs(k)
        kb = k2 * beta
        sol = _bdot16(tmat, jnp.concatenate([both_dirs(v) * beta, kb * eg], axis=2))
        glast = jnp.sum(jnp.where(mask_last, gc, 0.0), axis=1, keepdims=True)
        kdec = k2 * jnp.exp(glast - gc)
        kdec_t = jnp.stack([kdec[g].T for g in range(G)]).astype(BF16)
        sol16 = sol.astype(BF16)
        r_in = _bdot((qk * decay).astype(BF16), sol16)
        r_kd = _bdot(kdec_t, sol16)
        def put(ref, val, size):
            n = 2 * U * size
            ref[:, pl.ds(pl.multiple_of(c * n, n), n), :] = val.reshape(HB, n, val.shape[-1])

        put(oc_s, r_in[:, :, :LANES], C)
        put(qe_s, (both_dirs(q) * eg - r_in[:, :, LANES:]).astype(BF16), C)
        put(sc_s, r_kd[:, :, :LANES], LANES)
        put(sm_s, (-r_kd[:, :, LANES:]).astype(BF16), LANES)
        put(gl_s, jnp.broadcast_to(jnp.exp(glast), (G, 8, LANES)), 8)
        return carry

    lax.fori_loop(0, n_chunks // U, prep_group, 0)

    def scan_step(c, carry):
        slot_f = c * 2
        slot_b = (n_chunks - 1 - c) * 2 + 1

        def ld(ref, size):
            parts = []
            for hh in range(HB):
                parts.append(ref[hh, pl.ds(pl.multiple_of(slot_f * size, size), size), :])
                parts.append(ref[hh, pl.ds(pl.multiple_of(slot_b * size, size), size), :])
            return jnp.stack(parts)

        s = st_s[...]
        sb16 = s.astype(BF16)
        o = _bdot(ld(qe_s, C), sb16) + ld(oc_s, C)
        st_s[...] = s * ld(gl_s, 8)[:, 0:1, :] + (_bdot(ld(sm_s, LANES), sb16) + ld(sc_s, LANES))
        for hh in range(HB):
            oc_s[hh, pl.ds(pl.multiple_of(slot_f * C, C), C), :] = o[2 * hh]
            oc_s[hh, pl.ds(pl.multiple_of(slot_b * C, C), C), :] = o[2 * hh + 1]
        return carry

    lax.fori_loop(0, n_chunks, scan_step, 0)

    FIN = 4

    def finish(i, carry):
        rows = pl.ds(pl.multiple_of(i * (FIN * C), FIN * C), FIN * C)
        outs = []
        for hh in range(HB):
            both = oc_s[hh, pl.ds(pl.multiple_of(i * (FIN * 2 * C), FIN * 2 * C), FIN * 2 * C), :]
            both = both.reshape(FIN, 2, C, LANES)
            outs.append(_rms((both[:, 0] + both[:, 1]).reshape(FIN * C, LANES), ng_ref[...]))
        o = jnp.concatenate(outs, axis=1) * _silu(zg_ref[rows, :])
        o_ref[rows, :] = o.astype(o_ref.dtype)
        return carry

    lax.fori_loop(0, n_chunks // FIN, finish, 0)
    for hh in range(HB):
        sf_out_ref[0, hh] = st_s[2 * hh]
        sb_out_ref[0, hh] = st_s[2 * hh + 1]


def _gdn(z, zab, conv_w, norm_g, alog_row, dtb_row, B, L, state_f=None, state_b=None, layer_j=0):
    T = z.shape[0]
    H = GDN_HEADS
    has_state = state_f is not None
    HB = H if L <= GDN_ALL_HEADS_MAX_LEN else GDN_HEADS_LONG
    nh = H // HB
    w = HB * LANES
    blk = lambda i: pl.BlockSpec((L, w), lambda b, h: (b, i * nh + h))
    cw = lambda i: pl.BlockSpec((GDN_CONV_W, w), lambda b, h: (0, i * nh + h))
    in_specs = [blk(0), blk(1), blk(2), blk(3),
                pl.BlockSpec((L, LANES), lambda b, h: (b, 0)),
                cw(0), cw(1), cw(2),
                pl.BlockSpec((1, LANES), lambda b, h: (0, 0)),
                pl.BlockSpec((1, LANES), lambda b, h: (0, 0)),
                pl.BlockSpec((1, LANES), lambda b, h: (0, 0))]
    args = [z, z, z, z, zab, conv_w, conv_w, conv_w, norm_g, alog_row, dtb_row]
    if has_state:
        st = pl.BlockSpec((1, 1, HB, GDN_DK, LANES), lambda b, h: (b, layer_j, h, 0, 0))
        in_specs += [st, st]
        args += [state_f, state_b]
    sout = pl.BlockSpec((1, HB, GDN_DK, LANES), lambda b, h: (b, h, 0, 0))
    n_slots = 2 * (L // GDN_CHUNK)
    scratch = ([pltpu.VMEM((HB, L, LANES), F32) for _ in range(3)]
               + [pltpu.VMEM((L, LANES), F32) for _ in range(2)]
               + [pltpu.VMEM((2 * HB, GDN_DK, LANES), F32)]
               + [pltpu.VMEM((HB, 2 * L, LANES), F32), pltpu.VMEM((HB, 2 * L, LANES), BF16)]
               + [pltpu.VMEM((HB, n_slots * GDN_DK, LANES), F32),
                  pltpu.VMEM((HB, n_slots * GDN_DK, LANES), BF16)]
               + [pltpu.VMEM((HB, n_slots * 8, LANES), F32)]
               + [pltpu.VMEM((L + 16, LANES), F32)])
    return pl.pallas_call(
        functools.partial(_gdn_kernel, L=L, HB=HB, has_state=has_state),
        grid=(B, nh),
        in_specs=in_specs,
        out_specs=[pl.BlockSpec((L, w), lambda b, h: (b, h)), sout, sout],
        out_shape=[jax.ShapeDtypeStruct((T, H * LANES), MIX_DTYPE),
                   jax.ShapeDtypeStruct((B, H, GDN_DK, LANES), F32),
                   jax.ShapeDtypeStruct((B, H, GDN_DK, LANES), F32)],
        scratch_shapes=scratch,
        compiler_params=_cparams(("arbitrary", "arbitrary")),
        name="gdn",
    )(*args)


def _hy_filter_kernel(f_ref, w1_ref, b1_ref, sf0_ref, w2_ref, b2_ref, sf1_ref, w3_ref, dl_ref, o_ref, *, L):
    rowi = lax.broadcasted_iota(jnp.int32, (L, LANES), 0).astype(F32)
    lane = lax.broadcasted_iota(jnp.int32, (L, LANES), 1)
    t = rowi * (1.0 / (L - 1))
    wpos = rowi * (2.0 * math.pi / L)
    ang = f_ref[...] * wpos
    z = jnp.where(lane == 0, t,
                  jnp.where(lane <= HY_BANDS, jnp.cos(ang),
                            jnp.where(lane <= 2 * HY_BANDS, -jnp.sin(ang), 0.0)))
    h = jnp.sin(sf0_ref[...] * (_dot3(z, w1_ref[...]) + b1_ref[...]))
    h = jnp.sin(sf1_ref[...] * (_dot3(h, w2_ref[...]) + b2_ref[...]))
    tc = lax.broadcasted_iota(jnp.int32, (L, HY_CH), 0).astype(F32) * (1.0 / (L - 1))
    window = jnp.exp(-tc * dl_ref[...])
    for j in range(4):
        o_ref[:, j * HY_CH:(j + 1) * HY_CH] = _dot3(h, w3_ref[:, j * HY_CH:(j + 1) * HY_CH]) * window


def _pad_to(a, shape):
    return jnp.pad(a, [(0, s - d) for s, d in zip(shape, a.shape)])


def _hy_filters(L, w1, b1, w2, b2, w3, sin_freq):
    fvals = np.linspace(1e-4, HY_BANDS - 1, HY_BANDS, dtype=np.float32)
    frow = np.zeros((1, LANES), np.float32)
    frow[0, 1:1 + HY_BANDS] = fvals
    frow[0, 1 + HY_BANDS:1 + 2 * HY_BANDS] = fvals
    deltas = np.abs(np.linspace(math.log(HY_DECAY_TARGET) / HY_LONG_DECAY_PCT,
                                math.log(HY_DECAY_TARGET) / HY_SHORT_DECAY_PCT, HY_CH, dtype=np.float32))
    args = [jnp.asarray(frow),
            _pad_to(w1, (LANES, LANES)), _pad_to(b1[None, :], (1, LANES)), _pad_to(sin_freq[0][None, :], (1, LANES)),
            _pad_to(w2, (LANES, LANES)), _pad_to(b2[None, :], (1, LANES)), _pad_to(sin_freq[1][None, :], (1, LANES)),
            _pad_to(w3, (LANES, 4 * HY_CH)), jnp.asarray(deltas[None, :])]
    return pl.pallas_call(
        functools.partial(_hy_filter_kernel, L=L),
        out_shape=jax.ShapeDtypeStruct((L, 4 * HY_CH), F32),
        compiler_params=pltpu.CompilerParams(vmem_limit_bytes=VMEM_LIMIT),
        name="hyena_filters",
    )(*args)


def _dft_tables(L):
    N = 2 * L
    k = np.arange(L, dtype=np.int64)[:, None]
    s = np.arange(L, dtype=np.int64)[None, :]
    ang = ((2 * k + 1) * s % (2 * N)).astype(np.float64) * (2.0 * math.pi / (2 * N))
    return np.cos(ang).astype(np.float32), np.sin(ang).astype(np.float32)


def _dft_blocks(L, kb):
    cm, sm = _dft_tables(L)
    nk = L // kb
    fwd = np.concatenate([cm.reshape(nk, kb, L), sm.reshape(nk, kb, L)], axis=1)
    inv = np.concatenate([cm.T.reshape(L, nk, kb), sm.T.reshape(L, nk, kb)], axis=2)
    inv = np.ascontiguousarray(inv.transpose(1, 0, 2))
    return jnp.asarray(fwd).astype(BF16), jnp.asarray(inv).astype(BF16)


def _hy_spectrum_kernel(f_ref, flt_ref, hc_ref, hs_ref, *, L, kb):
    row = lax.broadcasted_iota(jnp.int32, (L, HY_CH), 0)
    scale = 1.0 / L
    for o in range(2):
        hf = flt_ref[:, (2 * o) * HY_CH:(2 * o + 1) * HY_CH]
        hb = jnp.where(row == 0, 0.0, flt_ref[:, (2 * o + 1) * HY_CH:(2 * o + 2) * HY_CH])
        a_hi, a_lo = _split(hf + hb)
        d_hi, d_lo = _split(hf - hb)
        fc = f_ref[0, :kb, :]
        fs = f_ref[0, kb:, :]
        hc_ref[o] = (_dot(fc, a_hi) + _dot(fc, a_lo)) * scale
        hs_ref[o] = (_dot(fs, d_hi) + _dot(fs, d_lo)) * scale


def _hy_spectrum(filt, fwd, L, kb):
    nk = L // kb
    return pl.pallas_call(
        functools.partial(_hy_spectrum_kernel, L=L, kb=kb),
        grid=(nk,),
        in_specs=[pl.BlockSpec((1, 2 * kb, L), lambda i: (i, 0, 0)),
                  pl.BlockSpec((L, 4 * HY_CH), lambda i: (0, 0))],
        out_specs=[pl.BlockSpec((2, kb, HY_CH), lambda i: (0, i, 0)),
                   pl.BlockSpec((2, kb, HY_CH), lambda i: (0, i, 0))],
        out_shape=[jax.ShapeDtypeStruct((2, L, HY_CH), F32)] * 2,
        compiler_params=_cparams(("arbitrary",)),
        name="hyena_spectrum",
    )(fwd, filt)


def _hy_conv_kernel(zin_ref, xo_ref, gate_ref, cwz_ref, cbz_ref, cwx_ref, cbx_ref, skip_ref,
                    f_ref, g_ref, hc_ref, hs_ref, o_ref, zb_s, acc_s, *, kb, first, last):
    kstep = pl.program_id(2)
    nk = pl.num_programs(2)

    def conv3(x_ref, w_ref, b_ref):
        x = x_ref[...]
        return (_shift_rows(x, -1) * w_ref[0:1, :] + x * w_ref[1:2, :]
                + _shift_rows(x, 1) * w_ref[2:3, :]) + b_ref[...]

    def z_value():
        return conv3(zin_ref, cwz_ref, cbz_ref) if first else zin_ref[...]

    @pl.when(kstep == 0)
    def _():
        zb_s[...] = z_value().astype(BF16)
        acc_s[...] = jnp.zeros_like(acc_s)

    zb = zb_s[...]
    kh = kb // HY_FREQ_SPLIT
    ycs, yss = [], []
    for h in range(HY_FREQ_SPLIT):
        cos_rows = slice(h * kh, (h + 1) * kh)
        sin_rows = slice(kb + h * kh, kb + (h + 1) * kh)
        xc = _dot(f_ref[0, cos_rows, :], zb)
        xs = _dot(f_ref[0, sin_rows, :], zb)
        hc = hc_ref[0, cos_rows, :]
        hs = hs_ref[0, cos_rows, :]
        ycs.append((xc * hc - xs * hs).astype(BF16))
        yss.append((xc * hs + xs * hc).astype(BF16))
    acc_s[...] += _dot(g_ref[0], jnp.concatenate(ycs + yss, axis=0))

    @pl.when(kstep == nk - 1)
    def _():
        xo = conv3(xo_ref, cwx_ref, cbx_ref)
        res = xo * (acc_s[...] + z_value() * skip_ref[...])
        if last:
            res = res * _silu(gate_ref[...])
        o_ref[...] = res.astype(o_ref.dtype)


def _hy_conv(zsrc, zcol, z, xcol, gcol, conv_w, conv_b, skip, order, fwd, inv, hc, hs, B, L, first, last):
    T = z.shape[0]
    cb = HY_CH_BLOCK
    ncb = HY_CH // cb
    nk, _, kb2 = inv.shape
    kb = kb2 // 2
    data = lambda off, **kw: pl.BlockSpec((L, cb), lambda b, c, k: (b, off + c), **kw)
    late = dict(pipeline_mode=pl.Buffered(1))
    cwspec = lambda off: pl.BlockSpec((3, cb), lambda b, c, k: (0, off + c))
    cbspec = lambda off: pl.BlockSpec((1, cb), lambda b, c, k: (0, off + c))
    in_specs = [data(zcol), data(xcol, **late), data(gcol, **late),
                cwspec(0), cbspec(0), cwspec((1 + order) * ncb), cbspec((1 + order) * ncb),
                pl.BlockSpec((1, cb), lambda b, c, k: (0, c)),
                pl.BlockSpec((1, 2 * kb, L), lambda b, c, k: (k, 0, 0)),
                pl.BlockSpec((1, L, 2 * kb), lambda b, c, k: (k, 0, 0)),
                pl.BlockSpec((1, kb, cb), lambda b, c, k: (order, k, c)),
                pl.BlockSpec((1, kb, cb), lambda b, c, k: (order, k, c))]
    return pl.pallas_call(
        functools.partial(_hy_conv_kernel, kb=kb, first=first, last=last),
        grid=(B, ncb, nk),
        in_specs=in_specs,
        out_specs=pl.BlockSpec((L, cb), lambda b, c, k: (b, c)),
        out_shape=jax.ShapeDtypeStruct((T, HY_CH), MIX_DTYPE if last else F32),
        scratch_shapes=[pltpu.VMEM((L, cb), BF16), pltpu.VMEM((L, cb), F32)],
        compiler_params=_cparams(("arbitrary", "arbitrary", "arbitrary")),
        name="hyena_conv",
    )(zsrc, z, z, conv_w, conv_b, conv_w, conv_b, skip[order:order + 1], fwd, inv, hc, hs)


OD_GQ, OD_GG, OD_MG, OD_GK, OD_GV, OD_MQ, OD_MKV, OD_MPE = 0, 4, 8, 12, 14, 16, 18, 19
OD_WIDTH = 20 * LANES


def _rope_swap(x, quarter):
    n = x.shape[1]
    lane = lax.broadcasted_iota(jnp.int32, x.shape, 1)
    first = (lane & (2 * quarter - 1)) < quarter
    return jnp.where(first, pltpu.roll(x, n - quarter, 1), pltpu.roll(x, quarter, 1))


def _odd_in_kernel(x_ref, sh_ref, sc_ref, g_ref, w_ref, qg_ref, kg_ref, mqg_ref, mqup_ref, mkvg_ref, wk_ref,
                   *rest, rope):
    if rope:
        cg_ref, sg_ref, cm_ref, sm_ref = rest[:4]
        rest = rest[4:]
    q_out, qc_out, k16_out, v16_out, kc16_out, gate_out, k_out, v_out, ckv_out, kpe_out = rest
    D = HEAD_DIM
    hb = (_rms(x_ref[...], g_ref[...]) * (1.0 + sc_ref[0]) + sh_ref[0]).astype(BF16)

    def proj(col, width):
        return _dot(hb, w_ref[:, col * D:(col + width) * D])

    def rot(x, c, s, quarter):
        return x * c + _rope_swap(x, quarter) * s if rope else x

    gscale = (D ** -0.5) * LOG2E
    zq = proj(OD_GQ, GQA_HEADS)
    for h in range(GQA_HEADS):
        q = rot(_rms(zq[:, h * D:(h + 1) * D], qg_ref[...]), cg_ref[...] if rope else None,
                sg_ref[...] if rope else None, D // 4)
        q_out[:, h * D:(h + 1) * D] = (q * gscale).astype(BF16)
    zk = proj(OD_GK, GQA_KV_HEADS)
    zv = proj(OD_GV, GQA_KV_HEADS)
    v_out[...] = zv
    v16_out[...] = zv.astype(BF16)
    for h in range(GQA_KV_HEADS):
        kn = _rms(zk[:, h * D:(h + 1) * D], kg_ref[...])
        k_out[:, h * D:(h + 1) * D] = kn
        k16_out[:, h * D:(h + 1) * D] = rot(kn, cg_ref[...] if rope else None,
                                            sg_ref[...] if rope else None, D // 4).astype(BF16)
    gate_out[...] = proj(OD_GG, GQA_HEADS + MLA_HEADS)

    mq = _rms(proj(OD_MQ, 2), mqg_ref[...]).astype(BF16)
    qm = _dot(mq, mqup_ref[...])
    mscale = ((D + MLA_ROPE_DIM) ** -0.5) * LOG2E
    qpe = qm[:, MLA_HEADS * D:]
    if rope:
        qpe = rot(qpe, cm_ref[...], sm_ref[...], MLA_ROPE_DIM // 4)
    for h in range(MLA_HEADS):
        qn = qm[:, h * D:(h + 1) * D].astype(BF16)
        qc_out[:, 2 * h * D:(2 * h + 1) * D] = (_dot_nt(qn, wk_ref[h]) * mscale).astype(BF16)
        qc_out[:, (2 * h + 1) * D:(2 * h + 2) * D] = (qpe[:, h * D:(h + 1) * D] * mscale).astype(BF16)
    ckv = _rms(proj(OD_MKV, 1), mkvg_ref[...])
    ckv_out[...] = ckv
    kpe = proj(OD_MPE, 1)
    kpe_out[...] = kpe
    if rope:
        kpe = rot(kpe, cm_ref[:, :D], sm_ref[:, :D], MLA_ROPE_DIM // 4)
    kc16_out[:, :D] = ckv.astype(BF16)
    kc16_out[:, D:] = kpe.astype(BF16)


def _rope_tables(L, R):
    rows = np.arange(L) // GRID_W
    cols = np.arange(L) % GRID_W
    quarter = R // 4
    inv = ROPE_THETA ** (-np.arange(quarter, dtype=np.float32) * 2.0 / (R // 2))
    a_r = rows[:, None].astype(np.float32) * inv[None, :]
    a_c = cols[:, None].astype(np.float32) * inv[None, :]
    cos = np.concatenate([np.cos(a_r), np.cos(a_r), np.cos(a_c), np.cos(a_c)], axis=1)
    sin = np.concatenate([-np.sin(a_r), np.sin(a_r), -np.sin(a_c), np.sin(a_c)], axis=1)
    return cos.astype(np.float32), sin.astype(np.float32)


def _odd_in(x2, shift, scale, pre_g, w_main, q_g, k_g, mq_g, mq_up, mkv_g, wk, L, tiles_per_mod, rope):
    T, dm = x2.shape
    tm = TOKEN_TILE
    D = HEAD_DIM
    if tiles_per_mod:
        mod_map = lambda i: (i // tiles_per_mod, 0, 0)
    else:
        mod_map = lambda i: (0, 0, 0)
    full = lambda shape: pl.BlockSpec(shape, lambda i: tuple(0 for _ in shape))
    in_specs = [pl.BlockSpec((tm, dm), lambda i: (i, 0)),
                pl.BlockSpec((1, 1, dm), mod_map), pl.BlockSpec((1, 1, dm), mod_map),
                full((1, dm)), full(w_main.shape),
                full((1, D)), full((1, D)), full((1, 2 * D)), full(mq_up.shape), full((1, D)), full(wk.shape)]
    args = [x2, shift, scale, pre_g, w_main, q_g, k_g, mq_g, mq_up, mkv_g, wk]
    if rope:
        cg, sg = _rope_tables(L, D)
        cm, sm = _rope_tables(L, MLA_ROPE_DIM)
        widen = lambda t: np.tile(np.concatenate([t, np.zeros_like(t)], axis=1), (1, MLA_HEADS))
        per = L // tm
        pos = lambda w: pl.BlockSpec((tm, w), lambda i: (i % per, 0))
        in_specs += [pos(D), pos(D), pos(MLA_HEADS * D), pos(MLA_HEADS * D)]
        args += [jnp.asarray(cg), jnp.asarray(sg), jnp.asarray(widen(cm)), jnp.asarray(widen(sm))]
    tile = lambda w: pl.BlockSpec((tm, w), lambda i: (i, 0))
    widths = [(4 * D, BF16), (8 * D, BF16), (2 * D, BF16), (2 * D, BF16), (2 * D, BF16), (8 * D, F32),
              (2 * D, F32), (2 * D, F32), (D, F32), (D, F32)]
    return pl.pallas_call(
        functools.partial(_odd_in_kernel, rope=rope),
        grid=(T // tm,),
        in_specs=in_specs,
        out_specs=[tile(w) for w, _ in widths],
        out_shape=[jax.ShapeDtypeStruct((T, w), dt) for w, dt in widths],
        compiler_params=_cparams(("arbitrary",)),
        name="odd_in_proj",
    )(*args)


def _softmax_pv(score_blocks, value_blocks):
    m = score_blocks[0].max(axis=-1, keepdims=True)
    for s in score_blocks[1:]:
        m = jnp.maximum(m, s.max(axis=-1, keepdims=True))
    acc = None
    den = None
    for s, v in zip(score_blocks, value_blocks):
        p = jnp.exp2(s - m)
        d = p.sum(axis=-1, keepdims=True)
        a = _dot(p.astype(BF16), v)
        acc = a if acc is None else acc + a
        den = d if den is None else den + d
    return acc / den


def _gqa_kernel(q_ref, k_ref, v_ref, gate_ref, *rest, cached):
    if cached:
        ck_ref, cv_ref, o_ref = rest
    else:
        (o_ref,) = rest
    D = HEAD_DIM
    group = GQA_HEADS // GQA_KV_HEADS
    q = q_ref[...]
    keys = [k_ref[...]] + ([ck_ref[0, 0]] if cached else [])
    values = [v_ref[...]] + ([cv_ref[0, 0]] if cached else [])
    outs = []
    for g in range(group):
        qh = q[:, g * D:(g + 1) * D]
        outs.append(_softmax_pv([_dot_nt(qh, kk) for kk in keys], values))
    o_ref[...] = (jnp.concatenate(outs, axis=1) * _silu(gate_ref[...])).astype(o_ref.dtype)


def _gqa(qg, kg, vg, gates, B, L, tq, cache_k=None, cache_v=None, layer_j=0):
    T = qg.shape[0]
    D = HEAD_DIM
    group = GQA_HEADS // GQA_KV_HEADS
    nq = L // tq
    cached = cache_k is not None
    in_specs = [pl.BlockSpec((tq, group * D), lambda b, h, i: (b * nq + i, h)),
                pl.BlockSpec((L, D), lambda b, h, i: (b, h)),
                pl.BlockSpec((L, D), lambda b, h, i: (b, h)),
                pl.BlockSpec((tq, group * D), lambda b, h, i: (b * nq + i, h))]
    args = [qg, kg, vg, gates]
    if cached:
        P = cache_k.shape[2]
        cspec = pl.BlockSpec((1, 1, P, D), lambda b, h, i: (b, layer_j, 0, h))
        in_specs += [cspec, cspec]
        args += [cache_k, cache_v]
    return pl.pallas_call(
        functools.partial(_gqa_kernel, cached=cached),
        grid=(B, GQA_KV_HEADS, nq),
        in_specs=in_specs,
        out_specs=pl.BlockSpec((tq, group * D), lambda b, h, i: (b * nq + i, h)),
        out_shape=jax.ShapeDtypeStruct((T, GQA_HEADS * D), MIX_DTYPE),
        compiler_params=_cparams(("arbitrary", "arbitrary", "arbitrary")),
        name="gqa_attention",
    )(*args)


def _mla_kernel(qc_ref, kc_ref, gate_ref, wv_ref, *rest, cached):
    if cached:
        ckc_ref, o_ref = rest
    else:
        (o_ref,) = rest
    D = HEAD_DIM
    H = MLA_HEADS
    qc = qc_ref[...]
    tq = qc.shape[0]
    keys = [kc_ref[...]] + ([ckc_ref[0, 0]] if cached else [])
    per = H // MLA_HEAD_STACKS
    outs = []
    for s in range(MLA_HEAD_STACKS):
        heads = range(s * per, (s + 1) * per)
        qs = jnp.concatenate([qc[:, 2 * h * D:(2 * h + 2) * D] for h in heads], axis=0)
        o = _softmax_pv([_dot_nt(qs, kk) for kk in keys], [kk[:, :D] for kk in keys]).astype(BF16)
        for i, h in enumerate(heads):
            outs.append(_dot(o[i * tq:(i + 1) * tq], wv_ref[h]))
    o_ref[...] = (jnp.concatenate(outs, axis=1) * _silu(gate_ref[...])).astype(o_ref.dtype)


def _mla(qc, kc, gates, wv, B, L, tq, cache_kc=None, layer_j=0):
    T = qc.shape[0]
    D = HEAD_DIM
    H = MLA_HEADS
    nq = L // tq
    cached = cache_kc is not None
    in_specs = [pl.BlockSpec((tq, 2 * H * D), lambda b, i: (b * nq + i, 0)),
                pl.BlockSpec((L, 2 * D), lambda b, i: (b, 0)),
                pl.BlockSpec((tq, H * D), lambda b, i: (b * nq + i, 1)),
                pl.BlockSpec(wv.shape, lambda b, i: (0, 0, 0))]
    args = [qc, kc, gates, wv]
    if cached:
        P = cache_kc.shape[2]
        in_specs += [pl.BlockSpec((1, 1, P, 2 * D), lambda b, i: (b, layer_j, 0, 0))]
        args += [cache_kc]
    return pl.pallas_call(
        functools.partial(_mla_kernel, cached=cached),
        grid=(B, nq),
        in_specs=in_specs,
        out_specs=pl.BlockSpec((tq, H * D), lambda b, i: (b * nq + i, 0)),
        out_shape=jax.ShapeDtypeStruct((T, H * D), MIX_DTYPE),
        compiler_params=_cparams(("arbitrary", "arbitrary")),
        name="mla_attention",
    )(*args)


def _even_weights(even_in_w, gdn_conv_w, gdn_a_log, gdn_dt_bias, hyena_conv_w, hyena_conv_b, j):
    w = even_in_w[j]
    qkv_w = 3 * GDN_HEADS * LANES
    n_ab = 4 * GDN_HEADS
    w_main = jnp.concatenate([w[:, :qkv_w], w[:, qkv_w + n_ab:]], axis=1).astype(BF16)
    w_small = _pad_to(w[:, qkv_w:qkv_w + n_ab], (w.shape[0], LANES)).astype(BF16)
    n_dir = 2 * GDN_HEADS
    alog_row = _pad_to(gdn_a_log[j].reshape(1, n_dir), (1, LANES))
    dtb_row = _pad_to(gdn_dt_bias[j].reshape(1, n_dir), (1, LANES))
    return w_main, w_small, alog_row, dtb_row


def _even_layer(x2, B, L, shift, scale, gate, tiles_per_mod, pre_g, post_g, wts, hy, out_w,
                gdn_conv_w, gdn_norm_g, hy_conv_w, hy_conv_b, hy_skip, state_f, state_b, j):
    w_main, w_small, alog_row, dtb_row = wts
    fwd, inv, hc, hs = hy
    z, zab = _in_proj(x2, shift, scale, pre_g, w_main, w_small, tiles_per_mod)
    oa, s_f, s_b = _gdn(z, zab, gdn_conv_w, gdn_norm_g, alog_row, dtb_row, B, L, state_f, state_b, j)
    ncb = HY_CH // HY_CH_BLOCK
    col = lambda idx: idx * ncb
    z1 = _hy_conv(z, col(4), z, col(5), col(7), hy_conv_w, hy_conv_b, hy_skip, 0, fwd, inv, hc, hs, B, L,
                  first=True, last=False)
    ob = _hy_conv(z1, 0, z, col(6), col(7), hy_conv_w, hy_conv_b, hy_skip, 1, fwd, inv, hc, hs, B, L,
                  first=False, last=True)
    x_new = _out_proj(oa, ob, x2, gate, post_g, out_w, tiles_per_mod)
    return x_new, s_f, s_b


def _odd_weights(odd_in_w, mla_q_up, mla_kv_up, j):
    w = odd_in_w[j]
    D = HEAD_DIM
    o = np.cumsum([0, 4 * D, 2 * D, 2 * D, 4 * D, 2 * D, D, MLA_ROPE_DIM, 4 * D])
    gq, gk, gv, gg, mq, mkv, mpe, mg = [w[:, o[i]:o[i + 1]] for i in range(8)]
    w_main = jnp.concatenate([gq, gg, mg, gk, gv, mq, mkv, _pad_to(mpe, (w.shape[0], D))], axis=1).astype(BF16)
    up = mla_q_up[j].reshape(-1, MLA_HEADS, D + MLA_ROPE_DIM)
    rope_cols = _pad_to(up[:, :, D:], (up.shape[0], MLA_HEADS, D))
    mq_up = jnp.concatenate([up[:, :, :D].reshape(-1, MLA_HEADS * D),
                             rope_cols.reshape(-1, MLA_HEADS * D)], axis=1).astype(BF16)
    kv = mla_kv_up[j].reshape(-1, MLA_HEADS, 2 * D)
    wk = kv[:, :, :D].transpose(1, 0, 2).astype(BF16)
    wv = kv[:, :, D:].transpose(1, 0, 2).astype(BF16)
    return w_main, mq_up, wk, wv


def _odd_layer(x2, B, L, shift, scale, gate, tiles_per_mod, pre_g, post_g, wts, out_w,
               q_g, k_g, mq_g, mkv_g, caches, j):
    w_main, mq_up, wk, wv = wts
    rope = caches is not None
    qg, qc, k16, v16, kc16, gates, kg, v32, ckv, kpe = _odd_in(
        x2, shift, scale, pre_g, w_main, q_g, k_g, mq_g, mq_up, mkv_g, wk, L, tiles_per_mod, rope)
    if rope:
        ck, cv, ckc = caches
        og = _gqa(qg, k16, v16, gates, B, L, GQA_Q_TILE, ck, cv, j)
        om = _mla(qc, kc16, gates, wv, B, L, MLA_Q_TILE, ckc, j)
    else:
        og = _gqa(qg, k16, v16, gates, B, L, GQA_Q_TILE)
        om = _mla(qc, kc16, gates, wv, B, L, MLA_Q_TILE)
    x_new = _out_proj(og, om, x2, gate, post_g, out_w, tiles_per_mod)
    return x_new, (kg, v32, ckv, kpe)


def kernel(x_prompt, x_sample, state_gdn_fwd, state_gdn_bwd, cache_gqa_k, cache_gqa_v, cache_mla_ckv, cache_mla_kpe, c, c_ctx, mod_w, mod_b, pre_norm_g, post_norm_g, even_in_w, gdn_conv_w, gdn_a_log, gdn_dt_bias, gdn_norm_g, hyena_conv_w, hyena_conv_b, hyena_ffn_w1, hyena_ffn_b1, hyena_ffn_w2, hyena_ffn_b2, hyena_ffn_w3, hyena_sin_freq, hyena_bias, even_out_w, odd_in_w, gqa_q_norm_g, gqa_k_norm_g, mla_q_norm_g, mla_q_up, mla_kv_norm_g, mla_kv_up, odd_out_w):
    Bp, Lp, D = x_prompt.shape
    Bs, Ls, _ = x_sample.shape
    depth = mod_w.shape[0]
    xp = x_prompt.reshape(Bp * Lp, D)
    xs = x_sample.reshape(Bs * Ls, D)

    n_cond = 1 + Bs
    rows = -(-n_cond // 8) * 8
    cond = _pad_to(jnp.concatenate([c_ctx[None, :], c], axis=0), (rows, D))
    mod = _modulation(cond, mod_w, mod_b)

    P = cache_gqa_k.shape[2]
    n_odd = cache_gqa_k.shape[1]
    ck = cache_gqa_k.reshape(Bs, n_odd, P, GQA_KV_HEADS * HEAD_DIM).astype(BF16)
    cv = cache_gqa_v.reshape(Bs, n_odd, P, GQA_KV_HEADS * HEAD_DIM).astype(BF16)
    ckc = jnp.concatenate([cache_mla_ckv, _pad_to(cache_mla_kpe, cache_mla_ckv.shape)], axis=-1).astype(BF16)

    dft = {L: _dft_blocks(L, min(HY_FREQ_BLOCK, L)) for L in (Lp, Ls)}
    tpm_s = Ls // TOKEN_TILE

    new_f, new_b, new_gk, new_gv, new_ckv, new_kpe = [], [], [], [], [], []
    for i in range(depth):
        j = i // 2
        m = mod[i]
        sh_p, sc_p, gt_p = [m[0:1, k * D:(k + 1) * D].reshape(1, 1, D) for k in range(3)]
        sh_s, sc_s, gt_s = [m[1:n_cond, k * D:(k + 1) * D].reshape(Bs, 1, D) for k in range(3)]
        pre_g = pre_norm_g[i][None, :]
        post_g = post_norm_g[i][None, :]
        if i % 2 == 0:
            wts = _even_weights(even_in_w, gdn_conv_w, gdn_a_log, gdn_dt_bias, hyena_conv_w, hyena_conv_b, j)
            out_w = even_out_w[j].astype(BF16)
            hy = {}
            for L in (Lp, Ls):
                fwd, inv = dft[L]
                filt = _hy_filters(L, hyena_ffn_w1[j], hyena_ffn_b1[j], hyena_ffn_w2[j], hyena_ffn_b2[j],
                                   hyena_ffn_w3[j], hyena_sin_freq[j])
                hc, hs = _hy_spectrum(filt, fwd, L, min(HY_FREQ_BLOCK, L))
                hy[L] = (fwd, inv, hc, hs)
            common = (gdn_conv_w[j], gdn_norm_g[j][None, :], hyena_conv_w[j], hyena_conv_b[j][None, :], hyena_bias[j])
            xp, sf, sb = _even_layer(xp, Bp, Lp, sh_p, sc_p, gt_p, 0, pre_g, post_g, wts, hy[Lp], out_w,
                                     *common, None, None, j)
            xs, _, _ = _even_layer(xs, Bs, Ls, sh_s, sc_s, gt_s, tpm_s, pre_g, post_g, wts, hy[Ls], out_w,
                                   *common, state_gdn_fwd, state_gdn_bwd, j)
            new_f.append(sf)
            new_b.append(sb)
        else:
            wts = _odd_weights(odd_in_w, mla_q_up, mla_kv_up, j)
            out_w = odd_out_w[j].astype(BF16)
            norms = (gqa_q_norm_g[j][None, :], gqa_k_norm_g[j][None, :], mla_q_norm_g[j][None, :],
                     mla_kv_norm_g[j][None, :])
            xp, (kg, v32, ckv, kpe) = _odd_layer(xp, Bp, Lp, sh_p, sc_p, gt_p, 0, pre_g, post_g, wts, out_w,
                                                 *norms, None, j)
            xs, _ = _odd_layer(xs, Bs, Ls, sh_s, sc_s, gt_s, tpm_s, pre_g, post_g, wts, out_w,
                               *norms, (ck, cv, ckc), j)
            new_gk.append(kg.reshape(Bp, Lp, GQA_KV_HEADS, HEAD_DIM))
            new_gv.append(v32.reshape(Bp, Lp, GQA_KV_HEADS, HEAD_DIM))
            new_ckv.append(ckv.reshape(Bp, Lp, HEAD_DIM))
            new_kpe.append(kpe[:, :MLA_ROPE_DIM].reshape(Bp, Lp, MLA_ROPE_DIM))
    return (xp.reshape(Bp, Lp, D), xs.reshape(Bs, Ls, D),
            jnp.stack(new_f, axis=1), jnp.stack(new_b, axis=1),
            jnp.stack(new_gk, axis=1), jnp.stack(new_gv, axis=1),
            jnp.stack(new_ckv, axis=1), jnp.stack(new_kpe, axis=1))
```

```python
import functools
import math

import numpy as np
import jax
import jax.numpy as jnp
from jax import lax
from jax.experimental import pallas as pl
from jax.experimental.pallas import tpu as pltpu

F32 = jnp.float32
BF16 = jnp.bfloat16
MIX_DTYPE = BF16

NORM_EPS = 1e-6
ROPE_THETA = 10000.0
GRID_W = 64

GDN_HEADS = 4
GDN_DK = 128
GDN_CHUNK = 64
GDN_CONV_W = 5
GDN_PREP_CHAINS = 32
GDN_ALL_HEADS_MAX_LEN = 256
GDN_HEADS_LONG = 1
HY_CH = 512
HY_BANDS = 16
HY_DECAY_TARGET = 1e-2
HY_SHORT_DECAY_PCT = 0.3
HY_LONG_DECAY_PCT = 1.5
GQA_HEADS = 4
GQA_KV_HEADS = 2
HEAD_DIM = 128
MLA_HEADS = 4
MLA_ROPE_DIM = 64

LOG2E = math.log2(math.e)

LANES = 128
VMEM_LIMIT = 56 * 1024 * 1024

TOKEN_TILE = 256
HY_FREQ_BLOCK = 512
HY_HALF_FREQ_BLOCK = 256
HY_CH_BLOCK = 256
GQA_Q_TILE = 256
MLA_Q_TILE = 128
MLA_HEAD_STACKS = 2


def _cparams(sem):
    return pltpu.CompilerParams(dimension_semantics=sem, vmem_limit_bytes=VMEM_LIMIT)


def _dot(a, b):
    return jnp.dot(a, b, preferred_element_type=F32)


def _dot_nt(a, b):
    return lax.dot_general(a, b, (((1,), (1,)), ((), ())), preferred_element_type=F32)


def _dot_tn(a, b):
    return lax.dot_general(a, b, (((0,), (0,)), ((), ())), preferred_element_type=F32)


def _split(a):
    hi = a.astype(BF16)
    lo = (a - hi.astype(F32)).astype(BF16)
    return hi, lo


def _dot3(a, b):
    ah, al = _split(a)
    bh, bl = _split(b)
    return _dot(ah, bh) + (_dot(ah, bl) + _dot(al, bh))


def _bdot(a, b):
    return jnp.einsum('gij,gjk->gik', a, b, preferred_element_type=F32)


def _bdot16(a, b):
    return _bdot(a.astype(BF16), b.astype(BF16))


def _silu(x):
    return x * (1.0 / (1.0 + jnp.exp(-x)))


def _sigmoid(x):
    return 1.0 / (1.0 + jnp.exp(-x))


def _softplus(x):
    return jnp.maximum(x, 0.0) + jnp.log(1.0 + jnp.exp(-jnp.abs(x)))


def _rms(x, g):
    return x * lax.rsqrt(jnp.mean(x * x, axis=-1, keepdims=True) + NORM_EPS) * g


def _shift_rows(x, s):
    L = x.shape[0]
    if s == 0:
        return x
    rolled = pltpu.roll(x, (-s) % L, 0)
    row = lax.broadcasted_iota(jnp.int32, x.shape, 0)
    valid = (row + s >= 0) & (row + s < L)
    return jnp.where(valid, rolled, 0.0)


def _mod_kernel(c_ref, w_ref, b_ref, o_ref):
    c = _silu(c_ref[...])
    o_ref[0] = _dot3(c, w_ref[0]) + b_ref[0]


def _modulation(cond, mod_w, mod_b):
    depth, d, d3 = mod_w.shape
    r = cond.shape[0]
    nb = d3 // d
    return pl.pallas_call(
        _mod_kernel,
        grid=(depth, nb),
        in_specs=[pl.BlockSpec((r, d), lambda i, n: (0, 0)),
                  pl.BlockSpec((1, d, d), lambda i, n: (i, 0, n)),
                  pl.BlockSpec((1, 1, d), lambda i, n: (i, 0, n))],
        out_specs=pl.BlockSpec((1, r, d), lambda i, n: (i, 0, n)),
        out_shape=jax.ShapeDtypeStruct((depth, r, d3), F32),
        compiler_params=_cparams(("arbitrary", "arbitrary")),
        name="adaln_modulation",
    )(cond, mod_w, mod_b.reshape(depth, 1, d3))


def _in_proj_kernel(x_ref, sh_ref, sc_ref, g_ref, w_ref, *rest, n_main, has_small):
    if has_small:
        ws_ref, o_ref, os_ref = rest
    else:
        (o_ref,) = rest
    x = x_ref[...]
    h = _rms(x, g_ref[...]) * (1.0 + sc_ref[0]) + sh_ref[0]
    hb = h.astype(BF16)
    step = 512
    for n0 in range(0, n_main, step):
        o_ref[:, n0:n0 + step] = _dot(hb, w_ref[:, n0:n0 + step])
    if has_small:
        os_ref[...] = _dot(hb, ws_ref[...])


def _in_proj(x2, shift, scale, pre_g, w_main, w_small, tiles_per_mod):
    T, D = x2.shape
    N = w_main.shape[1]
    tm = TOKEN_TILE
    if tiles_per_mod:
        mod_map = lambda i: (i // tiles_per_mod, 0, 0)
    else:
        mod_map = lambda i: (0, 0, 0)
    in_specs = [pl.BlockSpec((tm, D), lambda i: (i, 0)),
                pl.BlockSpec((1, 1, D), mod_map),
                pl.BlockSpec((1, 1, D), mod_map),
                pl.BlockSpec((1, D), lambda i: (0, 0)),
                pl.BlockSpec((D, N), lambda i: (0, 0))]
    out_specs = [pl.BlockSpec((tm, N), lambda i: (i, 0))]
    out_shape = [jax.ShapeDtypeStruct((T, N), F32)]
    args = [x2, shift, scale, pre_g, w_main]
    if w_small is not None:
        in_specs.append(pl.BlockSpec((D, LANES), lambda i: (0, 0)))
        out_specs.append(pl.BlockSpec((tm, LANES), lambda i: (i, 0)))
        out_shape.append(jax.ShapeDtypeStruct((T, LANES), F32))
        args.append(w_small)
    res = pl.pallas_call(
        functools.partial(_in_proj_kernel, n_main=N, has_small=w_small is not None),
        grid=(T // tm,),
        in_specs=in_specs, out_specs=out_specs, out_shape=out_shape,
        compiler_params=_cparams(("arbitrary",)),
        name="in_proj",
    )(*args)
    return res


def _out_proj_kernel(a_ref, be_ref, bo_ref, x_ref, gt_ref, g_ref, w_ref, o_ref, *, half, dm):
    for p, b_ref in ((0, be_ref), (1, bo_ref)):
        y = (_dot(a_ref[:, p * half:(p + 1) * half].astype(BF16), w_ref[:half, :])
             + _dot(b_ref[...].astype(BF16), w_ref[half:, :]))
        o_ref[:, p * dm:(p + 1) * dm] = x_ref[:, p * dm:(p + 1) * dm] + gt_ref[0] * _rms(y, g_ref[...])


def _out_proj(a, b_even, b_odd, x2, gate, post_g, w, tiles_per_mod):
    T, D = x2.shape
    half = a.shape[1]
    tm = TOKEN_TILE // 2
    if tiles_per_mod:
        mod_map = lambda i: (i // tiles_per_mod, 0, 0)
    else:
        mod_map = lambda i: (0, 0, 0)
    out = pl.pallas_call(
        functools.partial(_out_proj_kernel, half=half, dm=D),
        grid=(T // TOKEN_TILE,),
        in_specs=[pl.BlockSpec((tm, 2 * half), lambda i: (i, 0)),
                  pl.BlockSpec((tm, half), lambda i: (i, b_even[1])),
                  pl.BlockSpec((tm, half), lambda i: (i, b_odd[1])),
                  pl.BlockSpec((tm, 2 * D), lambda i: (i, 0)),
                  pl.BlockSpec((1, 1, D), mod_map),
                  pl.BlockSpec((1, D), lambda i: (0, 0)),
                  pl.BlockSpec((2 * half, D), lambda i: (0, 0))],
        out_specs=pl.BlockSpec((tm, 2 * D), lambda i: (i, 0)),
        out_shape=jax.ShapeDtypeStruct((T // 2, 2 * D), F32),
        compiler_params=_cparams(("arbitrary",)),
        name="out_proj",
    )(a.reshape(T // 2, 2 * half), b_even[0], b_odd[0], x2.reshape(T // 2, 2 * D), gate, post_g, w)
    return out.reshape(T, D)


def _gdn_kernel(zq_ref, zk_ref, zv_ref, zg_ref, zab_ref, cwq_ref, cwk_ref, cwv_ref, ng_ref,
                alog_ref, dtb_ref, *rest, L, HB, has_state):
    if has_state:
        s0f_ref, s0b_ref = rest[:2]
        rest = rest[2:]
    (o_ref, sf_out_ref, sb_out_ref, q_s, k_s, v_s, g_s, b_s, st_s,
     oc_s, qe_s, sc_s, sm_s, gl_s, pad_s) = rest
    C = GDN_CHUNK
    n_chunks = L // C
    U = min(GDN_PREP_CHAINS // (2 * HB), n_chunks)
    HU = HB * U
    G = 2 * HU
    head0 = pl.program_id(1) * HB

    pad = 8
    pad_s[0:pad, :] = jnp.zeros((pad, LANES), F32)
    pad_s[pad + L:2 * pad + L, :] = jnp.zeros((pad, LANES), F32)

    def conv_silu(x, w_ref, lanes):
        half = GDN_CONV_W // 2
        pad_s[pad:pad + L, :] = x
        acc = x * w_ref[half:half + 1, lanes]
        for i in range(GDN_CONV_W):
            if i != half:
                start = pad + i - half
                acc = acc + pad_s[start:start + L, :] * w_ref[i:i + 1, lanes]
        return _silu(acc)

    def l2n(x):
        return x * lax.rsqrt(jnp.sum(x * x, axis=-1, keepdims=True) + NORM_EPS)

    for hh in range(HB):
        lanes = slice(hh * LANES, (hh + 1) * LANES)
        q_s[hh] = l2n(conv_silu(zq_ref[:, lanes], cwq_ref, lanes)) * (GDN_DK ** -0.5)
        k_s[hh] = l2n(conv_silu(zk_ref[:, lanes], cwk_ref, lanes))
        v_s[hh] = conv_silu(zv_ref[:, lanes], cwv_ref, lanes)
        if has_state:
            st_s[2 * hh] = s0f_ref[0, 0, hh]
            st_s[2 * hh + 1] = s0b_ref[0, 0, hh]
    if not has_state:
        st_s[...] = jnp.zeros_like(st_s)
    zab = zab_ref[...]
    g_s[...] = -jnp.exp(alog_ref[...]) * _softplus(zab + dtb_ref[...])
    b_s[...] = _sigmoid(zab)

    def iota(shape, axis):
        return lax.broadcasted_iota(jnp.int32, shape, axis)

    def direction(shape):
        return iota(shape, 0) & 1

    chain_shift = (2 * U).bit_length() - 1
    assert 2 * U == 1 << chain_shift

    def chain_head(shape):
        return head0 + (iota(shape, 0) >> chain_shift)

    sq = (G, C, C)
    row = iota(sq, 1)
    col = iota(sq, 2)
    signed = (row - col) * (1 - 2 * direction(sq))
    incl = signed >= 0
    strict = signed > 0
    eye = (row == col).astype(F32)
    same = [(row >> s) == (col >> s) for s in (3, 4, 5)]
    off_blocks = [same[1] & jnp.logical_not(same[0]), same[2] & jnp.logical_not(same[1]),
                  jnp.logical_not(same[2])]
    wide = (G, C, LANES)
    sel_lane = direction(wide) * GDN_HEADS + chain_head(wide)
    mask_g = iota(wide, 2) == sel_lane
    mask_b = iota(wide, 2) == sel_lane + 2 * GDN_HEADS
    tall = (G, LANES, C)
    mask_t = iota(tall, 1) == direction(tall) * GDN_HEADS + chain_head(tall)
    colv = (G, C, 1)
    mask_last = iota(colv, 1) == (1 - direction(colv)) * (C - 1)
    r2 = lax.broadcasted_iota(jnp.int32, (C, C), 0)
    c2 = lax.broadcasted_iota(jnp.int32, (C, C), 1)
    tri = jnp.concatenate([(r2 >= c2).astype(BF16), (r2 <= c2).astype(BF16)], axis=0)

    def both_dirs(x):
        return jnp.broadcast_to(x[:, None], (HU, 2) + x.shape[1:]).reshape((G,) + x.shape[1:])

    def all_heads(x):
        return jnp.broadcast_to(x[None], (HB,) + x.shape).reshape((HB * x.shape[0],) + x.shape[1:])

    def prep_group(c, carry):
        rows = pl.ds(pl.multiple_of(c * (U * C), U * C), U * C)
        q = q_s[:, rows, :].reshape(HU, C, LANES)
        k = k_s[:, rows, :].reshape(HU, C, LANES)
        v = v_s[:, rows, :].reshape(HU, C, LANES)
        ball = all_heads(b_s[rows, :].reshape(U, C, LANES))
        g_hi, g_lo = _split(g_s[rows, :])
        g_hi = g_hi.reshape(U, C, LANES)
        g_lo = g_lo.reshape(U, C, LANES)
        gcum = jnp.stack([_dot(tri, g_hi[u]) + _dot(tri, g_lo[u]) for u in range(U)])
        gcum = gcum.reshape(2 * U, C, LANES)
        gcum_t = all_heads(jnp.stack([gcum[g].T for g in range(2 * U)]))
        gcum = all_heads(gcum)
        gc = jnp.sum(jnp.where(mask_g, gcum, 0.0), axis=2, keepdims=True)
        gr = jnp.sum(jnp.where(mask_t, gcum_t, 0.0), axis=1, keepdims=True)
        beta = jnp.sum(jnp.where(mask_b, both_dirs(ball), 0.0), axis=2, keepdims=True)
        kbf = k.astype(BF16)
        kq = jnp.einsum('uik,ujk->uij', jnp.concatenate([k, q], axis=1).astype(BF16), kbf,
                        preferred_element_type=F32)
        kk = both_dirs(kq[:, :C])
        qk = both_dirs(kq[:, C:])
        decay = jnp.where(incl, jnp.exp(jnp.where(incl, gc - gr, 0.0)), 0.0)
        lmat = jnp.where(strict, beta * kk * decay, 0.0)
        p = jnp.where(same[0], -lmat, 0.0)
        tmat = eye + p
        for _ in range(2):
            p = _bdot16(p, p)
            tmat = tmat + _bdot16(tmat, p)
        for off in off_blocks:
            t16 = tmat.astype(BF16)
            tmat = tmat - _bdot(_bdot(t16, jnp.where(off, lmat, 0.0).astype(BF16)).astype(BF16), t16)
        eg = jnp.exp(gc)
        k2 = both_dirs(k)
        kb = k2 * beta
        sol = _bdot16(tmat, jnp.concatenate([both_dirs(v) * beta, kb * eg], axis=2))
        glast = jnp.sum(jnp.where(mask_last, gc, 0.0), axis=1, keepdims=True)
        kdec = k2 * jnp.exp(glast - gc)
        kdec_t = jnp.stack([kdec[g].T for g in range(G)]).astype(BF16)
        sol16 = sol.astype(BF16)
        r_in = _bdot((qk * decay).astype(BF16), sol16)
        r_kd = _bdot(kdec_t, sol16)
        def put(ref, val, size):
            n = 2 * U * size
            ref[:, pl.ds(pl.multiple_of(c * n, n), n), :] = val.reshape(HB, n, val.shape[-1])

        put(oc_s, r_in[:, :, :LANES], C)
        put(qe_s, (both_dirs(q) * eg - r_in[:, :, LANES:]).astype(BF16), C)
        put(sc_s, r_kd[:, :, :LANES], LANES)
        put(sm_s, (-r_kd[:, :, LANES:]).astype(BF16), LANES)
        put(gl_s, jnp.broadcast_to(jnp.exp(glast), (G, 8, LANES)), 8)
        return carry

    lax.fori_loop(0, n_chunks // U, prep_group, 0)

    def scan_step(c, carry):
        slot_f = c * 2
        slot_b = (n_chunks - 1 - c) * 2 + 1

        def ld(ref, size):
            parts = []
            for hh in range(HB):
                parts.append(ref[hh, pl.ds(pl.multiple_of(slot_f * size, size), size), :])
                parts.append(ref[hh, pl.ds(pl.multiple_of(slot_b * size, size), size), :])
            return jnp.stack(parts)

        s = st_s[...]
        sb16 = s.astype(BF16)
        o = _bdot(ld(qe_s, C), sb16) + ld(oc_s, C)
        st_s[...] = s * ld(gl_s, 8)[:, 0:1, :] + (_bdot(ld(sm_s, LANES), sb16) + ld(sc_s, LANES))
        for hh in range(HB):
            oc_s[hh, pl.ds(pl.multiple_of(slot_f * C, C), C), :] = o[2 * hh]
            oc_s[hh, pl.ds(pl.multiple_of(slot_b * C, C), C), :] = o[2 * hh + 1]
        return carry

    lax.fori_loop(0, n_chunks, scan_step, 0)

    FIN = 4

    def finish(i, carry):
        rows = pl.ds(pl.multiple_of(i * (FIN * C), FIN * C), FIN * C)
        outs = []
        for hh in range(HB):
            both = oc_s[hh, pl.ds(pl.multiple_of(i * (FIN * 2 * C), FIN * 2 * C), FIN * 2 * C), :]
            both = both.reshape(FIN, 2, C, LANES)
            outs.append(_rms((both[:, 0] + both[:, 1]).reshape(FIN * C, LANES), ng_ref[...]))
        o = jnp.concatenate(outs, axis=1) * _silu(zg_ref[rows, :])
        o_ref[rows, :] = o.astype(o_ref.dtype)
        return carry

    lax.fori_loop(0, n_chunks // FIN, finish, 0)
    for hh in range(HB):
        sf_out_ref[0, hh] = st_s[2 * hh]
        sb_out_ref[0, hh] = st_s[2 * hh + 1]


def _gdn(z, zab, conv_w, norm_g, alog_row, dtb_row, B, L, state_f=None, state_b=None, layer_j=0):
    T = z.shape[0]
    H = GDN_HEADS
    has_state = state_f is not None
    HB = H if L <= GDN_ALL_HEADS_MAX_LEN else GDN_HEADS_LONG
    nh = H // HB
    w = HB * LANES
    blk = lambda i: pl.BlockSpec((L, w), lambda b, h: (b, i * nh + h))
    cw = lambda i: pl.BlockSpec((GDN_CONV_W, w), lambda b, h: (0, i * nh + h))
    in_specs = [blk(0), blk(1), blk(2), blk(3),
                pl.BlockSpec((L, LANES), lambda b, h: (b, 0)),
                cw(0), cw(1), cw(2),
                pl.BlockSpec((1, LANES), lambda b, h: (0, 0)),
                pl.BlockSpec((1, LANES), lambda b, h: (0, 0)),
                pl.BlockSpec((1, LANES), lambda b, h: (0, 0))]
    args = [z, z, z, z, zab, conv_w, conv_w, conv_w, norm_g, alog_row, dtb_row]
    if has_state:
        st = pl.BlockSpec((1, 1, HB, GDN_DK, LANES), lambda b, h: (b, layer_j, h, 0, 0))
        in_specs += [st, st]
        args += [state_f, state_b]
    sout = pl.BlockSpec((1, HB, GDN_DK, LANES), lambda b, h: (b, h, 0, 0))
    n_slots = 2 * (L // GDN_CHUNK)
    scratch = ([pltpu.VMEM((HB, L, LANES), F32) for _ in range(3)]
               + [pltpu.VMEM((L, LANES), F32) for _ in range(2)]
               + [pltpu.VMEM((2 * HB, GDN_DK, LANES), F32)]
               + [pltpu.VMEM((HB, 2 * L, LANES), F32), pltpu.VMEM((HB, 2 * L, LANES), BF16)]
               + [pltpu.VMEM((HB, n_slots * GDN_DK, LANES), F32),
                  pltpu.VMEM((HB, n_slots * GDN_DK, LANES), BF16)]
               + [pltpu.VMEM((HB, n_slots * 8, LANES), F32)]
               + [pltpu.VMEM((L + 16, LANES), F32)])
    return pl.pallas_call(
        functools.partial(_gdn_kernel, L=L, HB=HB, has_state=has_state),
        grid=(B, nh),
        in_specs=in_specs,
        out_specs=[pl.BlockSpec((L, w), lambda b, h: (b, h)), sout, sout],
        out_shape=[jax.ShapeDtypeStruct((T, H * LANES), MIX_DTYPE),
                   jax.ShapeDtypeStruct((B, H, GDN_DK, LANES), F32),
                   jax.ShapeDtypeStruct((B, H, GDN_DK, LANES), F32)],
        scratch_shapes=scratch,
        compiler_params=_cparams(("arbitrary", "arbitrary")),
        name="gdn",
    )(*args)


def _hy_filter_kernel(f_ref, w1_ref, b1_ref, sf0_ref, w2_ref, b2_ref, sf1_ref, w3_ref, dl_ref, o_ref, *, L):
    rowi = lax.broadcasted_iota(jnp.int32, (L, LANES), 0).astype(F32)
    lane = lax.broadcasted_iota(jnp.int32, (L, LANES), 1)
    t = rowi * (1.0 / (L - 1))
    wpos = rowi * (2.0 * math.pi / L)
    ang = f_ref[...] * wpos
    z = jnp.where(lane == 0, t,
                  jnp.where(lane <= HY_BANDS, jnp.cos(ang),
                            jnp.where(lane <= 2 * HY_BANDS, -jnp.sin(ang), 0.0)))
    h = jnp.sin(sf0_ref[...] * (_dot3(z, w1_ref[...]) + b1_ref[...]))
    h = jnp.sin(sf1_ref[...] * (_dot3(h, w2_ref[...]) + b2_ref[...]))
    tc = lax.broadcasted_iota(jnp.int32, (L, HY_CH), 0).astype(F32) * (1.0 / (L - 1))
    window = jnp.exp(-tc * dl_ref[...])
    for j in range(4):
        o_ref[:, j * HY_CH:(j + 1) * HY_CH] = _dot3(h, w3_ref[:, j * HY_CH:(j + 1) * HY_CH]) * window


def _pad_to(a, shape):
    return jnp.pad(a, [(0, s - d) for s, d in zip(shape, a.shape)])


def _hy_filters(L, w1, b1, w2, b2, w3, sin_freq):
    fvals = np.linspace(1e-4, HY_BANDS - 1, HY_BANDS, dtype=np.float32)
    frow = np.zeros((1, LANES), np.float32)
    frow[0, 1:1 + HY_BANDS] = fvals
    frow[0, 1 + HY_BANDS:1 + 2 * HY_BANDS] = fvals
    deltas = np.abs(np.linspace(math.log(HY_DECAY_TARGET) / HY_LONG_DECAY_PCT,
                                math.log(HY_DECAY_TARGET) / HY_SHORT_DECAY_PCT, HY_CH, dtype=np.float32))
    args = [jnp.asarray(frow),
            _pad_to(w1, (LANES, LANES)), _pad_to(b1[None, :], (1, LANES)), _pad_to(sin_freq[0][None, :], (1, LANES)),
            _pad_to(w2, (LANES, LANES)), _pad_to(b2[None, :], (1, LANES)), _pad_to(sin_freq[1][None, :], (1, LANES)),
            _pad_to(w3, (LANES, 4 * HY_CH)), jnp.asarray(deltas[None, :])]
    return pl.pallas_call(
        functools.partial(_hy_filter_kernel, L=L),
        out_shape=jax.ShapeDtypeStruct((L, 4 * HY_CH), F32),
        compiler_params=pltpu.CompilerParams(vmem_limit_bytes=VMEM_LIMIT),
        name="hyena_filters",
    )(*args)


def _dft_tables(L):
    N = 2 * L
    k = np.arange(L, dtype=np.int64)[:, None]
    s = np.arange(L, dtype=np.int64)[None, :]
    ang = ((2 * k + 1) * s % (2 * N)).astype(np.float64) * (2.0 * math.pi / (2 * N))
    return np.cos(ang).astype(np.float32), np.sin(ang).astype(np.float32)


def _dft_blocks(L, kb):
    cm, sm = _dft_tables(L)
    nk = L // kb
    fwd = np.concatenate([cm.reshape(nk, kb, L), sm.reshape(nk, kb, L)], axis=1)
    return jnp.asarray(fwd).astype(BF16)


def _hy_spectrum_kernel(f_ref, flt_ref, hc_ref, hs_ref, *, L, kb):
    row = lax.broadcasted_iota(jnp.int32, (L, HY_CH), 0)
    scale = 1.0 / L
    for o in range(2):
        hf = flt_ref[:, (2 * o) * HY_CH:(2 * o + 1) * HY_CH]
        hb = jnp.where(row == 0, 0.0, flt_ref[:, (2 * o + 1) * HY_CH:(2 * o + 2) * HY_CH])
        a_hi, a_lo = _split(hf + hb)
        d_hi, d_lo = _split(hf - hb)
        fc = f_ref[0, :kb, :]
        fs = f_ref[0, kb:, :]
        hc_ref[o] = (_dot(fc, a_hi) + _dot(fc, a_lo)) * scale
        hs_ref[o] = (_dot(fs, d_hi) + _dot(fs, d_lo)) * scale


def _hy_spectrum(filt, fwd, L, kb):
    nk = L // kb
    return pl.pallas_call(
        functools.partial(_hy_spectrum_kernel, L=L, kb=kb),
        grid=(nk,),
        in_specs=[pl.BlockSpec((1, 2 * kb, L), lambda i: (i, 0, 0)),
                  pl.BlockSpec((L, 4 * HY_CH), lambda i: (0, 0))],
        out_specs=[pl.BlockSpec((2, kb, HY_CH), lambda i: (0, i, 0)),
                   pl.BlockSpec((2, kb, HY_CH), lambda i: (0, i, 0))],
        out_shape=[jax.ShapeDtypeStruct((2, L, HY_CH), F32)] * 2,
        compiler_params=_cparams(("arbitrary",)),
        name="hyena_spectrum",
    )(fwd, filt)


def _hy_conv_kernel(ze_ref, zo_ref, xe_ref, xo_ref, ge_ref, go_ref, cwz_ref, cbz_ref, cwx_ref, cbx_ref, skip_ref,
                    f_ref, g_ref, har_ref, hai_ref, hbr_ref, hbi_ref, cw_ref, sw_ref, oe_ref, oo_ref,
                    *, kb, nk, first, last):
    cb = ze_ref.shape[1]

    def conv3(e_ref, o_ref, w_ref, b_ref):
        xe = e_ref[...]
        xo = o_ref[...]
        w0, w1, w2 = w_ref[0:1, :], w_ref[1:2, :], w_ref[2:3, :]
        ye = _shift_rows(xo, -1) * w0 + xe * w1 + xo * w2 + b_ref[...]
        yo = xe * w0 + xo * w1 + _shift_rows(xe, 1) * w2 + b_ref[...]
        return ye, yo

    if first:
        ze, zo = conv3(ze_ref, zo_ref, cwz_ref, cbz_ref)
    else:
        ze, zo = ze_ref[...], zo_ref[...]
    zb = jnp.concatenate([ze, zo], axis=1).astype(BF16)
    acc = None
    for k in range(nk):
        rows = slice(k * kb, (k + 1) * kb)
        xc = _dot(f_ref[k, :kb, :], zb)
        xs = _dot(f_ref[k, kb:, :], zb)
        ec, oc = xc[:, :cb], xc[:, cb:]
        es, os_ = xs[:, :cb], xs[:, cb:]
        cw = cw_ref[rows, :]
        sw = sw_ref[rows, :]
        p_re = cw * oc - sw * os_
        p_im = -(cw * os_ + sw * oc)
        xa_re, xa_im = ec + p_re, p_im - es
        xb_re, xb_im = ec - p_re, es + p_im
        har, hai = har_ref[0, rows, :], hai_ref[0, rows, :]
        hbr, hbi = hbr_ref[0, rows, :], hbi_ref[0, rows, :]
        ya_re = xa_re * har - xa_im * hai
        ya_im = xa_re * hai + xa_im * har
        yb_re = xb_re * hbr - xb_im * hbi
        yb_im = xb_re * hbi + xb_im * hbr
        ze_re, ze_im = ya_re + yb_re, ya_im - yb_im
        d_re, d_im = ya_re - yb_re, ya_im + yb_im
        zo_re = d_re * cw - d_im * sw
        zo_im = d_re * sw + d_im * cw
        spec = jnp.concatenate([jnp.concatenate([ze_re, zo_re], axis=1),
                                jnp.concatenate([-ze_im, -zo_im], axis=1)], axis=0).astype(BF16)
        part = _dot(g_ref[k], spec)
        acc = part if acc is None else acc + part
    xe, xo = conv3(xe_ref, xo_ref, cwx_ref, cbx_ref)
    re = xe * (acc[:, :cb] + ze * skip_ref[...])
    ro = xo * (acc[:, cb:] + zo * skip_ref[...])
    if last:
        re = re * _silu(ge_ref[...])
        ro = ro * _silu(go_ref[...])
    oe_ref[...] = re.astype(oe_ref.dtype)
    oo_ref[...] = ro.astype(oo_ref.dtype)


def _hy_half_tables(L, kb, cb):
    M = L // 2
    nk = M // kb
    k = np.arange(M, dtype=np.int64)[:, None]
    s = np.arange(M, dtype=np.int64)[None, :]
    ang = ((2 * k + 1) * s % (2 * L)).astype(np.float64) * (2.0 * math.pi / (2 * L))
    c2, s2 = np.cos(ang).astype(np.float32), np.sin(ang).astype(np.float32)
    fwd = np.concatenate([c2.reshape(nk, kb, M), s2.reshape(nk, kb, M)], axis=1)
    inv = np.concatenate([c2.T.reshape(M, nk, kb), s2.T.reshape(M, nk, kb)], axis=2)
    inv = np.ascontiguousarray(inv.transpose(1, 0, 2))
    w = (2 * k + 1).astype(np.float64) * (2.0 * math.pi / (4 * L))
    cw = np.ascontiguousarray(np.broadcast_to(np.cos(w), (M, cb))).astype(np.float32)
    sw = np.ascontiguousarray(np.broadcast_to(np.sin(w), (M, cb))).astype(np.float32)
    return (jnp.asarray(fwd).astype(BF16), jnp.asarray(inv).astype(BF16), jnp.asarray(cw), jnp.asarray(sw))


def _hy_conv(z_even, z_odd, x_even, x_odd, g_even, g_odd, conv_w, conv_b, skip, order, tables, spectra,
             B, L, first, last):
    M = L // 2
    cb = HY_CH_BLOCK
    ncb = HY_CH // cb
    fwd, inv, cw, sw = tables
    nk, _, kb2 = inv.shape
    kb = kb2 // 2
    once = dict(pipeline_mode=pl.Buffered(1))
    data = lambda op: pl.BlockSpec((M, cb), lambda b, c: (b, op[1] + c))
    cwspec = lambda off: pl.BlockSpec((3, cb), lambda b, c: (0, off + c))
    cbspec = lambda off: pl.BlockSpec((1, cb), lambda b, c: (0, off + c))
    hspec = pl.BlockSpec((1, M, cb), lambda b, c: (order, 0, c))
    operands = [z_even, z_odd, x_even, x_odd, g_even, g_odd]
    in_specs = ([data(op) for op in operands]
                + [cwspec(0), cbspec(0), cwspec((1 + order) * ncb), cbspec((1 + order) * ncb),
                   pl.BlockSpec((1, cb), lambda b, c: (0, c)),
                   pl.BlockSpec(fwd.shape, lambda b, c: (0, 0, 0), **once),
                   pl.BlockSpec(inv.shape, lambda b, c: (0, 0, 0), **once),
                   hspec, hspec, hspec, hspec,
                   pl.BlockSpec((M, cb), lambda b, c: (0, 0), **once),
                   pl.BlockSpec((M, cb), lambda b, c: (0, 0), **once)])
    out_dtype = MIX_DTYPE if last else F32
    out = pl.BlockSpec((M, cb), lambda b, c: (b, c))
    return pl.pallas_call(
        functools.partial(_hy_conv_kernel, kb=kb, nk=nk, first=first, last=last),
        grid=(B, ncb),
        in_specs=in_specs,
        out_specs=[out, out],
        out_shape=[jax.ShapeDtypeStruct((B * M, HY_CH), out_dtype)] * 2,
        compiler_params=_cparams(("arbitrary", "arbitrary")),
        name="hyena_conv",
    )(*[op[0] for op in operands], conv_w, conv_b, conv_w, conv_b, skip[order:order + 1],
      fwd, inv, *spectra, cw, sw)


OD_GQ, OD_GG, OD_MG, OD_GK, OD_GV, OD_MQ, OD_MKV, OD_MPE = 0, 4, 8, 12, 14, 16, 18, 19
OD_WIDTH = 20 * LANES


def _rope_swap(x, quarter):
    n = x.shape[1]
    lane = lax.broadcasted_iota(jnp.int32, x.shape, 1)
    first = (lane & (2 * quarter - 1)) < quarter
    return jnp.where(first, pltpu.roll(x, n - quarter, 1), pltpu.roll(x, quarter, 1))


def _odd_in_kernel(x_ref, sh_ref, sc_ref, g_ref, w_ref, qg_ref, kg_ref, mqg_ref, mqup_ref, mkvg_ref, wk_ref,
                   *rest, rope):
    if rope:
        cg_ref, sg_ref, cm_ref, sm_ref = rest[:4]
        rest = rest[4:]
    q_out, qc_out, k16_out, v16_out, kc16_out, gate_out, k_out, v_out, ckv_out, kpe_out = rest
    D = HEAD_DIM
    hb = (_rms(x_ref[...], g_ref[...]) * (1.0 + sc_ref[0]) + sh_ref[0]).astype(BF16)

    def proj(col, width):
        return _dot(hb, w_ref[:, col * D:(col + width) * D])

    def rot(x, c, s, quarter):
        return x * c + _rope_swap(x, quarter) * s if rope else x

    gscale = (D ** -0.5) * LOG2E
    zq = proj(OD_GQ, GQA_HEADS)
    for h in range(GQA_HEADS):
        q = rot(_rms(zq[:, h * D:(h + 1) * D], qg_ref[...]), cg_ref[...] if rope else None,
                sg_ref[...] if rope else None, D // 4)
        q_out[:, h * D:(h + 1) * D] = (q * gscale).astype(BF16)
    zk = proj(OD_GK, GQA_KV_HEADS)
    zv = proj(OD_GV, GQA_KV_HEADS)
    v_out[...] = zv
    v16_out[...] = zv.astype(BF16)
    for h in range(GQA_KV_HEADS):
        kn = _rms(zk[:, h * D:(h + 1) * D], kg_ref[...])
        k_out[:, h * D:(h + 1) * D] = kn
        k16_out[:, h * D:(h + 1) * D] = rot(kn, cg_ref[...] if rope else None,
                                            sg_ref[...] if rope else None, D // 4).astype(BF16)
    gate_out[...] = proj(OD_GG, GQA_HEADS + MLA_HEADS)

    mq = _rms(proj(OD_MQ, 2), mqg_ref[...]).astype(BF16)
    qm = _dot(mq, mqup_ref[...])
    mscale = ((D + MLA_ROPE_DIM) ** -0.5) * LOG2E
    qpe = qm[:, MLA_HEADS * D:]
    if rope:
        qpe = rot(qpe, cm_ref[...], sm_ref[...], MLA_ROPE_DIM // 4)
    for h in range(MLA_HEADS):
        qn = qm[:, h * D:(h + 1) * D].astype(BF16)
        qc_out[:, 2 * h * D:(2 * h + 1) * D] = (_dot_nt(qn, wk_ref[h]) * mscale).astype(BF16)
        qc_out[:, (2 * h + 1) * D:(2 * h + 2) * D] = (qpe[:, h * D:(h + 1) * D] * mscale).astype(BF16)
    ckv = _rms(proj(OD_MKV, 1), mkvg_ref[...])
    ckv_out[...] = ckv
    kpe = proj(OD_MPE, 1)
    kpe_out[...] = kpe
    if rope:
        kpe = rot(kpe, cm_ref[:, :D], sm_ref[:, :D], MLA_ROPE_DIM // 4)
    kc16_out[:, :D] = ckv.astype(BF16)
    kc16_out[:, D:] = kpe.astype(BF16)


def _rope_tables(L, R):
    rows = np.arange(L) // GRID_W
    cols = np.arange(L) % GRID_W
    quarter = R // 4
    inv = ROPE_THETA ** (-np.arange(quarter, dtype=np.float32) * 2.0 / (R // 2))
    a_r = rows[:, None].astype(np.float32) * inv[None, :]
    a_c = cols[:, None].astype(np.float32) * inv[None, :]
    cos = np.concatenate([np.cos(a_r), np.cos(a_r), np.cos(a_c), np.cos(a_c)], axis=1)
    sin = np.concatenate([-np.sin(a_r), np.sin(a_r), -np.sin(a_c), np.sin(a_c)], axis=1)
    return cos.astype(np.float32), sin.astype(np.float32)


def _odd_in(x2, shift, scale, pre_g, w_main, q_g, k_g, mq_g, mq_up, mkv_g, wk, L, tiles_per_mod, rope):
    T, dm = x2.shape
    tm = TOKEN_TILE
    D = HEAD_DIM
    if tiles_per_mod:
        mod_map = lambda i: (i // tiles_per_mod, 0, 0)
    else:
        mod_map = lambda i: (0, 0, 0)
    full = lambda shape: pl.BlockSpec(shape, lambda i: tuple(0 for _ in shape))
    in_specs = [pl.BlockSpec((tm, dm), lambda i: (i, 0)),
                pl.BlockSpec((1, 1, dm), mod_map), pl.BlockSpec((1, 1, dm), mod_map),
                full((1, dm)), full(w_main.shape),
                full((1, D)), full((1, D)), full((1, 2 * D)), full(mq_up.shape), full((1, D)), full(wk.shape)]
    args = [x2, shift, scale, pre_g, w_main, q_g, k_g, mq_g, mq_up, mkv_g, wk]
    if rope:
        cg, sg = _rope_tables(L, D)
        cm, sm = _rope_tables(L, MLA_ROPE_DIM)
        widen = lambda t: np.tile(np.concatenate([t, np.zeros_like(t)], axis=1), (1, MLA_HEADS))
        per = L // tm
        pos = lambda w: pl.BlockSpec((tm, w), lambda i: (i % per, 0))
        in_specs += [pos(D), pos(D), pos(MLA_HEADS * D), pos(MLA_HEADS * D)]
        args += [jnp.asarray(cg), jnp.asarray(sg), jnp.asarray(widen(cm)), jnp.asarray(widen(sm))]
    tile = lambda w: pl.BlockSpec((tm, w), lambda i: (i, 0))
    widths = [(4 * D, BF16), (8 * D, BF16), (2 * D, BF16), (2 * D, BF16), (2 * D, BF16), (8 * D, F32),
              (2 * D, F32), (2 * D, F32), (D, F32), (D, F32)]
    return pl.pallas_call(
        functools.partial(_odd_in_kernel, rope=rope),
        grid=(T // tm,),
        in_specs=in_specs,
        out_specs=[tile(w) for w, _ in widths],
        out_shape=[jax.ShapeDtypeStruct((T, w), dt) for w, dt in widths],
        compiler_params=_cparams(("arbitrary",)),
        name="odd_in_proj",
    )(*args)


def _softmax_pv(score_blocks, value_blocks):
    m = score_blocks[0].max(axis=-1, keepdims=True)
    for s in score_blocks[1:]:
        m = jnp.maximum(m, s.max(axis=-1, keepdims=True))
    acc = None
    den = None
    for s, v in zip(score_blocks, value_blocks):
        p = jnp.exp2(s - m)
        d = p.sum(axis=-1, keepdims=True)
        a = _dot(p.astype(BF16), v)
        acc = a if acc is None else acc + a
        den = d if den is None else den + d
    return acc / den


def _gqa_kernel(q_ref, k_ref, v_ref, gate_ref, *rest, cached):
    if cached:
        ck_ref, cv_ref, o_ref = rest
    else:
        (o_ref,) = rest
    D = HEAD_DIM
    group = GQA_HEADS // GQA_KV_HEADS
    q = q_ref[...]
    keys = [k_ref[...]] + ([ck_ref[0, 0]] if cached else [])
    values = [v_ref[...]] + ([cv_ref[0, 0]] if cached else [])
    outs = []
    for g in range(group):
        qh = q[:, g * D:(g + 1) * D]
        outs.append(_softmax_pv([_dot_nt(qh, kk) for kk in keys], values))
    o_ref[...] = (jnp.concatenate(outs, axis=1) * _silu(gate_ref[...])).astype(o_ref.dtype)


def _gqa(qg, kg, vg, gates, B, L, tq, cache_k=None, cache_v=None, layer_j=0):
    T = qg.shape[0]
    D = HEAD_DIM
    group = GQA_HEADS // GQA_KV_HEADS
    nq = L // tq
    cached = cache_k is not None
    in_specs = [pl.BlockSpec((tq, group * D), lambda b, h, i: (b * nq + i, h)),
                pl.BlockSpec((L, D), lambda b, h, i: (b, h)),
                pl.BlockSpec((L, D), lambda b, h, i: (b, h)),
                pl.BlockSpec((tq, group * D), lambda b, h, i: (b * nq + i, h))]
    args = [qg, kg, vg, gates]
    if cached:
        P = cache_k.shape[2]
        cspec = pl.BlockSpec((1, 1, P, D), lambda b, h, i: (b, layer_j, 0, h))
        in_specs += [cspec, cspec]
        args += [cache_k, cache_v]
    return pl.pallas_call(
        functools.partial(_gqa_kernel, cached=cached),
        grid=(B, GQA_KV_HEADS, nq),
        in_specs=in_specs,
        out_specs=pl.BlockSpec((tq, group * D), lambda b, h, i: (b * nq + i, h)),
        out_shape=jax.ShapeDtypeStruct((T, GQA_HEADS * D), MIX_DTYPE),
        compiler_params=_cparams(("arbitrary", "arbitrary", "arbitrary")),
        name="gqa_attention",
    )(*args)


def _mla_kernel(qc_ref, kc_ref, gate_ref, wv_ref, *rest, cached):
    if cached:
        ckc_ref, o_ref = rest
    else:
        (o_ref,) = rest
    D = HEAD_DIM
    H = MLA_HEADS
    qc = qc_ref[...]
    tq = qc.shape[0]
    keys = [kc_ref[...]] + ([ckc_ref[0, 0]] if cached else [])
    per = H // MLA_HEAD_STACKS
    outs = []
    for s in range(MLA_HEAD_STACKS):
        heads = range(s * per, (s + 1) * per)
        qs = jnp.concatenate([qc[:, 2 * h * D:(2 * h + 2) * D] for h in heads], axis=0)
        o = _softmax_pv([_dot_nt(qs, kk) for kk in keys], [kk[:, :D] for kk in keys]).astype(BF16)
        for i, h in enumerate(heads):
            outs.append(_dot(o[i * tq:(i + 1) * tq], wv_ref[h]))
    o_ref[...] = (jnp.concatenate(outs, axis=1) * _silu(gate_ref[...])).astype(o_ref.dtype)


def _mla(qc, kc, gates, wv, B, L, tq, cache_kc=None, layer_j=0):
    T = qc.shape[0]
    D = HEAD_DIM
    H = MLA_HEADS
    nq = L // tq
    cached = cache_kc is not None
    in_specs = [pl.BlockSpec((tq, 2 * H * D), lambda b, i: (b * nq + i, 0)),
                pl.BlockSpec((L, 2 * D), lambda b, i: (b, 0)),
                pl.BlockSpec((tq, H * D), lambda b, i: (b * nq + i, 1)),
                pl.BlockSpec(wv.shape, lambda b, i: (0, 0, 0))]
    args = [qc, kc, gates, wv]
    if cached:
        P = cache_kc.shape[2]
        in_specs += [pl.BlockSpec((1, 1, P, 2 * D), lambda b, i: (b, layer_j, 0, 0))]
        args += [cache_kc]
    return pl.pallas_call(
        functools.partial(_mla_kernel, cached=cached),
        grid=(B, nq),
        in_specs=in_specs,
        out_specs=pl.BlockSpec((tq, H * D), lambda b, i: (b * nq + i, 0)),
        out_shape=jax.ShapeDtypeStruct((T, H * D), MIX_DTYPE),
        compiler_params=_cparams(("arbitrary", "arbitrary")),
        name="mla_attention",
    )(*args)


def _even_weights(even_in_w, gdn_conv_w, gdn_a_log, gdn_dt_bias, hyena_conv_w, hyena_conv_b, j):
    w = even_in_w[j]
    qkv_w = 3 * GDN_HEADS * LANES
    n_ab = 4 * GDN_HEADS
    w_main = jnp.concatenate([w[:, :qkv_w], w[:, qkv_w + n_ab:]], axis=1).astype(BF16)
    w_small = _pad_to(w[:, qkv_w:qkv_w + n_ab], (w.shape[0], LANES)).astype(BF16)
    n_dir = 2 * GDN_HEADS
    alog_row = _pad_to(gdn_a_log[j].reshape(1, n_dir), (1, LANES))
    dtb_row = _pad_to(gdn_dt_bias[j].reshape(1, n_dir), (1, LANES))
    return w_main, w_small, alog_row, dtb_row


def _even_layer(x2, B, L, shift, scale, gate, tiles_per_mod, pre_g, post_g, wts, hy, out_w,
                gdn_conv_w, gdn_norm_g, hy_conv_w, hy_conv_b, hy_skip, state_f, state_b, j):
    w_main, w_small, alog_row, dtb_row = wts
    tables, spectra = hy
    z, zab = _in_proj(x2, shift, scale, pre_g, w_main, w_small, tiles_per_mod)
    oa, s_f, s_b = _gdn(z, zab, gdn_conv_w, gdn_norm_g, alog_row, dtb_row, B, L, state_f, state_b, j)
    T, width = z.shape
    zp = z.reshape(T // 2, 2 * width)
    ncb = HY_CH // HY_CH_BLOCK
    even = lambda idx: (zp, idx * ncb)
    odd = lambda idx: (zp, (width // HY_CH + idx) * ncb)
    z1e, z1o = _hy_conv(even(4), odd(4), even(5), odd(5), even(7), odd(7), hy_conv_w, hy_conv_b, hy_skip, 0,
                        tables, spectra, B, L, first=True, last=False)
    obe, obo = _hy_conv((z1e, 0), (z1o, 0), even(6), odd(6), even(7), odd(7), hy_conv_w, hy_conv_b, hy_skip, 1,
                        tables, spectra, B, L, first=False, last=True)
    x_new = _out_proj(oa, (obe, 0), (obo, 0), x2, gate, post_g, out_w, tiles_per_mod)
    return x_new, s_f, s_b


def _odd_weights(odd_in_w, mla_q_up, mla_kv_up, j):
    w = odd_in_w[j]
    D = HEAD_DIM
    o = np.cumsum([0, 4 * D, 2 * D, 2 * D, 4 * D, 2 * D, D, MLA_ROPE_DIM, 4 * D])
    gq, gk, gv, gg, mq, mkv, mpe, mg = [w[:, o[i]:o[i + 1]] for i in range(8)]
    w_main = jnp.concatenate([gq, gg, mg, gk, gv, mq, mkv, _pad_to(mpe, (w.shape[0], D))], axis=1).astype(BF16)
    up = mla_q_up[j].reshape(-1, MLA_HEADS, D + MLA_ROPE_DIM)
    rope_cols = _pad_to(up[:, :, D:], (up.shape[0], MLA_HEADS, D))
    mq_up = jnp.concatenate([up[:, :, :D].reshape(-1, MLA_HEADS * D),
                             rope_cols.reshape(-1, MLA_HEADS * D)], axis=1).astype(BF16)
    kv = mla_kv_up[j].reshape(-1, MLA_HEADS, 2 * D)
    wk = kv[:, :, :D].transpose(1, 0, 2).astype(BF16)
    wv = kv[:, :, D:].transpose(1, 0, 2).astype(BF16)
    return w_main, mq_up, wk, wv


def _odd_layer(x2, B, L, shift, scale, gate, tiles_per_mod, pre_g, post_g, wts, out_w,
               q_g, k_g, mq_g, mkv_g, caches, j):
    w_main, mq_up, wk, wv = wts
    rope = caches is not None
    qg, qc, k16, v16, kc16, gates, kg, v32, ckv, kpe = _odd_in(
        x2, shift, scale, pre_g, w_main, q_g, k_g, mq_g, mq_up, mkv_g, wk, L, tiles_per_mod, rope)
    if rope:
        ck, cv, ckc = caches
        og = _gqa(qg, k16, v16, gates, B, L, GQA_Q_TILE, ck, cv, j)
        om = _mla(qc, kc16, gates, wv, B, L, MLA_Q_TILE, ckc, j)
    else:
        og = _gqa(qg, k16, v16, gates, B, L, GQA_Q_TILE)
        om = _mla(qc, kc16, gates, wv, B, L, MLA_Q_TILE)
    om_pairs = om.reshape(om.shape[0] // 2, 2 * om.shape[1])
    x_new = _out_proj(og, (om_pairs, 0), (om_pairs, 1), x2, gate, post_g, out_w, tiles_per_mod)
    return x_new, (kg, v32, ckv, kpe)


def kernel(x_prompt, x_sample, state_gdn_fwd, state_gdn_bwd, cache_gqa_k, cache_gqa_v, cache_mla_ckv, cache_mla_kpe, c, c_ctx, mod_w, mod_b, pre_norm_g, post_norm_g, even_in_w, gdn_conv_w, gdn_a_log, gdn_dt_bias, gdn_norm_g, hyena_conv_w, hyena_conv_b, hyena_ffn_w1, hyena_ffn_b1, hyena_ffn_w2, hyena_ffn_b2, hyena_ffn_w3, hyena_sin_freq, hyena_bias, even_out_w, odd_in_w, gqa_q_norm_g, gqa_k_norm_g, mla_q_norm_g, mla_q_up, mla_kv_norm_g, mla_kv_up, odd_out_w):
    Bp, Lp, D = x_prompt.shape
    Bs, Ls, _ = x_sample.shape
    depth = mod_w.shape[0]
    xp = x_prompt.reshape(Bp * Lp, D)
    xs = x_sample.reshape(Bs * Ls, D)

    n_cond = 1 + Bs
    rows = -(-n_cond // 8) * 8
    cond = _pad_to(jnp.concatenate([c_ctx[None, :], c], axis=0), (rows, D))
    mod = _modulation(cond, mod_w, mod_b)

    P = cache_gqa_k.shape[2]
    n_odd = cache_gqa_k.shape[1]
    ck = cache_gqa_k.reshape(Bs, n_odd, P, GQA_KV_HEADS * HEAD_DIM).astype(BF16)
    cv = cache_gqa_v.reshape(Bs, n_odd, P, GQA_KV_HEADS * HEAD_DIM).astype(BF16)
    ckc = jnp.concatenate([cache_mla_ckv, _pad_to(cache_mla_kpe, cache_mla_ckv.shape)], axis=-1).astype(BF16)

    dft = {L: _dft_blocks(L, min(HY_FREQ_BLOCK, L)) for L in (Lp, Ls)}
    half = {L: _hy_half_tables(L, min(HY_HALF_FREQ_BLOCK, L // 2), HY_CH_BLOCK) for L in (Lp, Ls)}
    tpm_s = Ls // TOKEN_TILE

    new_f, new_b, new_gk, new_gv, new_ckv, new_kpe = [], [], [], [], [], []
    for i in range(depth):
        j = i // 2
        m = mod[i]
        sh_p, sc_p, gt_p = [m[0:1, k * D:(k + 1) * D].reshape(1, 1, D) for k in range(3)]
        sh_s, sc_s, gt_s = [m[1:n_cond, k * D:(k + 1) * D].reshape(Bs, 1, D) for k in range(3)]
        pre_g = pre_norm_g[i][None, :]
        post_g = post_norm_g[i][None, :]
        if i % 2 == 0:
            wts = _even_weights(even_in_w, gdn_conv_w, gdn_a_log, gdn_dt_bias, hyena_conv_w, hyena_conv_b, j)
            out_w = even_out_w[j].astype(BF16)
            hy = {}
            for L in (Lp, Ls):
                filt = _hy_filters(L, hyena_ffn_w1[j], hyena_ffn_b1[j], hyena_ffn_w2[j], hyena_ffn_b2[j],
                                   hyena_ffn_w3[j], hyena_sin_freq[j])
                hc, hs = _hy_spectrum(filt, dft[L], L, min(HY_FREQ_BLOCK, L))
                M = L // 2
                spectra = (hc[:, :M], -hs[:, :M], jnp.flip(hc[:, M:], axis=1), -jnp.flip(hs[:, M:], axis=1))
                hy[L] = (half[L], spectra)
            common = (gdn_conv_w[j], gdn_norm_g[j][None, :], hyena_conv_w[j], hyena_conv_b[j][None, :], hyena_bias[j])
            xp, sf, sb = _even_layer(xp, Bp, Lp, sh_p, sc_p, gt_p, 0, pre_g, post_g, wts, hy[Lp], out_w,
                                     *common, None, None, j)
            xs, _, _ = _even_layer(xs, Bs, Ls, sh_s, sc_s, gt_s, tpm_s, pre_g, post_g, wts, hy[Ls], out_w,
                                   *common, state_gdn_fwd, state_gdn_bwd, j)
            new_f.append(sf)
            new_b.append(sb)
        else:
            wts = _odd_weights(odd_in_w, mla_q_up, mla_kv_up, j)
            out_w = odd_out_w[j].astype(BF16)
            norms = (gqa_q_norm_g[j][None, :], gqa_k_norm_g[j][None, :], mla_q_norm_g[j][None, :],
                     mla_kv_norm_g[j][None, :])
            xp, (kg, v32, ckv, kpe) = _odd_layer(xp, Bp, Lp, sh_p, sc_p, gt_p, 0, pre_g, post_g, wts, out_w,
                                                 *norms, None, j)
            xs, _ = _odd_layer(xs, Bs, Ls, sh_s, sc_s, gt_s, tpm_s, pre_g, post_g, wts, out_w,
                               *norms, (ck, cv, ckc), j)
            new_gk.append(kg.reshape(Bp, Lp, GQA_KV_HEADS, HEAD_DIM))
            new_gv.append(v32.reshape(Bp, Lp, GQA_KV_HEADS, HEAD_DIM))
            new_ckv.append(ckv.reshape(Bp, Lp, HEAD_DIM))
            new_kpe.append(kpe[:, :MLA_ROPE_DIM].reshape(Bp, Lp, MLA_ROPE_DIM))
    return (xp.reshape(Bp, Lp, D), xs.reshape(Bs, Ls, D),
            jnp.stack(new_f, axis=1), jnp.stack(new_b, axis=1),
            jnp.stack(new_gk, axis=1), jnp.stack(new_gv, axis=1),
            jnp.stack(new_ckv, axis=1), jnp.stack(new_kpe, axis=1))
```

```python
import functools
import math

import numpy as np
import jax
import jax.numpy as jnp
from jax import lax
from jax.experimental import pallas as pl
from jax.experimental.pallas import tpu as pltpu

F32 = jnp.float32
BF16 = jnp.bfloat16
MIX_DTYPE = BF16

NORM_EPS = 1e-6
ROPE_THETA = 10000.0
GRID_W = 64

GDN_HEADS = 4
GDN_DK = 128
GDN_CHUNK = 64
GDN_CONV_W = 5
GDN_PREP_CHAINS = 32
GDN_ALL_HEADS_MAX_LEN = 256
GDN_HEADS_LONG = 1
HY_CH = 512
HY_BANDS = 16
HY_DECAY_TARGET = 1e-2
HY_SHORT_DECAY_PCT = 0.3
HY_LONG_DECAY_PCT = 1.5
GQA_HEADS = 4
GQA_KV_HEADS = 2
HEAD_DIM = 128
MLA_HEADS = 4
MLA_ROPE_DIM = 64

LOG2E = math.log2(math.e)

LANES = 128
VMEM_LIMIT = 56 * 1024 * 1024

TOKEN_TILE = 256
HY_FREQ_BLOCK = 512
HY_HALF_FREQ_BLOCK = 256
HY_CH_BLOCK = 256
GQA_Q_TILE = 256
MLA_Q_TILE = 128
MLA_HEAD_STACKS = 2


def _cparams(sem):
    return pltpu.CompilerParams(dimension_semantics=sem, vmem_limit_bytes=VMEM_LIMIT)


def _dot(a, b):
    return jnp.dot(a, b, preferred_element_type=F32)


def _dot_nt(a, b):
    return lax.dot_general(a, b, (((1,), (1,)), ((), ())), preferred_element_type=F32)


def _dot_tn(a, b):
    return lax.dot_general(a, b, (((0,), (0,)), ((), ())), preferred_element_type=F32)


def _split(a):
    hi = a.astype(BF16)
    lo = (a - hi.astype(F32)).astype(BF16)
    return hi, lo


def _dot3(a, b):
    ah, al = _split(a)
    bh, bl = _split(b)
    return _dot(ah, bh) + (_dot(ah, bl) + _dot(al, bh))


def _bdot(a, b):
    return jnp.einsum('gij,gjk->gik', a, b, preferred_element_type=F32)


def _bdot16(a, b):
    return _bdot(a.astype(BF16), b.astype(BF16))


def _silu(x):
    return x * (1.0 / (1.0 + jnp.exp(-x)))


def _sigmoid(x):
    return 1.0 / (1.0 + jnp.exp(-x))


def _softplus(x):
    return jnp.maximum(x, 0.0) + jnp.log(1.0 + jnp.exp(-jnp.abs(x)))


def _rms(x, g):
    return x * lax.rsqrt(jnp.mean(x * x, axis=-1, keepdims=True) + NORM_EPS) * g


def _shift_rows(x, s):
    L = x.shape[0]
    if s == 0:
        return x
    rolled = pltpu.roll(x, (-s) % L, 0)
    row = lax.broadcasted_iota(jnp.int32, x.shape, 0)
    valid = (row + s >= 0) & (row + s < L)
    return jnp.where(valid, rolled, 0.0)


def _mod_kernel(c_ref, w_ref, b_ref, o_ref):
    c = _silu(c_ref[...])
    o_ref[0] = _dot3(c, w_ref[0]) + b_ref[0]


def _modulation(cond, mod_w, mod_b):
    depth, d, d3 = mod_w.shape
    r = cond.shape[0]
    nb = d3 // d
    return pl.pallas_call(
        _mod_kernel,
        grid=(depth, nb),
        in_specs=[pl.BlockSpec((r, d), lambda i, n: (0, 0)),
                  pl.BlockSpec((1, d, d), lambda i, n: (i, 0, n)),
                  pl.BlockSpec((1, 1, d), lambda i, n: (i, 0, n))],
        out_specs=pl.BlockSpec((1, r, d), lambda i, n: (i, 0, n)),
        out_shape=jax.ShapeDtypeStruct((depth, r, d3), F32),
        compiler_params=_cparams(("arbitrary", "arbitrary")),
        name="adaln_modulation",
    )(cond, mod_w, mod_b.reshape(depth, 1, d3))


def _row_pick(n, parity):
    r = lax.broadcasted_iota(jnp.int32, (n // 2, n), 0)
    c = lax.broadcasted_iota(jnp.int32, (n // 2, n), 1)
    return (c == 2 * r + parity).astype(BF16)


def _in_proj_kernel(x_ref, sh_ref, sc_ref, g_ref, wa_ref, ws_ref, wb_ref, oa_ref, os_ref, ob_ref):
    x = x_ref[...]
    h = _rms(x, g_ref[...]) * (1.0 + sc_ref[0]) + sh_ref[0]
    hb = h.astype(BF16)
    tm = hb.shape[0]
    step = 512
    na = wa_ref.shape[1]
    for n0 in range(0, na, step):
        oa_ref[:, n0:n0 + step] = _dot(hb, wa_ref[:, n0:n0 + step])
    os_ref[...] = _dot(hb, ws_ref[...])
    nb = wb_ref.shape[1]
    for parity in (0, 1):
        hp = _dot(_row_pick(tm, parity), hb).astype(BF16)
        for n0 in range(0, nb, step):
            ob_ref[:, parity * nb + n0:parity * nb + n0 + step] = _dot(hp, wb_ref[:, n0:n0 + step])


def _in_proj(x2, shift, scale, pre_g, w_a, w_small, w_b, tiles_per_mod):
    T, D = x2.shape
    na, nb = w_a.shape[1], w_b.shape[1]
    tm = TOKEN_TILE
    if tiles_per_mod:
        mod_map = lambda i: (i // tiles_per_mod, 0, 0)
    else:
        mod_map = lambda i: (0, 0, 0)
    const = lambda shape: pl.BlockSpec(shape, lambda i: (0, 0))
    return pl.pallas_call(
        _in_proj_kernel,
        grid=(T // tm,),
        in_specs=[pl.BlockSpec((tm, D), lambda i: (i, 0)),
                  pl.BlockSpec((1, 1, D), mod_map),
                  pl.BlockSpec((1, 1, D), mod_map),
                  const((1, D)), const((D, na)), const((D, LANES)), const((D, nb))],
        out_specs=[pl.BlockSpec((tm, na), lambda i: (i, 0)),
                   pl.BlockSpec((tm, LANES), lambda i: (i, 0)),
                   pl.BlockSpec((tm // 2, 2 * nb), lambda i: (i, 0))],
        out_shape=[jax.ShapeDtypeStruct((T, na), F32), jax.ShapeDtypeStruct((T, LANES), F32),
                   jax.ShapeDtypeStruct((T // 2, 2 * nb), F32)],
        compiler_params=_cparams(("arbitrary",)),
        name="in_proj",
    )(x2, shift, scale, pre_g, w_a, w_small, w_b)


def _row_spread(n, parity):
    r = lax.broadcasted_iota(jnp.int32, (n, n // 2), 0)
    c = lax.broadcasted_iota(jnp.int32, (n, n // 2), 1)
    return (r == 2 * c + parity).astype(BF16)


def _out_proj_kernel(a_ref, *rest, half, split):
    if split:
        be_ref, bo_ref, x_ref, gt_ref, g_ref, w_ref, o_ref = rest
        tm = a_ref.shape[0]
        b = (_dot(_row_spread(tm, 0), be_ref[...]) + _dot(_row_spread(tm, 1), bo_ref[...])).astype(BF16)
    else:
        b_ref, x_ref, gt_ref, g_ref, w_ref, o_ref = rest
        b = b_ref[...].astype(BF16)
    y = _dot(a_ref[...].astype(BF16), w_ref[:half, :]) + _dot(b, w_ref[half:, :])
    o_ref[...] = x_ref[...] + gt_ref[0] * _rms(y, g_ref[...])


def _out_proj(a, b, x2, gate, post_g, w, tiles_per_mod):
    T, D = x2.shape
    half = a.shape[1]
    tm = TOKEN_TILE
    split = isinstance(b, tuple)
    if tiles_per_mod:
        mod_map = lambda i: (i // tiles_per_mod, 0, 0)
    else:
        mod_map = lambda i: (0, 0, 0)
    if split:
        b_specs = [pl.BlockSpec((tm // 2, half), lambda i: (i, 0))] * 2
        b_args = list(b)
    else:
        b_specs = [pl.BlockSpec((tm, half), lambda i: (i, 0))]
        b_args = [b]
    return pl.pallas_call(
        functools.partial(_out_proj_kernel, half=half, split=split),
        grid=(T // tm,),
        in_specs=[pl.BlockSpec((tm, half), lambda i: (i, 0))] + b_specs + [
                  pl.BlockSpec((tm, D), lambda i: (i, 0)),
                  pl.BlockSpec((1, 1, D), mod_map),
                  pl.BlockSpec((1, D), lambda i: (0, 0)),
                  pl.BlockSpec((2 * half, D), lambda i: (0, 0))],
        out_specs=pl.BlockSpec((tm, D), lambda i: (i, 0)),
        out_shape=jax.ShapeDtypeStruct((T, D), F32),
        compiler_params=_cparams(("arbitrary",)),
        name="out_proj",
    )(a, *b_args, x2, gate, post_g, w)


def _gdn_kernel(zq_ref, zk_ref, zv_ref, zg_ref, zab_ref, cwq_ref, cwk_ref, cwv_ref, ng_ref,
                alog_ref, dtb_ref, *rest, L, HB, has_state):
    if has_state:
        s0f_ref, s0b_ref = rest[:2]
        rest = rest[2:]
    (o_ref, sf_out_ref, sb_out_ref, q_s, k_s, v_s, g_s, b_s, st_s,
     oc_s, qe_s, sc_s, sm_s, gl_s, pad_s) = rest
    C = GDN_CHUNK
    n_chunks = L // C
    U = min(GDN_PREP_CHAINS // (2 * HB), n_chunks)
    HU = HB * U
    G = 2 * HU
    head0 = pl.program_id(1) * HB

    pad = 8
    pad_s[0:pad, :] = jnp.zeros((pad, LANES), F32)
    pad_s[pad + L:2 * pad + L, :] = jnp.zeros((pad, LANES), F32)

    def conv_silu(x, w_ref, lanes):
        half = GDN_CONV_W // 2
        pad_s[pad:pad + L, :] = x
        acc = x * w_ref[half:half + 1, lanes]
        for i in range(GDN_CONV_W):
            if i != half:
                start = pad + i - half
                acc = acc + pad_s[start:start + L, :] * w_ref[i:i + 1, lanes]
        return _silu(acc)

    def l2n(x):
        return x * lax.rsqrt(jnp.sum(x * x, axis=-1, keepdims=True) + NORM_EPS)

    for hh in range(HB):
        lanes = slice(hh * LANES, (hh + 1) * LANES)
        q_s[hh] = l2n(conv_silu(zq_ref[:, lanes], cwq_ref, lanes)) * (GDN_DK ** -0.5)
        k_s[hh] = l2n(conv_silu(zk_ref[:, lanes], cwk_ref, lanes))
        v_s[hh] = conv_silu(zv_ref[:, lanes], cwv_ref, lanes)
        if has_state:
            st_s[2 * hh] = s0f_ref[0, 0, hh]
            st_s[2 * hh + 1] = s0b_ref[0, 0, hh]
    if not has_state:
        st_s[...] = jnp.zeros_like(st_s)
    zab = zab_ref[...]
    g_s[...] = -jnp.exp(alog_ref[...]) * _softplus(zab + dtb_ref[...])
    b_s[...] = _sigmoid(zab)

    def iota(shape, axis):
        return lax.broadcasted_iota(jnp.int32, shape, axis)

    def direction(shape):
        return iota(shape, 0) & 1

    chain_shift = (2 * U).bit_length() - 1
    assert 2 * U == 1 << chain_shift

    def chain_head(shape):
        return head0 + (iota(shape, 0) >> chain_shift)

    sq = (G, C, C)
    row = iota(sq, 1)
    col = iota(sq, 2)
    signed = (row - col) * (1 - 2 * direction(sq))
    incl = signed >= 0
    strict = signed > 0
    eye = (row == col).astype(F32)
    same = [(row >> s) == (col >> s) for s in (3, 4, 5)]
    off_blocks = [same[1] & jnp.logical_not(same[0]), same[2] & jnp.logical_not(same[1]),
                  jnp.logical_not(same[2])]
    wide = (G, C, LANES)
    sel_lane = direction(wide) * GDN_HEADS + chain_head(wide)
    mask_g = iota(wide, 2) == sel_lane
    mask_b = iota(wide, 2) == sel_lane + 2 * GDN_HEADS
    tall = (G, LANES, C)
    mask_t = iota(tall, 1) == direction(tall) * GDN_HEADS + chain_head(tall)
    colv = (G, C, 1)
    mask_last = iota(colv, 1) == (1 - direction(colv)) * (C - 1)
    r2 = lax.broadcasted_iota(jnp.int32, (C, C), 0)
    c2 = lax.broadcasted_iota(jnp.int32, (C, C), 1)
    tri = jnp.concatenate([(r2 >= c2).astype(BF16), (r2 <= c2).astype(BF16)], axis=0)

    def both_dirs(x):
        return jnp.broadcast_to(x[:, None], (HU, 2) + x.shape[1:]).reshape((G,) + x.shape[1:])

    def all_heads(x):
        return jnp.broadcast_to(x[None], (HB,) + x.shape).reshape((HB * x.shape[0],) + x.shape[1:])

    def prep_group(c, carry):
        rows = pl.ds(pl.multiple_of(c * (U * C), U * C), U * C)
        q = q_s[:, rows, :].reshape(HU, C, LANES)
        k = k_s[:, rows, :].reshape(HU, C, LANES)
        v = v_s[:, rows, :].reshape(HU, C, LANES)
        ball = all_heads(b_s[rows, :].reshape(U, C, LANES))
        g_hi, g_lo = _split(g_s[rows, :])
        g_hi = g_hi.reshape(U, C, LANES)
        g_lo = g_lo.reshape(U, C, LANES)
        gcum = jnp.stack([_dot(tri, g_hi[u]) + _dot(tri, g_lo[u]) for u in range(U)])
        gcum = gcum.reshape(2 * U, C, LANES)
        gcum_t = all_heads(jnp.stack([gcum[g].T for g in range(2 * U)]))
        gcum = all_heads(gcum)
        gc = jnp.sum(jnp.where(mask_g, gcum, 0.0), axis=2, keepdims=True)
        gr = jnp.sum(jnp.where(mask_t, gcum_t, 0.0), axis=1, keepdims=True)
        beta = jnp.sum(jnp.where(mask_b, both_dirs(ball), 0.0), axis=2, keepdims=True)
        kbf = k.astype(BF16)
        kq = jnp.einsum('uik,ujk->uij', jnp.concatenate([k, q], axis=1).astype(BF16), kbf,
                        preferred_element_type=F32)
        kk = both_dirs(kq[:, :C])
        qk = both_dirs(kq[:, C:])
        decay = jnp.where(incl, jnp.exp(jnp.where(incl, gc - gr, 0.0)), 0.0)
        lmat = jnp.where(strict, beta * kk * decay, 0.0)
        p = jnp.where(same[0], -lmat, 0.0)
        tmat = eye + p
        for _ in range(2):
            p = _bdot16(p, p)
            tmat = tmat + _bdot16(tmat, p)
        for off in off_blocks:
            t16 = tmat.astype(BF16)
            tmat = tmat - _bdot(_bdot(t16, jnp.where(off, lmat, 0.0).astype(BF16)).astype(BF16), t16)
        eg = jnp.exp(gc)
        k2 = both_dirs(k)
        kb = k2 * beta
        sol = _bdot16(tmat, jnp.concatenate([both_dirs(v) * beta, kb * eg], axis=2))
        glast = jnp.sum(jnp.where(mask_last, gc, 0.0), axis=1, keepdims=True)
        kdec = k2 * jnp.exp(glast - gc)
        kdec_t = jnp.stack([kdec[g].T for g in range(G)]).astype(BF16)
        sol16 = sol.astype(BF16)
        r_in = _bdot((qk * decay).astype(BF16), sol16)
        r_kd = _bdot(kdec_t, sol16)
        def put(ref, val, size):
            n = 2 * U * size
            ref[:, pl.ds(pl.multiple_of(c * n, n), n), :] = val.reshape(HB, n, val.shape[-1])

        put(oc_s, r_in[:, :, :LANES], C)
        put(qe_s, (both_dirs(q) * eg - r_in[:, :, LANES:]).astype(BF16), C)
        put(sc_s, r_kd[:, :, :LANES], LANES)
        put(sm_s, (-r_kd[:, :, LANES:]).astype(BF16), LANES)
        put(gl_s, jnp.broadcast_to(jnp.exp(glast), (G, 8, LANES)), 8)
        return carry

    lax.fori_loop(0, n_chunks // U, prep_group, 0)

    def scan_step(c, carry):
        slot_f = c * 2
        slot_b = (n_chunks - 1 - c) * 2 + 1

        def ld(ref, size):
            parts = []
            for hh in range(HB):
                parts.append(ref[hh, pl.ds(pl.multiple_of(slot_f * size, size), size), :])
                parts.append(ref[hh, pl.ds(pl.multiple_of(slot_b * size, size), size), :])
            return jnp.stack(parts)

        s = st_s[...]
        sb16 = s.astype(BF16)
        o = _bdot(ld(qe_s, C), sb16) + ld(oc_s, C)
        st_s[...] = s * ld(gl_s, 8)[:, 0:1, :] + (_bdot(ld(sm_s, LANES), sb16) + ld(sc_s, LANES))
        for hh in range(HB):
            oc_s[hh, pl.ds(pl.multiple_of(slot_f * C, C), C), :] = o[2 * hh]
            oc_s[hh, pl.ds(pl.multiple_of(slot_b * C, C), C), :] = o[2 * hh + 1]
        return carry

    lax.fori_loop(0, n_chunks, scan_step, 0)

    FIN = 4

    def finish(i, carry):
        rows = pl.ds(pl.multiple_of(i * (FIN * C), FIN * C), FIN * C)
        outs = []
        for hh in range(HB):
            both = oc_s[hh, pl.ds(pl.multiple_of(i * (FIN * 2 * C), FIN * 2 * C), FIN * 2 * C), :]
            both = both.reshape(FIN, 2, C, LANES)
            outs.append(_rms((both[:, 0] + both[:, 1]).reshape(FIN * C, LANES), ng_ref[...]))
        o = jnp.concatenate(outs, axis=1) * _silu(zg_ref[rows, :])
        o_ref[rows, :] = o.astype(o_ref.dtype)
        return carry

    lax.fori_loop(0, n_chunks // FIN, finish, 0)
    for hh in range(HB):
        sf_out_ref[0, hh] = st_s[2 * hh]
        sb_out_ref[0, hh] = st_s[2 * hh + 1]


def _gdn(z, zab, conv_w, norm_g, alog_row, dtb_row, B, L, state_f=None, state_b=None, layer_j=0):
    T = z.shape[0]
    H = GDN_HEADS
    has_state = state_f is not None
    HB = H if L <= GDN_ALL_HEADS_MAX_LEN else GDN_HEADS_LONG
    nh = H // HB
    w = HB * LANES
    blk = lambda i: pl.BlockSpec((L, w), lambda b, h: (b, i * nh + h))
    cw = lambda i: pl.BlockSpec((GDN_CONV_W, w), lambda b, h: (0, i * nh + h))
    in_specs = [blk(0), blk(1), blk(2), blk(3),
                pl.BlockSpec((L, LANES), lambda b, h: (b, 0)),
                cw(0), cw(1), cw(2),
                pl.BlockSpec((1, LANES), lambda b, h: (0, 0)),
                pl.BlockSpec((1, LANES), lambda b, h: (0, 0)),
                pl.BlockSpec((1, LANES), lambda b, h: (0, 0))]
    args = [z, z, z, z, zab, conv_w, conv_w, conv_w, norm_g, alog_row, dtb_row]
    if has_state:
        st = pl.BlockSpec((1, 1, HB, GDN_DK, LANES), lambda b, h: (b, layer_j, h, 0, 0))
        in_specs += [st, st]
        args += [state_f, state_b]
    sout = pl.BlockSpec((1, HB, GDN_DK, LANES), lambda b, h: (b, h, 0, 0))
    n_slots = 2 * (L // GDN_CHUNK)
    scratch = ([pltpu.VMEM((HB, L, LANES), F32) for _ in range(3)]
               + [pltpu.VMEM((L, LANES), F32) for _ in range(2)]
               + [pltpu.VMEM((2 * HB, GDN_DK, LANES), F32)]
               + [pltpu.VMEM((HB, 2 * L, LANES), F32), pltpu.VMEM((HB, 2 * L, LANES), BF16)]
               + [pltpu.VMEM((HB, n_slots * GDN_DK, LANES), F32),
                  pltpu.VMEM((HB, n_slots * GDN_DK, LANES), BF16)]
               + [pltpu.VMEM((HB, n_slots * 8, LANES), F32)]
               + [pltpu.VMEM((L + 16, LANES), F32)])
    return pl.pallas_call(
        functools.partial(_gdn_kernel, L=L, HB=HB, has_state=has_state),
        grid=(B, nh),
        in_specs=in_specs,
        out_specs=[pl.BlockSpec((L, w), lambda b, h: (b, h)), sout, sout],
        out_shape=[jax.ShapeDtypeStruct((T, H * LANES), MIX_DTYPE),
                   jax.ShapeDtypeStruct((B, H, GDN_DK, LANES), F32),
                   jax.ShapeDtypeStruct((B, H, GDN_DK, LANES), F32)],
        scratch_shapes=scratch,
        compiler_params=_cparams(("arbitrary", "arbitrary")),
        name="gdn",
    )(*args)


def _hy_filter_kernel(f_ref, w1_ref, b1_ref, sf0_ref, w2_ref, b2_ref, sf1_ref, w3_ref, dl_ref, o_ref, *, L):
    rowi = lax.broadcasted_iota(jnp.int32, (L, LANES), 0).astype(F32)
    lane = lax.broadcasted_iota(jnp.int32, (L, LANES), 1)
    t = rowi * (1.0 / (L - 1))
    wpos = rowi * (2.0 * math.pi / L)
    ang = f_ref[...] * wpos
    z = jnp.where(lane == 0, t,
                  jnp.where(lane <= HY_BANDS, jnp.cos(ang),
                            jnp.where(lane <= 2 * HY_BANDS, -jnp.sin(ang), 0.0)))
    h = jnp.sin(sf0_ref[...] * (_dot3(z, w1_ref[...]) + b1_ref[...]))
    h = jnp.sin(sf1_ref[...] * (_dot3(h, w2_ref[...]) + b2_ref[...]))
    tc = lax.broadcasted_iota(jnp.int32, (L, HY_CH), 0).astype(F32) * (1.0 / (L - 1))
    window = jnp.exp(-tc * dl_ref[...])
    for j in range(4):
        o_ref[:, j * HY_CH:(j + 1) * HY_CH] = _dot3(h, w3_ref[:, j * HY_CH:(j + 1) * HY_CH]) * window


def _pad_to(a, shape):
    return jnp.pad(a, [(0, s - d) for s, d in zip(shape, a.shape)])


def _hy_filters(L, w1, b1, w2, b2, w3, sin_freq):
    fvals = np.linspace(1e-4, HY_BANDS - 1, HY_BANDS, dtype=np.float32)
    frow = np.zeros((1, LANES), np.float32)
    frow[0, 1:1 + HY_BANDS] = fvals
    frow[0, 1 + HY_BANDS:1 + 2 * HY_BANDS] = fvals
    deltas = np.abs(np.linspace(math.log(HY_DECAY_TARGET) / HY_LONG_DECAY_PCT,
                                math.log(HY_DECAY_TARGET) / HY_SHORT_DECAY_PCT, HY_CH, dtype=np.float32))
    args = [jnp.asarray(frow),
            _pad_to(w1, (LANES, LANES)), _pad_to(b1[None, :], (1, LANES)), _pad_to(sin_freq[0][None, :], (1, LANES)),
            _pad_to(w2, (LANES, LANES)), _pad_to(b2[None, :], (1, LANES)), _pad_to(sin_freq[1][None, :], (1, LANES)),
            _pad_to(w3, (LANES, 4 * HY_CH)), jnp.asarray(deltas[None, :])]
    return pl.pallas_call(
        functools.partial(_hy_filter_kernel, L=L),
        out_shape=jax.ShapeDtypeStruct((L, 4 * HY_CH), F32),
        compiler_params=pltpu.CompilerParams(vmem_limit_bytes=VMEM_LIMIT),
        name="hyena_filters",
    )(*args)


def _dft_tables(L):
    N = 2 * L
    k = np.arange(L, dtype=np.int64)[:, None]
    s = np.arange(L, dtype=np.int64)[None, :]
    ang = ((2 * k + 1) * s % (2 * N)).astype(np.float64) * (2.0 * math.pi / (2 * N))
    return np.cos(ang).astype(np.float32), np.sin(ang).astype(np.float32)


def _dft_blocks(L, kb):
    cm, sm = _dft_tables(L)
    nk = L // kb
    fwd = np.concatenate([cm.reshape(nk, kb, L), sm.reshape(nk, kb, L)], axis=1)
    return jnp.asarray(fwd).astype(BF16)


def _hy_spectrum_kernel(f_ref, flt_ref, hc_ref, hs_ref, *, L, kb):
    row = lax.broadcasted_iota(jnp.int32, (L, HY_CH), 0)
    scale = 1.0 / L
    for o in range(2):
        hf = flt_ref[:, (2 * o) * HY_CH:(2 * o + 1) * HY_CH]
        hb = jnp.where(row == 0, 0.0, flt_ref[:, (2 * o + 1) * HY_CH:(2 * o + 2) * HY_CH])
        a_hi, a_lo = _split(hf + hb)
        d_hi, d_lo = _split(hf - hb)
        fc = f_ref[0, :kb, :]
        fs = f_ref[0, kb:, :]
        hc_ref[o] = (_dot(fc, a_hi) + _dot(fc, a_lo)) * scale
        hs_ref[o] = (_dot(fs, d_hi) + _dot(fs, d_lo)) * scale


def _hy_spectrum(filt, fwd, L, kb):
    nk = L // kb
    return pl.pallas_call(
        functools.partial(_hy_spectrum_kernel, L=L, kb=kb),
        grid=(nk,),
        in_specs=[pl.BlockSpec((1, 2 * kb, L), lambda i: (i, 0, 0)),
                  pl.BlockSpec((L, 4 * HY_CH), lambda i: (0, 0))],
        out_specs=[pl.BlockSpec((2, kb, HY_CH), lambda i: (0, i, 0)),
                   pl.BlockSpec((2, kb, HY_CH), lambda i: (0, i, 0))],
        out_shape=[jax.ShapeDtypeStruct((2, L, HY_CH), F32)] * 2,
        compiler_params=_cparams(("arbitrary",)),
        name="hyena_spectrum",
    )(fwd, filt)


def _hy_conv_kernel(ze_ref, zo_ref, xe_ref, xo_ref, ge_ref, go_ref, cwz_ref, cbz_ref, cwx_ref, cbx_ref, skip_ref,
                    f_ref, g_ref, har_ref, hai_ref, hbr_ref, hbi_ref, cw_ref, sw_ref, oe_ref, oo_ref,
                    *, kb, nk, first, last):
    cb = ze_ref.shape[1]

    def conv3(e_ref, o_ref, w_ref, b_ref):
        xe = e_ref[...]
        xo = o_ref[...]
        w0, w1, w2 = w_ref[0:1, :], w_ref[1:2, :], w_ref[2:3, :]
        ye = _shift_rows(xo, -1) * w0 + xe * w1 + xo * w2 + b_ref[...]
        yo = xe * w0 + xo * w1 + _shift_rows(xe, 1) * w2 + b_ref[...]
        return ye, yo

    if first:
        ze, zo = conv3(ze_ref, zo_ref, cwz_ref, cbz_ref)
    else:
        ze, zo = ze_ref[...], zo_ref[...]
    zb = jnp.concatenate([ze, zo], axis=1).astype(BF16)
    acc = None
    for k in range(nk):
        rows = slice(k * kb, (k + 1) * kb)
        xc = _dot(f_ref[k, :kb, :], zb)
        xs = _dot(f_ref[k, kb:, :], zb)
        ec, oc = xc[:, :cb], xc[:, cb:]
        es, os_ = xs[:, :cb], xs[:, cb:]
        cw = cw_ref[rows, :]
        sw = sw_ref[rows, :]
        p_re = cw * oc - sw * os_
        p_im = -(cw * os_ + sw * oc)
        xa_re, xa_im = ec + p_re, p_im - es
        xb_re, xb_im = ec - p_re, es + p_im
        har, hai = har_ref[0, rows, :], hai_ref[0, rows, :]
        hbr, hbi = hbr_ref[0, rows, :], hbi_ref[0, rows, :]
        ya_re = xa_re * har - xa_im * hai
        ya_im = xa_re * hai + xa_im * har
        yb_re = xb_re * hbr - xb_im * hbi
        yb_im = xb_re * hbi + xb_im * hbr
        ze_re, ze_im = ya_re + yb_re, ya_im - yb_im
        d_re, d_im = ya_re - yb_re, ya_im + yb_im
        zo_re = d_re * cw - d_im * sw
        zo_im = d_re * sw + d_im * cw
        spec = jnp.concatenate([jnp.concatenate([ze_re, zo_re], axis=1),
                                jnp.concatenate([-ze_im, -zo_im], axis=1)], axis=0).astype(BF16)
        part = _dot(g_ref[k], spec)
        acc = part if acc is None else acc + part
    xe, xo = conv3(xe_ref, xo_ref, cwx_ref, cbx_ref)
    re = xe * (acc[:, :cb] + ze * skip_ref[...])
    ro = xo * (acc[:, cb:] + zo * skip_ref[...])
    if last:
        re = re * _silu(ge_ref[...])
        ro = ro * _silu(go_ref[...])
    oe_ref[...] = re.astype(oe_ref.dtype)
    oo_ref[...] = ro.astype(oo_ref.dtype)


def _hy_half_tables(L, kb, cb):
    M = L // 2
    nk = M // kb
    k = np.arange(M, dtype=np.int64)[:, None]
    s = np.arange(M, dtype=np.int64)[None, :]
    ang = ((2 * k + 1) * s % (2 * L)).astype(np.float64) * (2.0 * math.pi / (2 * L))
    c2, s2 = np.cos(ang).astype(np.float32), np.sin(ang).astype(np.float32)
    fwd = np.concatenate([c2.reshape(nk, kb, M), s2.reshape(nk, kb, M)], axis=1)
    inv = np.concatenate([c2.T.reshape(M, nk, kb), s2.T.reshape(M, nk, kb)], axis=2)
    inv = np.ascontiguousarray(inv.transpose(1, 0, 2))
    w = (2 * k + 1).astype(np.float64) * (2.0 * math.pi / (4 * L))
    cw = np.ascontiguousarray(np.broadcast_to(np.cos(w), (M, cb))).astype(np.float32)
    sw = np.ascontiguousarray(np.broadcast_to(np.sin(w), (M, cb))).astype(np.float32)
    return (jnp.asarray(fwd).astype(BF16), jnp.asarray(inv).astype(BF16), jnp.asarray(cw), jnp.asarray(sw))


def _hy_conv(z_even, z_odd, x_even, x_odd, g_even, g_odd, conv_w, conv_b, skip, order, tables, spectra,
             B, L, first, last):
    M = L // 2
    cb = HY_CH_BLOCK
    ncb = HY_CH // cb
    fwd, inv, cw, sw = tables
    nk, _, kb2 = inv.shape
    kb = kb2 // 2
    once = dict(pipeline_mode=pl.Buffered(1))
    data = lambda op: pl.BlockSpec((M, cb), lambda b, c: (b, op[1] + c))
    cwspec = lambda off: pl.BlockSpec((3, cb), lambda b, c: (0, off + c))
    cbspec = lambda off: pl.BlockSpec((1, cb), lambda b, c: (0, off + c))
    hspec = pl.BlockSpec((1, M, cb), lambda b, c: (order, 0, c))
    operands = [z_even, z_odd, x_even, x_odd, g_even, g_odd]
    in_specs = ([data(op) for op in operands]
                + [cwspec(0), cbspec(0), cwspec((1 + order) * ncb), cbspec((1 + order) * ncb),
                   pl.BlockSpec((1, cb), lambda b, c: (0, c)),
                   pl.BlockSpec(fwd.shape, lambda b, c: (0, 0, 0), **once),
                   pl.BlockSpec(inv.shape, lambda b, c: (0, 0, 0), **once),
                   hspec, hspec, hspec, hspec,
                   pl.BlockSpec((M, cb), lambda b, c: (0, 0), **once),
                   pl.BlockSpec((M, cb), lambda b, c: (0, 0), **once)])
    out_dtype = MIX_DTYPE if last else F32
    out = pl.BlockSpec((M, cb), lambda b, c: (b, c))
    return pl.pallas_call(
        functools.partial(_hy_conv_kernel, kb=kb, nk=nk, first=first, last=last),
        grid=(B, ncb),
        in_specs=in_specs,
        out_specs=[out, out],
        out_shape=[jax.ShapeDtypeStruct((B * M, HY_CH), out_dtype)] * 2,
        compiler_params=_cparams(("arbitrary", "arbitrary")),
        name="hyena_conv",
    )(*[op[0] for op in operands], conv_w, conv_b, conv_w, conv_b, skip[order:order + 1],
      fwd, inv, *spectra, cw, sw)


OD_GQ, OD_GG, OD_MG, OD_GK, OD_GV, OD_MQ, OD_MKV, OD_MPE = 0, 4, 8, 12, 14, 16, 18, 19
OD_WIDTH = 20 * LANES


def _rope_swap(x, quarter):
    n = x.shape[1]
    lane = lax.broadcasted_iota(jnp.int32, x.shape, 1)
    first = (lane & (2 * quarter - 1)) < quarter
    return jnp.where(first, pltpu.roll(x, n - quarter, 1), pltpu.roll(x, quarter, 1))


def _odd_in_kernel(x_ref, sh_ref, sc_ref, g_ref, w_ref, qg_ref, kg_ref, mqg_ref, mqup_ref, mkvg_ref, wk_ref,
                   *rest, rope):
    if rope:
        cg_ref, sg_ref, cm_ref, sm_ref = rest[:4]
        rest = rest[4:]
    q_out, qc_out, k16_out, v16_out, kc16_out, gate_out, k_out, v_out, ckv_out, kpe_out = rest
    D = HEAD_DIM
    hb = (_rms(x_ref[...], g_ref[...]) * (1.0 + sc_ref[0]) + sh_ref[0]).astype(BF16)

    def proj(col, width):
        return _dot(hb, w_ref[:, col * D:(col + width) * D])

    def rot(x, c, s, quarter):
        return x * c + _rope_swap(x, quarter) * s if rope else x

    gscale = (D ** -0.5) * LOG2E
    zq = proj(OD_GQ, GQA_HEADS)
    for h in range(GQA_HEADS):
        q = rot(_rms(zq[:, h * D:(h + 1) * D], qg_ref[...]), cg_ref[...] if rope else None,
                sg_ref[...] if rope else None, D // 4)
        q_out[:, h * D:(h + 1) * D] = (q * gscale).astype(BF16)
    zk = proj(OD_GK, GQA_KV_HEADS)
    zv = proj(OD_GV, GQA_KV_HEADS)
    v_out[...] = zv
    v16_out[...] = zv.astype(BF16)
    for h in range(GQA_KV_HEADS):
        kn = _rms(zk[:, h * D:(h + 1) * D], kg_ref[...])
        k_out[:, h * D:(h + 1) * D] = kn
        k16_out[:, h * D:(h + 1) * D] = rot(kn, cg_ref[...] if rope else None,
                                            sg_ref[...] if rope else None, D // 4).astype(BF16)
    gate_out[...] = proj(OD_GG, GQA_HEADS + MLA_HEADS)

    mq = _rms(proj(OD_MQ, 2), mqg_ref[...]).astype(BF16)
    qm = _dot(mq, mqup_ref[...])
    mscale = ((D + MLA_ROPE_DIM) ** -0.5) * LOG2E
    qpe = qm[:, MLA_HEADS * D:]
    if rope:
        qpe = rot(qpe, cm_ref[...], sm_ref[...], MLA_ROPE_DIM // 4)
    for h in range(MLA_HEADS):
        qn = qm[:, h * D:(h + 1) * D].astype(BF16)
        qc_out[:, 2 * h * D:(2 * h + 1) * D] = (_dot_nt(qn, wk_ref[h]) * mscale).astype(BF16)
        qc_out[:, (2 * h + 1) * D:(2 * h + 2) * D] = (qpe[:, h * D:(h + 1) * D] * mscale).astype(BF16)
    ckv = _rms(proj(OD_MKV, 1), mkvg_ref[...])
    ckv_out[...] = ckv
    kpe = proj(OD_MPE, 1)
    kpe_out[...] = kpe
    if rope:
        kpe = rot(kpe, cm_ref[:, :D], sm_ref[:, :D], MLA_ROPE_DIM // 4)
    kc16_out[:, :D] = ckv.astype(BF16)
    kc16_out[:, D:] = kpe.astype(BF16)


def _rope_tables(L, R):
    rows = np.arange(L) // GRID_W
    cols = np.arange(L) % GRID_W
    quarter = R // 4
    inv = ROPE_THETA ** (-np.arange(quarter, dtype=np.float32) * 2.0 / (R // 2))
    a_r = rows[:, None].astype(np.float32) * inv[None, :]
    a_c = cols[:, None].astype(np.float32) * inv[None, :]
    cos = np.concatenate([np.cos(a_r), np.cos(a_r), np.cos(a_c), np.cos(a_c)], axis=1)
    sin = np.concatenate([-np.sin(a_r), np.sin(a_r), -np.sin(a_c), np.sin(a_c)], axis=1)
    return cos.astype(np.float32), sin.astype(np.float32)


def _odd_in(x2, shift, scale, pre_g, w_main, q_g, k_g, mq_g, mq_up, mkv_g, wk, L, tiles_per_mod, rope):
    T, dm = x2.shape
    tm = TOKEN_TILE
    D = HEAD_DIM
    if tiles_per_mod:
        mod_map = lambda i: (i // tiles_per_mod, 0, 0)
    else:
        mod_map = lambda i: (0, 0, 0)
    full = lambda shape: pl.BlockSpec(shape, lambda i: tuple(0 for _ in shape))
    in_specs = [pl.BlockSpec((tm, dm), lambda i: (i, 0)),
                pl.BlockSpec((1, 1, dm), mod_map), pl.BlockSpec((1, 1, dm), mod_map),
                full((1, dm)), full(w_main.shape),
                full((1, D)), full((1, D)), full((1, 2 * D)), full(mq_up.shape), full((1, D)), full(wk.shape)]
    args = [x2, shift, scale, pre_g, w_main, q_g, k_g, mq_g, mq_up, mkv_g, wk]
    if rope:
        cg, sg = _rope_tables(L, D)
        cm, sm = _rope_tables(L, MLA_ROPE_DIM)
        widen = lambda t: np.tile(np.concatenate([t, np.zeros_like(t)], axis=1), (1, MLA_HEADS))
        per = L // tm
        pos = lambda w: pl.BlockSpec((tm, w), lambda i: (i % per, 0))
        in_specs += [pos(D), pos(D), pos(MLA_HEADS * D), pos(MLA_HEADS * D)]
        args += [jnp.asarray(cg), jnp.asarray(sg), jnp.asarray(widen(cm)), jnp.asarray(widen(sm))]
    tile = lambda w: pl.BlockSpec((tm, w), lambda i: (i, 0))
    widths = [(4 * D, BF16), (8 * D, BF16), (2 * D, BF16), (2 * D, BF16), (2 * D, BF16), (8 * D, F32),
              (2 * D, F32), (2 * D, F32), (D, F32), (D, F32)]
    return pl.pallas_call(
        functools.partial(_odd_in_kernel, rope=rope),
        grid=(T // tm,),
        in_specs=in_specs,
        out_specs=[tile(w) for w, _ in widths],
        out_shape=[jax.ShapeDtypeStruct((T, w), dt) for w, dt in widths],
        compiler_params=_cparams(("arbitrary",)),
        name="odd_in_proj",
    )(*args)


def _softmax_pv(score_blocks, value_blocks):
    m = score_blocks[0].max(axis=-1, keepdims=True)
    for s in score_blocks[1:]:
        m = jnp.maximum(m, s.max(axis=-1, keepdims=True))
    acc = None
    den = None
    for s, v in zip(score_blocks, value_blocks):
        p = jnp.exp2(s - m)
        d = p.sum(axis=-1, keepdims=True)
        a = _dot(p.astype(BF16), v)
        acc = a if acc is None else acc + a
        den = d if den is None else den + d
    return acc / den


def _gqa_kernel(q_ref, k_ref, v_ref, gate_ref, *rest, cached):
    if cached:
        ck_ref, cv_ref, o_ref = rest
    else:
        (o_ref,) = rest
    D = HEAD_DIM
    group = GQA_HEADS // GQA_KV_HEADS
    q = q_ref[...]
    keys = [k_ref[...]] + ([ck_ref[0, 0]] if cached else [])
    values = [v_ref[...]] + ([cv_ref[0, 0]] if cached else [])
    outs = []
    for g in range(group):
        qh = q[:, g * D:(g + 1) * D]
        outs.append(_softmax_pv([_dot_nt(qh, kk) for kk in keys], values))
    o_ref[...] = (jnp.concatenate(outs, axis=1) * _silu(gate_ref[...])).astype(o_ref.dtype)


def _gqa(qg, kg, vg, gates, B, L, tq, cache_k=None, cache_v=None, layer_j=0):
    T = qg.shape[0]
    D = HEAD_DIM
    group = GQA_HEADS // GQA_KV_HEADS
    nq = L // tq
    cached = cache_k is not None
    in_specs = [pl.BlockSpec((tq, group * D), lambda b, h, i: (b * nq + i, h)),
                pl.BlockSpec((L, D), lambda b, h, i: (b, h)),
                pl.BlockSpec((L, D), lambda b, h, i: (b, h)),
                pl.BlockSpec((tq, group * D), lambda b, h, i: (b * nq + i, h))]
    args = [qg, kg, vg, gates]
    if cached:
        P = cache_k.shape[2]
        cspec = pl.BlockSpec((1, 1, P, D), lambda b, h, i: (b, layer_j, 0, h))
        in_specs += [cspec, cspec]
        args += [cache_k, cache_v]
    return pl.pallas_call(
        functools.partial(_gqa_kernel, cached=cached),
        grid=(B, GQA_KV_HEADS, nq),
        in_specs=in_specs,
        out_specs=pl.BlockSpec((tq, group * D), lambda b, h, i: (b * nq + i, h)),
        out_shape=jax.ShapeDtypeStruct((T, GQA_HEADS * D), MIX_DTYPE),
        compiler_params=_cparams(("arbitrary", "arbitrary", "arbitrary")),
        name="gqa_attention",
    )(*args)


def _mla_kernel(qc_ref, kc_ref, gate_ref, wv_ref, *rest, cached):
    if cached:
        ckc_ref, o_ref = rest
    else:
        (o_ref,) = rest
    D = HEAD_DIM
    H = MLA_HEADS
    qc = qc_ref[...]
    tq = qc.shape[0]
    keys = [kc_ref[...]] + ([ckc_ref[0, 0]] if cached else [])
    per = H // MLA_HEAD_STACKS
    outs = []
    for s in range(MLA_HEAD_STACKS):
        heads = range(s * per, (s + 1) * per)
        qs = jnp.concatenate([qc[:, 2 * h * D:(2 * h + 2) * D] for h in heads], axis=0)
        o = _softmax_pv([_dot_nt(qs, kk) for kk in keys], [kk[:, :D] for kk in keys]).astype(BF16)
        for i, h in enumerate(heads):
            outs.append(_dot(o[i * tq:(i + 1) * tq], wv_ref[h]))
    o_ref[...] = (jnp.concatenate(outs, axis=1) * _silu(gate_ref[...])).astype(o_ref.dtype)


def _mla(qc, kc, gates, wv, B, L, tq, cache_kc=None, layer_j=0):
    T = qc.shape[0]
    D = HEAD_DIM
    H = MLA_HEADS
    nq = L // tq
    cached = cache_kc is not None
    in_specs = [pl.BlockSpec((tq, 2 * H * D), lambda b, i: (b * nq + i, 0)),
                pl.BlockSpec((L, 2 * D), lambda b, i: (b, 0)),
                pl.BlockSpec((tq, H * D), lambda b, i: (b * nq + i, 1)),
                pl.BlockSpec(wv.shape, lambda b, i: (0, 0, 0))]
    args = [qc, kc, gates, wv]
    if cached:
        P = cache_kc.shape[2]
        in_specs += [pl.BlockSpec((1, 1, P, 2 * D), lambda b, i: (b, layer_j, 0, 0))]
        args += [cache_kc]
    return pl.pallas_call(
        functools.partial(_mla_kernel, cached=cached),
        grid=(B, nq),
        in_specs=in_specs,
        out_specs=pl.BlockSpec((tq, H * D), lambda b, i: (b * nq + i, 0)),
        out_shape=jax.ShapeDtypeStruct((T, H * D), MIX_DTYPE),
        compiler_params=_cparams(("arbitrary", "arbitrary")),
        name="mla_attention",
    )(*args)


def _even_weights(even_in_w, gdn_conv_w, gdn_a_log, gdn_dt_bias, hyena_conv_w, hyena_conv_b, j):
    w = even_in_w[j]
    qkv_w = 3 * GDN_HEADS * LANES
    n_ab = 4 * GDN_HEADS
    gate_end = qkv_w + n_ab + GDN_HEADS * LANES
    w_a = jnp.concatenate([w[:, :qkv_w], w[:, qkv_w + n_ab:gate_end]], axis=1).astype(BF16)
    w_small = _pad_to(w[:, qkv_w:qkv_w + n_ab], (w.shape[0], LANES)).astype(BF16)
    w_b = w[:, gate_end:].astype(BF16)
    n_dir = 2 * GDN_HEADS
    alog_row = _pad_to(gdn_a_log[j].reshape(1, n_dir), (1, LANES))
    dtb_row = _pad_to(gdn_dt_bias[j].reshape(1, n_dir), (1, LANES))
    return w_a, w_small, w_b, alog_row, dtb_row


def _even_layer(x2, B, L, shift, scale, gate, tiles_per_mod, pre_g, post_g, wts, hy, out_w,
                gdn_conv_w, gdn_norm_g, hy_conv_w, hy_conv_b, hy_skip, state_f, state_b, j):
    w_a, w_small, w_b, alog_row, dtb_row = wts
    tables, spectra = hy
    z, zab, zp = _in_proj(x2, shift, scale, pre_g, w_a, w_small, w_b, tiles_per_mod)
    oa, s_f, s_b = _gdn(z, zab, gdn_conv_w, gdn_norm_g, alog_row, dtb_row, B, L, state_f, state_b, j)
    ncb = HY_CH // HY_CH_BLOCK
    n_groups = zp.shape[1] // (2 * HY_CH)
    even = lambda idx: (zp, idx * ncb)
    odd = lambda idx: (zp, (n_groups + idx) * ncb)
    z1e, z1o = _hy_conv(even(0), odd(0), even(1), odd(1), even(3), odd(3), hy_conv_w, hy_conv_b, hy_skip, 0,
                        tables, spectra, B, L, first=True, last=False)
    obe, obo = _hy_conv((z1e, 0), (z1o, 0), even(2), odd(2), even(3), odd(3), hy_conv_w, hy_conv_b, hy_skip, 1,
                        tables, spectra, B, L, first=False, last=True)
    x_new = _out_proj(oa, (obe, obo), x2, gate, post_g, out_w, tiles_per_mod)
    return x_new, s_f, s_b


def _odd_weights(odd_in_w, mla_q_up, mla_kv_up, j):
    w = odd_in_w[j]
    D = HEAD_DIM
    o = np.cumsum([0, 4 * D, 2 * D, 2 * D, 4 * D, 2 * D, D, MLA_ROPE_DIM, 4 * D])
    gq, gk, gv, gg, mq, mkv, mpe, mg = [w[:, o[i]:o[i + 1]] for i in range(8)]
    w_main = jnp.concatenate([gq, gg, mg, gk, gv, mq, mkv, _pad_to(mpe, (w.shape[0], D))], axis=1).astype(BF16)
    up = mla_q_up[j].reshape(-1, MLA_HEADS, D + MLA_ROPE_DIM)
    rope_cols = _pad_to(up[:, :, D:], (up.shape[0], MLA_HEADS, D))
    mq_up = jnp.concatenate([up[:, :, :D].reshape(-1, MLA_HEADS * D),
                             rope_cols.reshape(-1, MLA_HEADS * D)], axis=1).astype(BF16)
    kv = mla_kv_up[j].reshape(-1, MLA_HEADS, 2 * D)
    wk = kv[:, :, :D].transpose(1, 0, 2).astype(BF16)
    wv = kv[:, :, D:].transpose(1, 0, 2).astype(BF16)
    return w_main, mq_up, wk, wv


def _odd_layer(x2, B, L, shift, scale, gate, tiles_per_mod, pre_g, post_g, wts, out_w,
               q_g, k_g, mq_g, mkv_g, caches, j):
    w_main, mq_up, wk, wv = wts
    rope = caches is not None
    qg, qc, k16, v16, kc16, gates, kg, v32, ckv, kpe = _odd_in(
        x2, shift, scale, pre_g, w_main, q_g, k_g, mq_g, mq_up, mkv_g, wk, L, tiles_per_mod, rope)
    if rope:
        ck, cv, ckc = caches
        og = _gqa(qg, k16, v16, gates, B, L, GQA_Q_TILE, ck, cv, j)
        om = _mla(qc, kc16, gates, wv, B, L, MLA_Q_TILE, ckc, j)
    else:
        og = _gqa(qg, k16, v16, gates, B, L, GQA_Q_TILE)
        om = _mla(qc, kc16, gates, wv, B, L, MLA_Q_TILE)
    x_new = _out_proj(og, om, x2, gate, post_g, out_w, tiles_per_mod)
    return x_new, (kg, v32, ckv, kpe)


def kernel(x_prompt, x_sample, state_gdn_fwd, state_gdn_bwd, cache_gqa_k, cache_gqa_v, cache_mla_ckv, cache_mla_kpe, c, c_ctx, mod_w, mod_b, pre_norm_g, post_norm_g, even_in_w, gdn_conv_w, gdn_a_log, gdn_dt_bias, gdn_norm_g, hyena_conv_w, hyena_conv_b, hyena_ffn_w1, hyena_ffn_b1, hyena_ffn_w2, hyena_ffn_b2, hyena_ffn_w3, hyena_sin_freq, hyena_bias, even_out_w, odd_in_w, gqa_q_norm_g, gqa_k_norm_g, mla_q_norm_g, mla_q_up, mla_kv_norm_g, mla_kv_up, odd_out_w):
    Bp, Lp, D = x_prompt.shape
    Bs, Ls, _ = x_sample.shape
    depth = mod_w.shape[0]
    xp = x_prompt.reshape(Bp * Lp, D)
    xs = x_sample.reshape(Bs * Ls, D)

    n_cond = 1 + Bs
    rows = -(-n_cond // 8) * 8
    cond = _pad_to(jnp.concatenate([c_ctx[None, :], c], axis=0), (rows, D))
    mod = _modulation(cond, mod_w, mod_b)

    P = cache_gqa_k.shape[2]
    n_odd = cache_gqa_k.shape[1]
    ck = cache_gqa_k.reshape(Bs, n_odd, P, GQA_KV_HEADS * HEAD_DIM).astype(BF16)
    cv = cache_gqa_v.reshape(Bs, n_odd, P, GQA_KV_HEADS * HEAD_DIM).astype(BF16)
    ckc = jnp.concatenate([cache_mla_ckv, _pad_to(cache_mla_kpe, cache_mla_ckv.shape)], axis=-1).astype(BF16)

    dft = {L: _dft_blocks(L, min(HY_FREQ_BLOCK, L)) for L in (Lp, Ls)}
    half = {L: _hy_half_tables(L, min(HY_HALF_FREQ_BLOCK, L // 2), HY_CH_BLOCK) for L in (Lp, Ls)}
    tpm_s = Ls // TOKEN_TILE

    new_f, new_b, new_gk, new_gv, new_ckv, new_kpe = [], [], [], [], [], []
    for i in range(depth):
        j = i // 2
        m = mod[i]
        sh_p, sc_p, gt_p = [m[0:1, k * D:(k + 1) * D].reshape(1, 1, D) for k in range(3)]
        sh_s, sc_s, gt_s = [m[1:n_cond, k * D:(k + 1) * D].reshape(Bs, 1, D) for k in range(3)]
        pre_g = pre_norm_g[i][None, :]
        post_g = post_norm_g[i][None, :]
        if i % 2 == 0:
            wts = _even_weights(even_in_w, gdn_conv_w, gdn_a_log, gdn_dt_bias, hyena_conv_w, hyena_conv_b, j)
            out_w = even_out_w[j].astype(BF16)
            hy = {}
            for L in (Lp, Ls):
                filt = _hy_filters(L, hyena_ffn_w1[j], hyena_ffn_b1[j], hyena_ffn_w2[j], hyena_ffn_b2[j],
                                   hyena_ffn_w3[j], hyena_sin_freq[j])
                hc, hs = _hy_spectrum(filt, dft[L], L, min(HY_FREQ_BLOCK, L))
                M = L // 2
                spectra = (hc[:, :M], -hs[:, :M], jnp.flip(hc[:, M:], axis=1), -jnp.flip(hs[:, M:], axis=1))
                hy[L] = (half[L], spectra)
            common = (gdn_conv_w[j], gdn_norm_g[j][None, :], hyena_conv_w[j], hyena_conv_b[j][None, :], hyena_bias[j])
            xp, sf, sb = _even_layer(xp, Bp, Lp, sh_p, sc_p, gt_p, 0, pre_g, post_g, wts, hy[Lp], out_w,
                                     *common, None, None, j)
            xs, _, _ = _even_layer(xs, Bs, Ls, sh_s, sc_s, gt_s, tpm_s, pre_g, post_g, wts, hy[Ls], out_w,
                                   *common, state_gdn_fwd, state_gdn_bwd, j)
            new_f.append(sf)
            new_b.append(sb)
        else:
            wts = _odd_weights(odd_in_w, mla_q_up, mla_kv_up, j)
            out_w = odd_out_w[j].astype(BF16)
            norms = (gqa_q_norm_g[j][None, :], gqa_k_norm_g[j][None, :], mla_q_norm_g[j][None, :],
                     mla_kv_norm_g[j][None, :])
            xp, (kg, v32, ckv, kpe) = _odd_layer(xp, Bp, Lp, sh_p, sc_p, gt_p, 0, pre_g, post_g, wts, out_w,
                                                 *norms, None, j)
            xs, _ = _odd_layer(xs, Bs, Ls, sh_s, sc_s, gt_s, tpm_s, pre_g, post_g, wts, out_w,
                               *norms, (ck, cv, ckc), j)
            new_gk.append(kg.reshape(Bp, Lp, GQA_KV_HEADS, HEAD_DIM))
            new_gv.append(v32.reshape(Bp, Lp, GQA_KV_HEADS, HEAD_DIM))
            new_ckv.append(ckv.reshape(Bp, Lp, HEAD_DIM))
            new_kpe.append(kpe[:, :MLA_ROPE_DIM].reshape(Bp, Lp, MLA_ROPE_DIM))
    return (xp.reshape(Bp, Lp, D), xs.reshape(Bs, Ls, D),
            jnp.stack(new_f, axis=1), jnp.stack(new_b, axis=1),
            jnp.stack(new_gk, axis=1), jnp.stack(new_gv, axis=1),
            jnp.stack(new_ckv, axis=1), jnp.stack(new_kpe, axis=1))
```

```python
import functools
import math

import numpy as np
import jax
import jax.numpy as jnp
from jax import lax
from jax.experimental import pallas as pl
from jax.experimental.pallas import tpu as pltpu

F32 = jnp.float32
BF16 = jnp.bfloat16
MIX_DTYPE = BF16

NORM_EPS = 1e-6
ROPE_THETA = 10000.0
GRID_W = 64

GDN_HEADS = 4
GDN_DK = 128
GDN_CHUNK = 64
GDN_CONV_W = 5
GDN_PREP_CHAINS = 32
GDN_ALL_HEADS_MAX_LEN = 256
GDN_HEADS_LONG = 1
HY_CH = 512
HY_BANDS = 16
HY_DECAY_TARGET = 1e-2
HY_SHORT_DECAY_PCT = 0.3
HY_LONG_DECAY_PCT = 1.5
GQA_HEADS = 4
GQA_KV_HEADS = 2
HEAD_DIM = 128
MLA_HEADS = 4
MLA_ROPE_DIM = 64

LOG2E = math.log2(math.e)

LANES = 128
VMEM_LIMIT = 56 * 1024 * 1024

TOKEN_TILE = 256
HY_FREQ_BLOCK = 512
HY_HALF_FREQ_BLOCK = 256
HY_CH_BLOCK = 256
GQA_Q_TILE = 256
MLA_Q_TILE = 128
MLA_HEAD_STACKS = 2


def _cparams(sem):
    return pltpu.CompilerParams(dimension_semantics=sem, vmem_limit_bytes=VMEM_LIMIT)


def _dot(a, b):
    return jnp.dot(a, b, preferred_element_type=F32)


def _dot_nt(a, b):
    return lax.dot_general(a, b, (((1,), (1,)), ((), ())), preferred_element_type=F32)


def _dot_tn(a, b):
    return lax.dot_general(a, b, (((0,), (0,)), ((), ())), preferred_element_type=F32)


def _split(a):
    hi = a.astype(BF16)
    lo = (a - hi.astype(F32)).astype(BF16)
    return hi, lo


def _dot3(a, b):
    ah, al = _split(a)
    bh, bl = _split(b)
    return _dot(ah, bh) + (_dot(ah, bl) + _dot(al, bh))


def _bdot(a, b):
    return jnp.einsum('gij,gjk->gik', a, b, preferred_element_type=F32)


def _bdot16(a, b):
    return _bdot(a.astype(BF16), b.astype(BF16))


def _silu(x):
    return x * (1.0 / (1.0 + jnp.exp(-x)))


def _sigmoid(x):
    return 1.0 / (1.0 + jnp.exp(-x))


def _softplus(x):
    return jnp.maximum(x, 0.0) + jnp.log(1.0 + jnp.exp(-jnp.abs(x)))


def _rms(x, g):
    return x * lax.rsqrt(jnp.mean(x * x, axis=-1, keepdims=True) + NORM_EPS) * g


def _shift_rows(x, s):
    L = x.shape[0]
    if s == 0:
        return x
    rolled = pltpu.roll(x, (-s) % L, 0)
    row = lax.broadcasted_iota(jnp.int32, x.shape, 0)
    valid = (row + s >= 0) & (row + s < L)
    return jnp.where(valid, rolled, 0.0)


def _mod_kernel(c_ref, w_ref, b_ref, o_ref):
    c = _silu(c_ref[...])
    o_ref[0] = _dot3(c, w_ref[0]) + b_ref[0]


def _modulation(cond, mod_w, mod_b):
    depth, d, d3 = mod_w.shape
    r = cond.shape[0]
    nb = d3 // d
    return pl.pallas_call(
        _mod_kernel,
        grid=(depth, nb),
        in_specs=[pl.BlockSpec((r, d), lambda i, n: (0, 0)),
                  pl.BlockSpec((1, d, d), lambda i, n: (i, 0, n)),
                  pl.BlockSpec((1, 1, d), lambda i, n: (i, 0, n))],
        out_specs=pl.BlockSpec((1, r, d), lambda i, n: (i, 0, n)),
        out_shape=jax.ShapeDtypeStruct((depth, r, d3), F32),
        compiler_params=_cparams(("arbitrary", "arbitrary")),
        name="adaln_modulation",
    )(cond, mod_w, mod_b.reshape(depth, 1, d3))


def _in_proj_kernel(x_ref, sh_ref, sc_ref, g_ref, wa_ref, ws_ref, wb_ref, oa_ref, os_ref, ob_ref, h_s):
    x = x_ref[...]
    h = _rms(x, g_ref[...]) * (1.0 + sc_ref[0]) + sh_ref[0]
    n_lane_blocks = h.shape[1] // LANES
    for jb in range(n_lane_blocks):
        h_s[jb] = h[:, jb * LANES:(jb + 1) * LANES]
    hb = h.astype(BF16)
    tm = hb.shape[0]
    step = 512
    na = wa_ref.shape[1]
    for n0 in range(0, na, step):
        oa_ref[:, n0:n0 + step] = _dot(hb, wa_ref[:, n0:n0 + step])
    os_ref[...] = _dot(hb, ws_ref[...])
    nb = wb_ref.shape[1]
    for parity in (0, 1):
        hp = jnp.concatenate([h_s[jb, pl.ds(parity, tm // 2, stride=2), :] for jb in range(n_lane_blocks)],
                             axis=1).astype(BF16)
        for n0 in range(0, nb, step):
            ob_ref[:, parity * nb + n0:parity * nb + n0 + step] = _dot(hp, wb_ref[:, n0:n0 + step])


def _in_proj(x2, shift, scale, pre_g, w_a, w_small, w_b, tiles_per_mod):
    T, D = x2.shape
    na, nb = w_a.shape[1], w_b.shape[1]
    tm = TOKEN_TILE
    if tiles_per_mod:
        mod_map = lambda i: (i // tiles_per_mod, 0, 0)
    else:
        mod_map = lambda i: (0, 0, 0)
    const = lambda shape: pl.BlockSpec(shape, lambda i: (0, 0))
    return pl.pallas_call(
        _in_proj_kernel,
        grid=(T // tm,),
        in_specs=[pl.BlockSpec((tm, D), lambda i: (i, 0)),
                  pl.BlockSpec((1, 1, D), mod_map),
                  pl.BlockSpec((1, 1, D), mod_map),
                  const((1, D)), const((D, na)), const((D, LANES)), const((D, nb))],
        out_specs=[pl.BlockSpec((tm, na), lambda i: (i, 0)),
                   pl.BlockSpec((tm, LANES), lambda i: (i, 0)),
                   pl.BlockSpec((tm // 2, 2 * nb), lambda i: (i, 0))],
        out_shape=[jax.ShapeDtypeStruct((T, na), F32), jax.ShapeDtypeStruct((T, LANES), F32),
                   jax.ShapeDtypeStruct((T // 2, 2 * nb), F32)],
        scratch_shapes=[pltpu.VMEM((D // LANES, tm, LANES), F32)],
        compiler_params=_cparams(("arbitrary",)),
        name="in_proj",
    )(x2, shift, scale, pre_g, w_a, w_small, w_b)


def _row_spread(n, parity):
    r = lax.broadcasted_iota(jnp.int32, (n, n // 2), 0)
    c = lax.broadcasted_iota(jnp.int32, (n, n // 2), 1)
    return (r == 2 * c + parity).astype(BF16)


def _out_proj_kernel(a_ref, *rest, half, split):
    if split:
        be_ref, bo_ref, x_ref, gt_ref, g_ref, w_ref, o_ref = rest
        tm = a_ref.shape[0]
        b = (_dot(_row_spread(tm, 0), be_ref[...]) + _dot(_row_spread(tm, 1), bo_ref[...])).astype(BF16)
    else:
        b_ref, x_ref, gt_ref, g_ref, w_ref, o_ref = rest
        b = b_ref[...].astype(BF16)
    y = _dot(a_ref[...].astype(BF16), w_ref[:half, :]) + _dot(b, w_ref[half:, :])
    o_ref[...] = x_ref[...] + gt_ref[0] * _rms(y, g_ref[...])


def _out_proj(a, b, x2, gate, post_g, w, tiles_per_mod):
    T, D = x2.shape
    half = a.shape[1]
    tm = TOKEN_TILE
    split = isinstance(b, tuple)
    if tiles_per_mod:
        mod_map = lambda i: (i // tiles_per_mod, 0, 0)
    else:
        mod_map = lambda i: (0, 0, 0)
    if split:
        b_specs = [pl.BlockSpec((tm // 2, half), lambda i: (i, 0))] * 2
        b_args = list(b)
    else:
        b_specs = [pl.BlockSpec((tm, half), lambda i: (i, 0))]
        b_args = [b]
    return pl.pallas_call(
        functools.partial(_out_proj_kernel, half=half, split=split),
        grid=(T // tm,),
        in_specs=[pl.BlockSpec((tm, half), lambda i: (i, 0))] + b_specs + [
                  pl.BlockSpec((tm, D), lambda i: (i, 0)),
                  pl.BlockSpec((1, 1, D), mod_map),
                  pl.BlockSpec((1, D), lambda i: (0, 0)),
                  pl.BlockSpec((2 * half, D), lambda i: (0, 0))],
        out_specs=pl.BlockSpec((tm, D), lambda i: (i, 0)),
        out_shape=jax.ShapeDtypeStruct((T, D), F32),
        compiler_params=_cparams(("arbitrary",)),
        name="out_proj",
    )(a, *b_args, x2, gate, post_g, w)


def _gdn_kernel(zq_ref, zk_ref, zv_ref, zg_ref, zab_ref, cwq_ref, cwk_ref, cwv_ref, ng_ref,
                alog_ref, dtb_ref, *rest, L, HB, has_state):
    if has_state:
        s0f_ref, s0b_ref = rest[:2]
        rest = rest[2:]
    (o_ref, sf_out_ref, sb_out_ref, q_s, k_s, v_s, g_s, b_s, st_s,
     oc_s, qe_s, sc_s, sm_s, gl_s, pad_s) = rest
    C = GDN_CHUNK
    n_chunks = L // C
    U = min(GDN_PREP_CHAINS // (2 * HB), n_chunks)
    HU = HB * U
    G = 2 * HU
    head0 = pl.program_id(1) * HB

    pad = 8
    pad_s[0:pad, :] = jnp.zeros((pad, LANES), F32)
    pad_s[pad + L:2 * pad + L, :] = jnp.zeros((pad, LANES), F32)

    def conv_silu(x, w_ref, lanes):
        half = GDN_CONV_W // 2
        pad_s[pad:pad + L, :] = x
        acc = x * w_ref[half:half + 1, lanes]
        for i in range(GDN_CONV_W):
            if i != half:
                start = pad + i - half
                acc = acc + pad_s[start:start + L, :] * w_ref[i:i + 1, lanes]
        return _silu(acc)

    def l2n(x):
        return x * lax.rsqrt(jnp.sum(x * x, axis=-1, keepdims=True) + NORM_EPS)

    for hh in range(HB):
        lanes = slice(hh * LANES, (hh + 1) * LANES)
        q_s[hh] = l2n(conv_silu(zq_ref[:, lanes], cwq_ref, lanes)) * (GDN_DK ** -0.5)
        k_s[hh] = l2n(conv_silu(zk_ref[:, lanes], cwk_ref, lanes))
        v_s[hh] = conv_silu(zv_ref[:, lanes], cwv_ref, lanes)
        if has_state:
            st_s[2 * hh] = s0f_ref[0, 0, hh]
            st_s[2 * hh + 1] = s0b_ref[0, 0, hh]
    if not has_state:
        st_s[...] = jnp.zeros_like(st_s)
    zab = zab_ref[...]
    g_s[...] = -jnp.exp(alog_ref[...]) * _softplus(zab + dtb_ref[...])
    b_s[...] = _sigmoid(zab)

    def iota(shape, axis):
        return lax.broadcasted_iota(jnp.int32, shape, axis)

    def direction(shape):
        return iota(shape, 0) & 1

    chain_shift = (2 * U).bit_length() - 1
    assert 2 * U == 1 << chain_shift

    def chain_head(shape):
        return head0 + (iota(shape, 0) >> chain_shift)

    sq = (G, C, C)
    row = iota(sq, 1)
    col = iota(sq, 2)
    signed = (row - col) * (1 - 2 * direction(sq))
    incl = signed >= 0
    strict = signed > 0
    eye = (row == col).astype(F32)
    same = [(row >> s) == (col >> s) for s in (3, 4, 5)]
    off_blocks = [same[1] & jnp.logical_not(same[0]), same[2] & jnp.logical_not(same[1]),
                  jnp.logical_not(same[2])]
    wide = (G, C, LANES)
    sel_lane = direction(wide) * GDN_HEADS + chain_head(wide)
    mask_g = iota(wide, 2) == sel_lane
    mask_b = iota(wide, 2) == sel_lane + 2 * GDN_HEADS
    tall = (G, LANES, C)
    mask_t = iota(tall, 1) == direction(tall) * GDN_HEADS + chain_head(tall)
    colv = (G, C, 1)
    mask_last = iota(colv, 1) == (1 - direction(colv)) * (C - 1)
    r2 = lax.broadcasted_iota(jnp.int32, (C, C), 0)
    c2 = lax.broadcasted_iota(jnp.int32, (C, C), 1)
    tri = jnp.concatenate([(r2 >= c2).astype(BF16), (r2 <= c2).astype(BF16)], axis=0)

    def both_dirs(x):
        return jnp.broadcast_to(x[:, None], (HU, 2) + x.shape[1:]).reshape((G,) + x.shape[1:])

    def all_heads(x):
        return jnp.broadcast_to(x[None], (HB,) + x.shape).reshape((HB * x.shape[0],) + x.shape[1:])

    def prep_group(c, carry):
        rows = pl.ds(pl.multiple_of(c * (U * C), U * C), U * C)
        q = q_s[:, rows, :].reshape(HU, C, LANES)
        k = k_s[:, rows, :].reshape(HU, C, LANES)
        v = v_s[:, rows, :].reshape(HU, C, LANES)
        ball = all_heads(b_s[rows, :].reshape(U, C, LANES))
        g_hi, g_lo = _split(g_s[rows, :])
        g_hi = g_hi.reshape(U, C, LANES)
        g_lo = g_lo.reshape(U, C, LANES)
        gcum = jnp.stack([_dot(tri, g_hi[u]) + _dot(tri, g_lo[u]) for u in range(U)])
        gcum = gcum.reshape(2 * U, C, LANES)
        gcum_t = all_heads(jnp.stack([gcum[g].T for g in range(2 * U)]))
        gcum = all_heads(gcum)
        gc = jnp.sum(jnp.where(mask_g, gcum, 0.0), axis=2, keepdims=True)
        gr = jnp.sum(jnp.where(mask_t, gcum_t, 0.0), axis=1, keepdims=True)
        beta = jnp.sum(jnp.where(mask_b, both_dirs(ball), 0.0), axis=2, keepdims=True)
        kbf = k.astype(BF16)
        kq = jnp.einsum('uik,ujk->uij', jnp.concatenate([k, q], axis=1).astype(BF16), kbf,
                        preferred_element_type=F32)
        kk = both_dirs(kq[:, :C])
        qk = both_dirs(kq[:, C:])
        decay = jnp.where(incl, jnp.exp(jnp.where(incl, gc - gr, 0.0)), 0.0)
        lmat = jnp.where(strict, beta * kk * decay, 0.0)
        p = jnp.where(same[0], -lmat, 0.0)
        tmat = eye + p
        for _ in range(2):
            p = _bdot16(p, p)
            tmat = tmat + _bdot16(tmat, p)
        for off in off_blocks:
            t16 = tmat.astype(BF16)
            tmat = tmat - _bdot(_bdot(t16, jnp.where(off, lmat, 0.0).astype(BF16)).astype(BF16), t16)
        eg = jnp.exp(gc)
        k2 = both_dirs(k)
        kb = k2 * beta
        sol = _bdot16(tmat, jnp.concatenate([both_dirs(v) * beta, kb * eg], axis=2))
        glast = jnp.sum(jnp.where(mask_last, gc, 0.0), axis=1, keepdims=True)
        kdec = k2 * jnp.exp(glast - gc)
        kdec_t = jnp.stack([kdec[g].T for g in range(G)]).astype(BF16)
        sol16 = sol.astype(BF16)
        r_in = _bdot((qk * decay).astype(BF16), sol16)
        r_kd = _bdot(kdec_t, sol16)
        def put(ref, val, size):
            n = 2 * U * size
            ref[:, pl.ds(pl.multiple_of(c * n, n), n), :] = val.reshape(HB, n, val.shape[-1])

        put(oc_s, r_in[:, :, :LANES], C)
        put(qe_s, (both_dirs(q) * eg - r_in[:, :, LANES:]).astype(BF16), C)
        put(sc_s, r_kd[:, :, :LANES], LANES)
        put(sm_s, (-r_kd[:, :, LANES:]).astype(BF16), LANES)
        put(gl_s, jnp.broadcast_to(jnp.exp(glast), (G, 8, LANES)), 8)
        return carry

    lax.fori_loop(0, n_chunks // U, prep_group, 0)

    def scan_step(c, carry):
        slot_f = c * 2
        slot_b = (n_chunks - 1 - c) * 2 + 1

        def ld(ref, size):
            parts = []
            for hh in range(HB):
                parts.append(ref[hh, pl.ds(pl.multiple_of(slot_f * size, size), size), :])
                parts.append(ref[hh, pl.ds(pl.multiple_of(slot_b * size, size), size), :])
            return jnp.stack(parts)

        s = st_s[...]
        sb16 = s.astype(BF16)
        o = _bdot(ld(qe_s, C), sb16) + ld(oc_s, C)
        st_s[...] = s * ld(gl_s, 8)[:, 0:1, :] + (_bdot(ld(sm_s, LANES), sb16) + ld(sc_s, LANES))
        for hh in range(HB):
            oc_s[hh, pl.ds(pl.multiple_of(slot_f * C, C), C), :] = o[2 * hh]
            oc_s[hh, pl.ds(pl.multiple_of(slot_b * C, C), C), :] = o[2 * hh + 1]
        return carry

    lax.fori_loop(0, n_chunks, scan_step, 0)

    FIN = 4

    def finish(i, carry):
        rows = pl.ds(pl.multiple_of(i * (FIN * C), FIN * C), FIN * C)
        outs = []
        for hh in range(HB):
            both = oc_s[hh, pl.ds(pl.multiple_of(i * (FIN * 2 * C), FIN * 2 * C), FIN * 2 * C), :]
            both = both.reshape(FIN, 2, C, LANES)
            outs.append(_rms((both[:, 0] + both[:, 1]).reshape(FIN * C, LANES), ng_ref[...]))
        o = jnp.concatenate(outs, axis=1) * _silu(zg_ref[rows, :])
        o_ref[rows, :] = o.astype(o_ref.dtype)
        return carry

    lax.fori_loop(0, n_chunks // FIN, finish, 0)
    for hh in range(HB):
        sf_out_ref[0, hh] = st_s[2 * hh]
        sb_out_ref[0, hh] = st_s[2 * hh + 1]


def _gdn(z, zab, conv_w, norm_g, alog_row, dtb_row, B, L, state_f=None, state_b=None, layer_j=0):
    T = z.shape[0]
    H = GDN_HEADS
    has_state = state_f is not None
    HB = H if L <= GDN_ALL_HEADS_MAX_LEN else GDN_HEADS_LONG
    nh = H // HB
    w = HB * LANES
    blk = lambda i: pl.BlockSpec((L, w), lambda b, h: (b, i * nh + h))
    cw = lambda i: pl.BlockSpec((GDN_CONV_W, w), lambda b, h: (0, i * nh + h))
    in_specs = [blk(0), blk(1), blk(2), blk(3),
                pl.BlockSpec((L, LANES), lambda b, h: (b, 0)),
                cw(0), cw(1), cw(2),
                pl.BlockSpec((1, LANES), lambda b, h: (0, 0)),
                pl.BlockSpec((1, LANES), lambda b, h: (0, 0)),
                pl.BlockSpec((1, LANES), lambda b, h: (0, 0))]
    args = [z, z, z, z, zab, conv_w, conv_w, conv_w, norm_g, alog_row, dtb_row]
    if has_state:
        st = pl.BlockSpec((1, 1, HB, GDN_DK, LANES), lambda b, h: (b, layer_j, h, 0, 0))
        in_specs += [st, st]
        args += [state_f, state_b]
    sout = pl.BlockSpec((1, HB, GDN_DK, LANES), lambda b, h: (b, h, 0, 0))
    n_slots = 2 * (L // GDN_CHUNK)
    scratch = ([pltpu.VMEM((HB, L, LANES), F32) for _ in range(3)]
               + [pltpu.VMEM((L, LANES), F32) for _ in range(2)]
               + [pltpu.VMEM((2 * HB, GDN_DK, LANES), F32)]
               + [pltpu.VMEM((HB, 2 * L, LANES), F32), pltpu.VMEM((HB, 2 * L, LANES), BF16)]
               + [pltpu.VMEM((HB, n_slots * GDN_DK, LANES), F32),
                  pltpu.VMEM((HB, n_slots * GDN_DK, LANES), BF16)]
               + [pltpu.VMEM((HB, n_slots * 8, LANES), F32)]
               + [pltpu.VMEM((L + 16, LANES), F32)])
    return pl.pallas_call(
        functools.partial(_gdn_kernel, L=L, HB=HB, has_state=has_state),
        grid=(B, nh),
        in_specs=in_specs,
        out_specs=[pl.BlockSpec((L, w), lambda b, h: (b, h)), sout, sout],
        out_shape=[jax.ShapeDtypeStruct((T, H * LANES), MIX_DTYPE),
                   jax.ShapeDtypeStruct((B, H, GDN_DK, LANES), F32),
                   jax.ShapeDtypeStruct((B, H, GDN_DK, LANES), F32)],
        scratch_shapes=scratch,
        compiler_params=_cparams(("arbitrary", "arbitrary")),
        name="gdn",
    )(*args)


def _hy_filter_kernel(f_ref, w1_ref, b1_ref, sf0_ref, w2_ref, b2_ref, sf1_ref, w3_ref, dl_ref, o_ref, *, L):
    rowi = lax.broadcasted_iota(jnp.int32, (L, LANES), 0).astype(F32)
    lane = lax.broadcasted_iota(jnp.int32, (L, LANES), 1)
    t = rowi * (1.0 / (L - 1))
    wpos = rowi * (2.0 * math.pi / L)
    ang = f_ref[...] * wpos
    z = jnp.where(lane == 0, t,
                  jnp.where(lane <= HY_BANDS, jnp.cos(ang),
                            jnp.where(lane <= 2 * HY_BANDS, -jnp.sin(ang), 0.0)))
    h = jnp.sin(sf0_ref[...] * (_dot3(z, w1_ref[...]) + b1_ref[...]))
    h = jnp.sin(sf1_ref[...] * (_dot3(h, w2_ref[...]) + b2_ref[...]))
    tc = lax.broadcasted_iota(jnp.int32, (L, HY_CH), 0).astype(F32) * (1.0 / (L - 1))
    window = jnp.exp(-tc * dl_ref[...])
    for j in range(4):
        o_ref[:, j * HY_CH:(j + 1) * HY_CH] = _dot3(h, w3_ref[:, j * HY_CH:(j + 1) * HY_CH]) * window


def _pad_to(a, shape):
    return jnp.pad(a, [(0, s - d) for s, d in zip(shape, a.shape)])


def _hy_filters(L, w1, b1, w2, b2, w3, sin_freq):
    fvals = np.linspace(1e-4, HY_BANDS - 1, HY_BANDS, dtype=np.float32)
    frow = np.zeros((1, LANES), np.float32)
    frow[0, 1:1 + HY_BANDS] = fvals
    frow[0, 1 + HY_BANDS:1 + 2 * HY_BANDS] = fvals
    deltas = np.abs(np.linspace(math.log(HY_DECAY_TARGET) / HY_LONG_DECAY_PCT,
                                math.log(HY_DECAY_TARGET) / HY_SHORT_DECAY_PCT, HY_CH, dtype=np.float32))
    args = [jnp.asarray(frow),
            _pad_to(w1, (LANES, LANES)), _pad_to(b1[None, :], (1, LANES)), _pad_to(sin_freq[0][None, :], (1, LANES)),
            _pad_to(w2, (LANES, LANES)), _pad_to(b2[None, :], (1, LANES)), _pad_to(sin_freq[1][None, :], (1, LANES)),
            _pad_to(w3, (LANES, 4 * HY_CH)), jnp.asarray(deltas[None, :])]
    return pl.pallas_call(
        functools.partial(_hy_filter_kernel, L=L),
        out_shape=jax.ShapeDtypeStruct((L, 4 * HY_CH), F32),
        compiler_params=pltpu.CompilerParams(vmem_limit_bytes=VMEM_LIMIT),
        name="hyena_filters",
    )(*args)


def _dft_tables(L):
    N = 2 * L
    k = np.arange(L, dtype=np.int64)[:, None]
    s = np.arange(L, dtype=np.int64)[None, :]
    ang = ((2 * k + 1) * s % (2 * N)).astype(np.float64) * (2.0 * math.pi / (2 * N))
    return np.cos(ang).astype(np.float32), np.sin(ang).astype(np.float32)


def _dft_blocks(L, kb):
    cm, sm = _dft_tables(L)
    nk = L // kb
    fwd = np.concatenate([cm.reshape(nk, kb, L), sm.reshape(nk, kb, L)], axis=1)
    return jnp.asarray(fwd).astype(BF16)


def _hy_spectrum_kernel(f_ref, flt_ref, hc_ref, hs_ref, *, L, kb):
    row = lax.broadcasted_iota(jnp.int32, (L, HY_CH), 0)
    scale = 1.0 / L
    for o in range(2):
        hf = flt_ref[:, (2 * o) * HY_CH:(2 * o + 1) * HY_CH]
        hb = jnp.where(row == 0, 0.0, flt_ref[:, (2 * o + 1) * HY_CH:(2 * o + 2) * HY_CH])
        a_hi, a_lo = _split(hf + hb)
        d_hi, d_lo = _split(hf - hb)
        fc = f_ref[0, :kb, :]
        fs = f_ref[0, kb:, :]
        hc_ref[o] = (_dot(fc, a_hi) + _dot(fc, a_lo)) * scale
        hs_ref[o] = (_dot(fs, d_hi) + _dot(fs, d_lo)) * scale


def _hy_spectrum(filt, fwd, L, kb):
    nk = L // kb
    return pl.pallas_call(
        functools.partial(_hy_spectrum_kernel, L=L, kb=kb),
        grid=(nk,),
        in_specs=[pl.BlockSpec((1, 2 * kb, L), lambda i: (i, 0, 0)),
                  pl.BlockSpec((L, 4 * HY_CH), lambda i: (0, 0))],
        out_specs=[pl.BlockSpec((2, kb, HY_CH), lambda i: (0, i, 0)),
                   pl.BlockSpec((2, kb, HY_CH), lambda i: (0, i, 0))],
        out_shape=[jax.ShapeDtypeStruct((2, L, HY_CH), F32)] * 2,
        compiler_params=_cparams(("arbitrary",)),
        name="hyena_spectrum",
    )(fwd, filt)


def _hy_conv_kernel(ze_ref, zo_ref, xe_ref, xo_ref, ge_ref, go_ref, cwz_ref, cbz_ref, cwx_ref, cbx_ref, skip_ref,
                    f_ref, g_ref, har_ref, hai_ref, hbr_ref, hbi_ref, cw_ref, sw_ref, oe_ref, oo_ref,
                    *, kb, nk, first, last):
    cb = ze_ref.shape[1]

    def conv3(e_ref, o_ref, w_ref, b_ref):
        xe = e_ref[...]
        xo = o_ref[...]
        w0, w1, w2 = w_ref[0:1, :], w_ref[1:2, :], w_ref[2:3, :]
        ye = _shift_rows(xo, -1) * w0 + xe * w1 + xo * w2 + b_ref[...]
        yo = xe * w0 + xo * w1 + _shift_rows(xe, 1) * w2 + b_ref[...]
        return ye, yo

    if first:
        ze, zo = conv3(ze_ref, zo_ref, cwz_ref, cbz_ref)
    else:
        ze, zo = ze_ref[...], zo_ref[...]
    zb = jnp.concatenate([ze, zo], axis=1).astype(BF16)
    acc = None
    for k in range(nk):
        rows = slice(k * kb, (k + 1) * kb)
        xc = _dot(f_ref[k, :kb, :], zb)
        xs = _dot(f_ref[k, kb:, :], zb)
        ec, oc = xc[:, :cb], xc[:, cb:]
        es, os_ = xs[:, :cb], xs[:, cb:]
        cw = cw_ref[rows, :]
        sw = sw_ref[rows, :]
        p_re = cw * oc - sw * os_
        p_im = -(cw * os_ + sw * oc)
        xa_re, xa_im = ec + p_re, p_im - es
        xb_re, xb_im = ec - p_re, es + p_im
        har, hai = har_ref[0, rows, :], hai_ref[0, rows, :]
        hbr, hbi = hbr_ref[0, rows, :], hbi_ref[0, rows, :]
        ya_re = xa_re * har - xa_im * hai
        ya_im = xa_re * hai + xa_im * har
        yb_re = xb_re * hbr - xb_im * hbi
        yb_im = xb_re * hbi + xb_im * hbr
        ze_re, ze_im = ya_re + yb_re, ya_im - yb_im
        d_re, d_im = ya_re - yb_re, ya_im + yb_im
        zo_re = d_re * cw - d_im * sw
        zo_im = d_re * sw + d_im * cw
        spec = jnp.concatenate([jnp.concatenate([ze_re, zo_re], axis=1),
                                jnp.concatenate([-ze_im, -zo_im], axis=1)], axis=0).astype(BF16)
        part = _dot(g_ref[k], spec)
        acc = part if acc is None else acc + part
    xe, xo = conv3(xe_ref, xo_ref, cwx_ref, cbx_ref)
    re = xe * (acc[:, :cb] + ze * skip_ref[...])
    ro = xo * (acc[:, cb:] + zo * skip_ref[...])
    if last:
        re = re * _silu(ge_ref[...])
        ro = ro * _silu(go_ref[...])
    oe_ref[...] = re.astype(oe_ref.dtype)
    oo_ref[...] = ro.astype(oo_ref.dtype)


def _hy_half_tables(L, kb, cb):
    M = L // 2
    nk = M // kb
    k = np.arange(M, dtype=np.int64)[:, None]
    s = np.arange(M, dtype=np.int64)[None, :]
    ang = ((2 * k + 1) * s % (2 * L)).astype(np.float64) * (2.0 * math.pi / (2 * L))
    c2, s2 = np.cos(ang).astype(np.float32), np.sin(ang).astype(np.float32)
    fwd = np.concatenate([c2.reshape(nk, kb, M), s2.reshape(nk, kb, M)], axis=1)
    inv = np.concatenate([c2.T.reshape(M, nk, kb), s2.T.reshape(M, nk, kb)], axis=2)
    inv = np.ascontiguousarray(inv.transpose(1, 0, 2))
    w = (2 * k + 1).astype(np.float64) * (2.0 * math.pi / (4 * L))
    cw = np.ascontiguousarray(np.broadcast_to(np.cos(w), (M, cb))).astype(np.float32)
    sw = np.ascontiguousarray(np.broadcast_to(np.sin(w), (M, cb))).astype(np.float32)
    return (jnp.asarray(fwd).astype(BF16), jnp.asarray(inv).astype(BF16), jnp.asarray(cw), jnp.asarray(sw))


def _hy_conv(z_even, z_odd, x_even, x_odd, g_even, g_odd, conv_w, conv_b, skip, order, tables, spectra,
             B, L, first, last):
    M = L // 2
    cb = HY_CH_BLOCK
    ncb = HY_CH // cb
    fwd, inv, cw, sw = tables
    nk, _, kb2 = inv.shape
    kb = kb2 // 2
    once = dict(pipeline_mode=pl.Buffered(1))
    data = lambda op: pl.BlockSpec((M, cb), lambda b, c: (b, op[1] + c))
    cwspec = lambda off: pl.BlockSpec((3, cb), lambda b, c: (0, off + c))
    cbspec = lambda off: pl.BlockSpec((1, cb), lambda b, c: (0, off + c))
    hspec = pl.BlockSpec((1, M, cb), lambda b, c: (order, 0, c))
    operands = [z_even, z_odd, x_even, x_odd, g_even, g_odd]
    in_specs = ([data(op) for op in operands]
                + [cwspec(0), cbspec(0), cwspec((1 + order) * ncb), cbspec((1 + order) * ncb),
                   pl.BlockSpec((1, cb), lambda b, c: (0, c)),
                   pl.BlockSpec(fwd.shape, lambda b, c: (0, 0, 0), **once),
                   pl.BlockSpec(inv.shape, lambda b, c: (0, 0, 0), **once),
                   hspec, hspec, hspec, hspec,
                   pl.BlockSpec((M, cb), lambda b, c: (0, 0), **once),
                   pl.BlockSpec((M, cb), lambda b, c: (0, 0), **once)])
    out_dtype = MIX_DTYPE if last else F32
    out = pl.BlockSpec((M, cb), lambda b, c: (b, c))
    return pl.pallas_call(
        functools.partial(_hy_conv_kernel, kb=kb, nk=nk, first=first, last=last),
        grid=(B, ncb),
        in_specs=in_specs,
        out_specs=[out, out],
        out_shape=[jax.ShapeDtypeStruct((B * M, HY_CH), out_dtype)] * 2,
        compiler_params=_cparams(("arbitrary", "arbitrary")),
        name="hyena_conv",
    )(*[op[0] for op in operands], conv_w, conv_b, conv_w, conv_b, skip[order:order + 1],
      fwd, inv, *spectra, cw, sw)


OD_GQ, OD_GG, OD_MG, OD_GK, OD_GV, OD_MQ, OD_MKV, OD_MPE = 0, 4, 8, 12, 14, 16, 18, 19
OD_WIDTH = 20 * LANES


def _rope_swap(x, quarter):
    n = x.shape[1]
    lane = lax.broadcasted_iota(jnp.int32, x.shape, 1)
    first = (lane & (2 * quarter - 1)) < quarter
    return jnp.where(first, pltpu.roll(x, n - quarter, 1), pltpu.roll(x, quarter, 1))


def _odd_in_kernel(x_ref, sh_ref, sc_ref, g_ref, w_ref, qg_ref, kg_ref, mqg_ref, mqup_ref, mkvg_ref, wk_ref,
                   *rest, rope):
    if rope:
        cg_ref, sg_ref, cm_ref, sm_ref = rest[:4]
        rest = rest[4:]
    q_out, qc_out, k16_out, v16_out, kc16_out, gate_out, k_out, v_out, ckv_out, kpe_out = rest
    D = HEAD_DIM
    hb = (_rms(x_ref[...], g_ref[...]) * (1.0 + sc_ref[0]) + sh_ref[0]).astype(BF16)

    def proj(col, width):
        return _dot(hb, w_ref[:, col * D:(col + width) * D])

    def rot(x, c, s, quarter):
        return x * c + _rope_swap(x, quarter) * s if rope else x

    gscale = (D ** -0.5) * LOG2E
    zq = proj(OD_GQ, GQA_HEADS)
    for h in range(GQA_HEADS):
        q = rot(_rms(zq[:, h * D:(h + 1) * D], qg_ref[...]), cg_ref[...] if rope else None,
                sg_ref[...] if rope else None, D // 4)
        q_out[:, h * D:(h + 1) * D] = (q * gscale).astype(BF16)
    zk = proj(OD_GK, GQA_KV_HEADS)
    zv = proj(OD_GV, GQA_KV_HEADS)
    v_out[...] = zv
    v16_out[...] = zv.astype(BF16)
    for h in range(GQA_KV_HEADS):
        kn = _rms(zk[:, h * D:(h + 1) * D], kg_ref[...])
        k_out[:, h * D:(h + 1) * D] = kn
        k16_out[:, h * D:(h + 1) * D] = rot(kn, cg_ref[...] if rope else None,
                                            sg_ref[...] if rope else None, D // 4).astype(BF16)
    gate_out[...] = proj(OD_GG, GQA_HEADS + MLA_HEADS)

    mq = _rms(proj(OD_MQ, 2), mqg_ref[...]).astype(BF16)
    qm = _dot(mq, mqup_ref[...])
    mscale = ((D + MLA_ROPE_DIM) ** -0.5) * LOG2E
    qpe = qm[:, MLA_HEADS * D:]
    if rope:
        qpe = rot(qpe, cm_ref[...], sm_ref[...], MLA_ROPE_DIM // 4)
    for h in range(MLA_HEADS):
        qn = qm[:, h * D:(h + 1) * D].astype(BF16)
        qc_out[:, 2 * h * D:(2 * h + 1) * D] = (_dot_nt(qn, wk_ref[h]) * mscale).astype(BF16)
        qc_out[:, (2 * h + 1) * D:(2 * h + 2) * D] = (qpe[:, h * D:(h + 1) * D] * mscale).astype(BF16)
    ckv = _rms(proj(OD_MKV, 1), mkvg_ref[...])
    ckv_out[...] = ckv
    kpe = proj(OD_MPE, 1)
    kpe_out[...] = kpe
    if rope:
        kpe = rot(kpe, cm_ref[:, :D], sm_ref[:, :D], MLA_ROPE_DIM // 4)
    kc16_out[:, :D] = ckv.astype(BF16)
    kc16_out[:, D:] = kpe.astype(BF16)


def _rope_tables(L, R):
    rows = np.arange(L) // GRID_W
    cols = np.arange(L) % GRID_W
    quarter = R // 4
    inv = ROPE_THETA ** (-np.arange(quarter, dtype=np.float32) * 2.0 / (R // 2))
    a_r = rows[:, None].astype(np.float32) * inv[None, :]
    a_c = cols[:, None].astype(np.float32) * inv[None, :]
    cos = np.concatenate([np.cos(a_r), np.cos(a_r), np.cos(a_c), np.cos(a_c)], axis=1)
    sin = np.concatenate([-np.sin(a_r), np.sin(a_r), -np.sin(a_c), np.sin(a_c)], axis=1)
    return cos.astype(np.float32), sin.astype(np.float32)


def _odd_in(x2, shift, scale, pre_g, w_main, q_g, k_g, mq_g, mq_up, mkv_g, wk, L, tiles_per_mod, rope):
    T, dm = x2.shape
    tm = TOKEN_TILE
    D = HEAD_DIM
    if tiles_per_mod:
        mod_map = lambda i: (i // tiles_per_mod, 0, 0)
    else:
        mod_map = lambda i: (0, 0, 0)
    full = lambda shape: pl.BlockSpec(shape, lambda i: tuple(0 for _ in shape))
    in_specs = [pl.BlockSpec((tm, dm), lambda i: (i, 0)),
                pl.BlockSpec((1, 1, dm), mod_map), pl.BlockSpec((1, 1, dm), mod_map),
                full((1, dm)), full(w_main.shape),
                full((1, D)), full((1, D)), full((1, 2 * D)), full(mq_up.shape), full((1, D)), full(wk.shape)]
    args = [x2, shift, scale, pre_g, w_main, q_g, k_g, mq_g, mq_up, mkv_g, wk]
    if rope:
        cg, sg = _rope_tables(L, D)
        cm, sm = _rope_tables(L, MLA_ROPE_DIM)
        widen = lambda t: np.tile(np.concatenate([t, np.zeros_like(t)], axis=1), (1, MLA_HEADS))
        per = L // tm
        pos = lambda w: pl.BlockSpec((tm, w), lambda i: (i % per, 0))
        in_specs += [pos(D), pos(D), pos(MLA_HEADS * D), pos(MLA_HEADS * D)]
        args += [jnp.asarray(cg), jnp.asarray(sg), jnp.asarray(widen(cm)), jnp.asarray(widen(sm))]
    tile = lambda w: pl.BlockSpec((tm, w), lambda i: (i, 0))
    widths = [(4 * D, BF16), (8 * D, BF16), (2 * D, BF16), (2 * D, BF16), (2 * D, BF16), (8 * D, F32),
              (2 * D, F32), (2 * D, F32), (D, F32), (D, F32)]
    return pl.pallas_call(
        functools.partial(_odd_in_kernel, rope=rope),
        grid=(T // tm,),
        in_specs=in_specs,
        out_specs=[tile(w) for w, _ in widths],
        out_shape=[jax.ShapeDtypeStruct((T, w), dt) for w, dt in widths],
        compiler_params=_cparams(("arbitrary",)),
        name="odd_in_proj",
    )(*args)


def _softmax_pv(score_blocks, value_blocks):
    m = score_blocks[0].max(axis=-1, keepdims=True)
    for s in score_blocks[1:]:
        m = jnp.maximum(m, s.max(axis=-1, keepdims=True))
    acc = None
    den = None
    for s, v in zip(score_blocks, value_blocks):
        p = jnp.exp2(s - m)
        d = p.sum(axis=-1, keepdims=True)
        a = _dot(p.astype(BF16), v)
        acc = a if acc is None else acc + a
        den = d if den is None else den + d
    return acc / den


def _gqa_kernel(q_ref, k_ref, v_ref, gate_ref, *rest, cached):
    if cached:
        ck_ref, cv_ref, o_ref = rest
    else:
        (o_ref,) = rest
    D = HEAD_DIM
    group = GQA_HEADS // GQA_KV_HEADS
    q = q_ref[...]
    keys = [k_ref[...]] + ([ck_ref[0, 0]] if cached else [])
    values = [v_ref[...]] + ([cv_ref[0, 0]] if cached else [])
    outs = []
    for g in range(group):
        qh = q[:, g * D:(g + 1) * D]
        outs.append(_softmax_pv([_dot_nt(qh, kk) for kk in keys], values))
    o_ref[...] = (jnp.concatenate(outs, axis=1) * _silu(gate_ref[...])).astype(o_ref.dtype)


def _gqa(qg, kg, vg, gates, B, L, tq, cache_k=None, cache_v=None, layer_j=0):
    T = qg.shape[0]
    D = HEAD_DIM
    group = GQA_HEADS // GQA_KV_HEADS
    nq = L // tq
    cached = cache_k is not None
    in_specs = [pl.BlockSpec((tq, group * D), lambda b, h, i: (b * nq + i, h)),
                pl.BlockSpec((L, D), lambda b, h, i: (b, h)),
                pl.BlockSpec((L, D), lambda b, h, i: (b, h)),
                pl.BlockSpec((tq, group * D), lambda b, h, i: (b * nq + i, h))]
    args = [qg, kg, vg, gates]
    if cached:
        P = cache_k.shape[2]
        cspec = pl.BlockSpec((1, 1, P, D), lambda b, h, i: (b, layer_j, 0, h))
        in_specs += [cspec, cspec]
        args += [cache_k, cache_v]
    return pl.pallas_call(
        functools.partial(_gqa_kernel, cached=cached),
        grid=(B, GQA_KV_HEADS, nq),
        in_specs=in_specs,
        out_specs=pl.BlockSpec((tq, group * D), lambda b, h, i: (b * nq + i, h)),
        out_shape=jax.ShapeDtypeStruct((T, GQA_HEADS * D), MIX_DTYPE),
        compiler_params=_cparams(("arbitrary", "arbitrary", "arbitrary")),
        name="gqa_attention",
    )(*args)


def _mla_kernel(qc_ref, kc_ref, gate_ref, wv_ref, *rest, cached):
    if cached:
        ckc_ref, o_ref = rest
    else:
        (o_ref,) = rest
    D = HEAD_DIM
    H = MLA_HEADS
    qc = qc_ref[...]
    tq = qc.shape[0]
    keys = [kc_ref[...]] + ([ckc_ref[0, 0]] if cached else [])
    per = H // MLA_HEAD_STACKS
    outs = []
    for s in range(MLA_HEAD_STACKS):
        heads = range(s * per, (s + 1) * per)
        qs = jnp.concatenate([qc[:, 2 * h * D:(2 * h + 2) * D] for h in heads], axis=0)
        o = _softmax_pv([_dot_nt(qs, kk) for kk in keys], [kk[:, :D] for kk in keys]).astype(BF16)
        for i, h in enumerate(heads):
            outs.append(_dot(o[i * tq:(i + 1) * tq], wv_ref[h]))
    o_ref[...] = (jnp.concatenate(outs, axis=1) * _silu(gate_ref[...])).astype(o_ref.dtype)


def _mla(qc, kc, gates, wv, B, L, tq, cache_kc=None, layer_j=0):
    T = qc.shape[0]
    D = HEAD_DIM
    H = MLA_HEADS
    nq = L // tq
    cached = cache_kc is not None
    in_specs = [pl.BlockSpec((tq, 2 * H * D), lambda b, i: (b * nq + i, 0)),
                pl.BlockSpec((L, 2 * D), lambda b, i: (b, 0)),
                pl.BlockSpec((tq, H * D), lambda b, i: (b * nq + i, 1)),
                pl.BlockSpec(wv.shape, lambda b, i: (0, 0, 0))]
    args = [qc, kc, gates, wv]
    if cached:
        P = cache_kc.shape[2]
        in_specs += [pl.BlockSpec((1, 1, P, 2 * D), lambda b, i: (b, layer_j, 0, 0))]
        args += [cache_kc]
    return pl.pallas_call(
        functools.partial(_mla_kernel, cached=cached),
        grid=(B, nq),
        in_specs=in_specs,
        out_specs=pl.BlockSpec((tq, H * D), lambda b, i: (b * nq + i, 0)),
        out_shape=jax.ShapeDtypeStruct((T, H * D), MIX_DTYPE),
        compiler_params=_cparams(("arbitrary", "arbitrary")),
        name="mla_attention",
    )(*args)


def _even_weights(even_in_w, gdn_conv_w, gdn_a_log, gdn_dt_bias, hyena_conv_w, hyena_conv_b, j):
    w = even_in_w[j]
    qkv_w = 3 * GDN_HEADS * LANES
    n_ab = 4 * GDN_HEADS
    gate_end = qkv_w + n_ab + GDN_HEADS * LANES
    w_a = jnp.concatenate([w[:, :qkv_w], w[:, qkv_w + n_ab:gate_end]], axis=1).astype(BF16)
    w_small = _pad_to(w[:, qkv_w:qkv_w + n_ab], (w.shape[0], LANES)).astype(BF16)
    w_b = w[:, gate_end:].astype(BF16)
    n_dir = 2 * GDN_HEADS
    alog_row = _pad_to(gdn_a_log[j].reshape(1, n_dir), (1, LANES))
    dtb_row = _pad_to(gdn_dt_bias[j].reshape(1, n_dir), (1, LANES))
    return w_a, w_small, w_b, alog_row, dtb_row


def _even_layer(x2, B, L, shift, scale, gate, tiles_per_mod, pre_g, post_g, wts, hy, out_w,
                gdn_conv_w, gdn_norm_g, hy_conv_w, hy_conv_b, hy_skip, state_f, state_b, j):
    w_a, w_small, w_b, alog_row, dtb_row = wts
    tables, spectra = hy
    z, zab, zp = _in_proj(x2, shift, scale, pre_g, w_a, w_small, w_b, tiles_per_mod)
    oa, s_f, s_b = _gdn(z, zab, gdn_conv_w, gdn_norm_g, alog_row, dtb_row, B, L, state_f, state_b, j)
    ncb = HY_CH // HY_CH_BLOCK
    n_groups = zp.shape[1] // (2 * HY_CH)
    even = lambda idx: (zp, idx * ncb)
    odd = lambda idx: (zp, (n_groups + idx) * ncb)
    z1e, z1o = _hy_conv(even(0), odd(0), even(1), odd(1), even(3), odd(3), hy_conv_w, hy_conv_b, hy_skip, 0,
                        tables, spectra, B, L, first=True, last=False)
    obe, obo = _hy_conv((z1e, 0), (z1o, 0), even(2), odd(2), even(3), odd(3), hy_conv_w, hy_conv_b, hy_skip, 1,
                        tables, spectra, B, L, first=False, last=True)
    x_new = _out_proj(oa, (obe, obo), x2, gate, post_g, out_w, tiles_per_mod)
    return x_new, s_f, s_b


def _odd_weights(odd_in_w, mla_q_up, mla_kv_up, j):
    w = odd_in_w[j]
    D = HEAD_DIM
    o = np.cumsum([0, 4 * D, 2 * D, 2 * D, 4 * D, 2 * D, D, MLA_ROPE_DIM, 4 * D])
    gq, gk, gv, gg, mq, mkv, mpe, mg = [w[:, o[i]:o[i + 1]] for i in range(8)]
    w_main = jnp.concatenate([gq, gg, mg, gk, gv, mq, mkv, _pad_to(mpe, (w.shape[0], D))], axis=1).astype(BF16)
    up = mla_q_up[j].reshape(-1, MLA_HEADS, D + MLA_ROPE_DIM)
    rope_cols = _pad_to(up[:, :, D:], (up.shape[0], MLA_HEADS, D))
    mq_up = jnp.concatenate([up[:, :, :D].reshape(-1, MLA_HEADS * D),
                             rope_cols.reshape(-1, MLA_HEADS * D)], axis=1).astype(BF16)
    kv = mla_kv_up[j].reshape(-1, MLA_HEADS, 2 * D)
    wk = kv[:, :, :D].transpose(1, 0, 2).astype(BF16)
    wv = kv[:, :, D:].transpose(1, 0, 2).astype(BF16)
    return w_main, mq_up, wk, wv


def _odd_layer(x2, B, L, shift, scale, gate, tiles_per_mod, pre_g, post_g, wts, out_w,
               q_g, k_g, mq_g, mkv_g, caches, j):
    w_main, mq_up, wk, wv = wts
    rope = caches is not None
    qg, qc, k16, v16, kc16, gates, kg, v32, ckv, kpe = _odd_in(
        x2, shift, scale, pre_g, w_main, q_g, k_g, mq_g, mq_up, mkv_g, wk, L, tiles_per_mod, rope)
    if rope:
        ck, cv, ckc = caches
        og = _gqa(qg, k16, v16, gates, B, L, GQA_Q_TILE, ck, cv, j)
        om = _mla(qc, kc16, gates, wv, B, L, MLA_Q_TILE, ckc, j)
    else:
        og = _gqa(qg, k16, v16, gates, B, L, GQA_Q_TILE)
        om = _mla(qc, kc16, gates, wv, B, L, MLA_Q_TILE)
    x_new = _out_proj(og, om, x2, gate, post_g, out_w, tiles_per_mod)
    return x_new, (kg, v32, ckv, kpe)


def kernel(x_prompt, x_sample, state_gdn_fwd, state_gdn_bwd, cache_gqa_k, cache_gqa_v, cache_mla_ckv, cache_mla_kpe, c, c_ctx, mod_w, mod_b, pre_norm_g, post_norm_g, even_in_w, gdn_conv_w, gdn_a_log, gdn_dt_bias, gdn_norm_g, hyena_conv_w, hyena_conv_b, hyena_ffn_w1, hyena_ffn_b1, hyena_ffn_w2, hyena_ffn_b2, hyena_ffn_w3, hyena_sin_freq, hyena_bias, even_out_w, odd_in_w, gqa_q_norm_g, gqa_k_norm_g, mla_q_norm_g, mla_q_up, mla_kv_norm_g, mla_kv_up, odd_out_w):
    Bp, Lp, D = x_prompt.shape
    Bs, Ls, _ = x_sample.shape
    depth = mod_w.shape[0]
    xp = x_prompt.reshape(Bp * Lp, D)
    xs = x_sample.reshape(Bs * Ls, D)

    n_cond = 1 + Bs
    rows = -(-n_cond // 8) * 8
    cond = _pad_to(jnp.concatenate([c_ctx[None, :], c], axis=0), (rows, D))
    mod = _modulation(cond, mod_w, mod_b)

    P = cache_gqa_k.shape[2]
    n_odd = cache_gqa_k.shape[1]
    ck = cache_gqa_k.reshape(Bs, n_odd, P, GQA_KV_HEADS * HEAD_DIM).astype(BF16)
    cv = cache_gqa_v.reshape(Bs, n_odd, P, GQA_KV_HEADS * HEAD_DIM).astype(BF16)
    ckc = jnp.concatenate([cache_mla_ckv, _pad_to(cache_mla_kpe, cache_mla_ckv.shape)], axis=-1).astype(BF16)

    dft = {L: _dft_blocks(L, min(HY_FREQ_BLOCK, L)) for L in (Lp, Ls)}
    half = {L: _hy_half_tables(L, min(HY_HALF_FREQ_BLOCK, L // 2), HY_CH_BLOCK) for L in (Lp, Ls)}
    tpm_s = Ls // TOKEN_TILE

    new_f, new_b, new_gk, new_gv, new_ckv, new_kpe = [], [], [], [], [], []
    for i in range(depth):
        j = i // 2
        m = mod[i]
        sh_p, sc_p, gt_p = [m[0:1, k * D:(k + 1) * D].reshape(1, 1, D) for k in range(3)]
        sh_s, sc_s, gt_s = [m[1:n_cond, k * D:(k + 1) * D].reshape(Bs, 1, D) for k in range(3)]
        pre_g = pre_norm_g[i][None, :]
        post_g = post_norm_g[i][None, :]
        if i % 2 == 0:
            wts = _even_weights(even_in_w, gdn_conv_w, gdn_a_log, gdn_dt_bias, hyena_conv_w, hyena_conv_b, j)
            out_w = even_out_w[j].astype(BF16)
            hy = {}
            for L in (Lp, Ls):
                filt = _hy_filters(L, hyena_ffn_w1[j], hyena_ffn_b1[j], hyena_ffn_w2[j], hyena_ffn_b2[j],
                                   hyena_ffn_w3[j], hyena_sin_freq[j])
                hc, hs = _hy_spectrum(filt, dft[L], L, min(HY_FREQ_BLOCK, L))
                M = L // 2
                spectra = (hc[:, :M], -hs[:, :M], jnp.flip(hc[:, M:], axis=1), -jnp.flip(hs[:, M:], axis=1))
                hy[L] = (half[L], spectra)
            common = (gdn_conv_w[j], gdn_norm_g[j][None, :], hyena_conv_w[j], hyena_conv_b[j][None, :], hyena_bias[j])
            xp, sf, sb = _even_layer(xp, Bp, Lp, sh_p, sc_p, gt_p, 0, pre_g, post_g, wts, hy[Lp], out_w,
                                     *common, None, None, j)
            xs, _, _ = _even_layer(xs, Bs, Ls, sh_s, sc_s, gt_s, tpm_s, pre_g, post_g, wts, hy[Ls], out_w,
                                   *common, state_gdn_fwd, state_gdn_bwd, j)
            new_f.append(sf)
            new_b.append(sb)
        else:
            wts = _odd_weights(odd_in_w, mla_q_up, mla_kv_up, j)
            out_w = odd_out_w[j].astype(BF16)
            norms = (gqa_q_norm_g[j][None, :], gqa_k_norm_g[j][None, :], mla_q_norm_g[j][None, :],
                     mla_kv_norm_g[j][None, :])
            xp, (kg, v32, ckv, kpe) = _odd_layer(xp, Bp, Lp, sh_p, sc_p, gt_p, 0, pre_g, post_g, wts, out_w,
                                                 *norms, None, j)
            xs, _ = _odd_layer(xs, Bs, Ls, sh_s, sc_s, gt_s, tpm_s, pre_g, post_g, wts, out_w,
                               *norms, (ck, cv, ckc), j)
            new_gk.append(kg.reshape(Bp, Lp, GQA_KV_HEADS, HEAD_DIM))
            new_gv.append(v32.reshape(Bp, Lp, GQA_KV_HEADS, HEAD_DIM))
            new_ckv.append(ckv.reshape(Bp, Lp, HEAD_DIM))
            new_kpe.append(kpe[:, :MLA_ROPE_DIM].reshape(Bp, Lp, MLA_ROPE_DIM))
    return (xp.reshape(Bp, Lp, D), xs.reshape(Bs, Ls, D),
            jnp.stack(new_f, axis=1), jnp.stack(new_b, axis=1),
            jnp.stack(new_gk, axis=1), jnp.stack(new_gv, axis=1),
            jnp.stack(new_ckv, axis=1), jnp.stack(new_kpe, axis=1))
```

```python
import functools
import math

import numpy as np
import jax
import jax.numpy as jnp
from jax import lax
from jax.experimental import pallas as pl
from jax.experimental.pallas import tpu as pltpu

F32 = jnp.float32
BF16 = jnp.bfloat16
MIX_DTYPE = BF16

NORM_EPS = 1e-6
ROPE_THETA = 10000.0
GRID_W = 64

GDN_HEADS = 4
GDN_DK = 128
GDN_CHUNK = 64
GDN_CONV_W = 5
GDN_PREP_CHAINS = 64
GDN_ALL_HEADS_MAX_LEN = 256
GDN_HEADS_LONG = 1
HY_CH = 512
HY_BANDS = 16
HY_DECAY_TARGET = 1e-2
HY_SHORT_DECAY_PCT = 0.3
HY_LONG_DECAY_PCT = 1.5
GQA_HEADS = 4
GQA_KV_HEADS = 2
HEAD_DIM = 128
MLA_HEADS = 4
MLA_ROPE_DIM = 64

LOG2E = math.log2(math.e)

LANES = 128
VMEM_LIMIT = 56 * 1024 * 1024

TOKEN_TILE = 256
HY_FREQ_BLOCK = 512
HY_HALF_FREQ_BLOCK = 256
HY_CH_BLOCK = 256
GQA_Q_TILE = 256
MLA_Q_TILE = 128
MLA_HEAD_STACKS = 2


def _cparams(sem):
    return pltpu.CompilerParams(dimension_semantics=sem, vmem_limit_bytes=VMEM_LIMIT)


def _dot(a, b):
    return jnp.dot(a, b, preferred_element_type=F32)


def _dot_nt(a, b):
    return lax.dot_general(a, b, (((1,), (1,)), ((), ())), preferred_element_type=F32)


def _dot_tn(a, b):
    return lax.dot_general(a, b, (((0,), (0,)), ((), ())), preferred_element_type=F32)


def _split(a):
    hi = a.astype(BF16)
    lo = (a - hi.astype(F32)).astype(BF16)
    return hi, lo


def _dot3(a, b):
    ah, al = _split(a)
    bh, bl = _split(b)
    return _dot(ah, bh) + (_dot(ah, bl) + _dot(al, bh))


def _bdot(a, b):
    return jnp.einsum('gij,gjk->gik', a, b, preferred_element_type=F32)


def _bdot16(a, b):
    return _bdot(a.astype(BF16), b.astype(BF16))


def _silu(x):
    return x * (1.0 / (1.0 + jnp.exp(-x)))


def _sigmoid(x):
    return 1.0 / (1.0 + jnp.exp(-x))


def _softplus(x):
    return jnp.maximum(x, 0.0) + jnp.log(1.0 + jnp.exp(-jnp.abs(x)))


def _rms(x, g):
    return x * lax.rsqrt(jnp.mean(x * x, axis=-1, keepdims=True) + NORM_EPS) * g


def _shift_rows(x, s):
    L = x.shape[0]
    if s == 0:
        return x
    rolled = pltpu.roll(x, (-s) % L, 0)
    row = lax.broadcasted_iota(jnp.int32, x.shape, 0)
    valid = (row + s >= 0) & (row + s < L)
    return jnp.where(valid, rolled, 0.0)


def _mod_kernel(c_ref, w_ref, b_ref, o_ref):
    c = _silu(c_ref[...])
    o_ref[0] = _dot3(c, w_ref[0]) + b_ref[0]


def _modulation(cond, mod_w, mod_b):
    depth, d, d3 = mod_w.shape
    r = cond.shape[0]
    nb = d3 // d
    return pl.pallas_call(
        _mod_kernel,
        grid=(depth, nb),
        in_specs=[pl.BlockSpec((r, d), lambda i, n: (0, 0)),
                  pl.BlockSpec((1, d, d), lambda i, n: (i, 0, n)),
                  pl.BlockSpec((1, 1, d), lambda i, n: (i, 0, n))],
        out_specs=pl.BlockSpec((1, r, d), lambda i, n: (i, 0, n)),
        out_shape=jax.ShapeDtypeStruct((depth, r, d3), F32),
        compiler_params=_cparams(("arbitrary", "arbitrary")),
        name="adaln_modulation",
    )(cond, mod_w, mod_b.reshape(depth, 1, d3))


def _in_proj_kernel(x_ref, sh_ref, sc_ref, g_ref, wa_ref, ws_ref, wb_ref, oa_ref, os_ref, ob_ref, h_s):
    x = x_ref[...]
    h = _rms(x, g_ref[...]) * (1.0 + sc_ref[0]) + sh_ref[0]
    n_lane_blocks = h.shape[1] // LANES
    for jb in range(n_lane_blocks):
        h_s[jb] = h[:, jb * LANES:(jb + 1) * LANES]
    hb = h.astype(BF16)
    tm = hb.shape[0]
    step = 512
    na = wa_ref.shape[1]
    for n0 in range(0, na, step):
        oa_ref[:, n0:n0 + step] = _dot(hb, wa_ref[:, n0:n0 + step])
    os_ref[...] = _dot(hb, ws_ref[...])
    nb = wb_ref.shape[1]
    for parity in (0, 1):
        hp = jnp.concatenate([h_s[jb, pl.ds(parity, tm // 2, stride=2), :] for jb in range(n_lane_blocks)],
                             axis=1).astype(BF16)
        for n0 in range(0, nb, step):
            ob_ref[:, parity * nb + n0:parity * nb + n0 + step] = _dot(hp, wb_ref[:, n0:n0 + step])


def _in_proj(x2, shift, scale, pre_g, w_a, w_small, w_b, tiles_per_mod):
    T, D = x2.shape
    na, nb = w_a.shape[1], w_b.shape[1]
    tm = TOKEN_TILE
    if tiles_per_mod:
        mod_map = lambda i: (i // tiles_per_mod, 0, 0)
    else:
        mod_map = lambda i: (0, 0, 0)
    const = lambda shape: pl.BlockSpec(shape, lambda i: (0, 0))
    return pl.pallas_call(
        _in_proj_kernel,
        grid=(T // tm,),
        in_specs=[pl.BlockSpec((tm, D), lambda i: (i, 0)),
                  pl.BlockSpec((1, 1, D), mod_map),
                  pl.BlockSpec((1, 1, D), mod_map),
                  const((1, D)), const((D, na)), const((D, LANES)), const((D, nb))],
        out_specs=[pl.BlockSpec((tm, na), lambda i: (i, 0)),
                   pl.BlockSpec((tm, LANES), lambda i: (i, 0)),
                   pl.BlockSpec((tm // 2, 2 * nb), lambda i: (i, 0))],
        out_shape=[jax.ShapeDtypeStruct((T, na), F32), jax.ShapeDtypeStruct((T, LANES), F32),
                   jax.ShapeDtypeStruct((T // 2, 2 * nb), F32)],
        scratch_shapes=[pltpu.VMEM((D // LANES, tm, LANES), F32)],
        compiler_params=_cparams(("arbitrary",)),
        name="in_proj",
    )(x2, shift, scale, pre_g, w_a, w_small, w_b)


def _row_spread(n, parity):
    r = lax.broadcasted_iota(jnp.int32, (n, n // 2), 0)
    c = lax.broadcasted_iota(jnp.int32, (n, n // 2), 1)
    return (r == 2 * c + parity).astype(BF16)


def _out_proj_kernel(a_ref, *rest, half, split):
    if split:
        be_ref, bo_ref, x_ref, gt_ref, g_ref, w_ref, o_ref = rest
        tm = a_ref.shape[0]
        b = (_dot(_row_spread(tm, 0), be_ref[...]) + _dot(_row_spread(tm, 1), bo_ref[...])).astype(BF16)
    else:
        b_ref, x_ref, gt_ref, g_ref, w_ref, o_ref = rest
        b = b_ref[...].astype(BF16)
    y = _dot(a_ref[...].astype(BF16), w_ref[:half, :]) + _dot(b, w_ref[half:, :])
    o_ref[...] = x_ref[...] + gt_ref[0] * _rms(y, g_ref[...])


def _out_proj(a, b, x2, gate, post_g, w, tiles_per_mod):
    T, D = x2.shape
    half = a.shape[1]
    tm = TOKEN_TILE
    split = isinstance(b, tuple)
    if tiles_per_mod:
        mod_map = lambda i: (i // tiles_per_mod, 0, 0)
    else:
        mod_map = lambda i: (0, 0, 0)
    if split:
        b_specs = [pl.BlockSpec((tm // 2, half), lambda i: (i, 0))] * 2
        b_args = list(b)
    else:
        b_specs = [pl.BlockSpec((tm, half), lambda i: (i, 0))]
        b_args = [b]
    return pl.pallas_call(
        functools.partial(_out_proj_kernel, half=half, split=split),
        grid=(T // tm,),
        in_specs=[pl.BlockSpec((tm, half), lambda i: (i, 0))] + b_specs + [
                  pl.BlockSpec((tm, D), lambda i: (i, 0)),
                  pl.BlockSpec((1, 1, D), mod_map),
                  pl.BlockSpec((1, D), lambda i: (0, 0)),
                  pl.BlockSpec((2 * half, D), lambda i: (0, 0))],
        out_specs=pl.BlockSpec((tm, D), lambda i: (i, 0)),
        out_shape=jax.ShapeDtypeStruct((T, D), F32),
        compiler_params=_cparams(("arbitrary",)),
        name="out_proj",
    )(a, *b_args, x2, gate, post_g, w)


def _gdn_kernel(zq_ref, zk_ref, zv_ref, zg_ref, zab_ref, cwq_ref, cwk_ref, cwv_ref, ng_ref,
                alog_ref, dtb_ref, *rest, L, HB, has_state):
    if has_state:
        s0f_ref, s0b_ref = rest[:2]
        rest = rest[2:]
    (o_ref, sf_out_ref, sb_out_ref, q_s, k_s, v_s, g_s, b_s, st_s,
     oc_s, qe_s, sc_s, sm_s, gl_s, pad_s) = rest
    C = GDN_CHUNK
    n_chunks = L // C
    U = min(GDN_PREP_CHAINS // (2 * HB), n_chunks)
    HU = HB * U
    G = 2 * HU
    head0 = pl.program_id(1) * HB

    pad = 8
    pad_s[0:pad, :] = jnp.zeros((pad, LANES), F32)
    pad_s[pad + L:2 * pad + L, :] = jnp.zeros((pad, LANES), F32)

    def conv_silu(x, w_ref, lanes):
        half = GDN_CONV_W // 2
        pad_s[pad:pad + L, :] = x
        acc = x * w_ref[half:half + 1, lanes]
        for i in range(GDN_CONV_W):
            if i != half:
                start = pad + i - half
                acc = acc + pad_s[start:start + L, :] * w_ref[i:i + 1, lanes]
        return _silu(acc)

    def l2n(x):
        return x * lax.rsqrt(jnp.sum(x * x, axis=-1, keepdims=True) + NORM_EPS)

    for hh in range(HB):
        lanes = slice(hh * LANES, (hh + 1) * LANES)
        q_s[hh] = l2n(conv_silu(zq_ref[:, lanes], cwq_ref, lanes)) * (GDN_DK ** -0.5)
        k_s[hh] = l2n(conv_silu(zk_ref[:, lanes], cwk_ref, lanes))
        v_s[hh] = conv_silu(zv_ref[:, lanes], cwv_ref, lanes)
        if has_state:
            st_s[2 * hh] = s0f_ref[0, 0, hh]
            st_s[2 * hh + 1] = s0b_ref[0, 0, hh]
    if not has_state:
        st_s[...] = jnp.zeros_like(st_s)
    zab = zab_ref[...]
    g_s[...] = -jnp.exp(alog_ref[...]) * _softplus(zab + dtb_ref[...])
    b_s[...] = _sigmoid(zab)

    def iota(shape, axis):
        return lax.broadcasted_iota(jnp.int32, shape, axis)

    def direction(shape):
        return iota(shape, 0) & 1

    chain_shift = (2 * U).bit_length() - 1
    assert 2 * U == 1 << chain_shift

    def chain_head(shape):
        return head0 + (iota(shape, 0) >> chain_shift)

    sq = (G, C, C)
    row = iota(sq, 1)
    col = iota(sq, 2)
    signed = (row - col) * (1 - 2 * direction(sq))
    incl = signed >= 0
    strict = signed > 0
    eye = (row == col).astype(F32)
    same = [(row >> s) == (col >> s) for s in (3, 4, 5)]
    off_blocks = [same[1] & jnp.logical_not(same[0]), same[2] & jnp.logical_not(same[1]),
                  jnp.logical_not(same[2])]
    wide = (G, C, LANES)
    sel_lane = direction(wide) * GDN_HEADS + chain_head(wide)
    mask_g = iota(wide, 2) == sel_lane
    mask_b = iota(wide, 2) == sel_lane + 2 * GDN_HEADS
    tall = (G, LANES, C)
    mask_t = iota(tall, 1) == direction(tall) * GDN_HEADS + chain_head(tall)
    colv = (G, C, 1)
    mask_last = iota(colv, 1) == (1 - direction(colv)) * (C - 1)
    r2 = lax.broadcasted_iota(jnp.int32, (C, C), 0)
    c2 = lax.broadcasted_iota(jnp.int32, (C, C), 1)
    tri = jnp.concatenate([(r2 >= c2).astype(BF16), (r2 <= c2).astype(BF16)], axis=0)

    def both_dirs(x):
        return jnp.broadcast_to(x[:, None], (HU, 2) + x.shape[1:]).reshape((G,) + x.shape[1:])

    def all_heads(x):
        return jnp.broadcast_to(x[None], (HB,) + x.shape).reshape((HB * x.shape[0],) + x.shape[1:])

    def prep_group(c, carry):
        rows = pl.ds(pl.multiple_of(c * (U * C), U * C), U * C)
        q = q_s[:, rows, :].reshape(HU, C, LANES)
        k = k_s[:, rows, :].reshape(HU, C, LANES)
        v = v_s[:, rows, :].reshape(HU, C, LANES)
        ball = all_heads(b_s[rows, :].reshape(U, C, LANES))
        g_hi, g_lo = _split(g_s[rows, :])
        g_hi = g_hi.reshape(U, C, LANES)
        g_lo = g_lo.reshape(U, C, LANES)
        gcum = jnp.stack([_dot(tri, g_hi[u]) + _dot(tri, g_lo[u]) for u in range(U)])
        gcum = gcum.reshape(2 * U, C, LANES)
        gcum_t = all_heads(jnp.stack([gcum[g].T for g in range(2 * U)]))
        gcum = all_heads(gcum)
        gc = jnp.sum(jnp.where(mask_g, gcum, 0.0), axis=2, keepdims=True)
        gr = jnp.sum(jnp.where(mask_t, gcum_t, 0.0), axis=1, keepdims=True)
        beta = jnp.sum(jnp.where(mask_b, both_dirs(ball), 0.0), axis=2, keepdims=True)
        kbf = k.astype(BF16)
        kq = jnp.einsum('uik,ujk->uij', jnp.concatenate([k, q], axis=1).astype(BF16), kbf,
                        preferred_element_type=F32)
        kk = both_dirs(kq[:, :C])
        qk = both_dirs(kq[:, C:])
        decay = jnp.where(incl, jnp.exp(jnp.where(incl, gc - gr, 0.0)), 0.0)
        lmat = jnp.where(strict, beta * kk * decay, 0.0)
        p = jnp.where(same[0], -lmat, 0.0)
        tmat = eye + p
        for _ in range(2):
            p = _bdot16(p, p)
            tmat = tmat + _bdot16(tmat, p)
        for off in off_blocks:
            t16 = tmat.astype(BF16)
            tmat = tmat - _bdot(_bdot(t16, jnp.where(off, lmat, 0.0).astype(BF16)).astype(BF16), t16)
        eg = jnp.exp(gc)
        k2 = both_dirs(k)
        kb = k2 * beta
        sol = _bdot16(tmat, jnp.concatenate([both_dirs(v) * beta, kb * eg], axis=2))
        glast = jnp.sum(jnp.where(mask_last, gc, 0.0), axis=1, keepdims=True)
        kdec = k2 * jnp.exp(glast - gc)
        kdec_t = jnp.stack([kdec[g].T for g in range(G)]).astype(BF16)
        sol16 = sol.astype(BF16)
        r_in = _bdot((qk * decay).astype(BF16), sol16)
        r_kd = _bdot(kdec_t, sol16)
        def put(ref, val, size):
            n = 2 * U * size
            ref[:, pl.ds(pl.multiple_of(c * n, n), n), :] = val.reshape(HB, n, val.shape[-1])

        put(oc_s, r_in[:, :, :LANES], C)
        put(qe_s, (both_dirs(q) * eg - r_in[:, :, LANES:]).astype(BF16), C)
        put(sc_s, r_kd[:, :, :LANES], LANES)
        put(sm_s, (-r_kd[:, :, LANES:]).astype(BF16), LANES)
        put(gl_s, jnp.broadcast_to(jnp.exp(glast), (G, 8, LANES)), 8)
        return carry

    lax.fori_loop(0, n_chunks // U, prep_group, 0)

    def scan_step(c, carry):
        slot_f = c * 2
        slot_b = (n_chunks - 1 - c) * 2 + 1

        def ld(ref, size):
            parts = []
            for hh in range(HB):
                parts.append(ref[hh, pl.ds(pl.multiple_of(slot_f * size, size), size), :])
                parts.append(ref[hh, pl.ds(pl.multiple_of(slot_b * size, size), size), :])
            return jnp.stack(parts)

        s = st_s[...]
        sb16 = s.astype(BF16)
        o = _bdot(ld(qe_s, C), sb16) + ld(oc_s, C)
        st_s[...] = s * ld(gl_s, 8)[:, 0:1, :] + (_bdot(ld(sm_s, LANES), sb16) + ld(sc_s, LANES))
        for hh in range(HB):
            oc_s[hh, pl.ds(pl.multiple_of(slot_f * C, C), C), :] = o[2 * hh]
            oc_s[hh, pl.ds(pl.multiple_of(slot_b * C, C), C), :] = o[2 * hh + 1]
        return carry

    lax.fori_loop(0, n_chunks, scan_step, 0)

    FIN = 4

    def finish(i, carry):
        rows = pl.ds(pl.multiple_of(i * (FIN * C), FIN * C), FIN * C)
        outs = []
        for hh in range(HB):
            both = oc_s[hh, pl.ds(pl.multiple_of(i * (FIN * 2 * C), FIN * 2 * C), FIN * 2 * C), :]
            both = both.reshape(FIN, 2, C, LANES)
            outs.append(_rms((both[:, 0] + both[:, 1]).reshape(FIN * C, LANES), ng_ref[...]))
        o = jnp.concatenate(outs, axis=1) * _silu(zg_ref[rows, :])
        o_ref[rows, :] = o.astype(o_ref.dtype)
        return carry

    lax.fori_loop(0, n_chunks // FIN, finish, 0)
    for hh in range(HB):
        sf_out_ref[0, hh] = st_s[2 * hh]
        sb_out_ref[0, hh] = st_s[2 * hh + 1]


def _gdn(z, zab, conv_w, norm_g, alog_row, dtb_row, B, L, state_f=None, state_b=None, layer_j=0):
    T = z.shape[0]
    H = GDN_HEADS
    has_state = state_f is not None
    HB = H if L <= GDN_ALL_HEADS_MAX_LEN else GDN_HEADS_LONG
    nh = H // HB
    w = HB * LANES
    blk = lambda i: pl.BlockSpec((L, w), lambda b, h: (b, i * nh + h))
    cw = lambda i: pl.BlockSpec((GDN_CONV_W, w), lambda b, h: (0, i * nh + h))
    in_specs = [blk(0), blk(1), blk(2), blk(3),
                pl.BlockSpec((L, LANES), lambda b, h: (b, 0)),
                cw(0), cw(1), cw(2),
                pl.BlockSpec((1, LANES), lambda b, h: (0, 0)),
                pl.BlockSpec((1, LANES), lambda b, h: (0, 0)),
                pl.BlockSpec((1, LANES), lambda b, h: (0, 0))]
    args = [z, z, z, z, zab, conv_w, conv_w, conv_w, norm_g, alog_row, dtb_row]
    if has_state:
        st = pl.BlockSpec((1, 1, HB, GDN_DK, LANES), lambda b, h: (b, layer_j, h, 0, 0))
        in_specs += [st, st]
        args += [state_f, state_b]
    sout = pl.BlockSpec((1, HB, GDN_DK, LANES), lambda b, h: (b, h, 0, 0))
    n_slots = 2 * (L // GDN_CHUNK)
    scratch = ([pltpu.VMEM((HB, L, LANES), F32) for _ in range(3)]
               + [pltpu.VMEM((L, LANES), F32) for _ in range(2)]
               + [pltpu.VMEM((2 * HB, GDN_DK, LANES), F32)]
               + [pltpu.VMEM((HB, 2 * L, LANES), F32), pltpu.VMEM((HB, 2 * L, LANES), BF16)]
               + [pltpu.VMEM((HB, n_slots * GDN_DK, LANES), F32),
                  pltpu.VMEM((HB, n_slots * GDN_DK, LANES), BF16)]
               + [pltpu.VMEM((HB, n_slots * 8, LANES), F32)]
               + [pltpu.VMEM((L + 16, LANES), F32)])
    return pl.pallas_call(
        functools.partial(_gdn_kernel, L=L, HB=HB, has_state=has_state),
        grid=(B, nh),
        in_specs=in_specs,
        out_specs=[pl.BlockSpec((L, w), lambda b, h: (b, h)), sout, sout],
        out_shape=[jax.ShapeDtypeStruct((T, H * LANES), MIX_DTYPE),
                   jax.ShapeDtypeStruct((B, H, GDN_DK, LANES), F32),
                   jax.ShapeDtypeStruct((B, H, GDN_DK, LANES), F32)],
        scratch_shapes=scratch,
        compiler_params=_cparams(("arbitrary", "arbitrary")),
        name="gdn",
    )(*args)


def _hy_filter_kernel(f_ref, w1_ref, b1_ref, sf0_ref, w2_ref, b2_ref, sf1_ref, w3_ref, dl_ref, o_ref, *, L):
    rowi = lax.broadcasted_iota(jnp.int32, (L, LANES), 0).astype(F32)
    lane = lax.broadcasted_iota(jnp.int32, (L, LANES), 1)
    t = rowi * (1.0 / (L - 1))
    wpos = rowi * (2.0 * math.pi / L)
    ang = f_ref[...] * wpos
    z = jnp.where(lane == 0, t,
                  jnp.where(lane <= HY_BANDS, jnp.cos(ang),
                            jnp.where(lane <= 2 * HY_BANDS, -jnp.sin(ang), 0.0)))
    h = jnp.sin(sf0_ref[...] * (_dot3(z, w1_ref[...]) + b1_ref[...]))
    h = jnp.sin(sf1_ref[...] * (_dot3(h, w2_ref[...]) + b2_ref[...]))
    tc = lax.broadcasted_iota(jnp.int32, (L, HY_CH), 0).astype(F32) * (1.0 / (L - 1))
    window = jnp.exp(-tc * dl_ref[...])
    for j in range(4):
        o_ref[:, j * HY_CH:(j + 1) * HY_CH] = _dot3(h, w3_ref[:, j * HY_CH:(j + 1) * HY_CH]) * window


def _pad_to(a, shape):
    return jnp.pad(a, [(0, s - d) for s, d in zip(shape, a.shape)])


def _hy_filters(L, w1, b1, w2, b2, w3, sin_freq):
    fvals = np.linspace(1e-4, HY_BANDS - 1, HY_BANDS, dtype=np.float32)
    frow = np.zeros((1, LANES), np.float32)
    frow[0, 1:1 + HY_BANDS] = fvals
    frow[0, 1 + HY_BANDS:1 + 2 * HY_BANDS] = fvals
    deltas = np.abs(np.linspace(math.log(HY_DECAY_TARGET) / HY_LONG_DECAY_PCT,
                                math.log(HY_DECAY_TARGET) / HY_SHORT_DECAY_PCT, HY_CH, dtype=np.float32))
    args = [jnp.asarray(frow),
            _pad_to(w1, (LANES, LANES)), _pad_to(b1[None, :], (1, LANES)), _pad_to(sin_freq[0][None, :], (1, LANES)),
            _pad_to(w2, (LANES, LANES)), _pad_to(b2[None, :], (1, LANES)), _pad_to(sin_freq[1][None, :], (1, LANES)),
            _pad_to(w3, (LANES, 4 * HY_CH)), jnp.asarray(deltas[None, :])]
    return pl.pallas_call(
        functools.partial(_hy_filter_kernel, L=L),
        out_shape=jax.ShapeDtypeStruct((L, 4 * HY_CH), F32),
        compiler_params=pltpu.CompilerParams(vmem_limit_bytes=VMEM_LIMIT),
        name="hyena_filters",
    )(*args)


def _dft_tables(L):
    N = 2 * L
    k = np.arange(L, dtype=np.int64)[:, None]
    s = np.arange(L, dtype=np.int64)[None, :]
    ang = ((2 * k + 1) * s % (2 * N)).astype(np.float64) * (2.0 * math.pi / (2 * N))
    return np.cos(ang).astype(np.float32), np.sin(ang).astype(np.float32)


def _dft_blocks(L, kb):
    cm, sm = _dft_tables(L)
    nk = L // kb
    fwd = np.concatenate([cm.reshape(nk, kb, L), sm.reshape(nk, kb, L)], axis=1)
    return jnp.asarray(fwd).astype(BF16)


def _hy_spectrum_kernel(f_ref, flt_ref, hc_ref, hs_ref, *, L, kb):
    row = lax.broadcasted_iota(jnp.int32, (L, HY_CH), 0)
    scale = 1.0 / L
    for o in range(2):
        hf = flt_ref[:, (2 * o) * HY_CH:(2 * o + 1) * HY_CH]
        hb = jnp.where(row == 0, 0.0, flt_ref[:, (2 * o + 1) * HY_CH:(2 * o + 2) * HY_CH])
        a_hi, a_lo = _split(hf + hb)
        d_hi, d_lo = _split(hf - hb)
        fc = f_ref[0, :kb, :]
        fs = f_ref[0, kb:, :]
        hc_ref[o] = (_dot(fc, a_hi) + _dot(fc, a_lo)) * scale
        hs_ref[o] = (_dot(fs, d_hi) + _dot(fs, d_lo)) * scale


def _hy_spectrum(filt, fwd, L, kb):
    nk = L // kb
    return pl.pallas_call(
        functools.partial(_hy_spectrum_kernel, L=L, kb=kb),
        grid=(nk,),
        in_specs=[pl.BlockSpec((1, 2 * kb, L), lambda i: (i, 0, 0)),
                  pl.BlockSpec((L, 4 * HY_CH), lambda i: (0, 0))],
        out_specs=[pl.BlockSpec((2, kb, HY_CH), lambda i: (0, i, 0)),
                   pl.BlockSpec((2, kb, HY_CH), lambda i: (0, i, 0))],
        out_shape=[jax.ShapeDtypeStruct((2, L, HY_CH), F32)] * 2,
        compiler_params=_cparams(("arbitrary",)),
        name="hyena_spectrum",
    )(fwd, filt)


def _hy_conv_kernel(ze_ref, zo_ref, xe_ref, xo_ref, ge_ref, go_ref, cwz_ref, cbz_ref, cwx_ref, cbx_ref, skip_ref,
                    f_ref, g_ref, har_ref, hai_ref, hbr_ref, hbi_ref, cw_ref, sw_ref, oe_ref, oo_ref,
                    *, kb, nk, first, last):
    cb = ze_ref.shape[1]

    def conv3(e_ref, o_ref, w_ref, b_ref):
        xe = e_ref[...]
        xo = o_ref[...]
        w0, w1, w2 = w_ref[0:1, :], w_ref[1:2, :], w_ref[2:3, :]
        ye = _shift_rows(xo, -1) * w0 + xe * w1 + xo * w2 + b_ref[...]
        yo = xe * w0 + xo * w1 + _shift_rows(xe, 1) * w2 + b_ref[...]
        return ye, yo

    if first:
        ze, zo = conv3(ze_ref, zo_ref, cwz_ref, cbz_ref)
    else:
        ze, zo = ze_ref[...], zo_ref[...]
    zb = jnp.concatenate([ze, zo], axis=1).astype(BF16)
    acc = None
    for k in range(nk):
        rows = slice(k * kb, (k + 1) * kb)
        xc = _dot(f_ref[k, :kb, :], zb)
        xs = _dot(f_ref[k, kb:, :], zb)
        ec, oc = xc[:, :cb], xc[:, cb:]
        es, os_ = xs[:, :cb], xs[:, cb:]
        cw = cw_ref[rows, :]
        sw = sw_ref[rows, :]
        p_re = cw * oc - sw * os_
        p_im = -(cw * os_ + sw * oc)
        xa_re, xa_im = ec + p_re, p_im - es
        xb_re, xb_im = ec - p_re, es + p_im
        har, hai = har_ref[0, rows, :], hai_ref[0, rows, :]
        hbr, hbi = hbr_ref[0, rows, :], hbi_ref[0, rows, :]
        ya_re = xa_re * har - xa_im * hai
        ya_im = xa_re * hai + xa_im * har
        yb_re = xb_re * hbr - xb_im * hbi
        yb_im = xb_re * hbi + xb_im * hbr
        ze_re, ze_im = ya_re + yb_re, ya_im - yb_im
        d_re, d_im = ya_re - yb_re, ya_im + yb_im
        zo_re = d_re * cw - d_im * sw
        zo_im = d_re * sw + d_im * cw
        spec = jnp.concatenate([jnp.concatenate([ze_re, zo_re], axis=1),
                                jnp.concatenate([-ze_im, -zo_im], axis=1)], axis=0).astype(BF16)
        part = _dot(g_ref[k], spec)
        acc = part if acc is None else acc + part
    xe, xo = conv3(xe_ref, xo_ref, cwx_ref, cbx_ref)
    re = xe * (acc[:, :cb] + ze * skip_ref[...])
    ro = xo * (acc[:, cb:] + zo * skip_ref[...])
    if last:
        re = re * _silu(ge_ref[...])
        ro = ro * _silu(go_ref[...])
    oe_ref[...] = re.astype(oe_ref.dtype)
    oo_ref[...] = ro.astype(oo_ref.dtype)


def _hy_half_tables(L, kb, cb):
    M = L // 2
    nk = M // kb
    k = np.arange(M, dtype=np.int64)[:, None]
    s = np.arange(M, dtype=np.int64)[None, :]
    ang = ((2 * k + 1) * s % (2 * L)).astype(np.float64) * (2.0 * math.pi / (2 * L))
    c2, s2 = np.cos(ang).astype(np.float32), np.sin(ang).astype(np.float32)
    fwd = np.concatenate([c2.reshape(nk, kb, M), s2.reshape(nk, kb, M)], axis=1)
    inv = np.concatenate([c2.T.reshape(M, nk, kb), s2.T.reshape(M, nk, kb)], axis=2)
    inv = np.ascontiguousarray(inv.transpose(1, 0, 2))
    w = (2 * k + 1).astype(np.float64) * (2.0 * math.pi / (4 * L))
    cw = np.ascontiguousarray(np.broadcast_to(np.cos(w), (M, cb))).astype(np.float32)
    sw = np.ascontiguousarray(np.broadcast_to(np.sin(w), (M, cb))).astype(np.float32)
    return (jnp.asarray(fwd).astype(BF16), jnp.asarray(inv).astype(BF16), jnp.asarray(cw), jnp.asarray(sw))


def _hy_conv(z_even, z_odd, x_even, x_odd, g_even, g_odd, conv_w, conv_b, skip, order, tables, spectra,
             B, L, first, last):
    M = L // 2
    cb = HY_CH_BLOCK
    ncb = HY_CH // cb
    fwd, inv, cw, sw = tables
    nk, _, kb2 = inv.shape
    kb = kb2 // 2
    once = dict(pipeline_mode=pl.Buffered(1))
    data = lambda op: pl.BlockSpec((M, cb), lambda b, c: (b, op[1] + c))
    cwspec = lambda off: pl.BlockSpec((3, cb), lambda b, c: (0, off + c))
    cbspec = lambda off: pl.BlockSpec((1, cb), lambda b, c: (0, off + c))
    hspec = pl.BlockSpec((1, M, cb), lambda b, c: (order, 0, c))
    operands = [z_even, z_odd, x_even, x_odd, g_even, g_odd]
    in_specs = ([data(op) for op in operands]
                + [cwspec(0), cbspec(0), cwspec((1 + order) * ncb), cbspec((1 + order) * ncb),
                   pl.BlockSpec((1, cb), lambda b, c: (0, c)),
                   pl.BlockSpec(fwd.shape, lambda b, c: (0, 0, 0), **once),
                   pl.BlockSpec(inv.shape, lambda b, c: (0, 0, 0), **once),
                   hspec, hspec, hspec, hspec,
                   pl.BlockSpec((M, cb), lambda b, c: (0, 0), **once),
                   pl.BlockSpec((M, cb), lambda b, c: (0, 0), **once)])
    out_dtype = MIX_DTYPE if last else F32
    out = pl.BlockSpec((M, cb), lambda b, c: (b, c))
    return pl.pallas_call(
        functools.partial(_hy_conv_kernel, kb=kb, nk=nk, first=first, last=last),
        grid=(B, ncb),
        in_specs=in_specs,
        out_specs=[out, out],
        out_shape=[jax.ShapeDtypeStruct((B * M, HY_CH), out_dtype)] * 2,
        compiler_params=_cparams(("arbitrary", "arbitrary")),
        name="hyena_conv",
    )(*[op[0] for op in operands], conv_w, conv_b, conv_w, conv_b, skip[order:order + 1],
      fwd, inv, *spectra, cw, sw)


OD_GQ, OD_GG, OD_MG, OD_GK, OD_GV, OD_MQ, OD_MKV, OD_MPE = 0, 4, 8, 12, 14, 16, 18, 19
OD_WIDTH = 20 * LANES


def _rope_swap(x, quarter):
    n = x.shape[1]
    lane = lax.broadcasted_iota(jnp.int32, x.shape, 1)
    first = (lane & (2 * quarter - 1)) < quarter
    return jnp.where(first, pltpu.roll(x, n - quarter, 1), pltpu.roll(x, quarter, 1))


def _odd_in_kernel(x_ref, sh_ref, sc_ref, g_ref, w_ref, qg_ref, kg_ref, mqg_ref, mqup_ref, mkvg_ref, wk_ref,
                   *rest, rope):
    if rope:
        cg_ref, sg_ref, cm_ref, sm_ref = rest[:4]
        rest = rest[4:]
    q_out, qc_out, k16_out, v16_out, kc16_out, gate_out, k_out, v_out, ckv_out, kpe_out = rest
    D = HEAD_DIM
    hb = (_rms(x_ref[...], g_ref[...]) * (1.0 + sc_ref[0]) + sh_ref[0]).astype(BF16)

    def proj(col, width):
        return _dot(hb, w_ref[:, col * D:(col + width) * D])

    def rot(x, c, s, quarter):
        return x * c + _rope_swap(x, quarter) * s if rope else x

    gscale = (D ** -0.5) * LOG2E
    zq = proj(OD_GQ, GQA_HEADS)
    for h in range(GQA_HEADS):
        q = rot(_rms(zq[:, h * D:(h + 1) * D], qg_ref[...]), cg_ref[...] if rope else None,
                sg_ref[...] if rope else None, D // 4)
        q_out[:, h * D:(h + 1) * D] = (q * gscale).astype(BF16)
    zk = proj(OD_GK, GQA_KV_HEADS)
    zv = proj(OD_GV, GQA_KV_HEADS)
    v_out[...] = zv
    v16_out[...] = zv.astype(BF16)
    for h in range(GQA_KV_HEADS):
        kn = _rms(zk[:, h * D:(h + 1) * D], kg_ref[...])
        k_out[:, h * D:(h + 1) * D] = kn
        k16_out[:, h * D:(h + 1) * D] = rot(kn, cg_ref[...] if rope else None,
                                            sg_ref[...] if rope else None, D // 4).astype(BF16)
    gate_out[...] = proj(OD_GG, GQA_HEADS + MLA_HEADS)

    mq = _rms(proj(OD_MQ, 2), mqg_ref[...]).astype(BF16)
    qm = _dot(mq, mqup_ref[...])
    mscale = ((D + MLA_ROPE_DIM) ** -0.5) * LOG2E
    qpe = qm[:, MLA_HEADS * D:]
    if rope:
        qpe = rot(qpe, cm_ref[...], sm_ref[...], MLA_ROPE_DIM // 4)
    for h in range(MLA_HEADS):
        qn = qm[:, h * D:(h + 1) * D].astype(BF16)
        qc_out[:, 2 * h * D:(2 * h + 1) * D] = (_dot_nt(qn, wk_ref[h]) * mscale).astype(BF16)
        qc_out[:, (2 * h + 1) * D:(2 * h + 2) * D] = (qpe[:, h * D:(h + 1) * D] * mscale).astype(BF16)
    ckv = _rms(proj(OD_MKV, 1), mkvg_ref[...])
    ckv_out[...] = ckv
    kpe = proj(OD_MPE, 1)
    kpe_out[...] = kpe
    if rope:
        kpe = rot(kpe, cm_ref[:, :D], sm_ref[:, :D], MLA_ROPE_DIM // 4)
    kc16_out[:, :D] = ckv.astype(BF16)
    kc16_out[:, D:] = kpe.astype(BF16)


def _rope_tables(L, R):
    rows = np.arange(L) // GRID_W
    cols = np.arange(L) % GRID_W
    quarter = R // 4
    inv = ROPE_THETA ** (-np.arange(quarter, dtype=np.float32) * 2.0 / (R // 2))
    a_r = rows[:, None].astype(np.float32) * inv[None, :]
    a_c = cols[:, None].astype(np.float32) * inv[None, :]
    cos = np.concatenate([np.cos(a_r), np.cos(a_r), np.cos(a_c), np.cos(a_c)], axis=1)
    sin = np.concatenate([-np.sin(a_r), np.sin(a_r), -np.sin(a_c), np.sin(a_c)], axis=1)
    return cos.astype(np.float32), sin.astype(np.float32)


def _odd_in(x2, shift, scale, pre_g, w_main, q_g, k_g, mq_g, mq_up, mkv_g, wk, L, tiles_per_mod, rope):
    T, dm = x2.shape
    tm = TOKEN_TILE
    D = HEAD_DIM
    if tiles_per_mod:
        mod_map = lambda i: (i // tiles_per_mod, 0, 0)
    else:
        mod_map = lambda i: (0, 0, 0)
    full = lambda shape: pl.BlockSpec(shape, lambda i: tuple(0 for _ in shape))
    in_specs = [pl.BlockSpec((tm, dm), lambda i: (i, 0)),
                pl.BlockSpec((1, 1, dm), mod_map), pl.BlockSpec((1, 1, dm), mod_map),
                full((1, dm)), full(w_main.shape),
                full((1, D)), full((1, D)), full((1, 2 * D)), full(mq_up.shape), full((1, D)), full(wk.shape)]
    args = [x2, shift, scale, pre_g, w_main, q_g, k_g, mq_g, mq_up, mkv_g, wk]
    if rope:
        cg, sg = _rope_tables(L, D)
        cm, sm = _rope_tables(L, MLA_ROPE_DIM)
        widen = lambda t: np.tile(np.concatenate([t, np.zeros_like(t)], axis=1), (1, MLA_HEADS))
        per = L // tm
        pos = lambda w: pl.BlockSpec((tm, w), lambda i: (i % per, 0))
        in_specs += [pos(D), pos(D), pos(MLA_HEADS * D), pos(MLA_HEADS * D)]
        args += [jnp.asarray(cg), jnp.asarray(sg), jnp.asarray(widen(cm)), jnp.asarray(widen(sm))]
    tile = lambda w: pl.BlockSpec((tm, w), lambda i: (i, 0))
    widths = [(4 * D, BF16), (8 * D, BF16), (2 * D, BF16), (2 * D, BF16), (2 * D, BF16), (8 * D, F32),
              (2 * D, F32), (2 * D, F32), (D, F32), (D, F32)]
    return pl.pallas_call(
        functools.partial(_odd_in_kernel, rope=rope),
        grid=(T // tm,),
        in_specs=in_specs,
        out_specs=[tile(w) for w, _ in widths],
        out_shape=[jax.ShapeDtypeStruct((T, w), dt) for w, dt in widths],
        compiler_params=_cparams(("arbitrary",)),
        name="odd_in_proj",
    )(*args)


def _softmax_pv(score_blocks, value_blocks):
    m = score_blocks[0].max(axis=-1, keepdims=True)
    for s in score_blocks[1:]:
        m = jnp.maximum(m, s.max(axis=-1, keepdims=True))
    acc = None
    den = None
    for s, v in zip(score_blocks, value_blocks):
        p = jnp.exp2(s - m)
        d = p.sum(axis=-1, keepdims=True)
        a = _dot(p.astype(BF16), v)
        acc = a if acc is None else acc + a
        den = d if den is None else den + d
    return acc / den


def _gqa_kernel(q_ref, k_ref, v_ref, gate_ref, *rest, cached):
    if cached:
        ck_ref, cv_ref, o_ref = rest
    else:
        (o_ref,) = rest
    D = HEAD_DIM
    group = GQA_HEADS // GQA_KV_HEADS
    q = q_ref[...]
    keys = [k_ref[...]] + ([ck_ref[0, 0]] if cached else [])
    values = [v_ref[...]] + ([cv_ref[0, 0]] if cached else [])
    outs = []
    for g in range(group):
        qh = q[:, g * D:(g + 1) * D]
        outs.append(_softmax_pv([_dot_nt(qh, kk) for kk in keys], values))
    o_ref[...] = (jnp.concatenate(outs, axis=1) * _silu(gate_ref[...])).astype(o_ref.dtype)


def _gqa(qg, kg, vg, gates, B, L, tq, cache_k=None, cache_v=None, layer_j=0):
    T = qg.shape[0]
    D = HEAD_DIM
    group = GQA_HEADS // GQA_KV_HEADS
    nq = L // tq
    cached = cache_k is not None
    in_specs = [pl.BlockSpec((tq, group * D), lambda b, h, i: (b * nq + i, h)),
                pl.BlockSpec((L, D), lambda b, h, i: (b, h)),
                pl.BlockSpec((L, D), lambda b, h, i: (b, h)),
                pl.BlockSpec((tq, group * D), lambda b, h, i: (b * nq + i, h))]
    args = [qg, kg, vg, gates]
    if cached:
        P = cache_k.shape[2]
        cspec = pl.BlockSpec((1, 1, P, D), lambda b, h, i: (b, layer_j, 0, h))
        in_specs += [cspec, cspec]
        args += [cache_k, cache_v]
    return pl.pallas_call(
        functools.partial(_gqa_kernel, cached=cached),
        grid=(B, GQA_KV_HEADS, nq),
        in_specs=in_specs,
        out_specs=pl.BlockSpec((tq, group * D), lambda b, h, i: (b * nq + i, h)),
        out_shape=jax.ShapeDtypeStruct((T, GQA_HEADS * D), MIX_DTYPE),
        compiler_params=_cparams(("arbitrary", "arbitrary", "arbitrary")),
        name="gqa_attention",
    )(*args)


def _mla_kernel(qc_ref, kc_ref, gate_ref, wv_ref, *rest, cached):
    if cached:
        ckc_ref, o_ref = rest
    else:
        (o_ref,) = rest
    D = HEAD_DIM
    H = MLA_HEADS
    qc = qc_ref[...]
    tq = qc.shape[0]
    keys = [kc_ref[...]] + ([ckc_ref[0, 0]] if cached else [])
    per = H // MLA_HEAD_STACKS
    outs = []
    for s in range(MLA_HEAD_STACKS):
        heads = range(s * per, (s + 1) * per)
        qs = jnp.concatenate([qc[:, 2 * h * D:(2 * h + 2) * D] for h in heads], axis=0)
        o = _softmax_pv([_dot_nt(qs, kk) for kk in keys], [kk[:, :D] for kk in keys]).astype(BF16)
        for i, h in enumerate(heads):
            outs.append(_dot(o[i * tq:(i + 1) * tq], wv_ref[h]))
    o_ref[...] = (jnp.concatenate(outs, axis=1) * _silu(gate_ref[...])).astype(o_ref.dtype)


def _mla(qc, kc, gates, wv, B, L, tq, cache_kc=None, layer_j=0):
    T = qc.shape[0]
    D = HEAD_DIM
    H = MLA_HEADS
    nq = L // tq
    cached = cache_kc is not None
    in_specs = [pl.BlockSpec((tq, 2 * H * D), lambda b, i: (b * nq + i, 0)),
                pl.BlockSpec((L, 2 * D), lambda b, i: (b, 0)),
                pl.BlockSpec((tq, H * D), lambda b, i: (b * nq + i, 1)),
                pl.BlockSpec(wv.shape, lambda b, i: (0, 0, 0))]
    args = [qc, kc, gates, wv]
    if cached:
        P = cache_kc.shape[2]
        in_specs += [pl.BlockSpec((1, 1, P, 2 * D), lambda b, i: (b, layer_j, 0, 0))]
        args += [cache_kc]
    return pl.pallas_call(
        functools.partial(_mla_kernel, cached=cached),
        grid=(B, nq),
        in_specs=in_specs,
        out_specs=pl.BlockSpec((tq, H * D), lambda b, i: (b * nq + i, 0)),
        out_shape=jax.ShapeDtypeStruct((T, H * D), MIX_DTYPE),
        compiler_params=_cparams(("arbitrary", "arbitrary")),
        name="mla_attention",
    )(*args)


def _even_weights(even_in_w, gdn_conv_w, gdn_a_log, gdn_dt_bias, hyena_conv_w, hyena_conv_b, j):
    w = even_in_w[j]
    qkv_w = 3 * GDN_HEADS * LANES
    n_ab = 4 * GDN_HEADS
    gate_end = qkv_w + n_ab + GDN_HEADS * LANES
    w_a = jnp.concatenate([w[:, :qkv_w], w[:, qkv_w + n_ab:gate_end]], axis=1).astype(BF16)
    w_small = _pad_to(w[:, qkv_w:qkv_w + n_ab], (w.shape[0], LANES)).astype(BF16)
    w_b = w[:, gate_end:].astype(BF16)
    n_dir = 2 * GDN_HEADS
    alog_row = _pad_to(gdn_a_log[j].reshape(1, n_dir), (1, LANES))
    dtb_row = _pad_to(gdn_dt_bias[j].reshape(1, n_dir), (1, LANES))
    return w_a, w_small, w_b, alog_row, dtb_row


def _even_layer(x2, B, L, shift, scale, gate, tiles_per_mod, pre_g, post_g, wts, hy, out_w,
                gdn_conv_w, gdn_norm_g, hy_conv_w, hy_conv_b, hy_skip, state_f, state_b, j):
    w_a, w_small, w_b, alog_row, dtb_row = wts
    tables, spectra = hy
    z, zab, zp = _in_proj(x2, shift, scale, pre_g, w_a, w_small, w_b, tiles_per_mod)
    oa, s_f, s_b = _gdn(z, zab, gdn_conv_w, gdn_norm_g, alog_row, dtb_row, B, L, state_f, state_b, j)
    ncb = HY_CH // HY_CH_BLOCK
    n_groups = zp.shape[1] // (2 * HY_CH)
    even = lambda idx: (zp, idx * ncb)
    odd = lambda idx: (zp, (n_groups + idx) * ncb)
    z1e, z1o = _hy_conv(even(0), odd(0), even(1), odd(1), even(3), odd(3), hy_conv_w, hy_conv_b, hy_skip, 0,
                        tables, spectra, B, L, first=True, last=False)
    obe, obo = _hy_conv((z1e, 0), (z1o, 0), even(2), odd(2), even(3), odd(3), hy_conv_w, hy_conv_b, hy_skip, 1,
                        tables, spectra, B, L, first=False, last=True)
    x_new = _out_proj(oa, (obe, obo), x2, gate, post_g, out_w, tiles_per_mod)
    return x_new, s_f, s_b


def _odd_weights(odd_in_w, mla_q_up, mla_kv_up, j):
    w = odd_in_w[j]
    D = HEAD_DIM
    o = np.cumsum([0, 4 * D, 2 * D, 2 * D, 4 * D, 2 * D, D, MLA_ROPE_DIM, 4 * D])
    gq, gk, gv, gg, mq, mkv, mpe, mg = [w[:, o[i]:o[i + 1]] for i in range(8)]
    w_main = jnp.concatenate([gq, gg, mg, gk, gv, mq, mkv, _pad_to(mpe, (w.shape[0], D))], axis=1).astype(BF16)
    up = mla_q_up[j].reshape(-1, MLA_HEADS, D + MLA_ROPE_DIM)
    rope_cols = _pad_to(up[:, :, D:], (up.shape[0], MLA_HEADS, D))
    mq_up = jnp.concatenate([up[:, :, :D].reshape(-1, MLA_HEADS * D),
                             rope_cols.reshape(-1, MLA_HEADS * D)], axis=1).astype(BF16)
    kv = mla_kv_up[j].reshape(-1, MLA_HEADS, 2 * D)
    wk = kv[:, :, :D].transpose(1, 0, 2).astype(BF16)
    wv = kv[:, :, D:].transpose(1, 0, 2).astype(BF16)
    return w_main, mq_up, wk, wv


def _odd_layer(x2, B, L, shift, scale, gate, tiles_per_mod, pre_g, post_g, wts, out_w,
               q_g, k_g, mq_g, mkv_g, caches, j):
    w_main, mq_up, wk, wv = wts
    rope = caches is not None
    qg, qc, k16, v16, kc16, gates, kg, v32, ckv, kpe = _odd_in(
        x2, shift, scale, pre_g, w_main, q_g, k_g, mq_g, mq_up, mkv_g, wk, L, tiles_per_mod, rope)
    if rope:
        ck, cv, ckc = caches
        og = _gqa(qg, k16, v16, gates, B, L, GQA_Q_TILE, ck, cv, j)
        om = _mla(qc, kc16, gates, wv, B, L, MLA_Q_TILE, ckc, j)
    else:
        og = _gqa(qg, k16, v16, gates, B, L, GQA_Q_TILE)
        om = _mla(qc, kc16, gates, wv, B, L, MLA_Q_TILE)
    x_new = _out_proj(og, om, x2, gate, post_g, out_w, tiles_per_mod)
    return x_new, (kg, v32, ckv, kpe)


def kernel(x_prompt, x_sample, state_gdn_fwd, state_gdn_bwd, cache_gqa_k, cache_gqa_v, cache_mla_ckv, cache_mla_kpe, c, c_ctx, mod_w, mod_b, pre_norm_g, post_norm_g, even_in_w, gdn_conv_w, gdn_a_log, gdn_dt_bias, gdn_norm_g, hyena_conv_w, hyena_conv_b, hyena_ffn_w1, hyena_ffn_b1, hyena_ffn_w2, hyena_ffn_b2, hyena_ffn_w3, hyena_sin_freq, hyena_bias, even_out_w, odd_in_w, gqa_q_norm_g, gqa_k_norm_g, mla_q_norm_g, mla_q_up, mla_kv_norm_g, mla_kv_up, odd_out_w):
    Bp, Lp, D = x_prompt.shape
    Bs, Ls, _ = x_sample.shape
    depth = mod_w.shape[0]
    xp = x_prompt.reshape(Bp * Lp, D)
    xs = x_sample.reshape(Bs * Ls, D)

    n_cond = 1 + Bs
    rows = -(-n_cond // 8) * 8
    cond = _pad_to(jnp.concatenate([c_ctx[None, :], c], axis=0), (rows, D))
    mod = _modulation(cond, mod_w, mod_b)

    P = cache_gqa_k.shape[2]
    n_odd = cache_gqa_k.shape[1]
    ck = cache_gqa_k.reshape(Bs, n_odd, P, GQA_KV_HEADS * HEAD_DIM).astype(BF16)
    cv = cache_gqa_v.reshape(Bs, n_odd, P, GQA_KV_HEADS * HEAD_DIM).astype(BF16)
    ckc = jnp.concatenate([cache_mla_ckv, _pad_to(cache_mla_kpe, cache_mla_ckv.shape)], axis=-1).astype(BF16)

    dft = {L: _dft_blocks(L, min(HY_FREQ_BLOCK, L)) for L in (Lp, Ls)}
    half = {L: _hy_half_tables(L, min(HY_HALF_FREQ_BLOCK, L // 2), HY_CH_BLOCK) for L in (Lp, Ls)}
    tpm_s = Ls // TOKEN_TILE

    new_f, new_b, new_gk, new_gv, new_ckv, new_kpe = [], [], [], [], [], []
    for i in range(depth):
        j = i // 2
        m = mod[i]
        sh_p, sc_p, gt_p = [m[0:1, k * D:(k + 1) * D].reshape(1, 1, D) for k in range(3)]
        sh_s, sc_s, gt_s = [m[1:n_cond, k * D:(k + 1) * D].reshape(Bs, 1, D) for k in range(3)]
        pre_g = pre_norm_g[i][None, :]
        post_g = post_norm_g[i][None, :]
        if i % 2 == 0:
            wts = _even_weights(even_in_w, gdn_conv_w, gdn_a_log, gdn_dt_bias, hyena_conv_w, hyena_conv_b, j)
            out_w = even_out_w[j].astype(BF16)
            hy = {}
            for L in (Lp, Ls):
                filt = _hy_filters(L, hyena_ffn_w1[j], hyena_ffn_b1[j], hyena_ffn_w2[j], hyena_ffn_b2[j],
                                   hyena_ffn_w3[j], hyena_sin_freq[j])
                hc, hs = _hy_spectrum(filt, dft[L], L, min(HY_FREQ_BLOCK, L))
                M = L // 2
                spectra = (hc[:, :M], -hs[:, :M], jnp.flip(hc[:, M:], axis=1), -jnp.flip(hs[:, M:], axis=1))
                hy[L] = (half[L], spectra)
            common = (gdn_conv_w[j], gdn_norm_g[j][None, :], hyena_conv_w[j], hyena_conv_b[j][None, :], hyena_bias[j])
            xp, sf, sb = _even_layer(xp, Bp, Lp, sh_p, sc_p, gt_p, 0, pre_g, post_g, wts, hy[Lp], out_w,
                                     *common, None, None, j)
            xs, _, _ = _even_layer(xs, Bs, Ls, sh_s, sc_s, gt_s, tpm_s, pre_g, post_g, wts, hy[Ls], out_w,
                                   *common, state_gdn_fwd, state_gdn_bwd, j)
            new_f.append(sf)
            new_b.append(sb)
        else:
            wts = _odd_weights(odd_in_w, mla_q_up, mla_kv_up, j)
            out_w = odd_out_w[j].astype(BF16)
            norms = (gqa_q_norm_g[j][None, :], gqa_k_norm_g[j][None, :], mla_q_norm_g[j][None, :],
                     mla_kv_norm_g[j][None, :])
            xp, (kg, v32, ckv, kpe) = _odd_layer(xp, Bp, Lp, sh_p, sc_p, gt_p, 0, pre_g, post_g, wts, out_w,
                                                 *norms, None, j)
            xs, _ = _odd_layer(xs, Bs, Ls, sh_s, sc_s, gt_s, tpm_s, pre_g, post_g, wts, out_w,
                               *norms, (ck, cv, ckc), j)
            new_gk.append(kg.reshape(Bp, Lp, GQA_KV_HEADS, HEAD_DIM))
            new_gv.append(v32.reshape(Bp, Lp, GQA_KV_HEADS, HEAD_DIM))
            new_ckv.append(ckv.reshape(Bp, Lp, HEAD_DIM))
            new_kpe.append(kpe[:, :MLA_ROPE_DIM].reshape(Bp, Lp, MLA_ROPE_DIM))
    return (xp.reshape(Bp, Lp, D), xs.reshape(Bs, Ls, D),
            jnp.stack(new_f, axis=1), jnp.stack(new_b, axis=1),
            jnp.stack(new_gk, axis=1), jnp.stack(new_gv, axis=1),
            jnp.stack(new_ckv, axis=1), jnp.stack(new_kpe, axis=1))
```

```python
import functools
import math

import numpy as np
import jax
import jax.numpy as jnp
from jax import lax
from jax.experimental import pallas as pl
from jax.experimental.pallas import tpu as pltpu

F32 = jnp.float32
BF16 = jnp.bfloat16
MIX_DTYPE = BF16

NORM_EPS = 1e-6
ROPE_THETA = 10000.0
GRID_W = 64

GDN_HEADS = 4
GDN_DK = 128
GDN_CHUNK = 64
GDN_CONV_W = 5
GDN_PREP_CHAINS = 64
GDN_ALL_HEADS_MAX_LEN = 256
GDN_HEADS_LONG = 1
HY_CH = 512
HY_BANDS = 16
HY_DECAY_TARGET = 1e-2
HY_SHORT_DECAY_PCT = 0.3
HY_LONG_DECAY_PCT = 1.5
GQA_HEADS = 4
GQA_KV_HEADS = 2
HEAD_DIM = 128
MLA_HEADS = 4
MLA_ROPE_DIM = 64

LOG2E = math.log2(math.e)

LANES = 128
VMEM_LIMIT = 56 * 1024 * 1024

TOKEN_TILE = 256
HY_FREQ_BLOCK = 512
HY_HALF_FREQ_BLOCK = 512
HY_CH_BLOCK = 256
GQA_Q_TILE = 256
MLA_Q_TILE = 128
MLA_HEAD_STACKS = 2


def _cparams(sem):
    return pltpu.CompilerParams(dimension_semantics=sem, vmem_limit_bytes=VMEM_LIMIT)


def _dot(a, b):
    return jnp.dot(a, b, preferred_element_type=F32)


def _dot_nt(a, b):
    return lax.dot_general(a, b, (((1,), (1,)), ((), ())), preferred_element_type=F32)


def _dot_tn(a, b):
    return lax.dot_general(a, b, (((0,), (0,)), ((), ())), preferred_element_type=F32)


def _split(a):
    hi = a.astype(BF16)
    lo = (a - hi.astype(F32)).astype(BF16)
    return hi, lo


def _dot3(a, b):
    ah, al = _split(a)
    bh, bl = _split(b)
    return _dot(ah, bh) + (_dot(ah, bl) + _dot(al, bh))


def _bdot(a, b):
    return jnp.einsum('gij,gjk->gik', a, b, preferred_element_type=F32)


def _bdot16(a, b):
    return _bdot(a.astype(BF16), b.astype(BF16))


def _silu(x):
    return x * (1.0 / (1.0 + jnp.exp(-x)))


def _sigmoid(x):
    return 1.0 / (1.0 + jnp.exp(-x))


def _softplus(x):
    return jnp.maximum(x, 0.0) + jnp.log(1.0 + jnp.exp(-jnp.abs(x)))


def _rms(x, g):
    return x * lax.rsqrt(jnp.mean(x * x, axis=-1, keepdims=True) + NORM_EPS) * g


def _shift_rows(x, s):
    L = x.shape[0]
    if s == 0:
        return x
    rolled = pltpu.roll(x, (-s) % L, 0)
    row = lax.broadcasted_iota(jnp.int32, x.shape, 0)
    valid = (row + s >= 0) & (row + s < L)
    return jnp.where(valid, rolled, 0.0)


def _mod_kernel(c_ref, w_ref, b_ref, o_ref):
    c = _silu(c_ref[...])
    o_ref[0] = _dot3(c, w_ref[0]) + b_ref[0]


def _modulation(cond, mod_w, mod_b):
    depth, d, d3 = mod_w.shape
    r = cond.shape[0]
    nb = d3 // d
    return pl.pallas_call(
        _mod_kernel,
        grid=(depth, nb),
        in_specs=[pl.BlockSpec((r, d), lambda i, n: (0, 0)),
                  pl.BlockSpec((1, d, d), lambda i, n: (i, 0, n)),
                  pl.BlockSpec((1, 1, d), lambda i, n: (i, 0, n))],
        out_specs=pl.BlockSpec((1, r, d), lambda i, n: (i, 0, n)),
        out_shape=jax.ShapeDtypeStruct((depth, r, d3), F32),
        compiler_params=_cparams(("arbitrary", "arbitrary")),
        name="adaln_modulation",
    )(cond, mod_w, mod_b.reshape(depth, 1, d3))


def _in_proj_kernel(x_ref, sh_ref, sc_ref, g_ref, wa_ref, ws_ref, wb_ref, oa_ref, os_ref, ob_ref, h_s):
    x = x_ref[...]
    h = _rms(x, g_ref[...]) * (1.0 + sc_ref[0]) + sh_ref[0]
    n_lane_blocks = h.shape[1] // LANES
    for jb in range(n_lane_blocks):
        h_s[jb] = h[:, jb * LANES:(jb + 1) * LANES]
    hb = h.astype(BF16)
    tm = hb.shape[0]
    step = 512
    na = wa_ref.shape[1]
    for n0 in range(0, na, step):
        oa_ref[:, n0:n0 + step] = _dot(hb, wa_ref[:, n0:n0 + step])
    os_ref[...] = _dot(hb, ws_ref[...])
    nb = wb_ref.shape[1]
    for parity in (0, 1):
        hp = jnp.concatenate([h_s[jb, pl.ds(parity, tm // 2, stride=2), :] for jb in range(n_lane_blocks)],
                             axis=1).astype(BF16)
        for n0 in range(0, nb, step):
            ob_ref[:, parity * nb + n0:parity * nb + n0 + step] = _dot(hp, wb_ref[:, n0:n0 + step])


def _in_proj(x2, shift, scale, pre_g, w_a, w_small, w_b, tiles_per_mod):
    T, D = x2.shape
    na, nb = w_a.shape[1], w_b.shape[1]
    tm = TOKEN_TILE
    if tiles_per_mod:
        mod_map = lambda i: (i // tiles_per_mod, 0, 0)
    else:
        mod_map = lambda i: (0, 0, 0)
    const = lambda shape: pl.BlockSpec(shape, lambda i: (0, 0))
    return pl.pallas_call(
        _in_proj_kernel,
        grid=(T // tm,),
        in_specs=[pl.BlockSpec((tm, D), lambda i: (i, 0)),
                  pl.BlockSpec((1, 1, D), mod_map),
                  pl.BlockSpec((1, 1, D), mod_map),
                  const((1, D)), const((D, na)), const((D, LANES)), const((D, nb))],
        out_specs=[pl.BlockSpec((tm, na), lambda i: (i, 0)),
                   pl.BlockSpec((tm, LANES), lambda i: (i, 0)),
                   pl.BlockSpec((tm // 2, 2 * nb), lambda i: (i, 0))],
        out_shape=[jax.ShapeDtypeStruct((T, na), F32), jax.ShapeDtypeStruct((T, LANES), F32),
                   jax.ShapeDtypeStruct((T // 2, 2 * nb), F32)],
        scratch_shapes=[pltpu.VMEM((D // LANES, tm, LANES), F32)],
        compiler_params=_cparams(("arbitrary",)),
        name="in_proj",
    )(x2, shift, scale, pre_g, w_a, w_small, w_b)


def _row_spread(n, parity):
    r = lax.broadcasted_iota(jnp.int32, (n, n // 2), 0)
    c = lax.broadcasted_iota(jnp.int32, (n, n // 2), 1)
    return (r == 2 * c + parity).astype(BF16)


def _out_proj_kernel(a_ref, *rest, half, split):
    if split:
        be_ref, bo_ref, x_ref, gt_ref, g_ref, w_ref, o_ref = rest
        tm = a_ref.shape[0]
        b = (_dot(_row_spread(tm, 0), be_ref[...]) + _dot(_row_spread(tm, 1), bo_ref[...])).astype(BF16)
    else:
        b_ref, x_ref, gt_ref, g_ref, w_ref, o_ref = rest
        b = b_ref[...].astype(BF16)
    y = _dot(a_ref[...].astype(BF16), w_ref[:half, :]) + _dot(b, w_ref[half:, :])
    o_ref[...] = x_ref[...] + gt_ref[0] * _rms(y, g_ref[...])


def _out_proj(a, b, x2, gate, post_g, w, tiles_per_mod):
    T, D = x2.shape
    half = a.shape[1]
    tm = TOKEN_TILE
    split = isinstance(b, tuple)
    if tiles_per_mod:
        mod_map = lambda i: (i // tiles_per_mod, 0, 0)
    else:
        mod_map = lambda i: (0, 0, 0)
    if split:
        b_specs = [pl.BlockSpec((tm // 2, half), lambda i: (i, 0))] * 2
        b_args = list(b)
    else:
        b_specs = [pl.BlockSpec((tm, half), lambda i: (i, 0))]
        b_args = [b]
    return pl.pallas_call(
        functools.partial(_out_proj_kernel, half=half, split=split),
        grid=(T // tm,),
        in_specs=[pl.BlockSpec((tm, half), lambda i: (i, 0))] + b_specs + [
                  pl.BlockSpec((tm, D), lambda i: (i, 0)),
                  pl.BlockSpec((1, 1, D), mod_map),
                  pl.BlockSpec((1, D), lambda i: (0, 0)),
                  pl.BlockSpec((2 * half, D), lambda i: (0, 0))],
        out_specs=pl.BlockSpec((tm, D), lambda i: (i, 0)),
        out_shape=jax.ShapeDtypeStruct((T, D), F32),
        compiler_params=_cparams(("arbitrary",)),
        name="out_proj",
    )(a, *b_args, x2, gate, post_g, w)


def _gdn_kernel(zq_ref, zk_ref, zv_ref, zg_ref, zab_ref, cwq_ref, cwk_ref, cwv_ref, ng_ref,
                alog_ref, dtb_ref, *rest, L, HB, has_state):
    if has_state:
        s0f_ref, s0b_ref = rest[:2]
        rest = rest[2:]
    (o_ref, sf_out_ref, sb_out_ref, q_s, k_s, v_s, g_s, b_s, st_s,
     oc_s, qe_s, sc_s, sm_s, gl_s, pad_s) = rest
    C = GDN_CHUNK
    n_chunks = L // C
    U = min(GDN_PREP_CHAINS // (2 * HB), n_chunks)
    HU = HB * U
    G = 2 * HU
    head0 = pl.program_id(1) * HB

    pad = 8
    pad_s[0:pad, :] = jnp.zeros((pad, LANES), F32)
    pad_s[pad + L:2 * pad + L, :] = jnp.zeros((pad, LANES), F32)

    def conv_silu(x, w_ref, lanes):
        half = GDN_CONV_W // 2
        pad_s[pad:pad + L, :] = x
        acc = x * w_ref[half:half + 1, lanes]
        for i in range(GDN_CONV_W):
            if i != half:
                start = pad + i - half
                acc = acc + pad_s[start:start + L, :] * w_ref[i:i + 1, lanes]
        return _silu(acc)

    def l2n(x):
        return x * lax.rsqrt(jnp.sum(x * x, axis=-1, keepdims=True) + NORM_EPS)

    for hh in range(HB):
        lanes = slice(hh * LANES, (hh + 1) * LANES)
        q_s[hh] = l2n(conv_silu(zq_ref[:, lanes], cwq_ref, lanes)) * (GDN_DK ** -0.5)
        k_s[hh] = l2n(conv_silu(zk_ref[:, lanes], cwk_ref, lanes))
        v_s[hh] = conv_silu(zv_ref[:, lanes], cwv_ref, lanes)
        if has_state:
            st_s[2 * hh] = s0f_ref[0, 0, hh]
            st_s[2 * hh + 1] = s0b_ref[0, 0, hh]
    if not has_state:
        st_s[...] = jnp.zeros_like(st_s)
    zab = zab_ref[...]
    g_s[...] = -jnp.exp(alog_ref[...]) * _softplus(zab + dtb_ref[...])
    b_s[...] = _sigmoid(zab)

    def iota(shape, axis):
        return lax.broadcasted_iota(jnp.int32, shape, axis)

    def direction(shape):
        return iota(shape, 0) & 1

    chain_shift = (2 * U).bit_length() - 1
    assert 2 * U == 1 << chain_shift

    def chain_head(shape):
        return head0 + (iota(shape, 0) >> chain_shift)

    sq = (G, C, C)
    row = iota(sq, 1)
    col = iota(sq, 2)
    signed = (row - col) * (1 - 2 * direction(sq))
    incl = signed >= 0
    strict = signed > 0
    eye = (row == col).astype(F32)
    same = [(row >> s) == (col >> s) for s in (3, 4, 5)]
    off_blocks = [same[1] & jnp.logical_not(same[0]), same[2] & jnp.logical_not(same[1]),
                  jnp.logical_not(same[2])]
    wide = (G, C, LANES)
    sel_lane = direction(wide) * GDN_HEADS + chain_head(wide)
    mask_g = iota(wide, 2) == sel_lane
    mask_b = iota(wide, 2) == sel_lane + 2 * GDN_HEADS
    tall = (G, LANES, C)
    mask_t = iota(tall, 1) == direction(tall) * GDN_HEADS + chain_head(tall)
    colv = (G, C, 1)
    mask_last = iota(colv, 1) == (1 - direction(colv)) * (C - 1)
    r2 = lax.broadcasted_iota(jnp.int32, (C, C), 0)
    c2 = lax.broadcasted_iota(jnp.int32, (C, C), 1)
    tri = jnp.concatenate([(r2 >= c2).astype(BF16), (r2 <= c2).astype(BF16)], axis=0)

    def both_dirs(x):
        return jnp.broadcast_to(x[:, None], (HU, 2) + x.shape[1:]).reshape((G,) + x.shape[1:])

    def all_heads(x):
        return jnp.broadcast_to(x[None], (HB,) + x.shape).reshape((HB * x.shape[0],) + x.shape[1:])

    def prep_group(c, carry):
        rows = pl.ds(pl.multiple_of(c * (U * C), U * C), U * C)
        q = q_s[:, rows, :].reshape(HU, C, LANES)
        k = k_s[:, rows, :].reshape(HU, C, LANES)
        v = v_s[:, rows, :].reshape(HU, C, LANES)
        ball = all_heads(b_s[rows, :].reshape(U, C, LANES))
        g_hi, g_lo = _split(g_s[rows, :])
        g_hi = g_hi.reshape(U, C, LANES)
        g_lo = g_lo.reshape(U, C, LANES)
        gcum = jnp.stack([_dot(tri, g_hi[u]) + _dot(tri, g_lo[u]) for u in range(U)])
        gcum = gcum.reshape(2 * U, C, LANES)
        gcum_t = all_heads(jnp.stack([gcum[g].T for g in range(2 * U)]))
        gcum = all_heads(gcum)
        gc = jnp.sum(jnp.where(mask_g, gcum, 0.0), axis=2, keepdims=True)
        gr = jnp.sum(jnp.where(mask_t, gcum_t, 0.0), axis=1, keepdims=True)
        beta = jnp.sum(jnp.where(mask_b, both_dirs(ball), 0.0), axis=2, keepdims=True)
        kbf = k.astype(BF16)
        kq = jnp.einsum('uik,ujk->uij', jnp.concatenate([k, q], axis=1).astype(BF16), kbf,
                        preferred_element_type=F32)
        kk = both_dirs(kq[:, :C])
        qk = both_dirs(kq[:, C:])
        decay = jnp.where(incl, jnp.exp(jnp.where(incl, gc - gr, 0.0)), 0.0)
        lmat = jnp.where(strict, beta * kk * decay, 0.0)
        p = jnp.where(same[0], -lmat, 0.0)
        tmat = eye + p
        for _ in range(2):
            p = _bdot16(p, p)
            tmat = tmat + _bdot16(tmat, p)
        for off in off_blocks:
            t16 = tmat.astype(BF16)
            tmat = tmat - _bdot(_bdot(t16, jnp.where(off, lmat, 0.0).astype(BF16)).astype(BF16), t16)
        eg = jnp.exp(gc)
        k2 = both_dirs(k)
        kb = k2 * beta
        sol = _bdot16(tmat, jnp.concatenate([both_dirs(v) * beta, kb * eg], axis=2))
        glast = jnp.sum(jnp.where(mask_last, gc, 0.0), axis=1, keepdims=True)
        kdec = k2 * jnp.exp(glast - gc)
        kdec_t = jnp.stack([kdec[g].T for g in range(G)]).astype(BF16)
        sol16 = sol.astype(BF16)
        r_in = _bdot((qk * decay).astype(BF16), sol16)
        r_kd = _bdot(kdec_t, sol16)
        def put(ref, val, size):
            n = 2 * U * size
            ref[:, pl.ds(pl.multiple_of(c * n, n), n), :] = val.reshape(HB, n, val.shape[-1])

        put(oc_s, r_in[:, :, :LANES], C)
        put(qe_s, (both_dirs(q) * eg - r_in[:, :, LANES:]).astype(BF16), C)
        put(sc_s, r_kd[:, :, :LANES], LANES)
        put(sm_s, (-r_kd[:, :, LANES:]).astype(BF16), LANES)
        put(gl_s, jnp.broadcast_to(jnp.exp(glast), (G, 8, LANES)), 8)
        return carry

    lax.fori_loop(0, n_chunks // U, prep_group, 0)

    def scan_step(c, carry):
        slot_f = c * 2
        slot_b = (n_chunks - 1 - c) * 2 + 1

        def ld(ref, size):
            parts = []
            for hh in range(HB):
                parts.append(ref[hh, pl.ds(pl.multiple_of(slot_f * size, size), size), :])
                parts.append(ref[hh, pl.ds(pl.multiple_of(slot_b * size, size), size), :])
            return jnp.stack(parts)

        s = st_s[...]
        sb16 = s.astype(BF16)
        o = _bdot(ld(qe_s, C), sb16) + ld(oc_s, C)
        st_s[...] = s * ld(gl_s, 8)[:, 0:1, :] + (_bdot(ld(sm_s, LANES), sb16) + ld(sc_s, LANES))
        for hh in range(HB):
            oc_s[hh, pl.ds(pl.multiple_of(slot_f * C, C), C), :] = o[2 * hh]
            oc_s[hh, pl.ds(pl.multiple_of(slot_b * C, C), C), :] = o[2 * hh + 1]
        return carry

    lax.fori_loop(0, n_chunks, scan_step, 0)

    FIN = 4

    def finish(i, carry):
        rows = pl.ds(pl.multiple_of(i * (FIN * C), FIN * C), FIN * C)
        outs = []
        for hh in range(HB):
            both = oc_s[hh, pl.ds(pl.multiple_of(i * (FIN * 2 * C), FIN * 2 * C), FIN * 2 * C), :]
            both = both.reshape(FIN, 2, C, LANES)
            outs.append(_rms((both[:, 0] + both[:, 1]).reshape(FIN * C, LANES), ng_ref[...]))
        o = jnp.concatenate(outs, axis=1) * _silu(zg_ref[rows, :])
        o_ref[rows, :] = o.astype(o_ref.dtype)
        return carry

    lax.fori_loop(0, n_chunks // FIN, finish, 0)
    for hh in range(HB):
        sf_out_ref[0, hh] = st_s[2 * hh]
        sb_out_ref[0, hh] = st_s[2 * hh + 1]


def _gdn(z, zab, conv_w, norm_g, alog_row, dtb_row, B, L, state_f=None, state_b=None, layer_j=0):
    T = z.shape[0]
    H = GDN_HEADS
    has_state = state_f is not None
    HB = H if L <= GDN_ALL_HEADS_MAX_LEN else GDN_HEADS_LONG
    nh = H // HB
    w = HB * LANES
    blk = lambda i: pl.BlockSpec((L, w), lambda b, h: (b, i * nh + h))
    cw = lambda i: pl.BlockSpec((GDN_CONV_W, w), lambda b, h: (0, i * nh + h))
    in_specs = [blk(0), blk(1), blk(2), blk(3),
                pl.BlockSpec((L, LANES), lambda b, h: (b, 0)),
                cw(0), cw(1), cw(2),
                pl.BlockSpec((1, LANES), lambda b, h: (0, 0)),
                pl.BlockSpec((1, LANES), lambda b, h: (0, 0)),
                pl.BlockSpec((1, LANES), lambda b, h: (0, 0))]
    args = [z, z, z, z, zab, conv_w, conv_w, conv_w, norm_g, alog_row, dtb_row]
    if has_state:
        st = pl.BlockSpec((1, 1, HB, GDN_DK, LANES), lambda b, h: (b, layer_j, h, 0, 0))
        in_specs += [st, st]
        args += [state_f, state_b]
    sout = pl.BlockSpec((1, HB, GDN_DK, LANES), lambda b, h: (b, h, 0, 0))
    n_slots = 2 * (L // GDN_CHUNK)
    scratch = ([pltpu.VMEM((HB, L, LANES), F32) for _ in range(3)]
               + [pltpu.VMEM((L, LANES), F32) for _ in range(2)]
               + [pltpu.VMEM((2 * HB, GDN_DK, LANES), F32)]
               + [pltpu.VMEM((HB, 2 * L, LANES), F32), pltpu.VMEM((HB, 2 * L, LANES), BF16)]
               + [pltpu.VMEM((HB, n_slots * GDN_DK, LANES), F32),
                  pltpu.VMEM((HB, n_slots * GDN_DK, LANES), BF16)]
               + [pltpu.VMEM((HB, n_slots * 8, LANES), F32)]
               + [pltpu.VMEM((L + 16, LANES), F32)])
    return pl.pallas_call(
        functools.partial(_gdn_kernel, L=L, HB=HB, has_state=has_state),
        grid=(B, nh),
        in_specs=in_specs,
        out_specs=[pl.BlockSpec((L, w), lambda b, h: (b, h)), sout, sout],
        out_shape=[jax.ShapeDtypeStruct((T, H * LANES), MIX_DTYPE),
                   jax.ShapeDtypeStruct((B, H, GDN_DK, LANES), F32),
                   jax.ShapeDtypeStruct((B, H, GDN_DK, LANES), F32)],
        scratch_shapes=scratch,
        compiler_params=_cparams(("arbitrary", "arbitrary")),
        name="gdn",
    )(*args)


def _hy_filter_kernel(f_ref, w1_ref, b1_ref, sf0_ref, w2_ref, b2_ref, sf1_ref, w3_ref, dl_ref, o_ref, *, L):
    rowi = lax.broadcasted_iota(jnp.int32, (L, LANES), 0).astype(F32)
    lane = lax.broadcasted_iota(jnp.int32, (L, LANES), 1)
    t = rowi * (1.0 / (L - 1))
    wpos = rowi * (2.0 * math.pi / L)
    ang = f_ref[...] * wpos
    z = jnp.where(lane == 0, t,
                  jnp.where(lane <= HY_BANDS, jnp.cos(ang),
                            jnp.where(lane <= 2 * HY_BANDS, -jnp.sin(ang), 0.0)))
    h = jnp.sin(sf0_ref[...] * (_dot3(z, w1_ref[...]) + b1_ref[...]))
    h = jnp.sin(sf1_ref[...] * (_dot3(h, w2_ref[...]) + b2_ref[...]))
    tc = lax.broadcasted_iota(jnp.int32, (L, HY_CH), 0).astype(F32) * (1.0 / (L - 1))
    window = jnp.exp(-tc * dl_ref[...])
    for j in range(4):
        o_ref[:, j * HY_CH:(j + 1) * HY_CH] = _dot3(h, w3_ref[:, j * HY_CH:(j + 1) * HY_CH]) * window


def _pad_to(a, shape):
    return jnp.pad(a, [(0, s - d) for s, d in zip(shape, a.shape)])


def _hy_filters(L, w1, b1, w2, b2, w3, sin_freq):
    fvals = np.linspace(1e-4, HY_BANDS - 1, HY_BANDS, dtype=np.float32)
    frow = np.zeros((1, LANES), np.float32)
    frow[0, 1:1 + HY_BANDS] = fvals
    frow[0, 1 + HY_BANDS:1 + 2 * HY_BANDS] = fvals
    deltas = np.abs(np.linspace(math.log(HY_DECAY_TARGET) / HY_LONG_DECAY_PCT,
                                math.log(HY_DECAY_TARGET) / HY_SHORT_DECAY_PCT, HY_CH, dtype=np.float32))
    args = [jnp.asarray(frow),
            _pad_to(w1, (LANES, LANES)), _pad_to(b1[None, :], (1, LANES)), _pad_to(sin_freq[0][None, :], (1, LANES)),
            _pad_to(w2, (LANES, LANES)), _pad_to(b2[None, :], (1, LANES)), _pad_to(sin_freq[1][None, :], (1, LANES)),
            _pad_to(w3, (LANES, 4 * HY_CH)), jnp.asarray(deltas[None, :])]
    return pl.pallas_call(
        functools.partial(_hy_filter_kernel, L=L),
        out_shape=jax.ShapeDtypeStruct((L, 4 * HY_CH), F32),
        compiler_params=pltpu.CompilerParams(vmem_limit_bytes=VMEM_LIMIT),
        name="hyena_filters",
    )(*args)


def _dft_tables(L):
    N = 2 * L
    k = np.arange(L, dtype=np.int64)[:, None]
    s = np.arange(L, dtype=np.int64)[None, :]
    ang = ((2 * k + 1) * s % (2 * N)).astype(np.float64) * (2.0 * math.pi / (2 * N))
    return np.cos(ang).astype(np.float32), np.sin(ang).astype(np.float32)


def _dft_blocks(L, kb):
    cm, sm = _dft_tables(L)
    nk = L // kb
    fwd = np.concatenate([cm.reshape(nk, kb, L), sm.reshape(nk, kb, L)], axis=1)
    return jnp.asarray(fwd).astype(BF16)


def _hy_spectrum_kernel(f_ref, flt_ref, hc_ref, hs_ref, *, L, kb):
    row = lax.broadcasted_iota(jnp.int32, (L, HY_CH), 0)
    scale = 1.0 / L
    for o in range(2):
        hf = flt_ref[:, (2 * o) * HY_CH:(2 * o + 1) * HY_CH]
        hb = jnp.where(row == 0, 0.0, flt_ref[:, (2 * o + 1) * HY_CH:(2 * o + 2) * HY_CH])
        a_hi, a_lo = _split(hf + hb)
        d_hi, d_lo = _split(hf - hb)
        fc = f_ref[0, :kb, :]
        fs = f_ref[0, kb:, :]
        hc_ref[o] = (_dot(fc, a_hi) + _dot(fc, a_lo)) * scale
        hs_ref[o] = (_dot(fs, d_hi) + _dot(fs, d_lo)) * scale


def _hy_spectrum(filt, fwd, L, kb):
    nk = L // kb
    return pl.pallas_call(
        functools.partial(_hy_spectrum_kernel, L=L, kb=kb),
        grid=(nk,),
        in_specs=[pl.BlockSpec((1, 2 * kb, L), lambda i: (i, 0, 0)),
                  pl.BlockSpec((L, 4 * HY_CH), lambda i: (0, 0))],
        out_specs=[pl.BlockSpec((2, kb, HY_CH), lambda i: (0, i, 0)),
                   pl.BlockSpec((2, kb, HY_CH), lambda i: (0, i, 0))],
        out_shape=[jax.ShapeDtypeStruct((2, L, HY_CH), F32)] * 2,
        compiler_params=_cparams(("arbitrary",)),
        name="hyena_spectrum",
    )(fwd, filt)


def _hy_conv_kernel(ze_ref, zo_ref, xe_ref, xo_ref, ge_ref, go_ref, cwz_ref, cbz_ref, cwx_ref, cbx_ref, skip_ref,
                    f_ref, g_ref, har_ref, hai_ref, hbr_ref, hbi_ref, cw_ref, sw_ref, oe_ref, oo_ref,
                    *, kb, nk, first, last):
    cb = ze_ref.shape[1]

    def conv3(e_ref, o_ref, w_ref, b_ref):
        xe = e_ref[...]
        xo = o_ref[...]
        w0, w1, w2 = w_ref[0:1, :], w_ref[1:2, :], w_ref[2:3, :]
        ye = _shift_rows(xo, -1) * w0 + xe * w1 + xo * w2 + b_ref[...]
        yo = xe * w0 + xo * w1 + _shift_rows(xe, 1) * w2 + b_ref[...]
        return ye, yo

    if first:
        ze, zo = conv3(ze_ref, zo_ref, cwz_ref, cbz_ref)
    else:
        ze, zo = ze_ref[...], zo_ref[...]
    zb = jnp.concatenate([ze, zo], axis=1).astype(BF16)
    acc = None
    for k in range(nk):
        rows = slice(k * kb, (k + 1) * kb)
        xc = _dot(f_ref[k, :kb, :], zb)
        xs = _dot(f_ref[k, kb:, :], zb)
        ec, oc = xc[:, :cb], xc[:, cb:]
        es, os_ = xs[:, :cb], xs[:, cb:]
        cw = cw_ref[rows, :]
        sw = sw_ref[rows, :]
        p_re = cw * oc - sw * os_
        p_im = -(cw * os_ + sw * oc)
        xa_re, xa_im = ec + p_re, p_im - es
        xb_re, xb_im = ec - p_re, es + p_im
        har, hai = har_ref[0, rows, :], hai_ref[0, rows, :]
        hbr, hbi = hbr_ref[0, rows, :], hbi_ref[0, rows, :]
        ya_re = xa_re * har - xa_im * hai
        ya_im = xa_re * hai + xa_im * har
        yb_re = xb_re * hbr - xb_im * hbi
        yb_im = xb_re * hbi + xb_im * hbr
        ze_re, ze_im = ya_re + yb_re, ya_im - yb_im
        d_re, d_im = ya_re - yb_re, ya_im + yb_im
        zo_re = d_re * cw - d_im * sw
        zo_im = d_re * sw + d_im * cw
        spec = jnp.concatenate([jnp.concatenate([ze_re, zo_re], axis=1),
                                jnp.concatenate([-ze_im, -zo_im], axis=1)], axis=0).astype(BF16)
        part = _dot(g_ref[k], spec)
        acc = part if acc is None else acc + part
    xe, xo = conv3(xe_ref, xo_ref, cwx_ref, cbx_ref)
    re = xe * (acc[:, :cb] + ze * skip_ref[...])
    ro = xo * (acc[:, cb:] + zo * skip_ref[...])
    if last:
        re = re * _silu(ge_ref[...])
        ro = ro * _silu(go_ref[...])
    oe_ref[...] = re.astype(oe_ref.dtype)
    oo_ref[...] = ro.astype(oo_ref.dtype)


def _hy_half_tables(L, kb, cb):
    M = L // 2
    nk = M // kb
    k = np.arange(M, dtype=np.int64)[:, None]
    s = np.arange(M, dtype=np.int64)[None, :]
    ang = ((2 * k + 1) * s % (2 * L)).astype(np.float64) * (2.0 * math.pi / (2 * L))
    c2, s2 = np.cos(ang).astype(np.float32), np.sin(ang).astype(np.float32)
    fwd = np.concatenate([c2.reshape(nk, kb, M), s2.reshape(nk, kb, M)], axis=1)
    inv = np.concatenate([c2.T.reshape(M, nk, kb), s2.T.reshape(M, nk, kb)], axis=2)
    inv = np.ascontiguousarray(inv.transpose(1, 0, 2))
    w = (2 * k + 1).astype(np.float64) * (2.0 * math.pi / (4 * L))
    cw = np.ascontiguousarray(np.broadcast_to(np.cos(w), (M, cb))).astype(np.float32)
    sw = np.ascontiguousarray(np.broadcast_to(np.sin(w), (M, cb))).astype(np.float32)
    return (jnp.asarray(fwd).astype(BF16), jnp.asarray(inv).astype(BF16), jnp.asarray(cw), jnp.asarray(sw))


def _hy_conv(z_even, z_odd, x_even, x_odd, g_even, g_odd, conv_w, conv_b, skip, order, tables, spectra,
             B, L, first, last):
    M = L // 2
    cb = HY_CH_BLOCK
    ncb = HY_CH // cb
    fwd, inv, cw, sw = tables
    nk, _, kb2 = inv.shape
    kb = kb2 // 2
    once = dict(pipeline_mode=pl.Buffered(1))
    data = lambda op: pl.BlockSpec((M, cb), lambda b, c: (b, op[1] + c))
    cwspec = lambda off: pl.BlockSpec((3, cb), lambda b, c: (0, off + c))
    cbspec = lambda off: pl.BlockSpec((1, cb), lambda b, c: (0, off + c))
    hspec = pl.BlockSpec((1, M, cb), lambda b, c: (order, 0, c))
    operands = [z_even, z_odd, x_even, x_odd, g_even, g_odd]
    in_specs = ([data(op) for op in operands]
                + [cwspec(0), cbspec(0), cwspec((1 + order) * ncb), cbspec((1 + order) * ncb),
                   pl.BlockSpec((1, cb), lambda b, c: (0, c)),
                   pl.BlockSpec(fwd.shape, lambda b, c: (0, 0, 0), **once),
                   pl.BlockSpec(inv.shape, lambda b, c: (0, 0, 0), **once),
                   hspec, hspec, hspec, hspec,
                   pl.BlockSpec((M, cb), lambda b, c: (0, 0), **once),
                   pl.BlockSpec((M, cb), lambda b, c: (0, 0), **once)])
    out_dtype = MIX_DTYPE if last else F32
    out = pl.BlockSpec((M, cb), lambda b, c: (b, c))
    return pl.pallas_call(
        functools.partial(_hy_conv_kernel, kb=kb, nk=nk, first=first, last=last),
        grid=(B, ncb),
        in_specs=in_specs,
        out_specs=[out, out],
        out_shape=[jax.ShapeDtypeStruct((B * M, HY_CH), out_dtype)] * 2,
        compiler_params=_cparams(("arbitrary", "arbitrary")),
        name="hyena_conv",
    )(*[op[0] for op in operands], conv_w, conv_b, conv_w, conv_b, skip[order:order + 1],
      fwd, inv, *spectra, cw, sw)


OD_GQ, OD_GG, OD_MG, OD_GK, OD_GV, OD_MQ, OD_MKV, OD_MPE = 0, 4, 8, 12, 14, 16, 18, 19
OD_WIDTH = 20 * LANES


def _rope_swap(x, quarter):
    n = x.shape[1]
    lane = lax.broadcasted_iota(jnp.int32, x.shape, 1)
    first = (lane & (2 * quarter - 1)) < quarter
    return jnp.where(first, pltpu.roll(x, n - quarter, 1), pltpu.roll(x, quarter, 1))


def _odd_in_kernel(x_ref, sh_ref, sc_ref, g_ref, w_ref, qg_ref, kg_ref, mqg_ref, mqup_ref, mkvg_ref, wk_ref,
                   *rest, rope):
    if rope:
        cg_ref, sg_ref, cm_ref, sm_ref = rest[:4]
        rest = rest[4:]
    q_out, qc_out, k16_out, v16_out, kc16_out, gate_out, k_out, v_out, ckv_out, kpe_out = rest
    D = HEAD_DIM
    hb = (_rms(x_ref[...], g_ref[...]) * (1.0 + sc_ref[0]) + sh_ref[0]).astype(BF16)

    def proj(col, width):
        return _dot(hb, w_ref[:, col * D:(col + width) * D])

    def rot(x, c, s, quarter):
        return x * c + _rope_swap(x, quarter) * s if rope else x

    gscale = (D ** -0.5) * LOG2E
    zq = proj(OD_GQ, GQA_HEADS)
    for h in range(GQA_HEADS):
        q = rot(_rms(zq[:, h * D:(h + 1) * D], qg_ref[...]), cg_ref[...] if rope else None,
                sg_ref[...] if rope else None, D // 4)
        q_out[:, h * D:(h + 1) * D] = (q * gscale).astype(BF16)
    zk = proj(OD_GK, GQA_KV_HEADS)
    zv = proj(OD_GV, GQA_KV_HEADS)
    v_out[...] = zv
    v16_out[...] = zv.astype(BF16)
    for h in range(GQA_KV_HEADS):
        kn = _rms(zk[:, h * D:(h + 1) * D], kg_ref[...])
        k_out[:, h * D:(h + 1) * D] = kn
        k16_out[:, h * D:(h + 1) * D] = rot(kn, cg_ref[...] if rope else None,
                                            sg_ref[...] if rope else None, D // 4).astype(BF16)
    gate_out[...] = proj(OD_GG, GQA_HEADS + MLA_HEADS)

    mq = _rms(proj(OD_MQ, 2), mqg_ref[...]).astype(BF16)
    qm = _dot(mq, mqup_ref[...])
    mscale = ((D + MLA_ROPE_DIM) ** -0.5) * LOG2E
    qpe = qm[:, MLA_HEADS * D:]
    if rope:
        qpe = rot(qpe, cm_ref[...], sm_ref[...], MLA_ROPE_DIM // 4)
    for h in range(MLA_HEADS):
        qn = qm[:, h * D:(h + 1) * D].astype(BF16)
        qc_out[:, 2 * h * D:(2 * h + 1) * D] = (_dot_nt(qn, wk_ref[h]) * mscale).astype(BF16)
        qc_out[:, (2 * h + 1) * D:(2 * h + 2) * D] = (qpe[:, h * D:(h + 1) * D] * mscale).astype(BF16)
    ckv = _rms(proj(OD_MKV, 1), mkvg_ref[...])
    ckv_out[...] = ckv
    kpe = proj(OD_MPE, 1)
    kpe_out[...] = kpe
    if rope:
        kpe = rot(kpe, cm_ref[:, :D], sm_ref[:, :D], MLA_ROPE_DIM // 4)
    kc16_out[:, :D] = ckv.astype(BF16)
    kc16_out[:, D:] = kpe.astype(BF16)


def _rope_tables(L, R):
    rows = np.arange(L) // GRID_W
    cols = np.arange(L) % GRID_W
    quarter = R // 4
    inv = ROPE_THETA ** (-np.arange(quarter, dtype=np.float32) * 2.0 / (R // 2))
    a_r = rows[:, None].astype(np.float32) * inv[None, :]
    a_c = cols[:, None].astype(np.float32) * inv[None, :]
    cos = np.concatenate([np.cos(a_r), np.cos(a_r), np.cos(a_c), np.cos(a_c)], axis=1)
    sin = np.concatenate([-np.sin(a_r), np.sin(a_r), -np.sin(a_c), np.sin(a_c)], axis=1)
    return cos.astype(np.float32), sin.astype(np.float32)


def _odd_in(x2, shift, scale, pre_g, w_main, q_g, k_g, mq_g, mq_up, mkv_g, wk, L, tiles_per_mod, rope):
    T, dm = x2.shape
    tm = TOKEN_TILE
    D = HEAD_DIM
    if tiles_per_mod:
        mod_map = lambda i: (i // tiles_per_mod, 0, 0)
    else:
        mod_map = lambda i: (0, 0, 0)
    full = lambda shape: pl.BlockSpec(shape, lambda i: tuple(0 for _ in shape))
    in_specs = [pl.BlockSpec((tm, dm), lambda i: (i, 0)),
                pl.BlockSpec((1, 1, dm), mod_map), pl.BlockSpec((1, 1, dm), mod_map),
                full((1, dm)), full(w_main.shape),
                full((1, D)), full((1, D)), full((1, 2 * D)), full(mq_up.shape), full((1, D)), full(wk.shape)]
    args = [x2, shift, scale, pre_g, w_main, q_g, k_g, mq_g, mq_up, mkv_g, wk]
    if rope:
        cg, sg = _rope_tables(L, D)
        cm, sm = _rope_tables(L, MLA_ROPE_DIM)
        widen = lambda t: np.tile(np.concatenate([t, np.zeros_like(t)], axis=1), (1, MLA_HEADS))
        per = L // tm
        pos = lambda w: pl.BlockSpec((tm, w), lambda i: (i % per, 0))
        in_specs += [pos(D), pos(D), pos(MLA_HEADS * D), pos(MLA_HEADS * D)]
        args += [jnp.asarray(cg), jnp.asarray(sg), jnp.asarray(widen(cm)), jnp.asarray(widen(sm))]
    tile = lambda w: pl.BlockSpec((tm, w), lambda i: (i, 0))
    widths = [(4 * D, BF16), (8 * D, BF16), (2 * D, BF16), (2 * D, BF16), (2 * D, BF16), (8 * D, F32),
              (2 * D, F32), (2 * D, F32), (D, F32), (D, F32)]
    return pl.pallas_call(
        functools.partial(_odd_in_kernel, rope=rope),
        grid=(T // tm,),
        in_specs=in_specs,
        out_specs=[tile(w) for w, _ in widths],
        out_shape=[jax.ShapeDtypeStruct((T, w), dt) for w, dt in widths],
        compiler_params=_cparams(("arbitrary",)),
        name="odd_in_proj",
    )(*args)


def _softmax_pv(score_blocks, value_blocks):
    m = score_blocks[0].max(axis=-1, keepdims=True)
    for s in score_blocks[1:]:
        m = jnp.maximum(m, s.max(axis=-1, keepdims=True))
    acc = None
    den = None
    for s, v in zip(score_blocks, value_blocks):
        p = jnp.exp2(s - m)
        d = p.sum(axis=-1, keepdims=True)
        a = _dot(p.astype(BF16), v)
        acc = a if acc is None else acc + a
        den = d if den is None else den + d
    return acc / den


def _gqa_kernel(q_ref, k_ref, v_ref, gate_ref, *rest, cached):
    if cached:
        ck_ref, cv_ref, o_ref = rest
    else:
        (o_ref,) = rest
    D = HEAD_DIM
    group = GQA_HEADS // GQA_KV_HEADS
    q = q_ref[...]
    keys = [k_ref[...]] + ([ck_ref[0, 0]] if cached else [])
    values = [v_ref[...]] + ([cv_ref[0, 0]] if cached else [])
    outs = []
    for g in range(group):
        qh = q[:, g * D:(g + 1) * D]
        outs.append(_softmax_pv([_dot_nt(qh, kk) for kk in keys], values))
    o_ref[...] = (jnp.concatenate(outs, axis=1) * _silu(gate_ref[...])).astype(o_ref.dtype)


def _gqa(qg, kg, vg, gates, B, L, tq, cache_k=None, cache_v=None, layer_j=0):
    T = qg.shape[0]
    D = HEAD_DIM
    group = GQA_HEADS // GQA_KV_HEADS
    nq = L // tq
    cached = cache_k is not None
    in_specs = [pl.BlockSpec((tq, group * D), lambda b, h, i: (b * nq + i, h)),
                pl.BlockSpec((L, D), lambda b, h, i: (b, h)),
                pl.BlockSpec((L, D), lambda b, h, i: (b, h)),
                pl.BlockSpec((tq, group * D), lambda b, h, i: (b * nq + i, h))]
    args = [qg, kg, vg, gates]
    if cached:
        P = cache_k.shape[2]
        cspec = pl.BlockSpec((1, 1, P, D), lambda b, h, i: (b, layer_j, 0, h))
        in_specs += [cspec, cspec]
        args += [cache_k, cache_v]
    return pl.pallas_call(
        functools.partial(_gqa_kernel, cached=cached),
        grid=(B, GQA_KV_HEADS, nq),
        in_specs=in_specs,
        out_specs=pl.BlockSpec((tq, group * D), lambda b, h, i: (b * nq + i, h)),
        out_shape=jax.ShapeDtypeStruct((T, GQA_HEADS * D), MIX_DTYPE),
        compiler_params=_cparams(("arbitrary", "arbitrary", "arbitrary")),
        name="gqa_attention",
    )(*args)


def _mla_kernel(qc_ref, kc_ref, gate_ref, wv_ref, *rest, cached):
    if cached:
        ckc_ref, o_ref = rest
    else:
        (o_ref,) = rest
    D = HEAD_DIM
    H = MLA_HEADS
    qc = qc_ref[...]
    tq = qc.shape[0]
    keys = [kc_ref[...]] + ([ckc_ref[0, 0]] if cached else [])
    per = H // MLA_HEAD_STACKS
    outs = []
    for s in range(MLA_HEAD_STACKS):
        heads = range(s * per, (s + 1) * per)
        qs = jnp.concatenate([qc[:, 2 * h * D:(2 * h + 2) * D] for h in heads], axis=0)
        o = _softmax_pv([_dot_nt(qs, kk) for kk in keys], [kk[:, :D] for kk in keys]).astype(BF16)
        for i, h in enumerate(heads):
            outs.append(_dot(o[i * tq:(i + 1) * tq], wv_ref[h]))
    o_ref[...] = (jnp.concatenate(outs, axis=1) * _silu(gate_ref[...])).astype(o_ref.dtype)


def _mla(qc, kc, gates, wv, B, L, tq, cache_kc=None, layer_j=0):
    T = qc.shape[0]
    D = HEAD_DIM
    H = MLA_HEADS
    nq = L // tq
    cached = cache_kc is not None
    in_specs = [pl.BlockSpec((tq, 2 * H * D), lambda b, i: (b * nq + i, 0)),
                pl.BlockSpec((L, 2 * D), lambda b, i: (b, 0)),
                pl.BlockSpec((tq, H * D), lambda b, i: (b * nq + i, 1)),
                pl.BlockSpec(wv.shape, lambda b, i: (0, 0, 0))]
    args = [qc, kc, gates, wv]
    if cached:
        P = cache_kc.shape[2]
        in_specs += [pl.BlockSpec((1, 1, P, 2 * D), lambda b, i: (b, layer_j, 0, 0))]
        args += [cache_kc]
    return pl.pallas_call(
        functools.partial(_mla_kernel, cached=cached),
        grid=(B, nq),
        in_specs=in_specs,
        out_specs=pl.BlockSpec((tq, H * D), lambda b, i: (b * nq + i, 0)),
        out_shape=jax.ShapeDtypeStruct((T, H * D), MIX_DTYPE),
        compiler_params=_cparams(("arbitrary", "arbitrary")),
        name="mla_attention",
    )(*args)


def _even_weights(even_in_w, gdn_conv_w, gdn_a_log, gdn_dt_bias, hyena_conv_w, hyena_conv_b, j):
    w = even_in_w[j]
    qkv_w = 3 * GDN_HEADS * LANES
    n_ab = 4 * GDN_HEADS
    gate_end = qkv_w + n_ab + GDN_HEADS * LANES
    w_a = jnp.concatenate([w[:, :qkv_w], w[:, qkv_w + n_ab:gate_end]], axis=1).astype(BF16)
    w_small = _pad_to(w[:, qkv_w:qkv_w + n_ab], (w.shape[0], LANES)).astype(BF16)
    w_b = w[:, gate_end:].astype(BF16)
    n_dir = 2 * GDN_HEADS
    alog_row = _pad_to(gdn_a_log[j].reshape(1, n_dir), (1, LANES))
    dtb_row = _pad_to(gdn_dt_bias[j].reshape(1, n_dir), (1, LANES))
    return w_a, w_small, w_b, alog_row, dtb_row


def _even_layer(x2, B, L, shift, scale, gate, tiles_per_mod, pre_g, post_g, wts, hy, out_w,
                gdn_conv_w, gdn_norm_g, hy_conv_w, hy_conv_b, hy_skip, state_f, state_b, j):
    w_a, w_small, w_b, alog_row, dtb_row = wts
    tables, spectra = hy
    z, zab, zp = _in_proj(x2, shift, scale, pre_g, w_a, w_small, w_b, tiles_per_mod)
    oa, s_f, s_b = _gdn(z, zab, gdn_conv_w, gdn_norm_g, alog_row, dtb_row, B, L, state_f, state_b, j)
    ncb = HY_CH // HY_CH_BLOCK
    n_groups = zp.shape[1] // (2 * HY_CH)
    even = lambda idx: (zp, idx * ncb)
    odd = lambda idx: (zp, (n_groups + idx) * ncb)
    z1e, z1o = _hy_conv(even(0), odd(0), even(1), odd(1), even(3), odd(3), hy_conv_w, hy_conv_b, hy_skip, 0,
                        tables, spectra, B, L, first=True, last=False)
    obe, obo = _hy_conv((z1e, 0), (z1o, 0), even(2), odd(2), even(3), odd(3), hy_conv_w, hy_conv_b, hy_skip, 1,
                        tables, spectra, B, L, first=False, last=True)
    x_new = _out_proj(oa, (obe, obo), x2, gate, post_g, out_w, tiles_per_mod)
    return x_new, s_f, s_b


def _odd_weights(odd_in_w, mla_q_up, mla_kv_up, j):
    w = odd_in_w[j]
    D = HEAD_DIM
    o = np.cumsum([0, 4 * D, 2 * D, 2 * D, 4 * D, 2 * D, D, MLA_ROPE_DIM, 4 * D])
    gq, gk, gv, gg, mq, mkv, mpe, mg = [w[:, o[i]:o[i + 1]] for i in range(8)]
    w_main = jnp.concatenate([gq, gg, mg, gk, gv, mq, mkv, _pad_to(mpe, (w.shape[0], D))], axis=1).astype(BF16)
    up = mla_q_up[j].reshape(-1, MLA_HEADS, D + MLA_ROPE_DIM)
    rope_cols = _pad_to(up[:, :, D:], (up.shape[0], MLA_HEADS, D))
    mq_up = jnp.concatenate([up[:, :, :D].reshape(-1, MLA_HEADS * D),
                             rope_cols.reshape(-1, MLA_HEADS * D)], axis=1).astype(BF16)
    kv = mla_kv_up[j].reshape(-1, MLA_HEADS, 2 * D)
    wk = kv[:, :, :D].transpose(1, 0, 2).astype(BF16)
    wv = kv[:, :, D:].transpose(1, 0, 2).astype(BF16)
    return w_main, mq_up, wk, wv


def _odd_layer(x2, B, L, shift, scale, gate, tiles_per_mod, pre_g, post_g, wts, out_w,
               q_g, k_g, mq_g, mkv_g, caches, j):
    w_main, mq_up, wk, wv = wts
    rope = caches is not None
    qg, qc, k16, v16, kc16, gates, kg, v32, ckv, kpe = _odd_in(
        x2, shift, scale, pre_g, w_main, q_g, k_g, mq_g, mq_up, mkv_g, wk, L, tiles_per_mod, rope)
    if rope:
        ck, cv, ckc = caches
        og = _gqa(qg, k16, v16, gates, B, L, GQA_Q_TILE, ck, cv, j)
        om = _mla(qc, kc16, gates, wv, B, L, MLA_Q_TILE, ckc, j)
    else:
        og = _gqa(qg, k16, v16, gates, B, L, GQA_Q_TILE)
        om = _mla(qc, kc16, gates, wv, B, L, MLA_Q_TILE)
    x_new = _out_proj(og, om, x2, gate, post_g, out_w, tiles_per_mod)
    return x_new, (kg, v32, ckv, kpe)


def kernel(x_prompt, x_sample, state_gdn_fwd, state_gdn_bwd, cache_gqa_k, cache_gqa_v, cache_mla_ckv, cache_mla_kpe, c, c_ctx, mod_w, mod_b, pre_norm_g, post_norm_g, even_in_w, gdn_conv_w, gdn_a_log, gdn_dt_bias, gdn_norm_g, hyena_conv_w, hyena_conv_b, hyena_ffn_w1, hyena_ffn_b1, hyena_ffn_w2, hyena_ffn_b2, hyena_ffn_w3, hyena_sin_freq, hyena_bias, even_out_w, odd_in_w, gqa_q_norm_g, gqa_k_norm_g, mla_q_norm_g, mla_q_up, mla_kv_norm_g, mla_kv_up, odd_out_w):
    Bp, Lp, D = x_prompt.shape
    Bs, Ls, _ = x_sample.shape
    depth = mod_w.shape[0]
    xp = x_prompt.reshape(Bp * Lp, D)
    xs = x_sample.reshape(Bs * Ls, D)

    n_cond = 1 + Bs
    rows = -(-n_cond // 8) * 8
    cond = _pad_to(jnp.concatenate([c_ctx[None, :], c], axis=0), (rows, D))
    mod = _modulation(cond, mod_w, mod_b)

    P = cache_gqa_k.shape[2]
    n_odd = cache_gqa_k.shape[1]
    ck = cache_gqa_k.reshape(Bs, n_odd, P, GQA_KV_HEADS * HEAD_DIM).astype(BF16)
    cv = cache_gqa_v.reshape(Bs, n_odd, P, GQA_KV_HEADS * HEAD_DIM).astype(BF16)
    ckc = jnp.concatenate([cache_mla_ckv, _pad_to(cache_mla_kpe, cache_mla_ckv.shape)], axis=-1).astype(BF16)

    dft = {L: _dft_blocks(L, min(HY_FREQ_BLOCK, L)) for L in (Lp, Ls)}
    half = {L: _hy_half_tables(L, min(HY_HALF_FREQ_BLOCK, L // 2), HY_CH_BLOCK) for L in (Lp, Ls)}
    tpm_s = Ls // TOKEN_TILE

    new_f, new_b, new_gk, new_gv, new_ckv, new_kpe = [], [], [], [], [], []
    for i in range(depth):
        j = i // 2
        m = mod[i]
        sh_p, sc_p, gt_p = [m[0:1, k * D:(k + 1) * D].reshape(1, 1, D) for k in range(3)]
        sh_s, sc_s, gt_s = [m[1:n_cond, k * D:(k + 1) * D].reshape(Bs, 1, D) for k in range(3)]
        pre_g = pre_norm_g[i][None, :]
        post_g = post_norm_g[i][None, :]
        if i % 2 == 0:
            wts = _even_weights(even_in_w, gdn_conv_w, gdn_a_log, gdn_dt_bias, hyena_conv_w, hyena_conv_b, j)
            out_w = even_out_w[j].astype(BF16)
            hy = {}
            for L in (Lp, Ls):
                filt = _hy_filters(L, hyena_ffn_w1[j], hyena_ffn_b1[j], hyena_ffn_w2[j], hyena_ffn_b2[j],
                                   hyena_ffn_w3[j], hyena_sin_freq[j])
                hc, hs = _hy_spectrum(filt, dft[L], L, min(HY_FREQ_BLOCK, L))
                M = L // 2
                spectra = (hc[:, :M], -hs[:, :M], jnp.flip(hc[:, M:], axis=1), -jnp.flip(hs[:, M:], axis=1))
                hy[L] = (half[L], spectra)
            common = (gdn_conv_w[j], gdn_norm_g[j][None, :], hyena_conv_w[j], hyena_conv_b[j][None, :], hyena_bias[j])
            xp, sf, sb = _even_layer(xp, Bp, Lp, sh_p, sc_p, gt_p, 0, pre_g, post_g, wts, hy[Lp], out_w,
                                     *common, None, None, j)
            xs, _, _ = _even_layer(xs, Bs, Ls, sh_s, sc_s, gt_s, tpm_s, pre_g, post_g, wts, hy[Ls], out_w,
                                   *common, state_gdn_fwd, state_gdn_bwd, j)
            new_f.append(sf)
            new_b.append(sb)
        else:
            wts = _odd_weights(odd_in_w, mla_q_up, mla_kv_up, j)
            out_w = odd_out_w[j].astype(BF16)
            norms = (gqa_q_norm_g[j][None, :], gqa_k_norm_g[j][None, :], mla_q_norm_g[j][None, :],
                     mla_kv_norm_g[j][None, :])
            xp, (kg, v32, ckv, kpe) = _odd_layer(xp, Bp, Lp, sh_p, sc_p, gt_p, 0, pre_g, post_g, wts, out_w,
                                                 *norms, None, j)
            xs, _ = _odd_layer(xs, Bs, Ls, sh_s, sc_s, gt_s, tpm_s, pre_g, post_g, wts, out_w,
                               *norms, (ck, cv, ckc), j)
            new_gk.append(kg.reshape(Bp, Lp, GQA_KV_HEADS, HEAD_DIM))
            new_gv.append(v32.reshape(Bp, Lp, GQA_KV_HEADS, HEAD_DIM))
            new_ckv.append(ckv.reshape(Bp, Lp, HEAD_DIM))
            new_kpe.append(kpe[:, :MLA_ROPE_DIM].reshape(Bp, Lp, MLA_ROPE_DIM))
    return (xp.reshape(Bp, Lp, D), xs.reshape(Bs, Ls, D),
            jnp.stack(new_f, axis=1), jnp.stack(new_b, axis=1),
            jnp.stack(new_gk, axis=1), jnp.stack(new_gv, axis=1),
            jnp.stack(new_ckv, axis=1), jnp.stack(new_kpe, axis=1))
```
